```python
import jax, jax.numpy as jnp
from jax import lax
import numpy as np

D_MODEL = 1024
BATCH = 32
SEQ = 256
DEPTH = 1
DEC_BATCH = 4
DEC_SEQ = 4096
PAST_LEN = 512

GRID_W = 64
H_M = 4
DH_M = 128
MLSTM_W = H_M * DH_M
CHUNK = 128
M_INIT = -1e30
H_A = 8
NOPE = 64
ROPE_DIM = 32
V_DIM = 64
Q_LORA = 384
KV_LORA = 256
AX_DIM = ROPE_DIM // 2
ROPE_BASE = 10000.0
Q_BLOCK = 128
D_FF = 4 * D_MODEL
N_MOD = 6
EPS = 1e-6
IN_SIZES = (MLSTM_W, MLSTM_W, MLSTM_W, MLSTM_W, 4 * H_M, Q_LORA, KV_LORA, ROPE_DIM, 2 * D_MODEL)
IN_COLS = 4 * MLSTM_W + 4 * H_M + Q_LORA + KV_LORA + ROPE_DIM + 2 * D_MODEL

kernel_name = "hybrid_mlstm_mla_prefix_diffusion_step"


def rmsnorm(x, w):
    xf = x.astype(jnp.float32)
    y = xf * lax.rsqrt(jnp.mean(xf * xf, axis=-1, keepdims=True) + EPS)
    return (y * w.astype(jnp.float32)).astype(x.dtype)


def axial_rope(n_tokens):
    rows = n_tokens // GRID_W
    row = jnp.broadcast_to(jnp.arange(rows, dtype=jnp.float32)[:, None], (rows, GRID_W)).reshape(-1)
    col = jnp.broadcast_to(jnp.arange(GRID_W, dtype=jnp.float32)[None, :], (rows, GRID_W)).reshape(-1)
    inv = ROPE_BASE ** (-jnp.arange(0, AX_DIM, 2, dtype=jnp.float32) / AX_DIM)
    ang = jnp.concatenate([row[:, None] * inv, col[:, None] * inv], axis=-1)
    return jnp.cos(ang), jnp.sin(ang)


def apply_rope(x, cos, sin):
    x1, x2 = x[..., 0::2], x[..., 1::2]
    cos, sin = cos.astype(x.dtype), sin.astype(x.dtype)
    return jnp.stack([x1 * cos - x2 * sin, x1 * sin + x2 * cos], axis=-1).reshape(x.shape)


def blocked_attention(q, k, v):
    B, T, H, Dq = q.shape
    nb = T // Q_BLOCK
    scale = Dq ** -0.5
    kf, vf = k.astype(jnp.float32), v.astype(jnp.float32)
    qb = jnp.moveaxis(q.reshape(B, nb, Q_BLOCK, H, Dq), 1, 0)

    def one_block(qi):
        s = jnp.einsum('bqhd,bkhd->bhqk', qi.astype(jnp.float32), kf) * scale
        p = jax.nn.softmax(s, axis=-1)
        return jnp.einsum('bhqk,bkhd->bqhd', p, vf)

    o = lax.map(one_block, qb)
    return jnp.moveaxis(o, 0, 1).reshape(B, T, H, v.shape[-1]).astype(q.dtype)


def _to_chunks(a, nc):
    B, T, H = a.shape[:3]
    a = a.astype(jnp.float32).reshape((B, nc, CHUNK, H) + a.shape[3:])
    return jnp.moveaxis(a, (1, 3), (0, 2))


def mlstm_chunkwise(q, k, v, log_i, log_f, C0, n0, m0):
    B, T, H, Dh = q.shape
    nc = T // CHUNK
    xs = (_to_chunks(q, nc), _to_chunks(k, nc), _to_chunks(v, nc), _to_chunks(log_i, nc), _to_chunks(log_f, nc))
    causal = jnp.tril(jnp.ones((CHUNK, CHUNK), dtype=bool))

    def step(carry, inp):
        C, n, m = carry
        qc, kc, vc, ic, fc = inp
        b = jnp.cumsum(fc, axis=-1)
        dmat = jnp.where(causal, b[..., :, None] - b[..., None, :] + ic[..., None, :], -jnp.inf)
        inter = b + m[..., None]
        m_t = jnp.maximum(inter, jnp.max(dmat, axis=-1))
        w = jnp.exp(dmat - m_t[..., None])
        s_inter = jnp.exp(inter - m_t)
        qk = jnp.einsum('bhtd,bhsd->bhts', qc, kc) * w
        num = s_inter[..., None] * jnp.einsum('bhtd,bhde->bhte', qc, C) + jnp.einsum('bhts,bhse->bhte', qk, vc)
        den = s_inter * jnp.einsum('bhtd,bhd->bht', qc, n) + jnp.sum(qk, axis=-1)
        h = num / jnp.maximum(jnp.abs(den), jnp.exp(-m_t))[..., None]
        bL = b[..., -1]
        g = bL[..., None] - b + ic
        m_new = jnp.maximum(bL + m, jnp.max(g, axis=-1))
        decay = jnp.exp(bL + m - m_new)
        wk = jnp.exp(g - m_new[..., None])[..., None] * kc
        C_new = decay[..., None, None] * C + jnp.einsum('bhsd,bhse->bhde', wk, vc)
        n_new = decay[..., None] * n + jnp.sum(wk, axis=-2)
        return (C_new, n_new, m_new), h

    carry0 = (C0.astype(jnp.float32), n0.astype(jnp.float32), m0.astype(jnp.float32))
    (C, n, m), h = lax.scan(step, carry0, xs)
    h = jnp.moveaxis(h, (0, 2), (1, 3)).reshape(B, T, H, Dh)
    return h, C, n, m


def mlstm_bidirectional(q, k, v, log_i, log_f, C0, n0, m0):
    rev = lambda a: jnp.flip(a, axis=1)
    h_f, Cf, nf, mf = mlstm_chunkwise(q, k, v, log_i[:, :, 0], log_f[:, :, 0], C0[:, 0], n0[:, 0], m0[:, 0])
    h_b, Cb, nb, mb = mlstm_chunkwise(rev(q), rev(k), rev(v), rev(log_i[:, :, 1]), rev(log_f[:, :, 1]),
                                      C0[:, 1], n0[:, 1], m0[:, 1])
    h = h_f + rev(h_b)
    return h, jnp.stack([Cf, Cb], axis=1), jnp.stack([nf, nb], axis=1), jnp.stack([mf, mb], axis=1)


def token_mixer(h, rope, ctx_ckv, ctx_krope, state0, w_in, gate_b, q_norm, kv_norm, w_uq, w_ukv,
                w_mla_o, head_norm, w_mlstm_o, w_out):
    B, T, _ = h.shape
    idx = np.cumsum(IN_SIZES)[:-1].tolist()
    mq, mk, mv, mo, mg, cq, ckv, krope, merge = jnp.split(h @ w_in, idx, axis=-1)

    ckv = rmsnorm(ckv, kv_norm)
    q = jnp.einsum('btr,rhd->bthd', rmsnorm(cq, q_norm), w_uq)
    q_nope, q_rope = q[..., :NOPE], q[..., NOPE:]
    if rope is not None:
        cos, sin = rope
        q_rope = apply_rope(q_rope, cos[:, None, :], sin[:, None, :])
        keys_ckv = jnp.concatenate([ckv, ctx_ckv.astype(ckv.dtype)], axis=1)
        keys_rope = jnp.concatenate([apply_rope(krope, cos, sin), ctx_krope.astype(krope.dtype)], axis=1)
    else:
        keys_ckv, keys_rope = ckv, krope
    S = keys_ckv.shape[1]
    kv = jnp.einsum('bsr,rhd->bshd', keys_ckv, w_ukv)
    k = jnp.concatenate([kv[..., :NOPE], jnp.broadcast_to(keys_rope[:, :, None, :], (B, S, H_A, ROPE_DIM))], axis=-1)
    attn = blocked_attention(jnp.concatenate([q_nope, q_rope], axis=-1), k, kv[..., NOPE:])
    y_attn = attn.reshape(B, T, H_A * V_DIM) @ w_mla_o

    g = mg.reshape(B, T, 2, 2, H_M).astype(jnp.float32) + gate_b.astype(jnp.float32)
    log_i, log_f = g[:, :, :, 0], jax.nn.log_sigmoid(g[:, :, :, 1])
    heads = lambda a: a.reshape(B, T, H_M, DH_M)
    hm, C, n, m = mlstm_bidirectional(heads(mq) * DH_M ** -0.5, heads(mk), heads(mv), log_i, log_f, *state0)
    hm = rmsnorm(hm, head_norm).astype(h.dtype).reshape(B, T, MLSTM_W)
    y_mlstm = (jax.nn.sigmoid(mo) * hm) @ w_mlstm_o

    g_a, g_b = jnp.split(jax.nn.sigmoid(merge), 2, axis=-1)
    out = (g_a * y_mlstm + g_b * y_attn) @ w_out
    return out, ckv, krope, C, n, m


def trunk_layer(x, mod, rope, ctx_ckv, ctx_krope, state0, pre1, post1, pre2, post2, w_mlp1, w_mlp2, mix_w):
    shift1, scale1, gate1, shift2, scale2, gate2 = jnp.split(mod, N_MOD, axis=-1)
    h = rmsnorm(x, pre1) * (1 + scale1) + shift1
    mix, ckv, krope, C, n, m = token_mixer(h, rope, ctx_ckv, ctx_krope, state0, *mix_w)
    x = x + gate1 * rmsnorm(mix, post1)
    h = rmsnorm(x, pre2) * (1 + scale2) + shift2
    ff = jnp.square(jax.nn.relu(h @ w_mlp1)) @ w_mlp2
    x = x + gate2 * rmsnorm(ff, post2)
    return x, ckv, krope, C, n, m


def setup_inputs(seed: int = 0) -> dict:
    key = jax.random.key(seed)
    ks = jax.random.split(key, 27)
    f32 = jnp.float32

    def nrm(i, shape, scale=1.0):
        return jax.random.normal(ks[i], shape, f32) * scale

    def gain(i, shape):
        return 1.0 + 0.05 * nrm(i, shape)

    L = DEPTH
    gate_base = jnp.array([0.0, 3.0], f32)[None, None, :, None]
    return {
        "x_prompt": nrm(0, (BATCH, SEQ, D_MODEL)),
        "x_sample": nrm(1, (DEC_BATCH, DEC_SEQ, D_MODEL)),
        "cache_mla_ckv": nrm(2, (DEC_BATCH, L, PAST_LEN, KV_LORA)),
        "cache_mla_krope": nrm(3, (DEC_BATCH, L, PAST_LEN, ROPE_DIM)),
        "state_mlstm_C": nrm(4, (DEC_BATCH, L, 2, H_M, DH_M, DH_M), 0.1),
        "state_mlstm_n": nrm(5, (DEC_BATCH, L, 2, H_M, DH_M), 0.1),
        "state_mlstm_m": nrm(6, (DEC_BATCH, L, 2, H_M)),
        "c": nrm(7, (DEC_BATCH, D_MODEL)),
        "c_ctx": nrm(8, (D_MODEL,)),
        "w_ada": nrm(9, (L, D_MODEL, N_MOD * D_MODEL), D_MODEL ** -0.5),
        "b_ada": nrm(10, (L, N_MOD * D_MODEL), 0.02),
        "norm_pre1": gain(11, (L, D_MODEL)),
        "norm_post1": gain(12, (L, D_MODEL)),
        "norm_pre2": gain(13, (L, D_MODEL)),
        "norm_post2": gain(14, (L, D_MODEL)),
        "w_in": nrm(15, (L, D_MODEL, IN_COLS), D_MODEL ** -0.5),
        "mlstm_gate_b": gate_base + 0.1 * nrm(16, (L, 2, 2, H_M)),
        "mla_q_norm": gain(17, (L, Q_LORA)),
        "mla_kv_norm": gain(18, (L, KV_LORA)),
        "w_uq": nrm(19, (L, Q_LORA, H_A, NOPE + ROPE_DIM), Q_LORA ** -0.5),
        "w_ukv": nrm(20, (L, KV_LORA, H_A, NOPE + V_DIM), KV_LORA ** -0.5),
        "w_mla_o": nrm(21, (L, H_A * V_DIM, D_MODEL), (H_A * V_DIM) ** -0.5),
        "mlstm_head_norm": gain(22, (L, H_M, DH_M)),
        "w_mlstm_o": nrm(23, (L, MLSTM_W, D_MODEL), MLSTM_W ** -0.5),
        "w_out": nrm(24, (L, D_MODEL, D_MODEL), D_MODEL ** -0.5),
        "w_mlp1": nrm(25, (L, D_MODEL, D_FF), D_MODEL ** -0.5),
        "w_mlp2": nrm(26, (L, D_FF, D_MODEL), D_FF ** -0.5),
    }


def reference(x_prompt, x_sample, cache_mla_ckv, cache_mla_krope, state_mlstm_C, state_mlstm_n, state_mlstm_m,
              c, c_ctx, w_ada, b_ada, norm_pre1, norm_post1, norm_pre2, norm_post2, w_in, mlstm_gate_b,
              mla_q_norm, mla_kv_norm, w_uq, w_ukv, w_mla_o, mlstm_head_norm, w_mlstm_o, w_out, w_mlp1, w_mlp2):
    f32 = jnp.float32
    Bp = x_prompt.shape[0]
    rope = axial_rope(x_sample.shape[1])
    zero_state = (jnp.zeros((Bp, 2, H_M, DH_M, DH_M), f32), jnp.zeros((Bp, 2, H_M, DH_M), f32),
                  jnp.full((Bp, 2, H_M), M_INIT, f32))
    xp, xs = x_prompt, x_sample
    new_ckv, new_krope, new_C, new_n, new_m = [], [], [], [], []
    for l in range(DEPTH):
        mix_w = (w_in[l], mlstm_gate_b[l], mla_q_norm[l], mla_kv_norm[l], w_uq[l], w_ukv[l], w_mla_o[l],
                 mlstm_head_norm[l], w_mlstm_o[l], w_out[l])
        norms = (norm_pre1[l], norm_post1[l], norm_pre2[l], norm_post2[l])
        mod_ctx = (jax.nn.silu(c_ctx) @ w_ada[l] + b_ada[l])[None, None, :]
        mod_lat = (jax.nn.silu(c) @ w_ada[l] + b_ada[l])[:, None, :]
        xp, ckv, krope, C, n, m = trunk_layer(xp, mod_ctx, None, None, None, zero_state, *norms,
                                              w_mlp1[l], w_mlp2[l], mix_w)
        new_ckv.append(ckv)
        new_krope.append(krope)
        new_C.append(C.astype(x_prompt.dtype))
        new_n.append(n.astype(x_prompt.dtype))
        new_m.append(m.astype(x_prompt.dtype))
        state_l = (state_mlstm_C[:, l], state_mlstm_n[:, l], state_mlstm_m[:, l])
        xs = trunk_layer(xs, mod_lat, rope, cache_mla_ckv[:, l], cache_mla_krope[:, l], state_l, *norms,
                         w_mlp1[l], w_mlp2[l], mix_w)[0]
    return (xp, xs, jnp.stack(new_ckv, axis=1), jnp.stack(new_krope, axis=1), jnp.stack(new_C, axis=1),
            jnp.stack(new_n, axis=1), jnp.stack(new_m, axis=1))
```

```python
import functools

import numpy as np
import jax
import jax.numpy as jnp
from jax import lax
from jax.experimental import pallas as pl
from jax.experimental.pallas import tpu as pltpu

F32 = jnp.float32
BF16 = jnp.bfloat16

D_MODEL = 1024
H_M = 4
DH_M = 128
MLSTM_W = H_M * DH_M
CHUNK = 128
H_A = 8
NOPE = 64
ROPE_DIM = 32
V_DIM = 64
Q_LORA = 384
KV_LORA = 256
AX_DIM = ROPE_DIM // 2
ROPE_BASE = 10000.0
GRID_W = 64
D_FF = 4 * D_MODEL
N_MOD = 6
EPS = 1e-6
N_GATES = 4 * H_M

LANE = 128
HEAD_PAD = LANE
MISC_W = LANE
GATE_LANE0 = ROPE_DIM
OFF_MQ, OFF_MK, OFF_MV, OFF_MO = 0, MLSTM_W, 2 * MLSTM_W, 3 * MLSTM_W
OFF_GA = 4 * MLSTM_W
OFF_GB = OFF_GA + D_MODEL
OFF_CQ = OFF_GB + D_MODEL
OFF_CKV = OFF_CQ + Q_LORA
OFF_MISC = OFF_CKV + KV_LORA
NA_COLS = OFF_MISC + MISC_W

VMEM_LIMIT = 56 * 1024 * 1024


def _cparams(sem):
    return pltpu.CompilerParams(dimension_semantics=sem, vmem_limit_bytes=VMEM_LIMIT)


def _const_spec(shape):
    nd = len(shape)
    return pl.BlockSpec(shape, lambda *_: (0,) * nd, pipeline_mode=pl.Buffered(1))


def _rms(x, w):
    return x * lax.rsqrt(jnp.mean(x * x, axis=-1, keepdims=True) + EPS) * w


def _sigmoid(x):
    return 1.0 / (1.0 + jnp.exp(-x))


def _log_sigmoid(x):
    return jnp.minimum(x, 0.0) - jnp.log(1.0 + jnp.exp(-jnp.abs(x)))


def _bdot(a, b):
    return jnp.dot(a, b, preferred_element_type=F32)


def _mod_kernel(c_ref, w_ref, b_ref, o_ref):
    c = c_ref[...]
    s = c * _sigmoid(c)
    o_ref[...] = jnp.dot(s, w_ref[...], preferred_element_type=F32,
                         precision=lax.Precision.HIGHEST) + b_ref[...]


def _modulation(cc, w_ada, b_ada):
    n_out = w_ada.shape[1]
    tn = 1536
    return pl.pallas_call(
        _mod_kernel,
        grid=(n_out // tn,),
        in_specs=[pl.BlockSpec((8, D_MODEL), lambda j: (0, 0)),
                  pl.BlockSpec((D_MODEL, tn), lambda j: (0, j)),
                  pl.BlockSpec((1, tn), lambda j: (0, j))],
        out_specs=pl.BlockSpec((8, tn), lambda j: (0, j)),
        out_shape=jax.ShapeDtypeStruct((8, n_out), F32),
        compiler_params=_cparams(("arbitrary",)),
        name="modulation",
    )(cc, w_ada, b_ada)


def _stage_a_kernel(x_ref, mod_ref, pre1_ref, wa_ref, gbias_ref, qn_ref, kvn_ref, wq_ref, wk_ref, wv_ref,
                    ta_ref, tb_ref,
                    mq_ref, mk_ref, mv_ref, og_ref, ga_ref, gb_ref, gates_ref, q_ref, k_ref, v_ref,
                    ckv_ref, kro_ref):
    x = x_ref[...]
    mod = mod_ref[0]
    shift1, scale1 = mod[0:1], mod[1:2]
    h = _rms(x, pre1_ref[...]) * (1.0 + scale1) + shift1
    hb = h.astype(BF16)

    def proj(lo, hi):
        return _bdot(hb, wa_ref[:, lo:hi])

    mq_ref[...] = (proj(OFF_MQ, OFF_MK) * DH_M ** -0.5).astype(BF16)
    mk_ref[...] = proj(OFF_MK, OFF_MV).astype(BF16)
    mv_ref[...] = proj(OFF_MV, OFF_MO).astype(BF16)
    og_ref[...] = _sigmoid(proj(OFF_MO, OFF_GA)).astype(BF16)
    ga_ref[...] = _sigmoid(proj(OFF_GA, OFF_GB)).astype(BF16)
    gb_ref[...] = _sigmoid(proj(OFF_GB, OFF_CQ)).astype(BF16)
    cq = proj(OFF_CQ, OFF_CKV)
    ckv = proj(OFF_CKV, OFF_MISC)
    misc = proj(OFF_MISC, NA_COLS)

    lane = lax.broadcasted_iota(jnp.int32, misc.shape, 1)
    g = misc + gbias_ref[...]
    is_f = ((lane - GATE_LANE0) // H_M) % 2 == 1
    gates_ref[...] = jnp.where(is_f, _log_sigmoid(g), g)
    kro_ref[...] = misc[:, 0:ROPE_DIM]

    ckv_n = _rms(ckv, kvn_ref[...])
    ckv_ref[...] = ckv_n
    ckv_b = ckv_n.astype(BF16)

    ta = ta_ref[...]
    tb = tb_ref[...]
    qf = _bdot(_rms(cq, qn_ref[...]).astype(BF16), wq_ref[...])
    ta8 = jnp.concatenate([ta] * H_A, axis=1)
    tb8 = jnp.concatenate([tb] * H_A, axis=1)
    q = qf * ta8 + pltpu.roll(qf, H_A * HEAD_PAD - ROPE_DIM, 1) * tb8
    q_ref[...] = (q * (NOPE + ROPE_DIM) ** -0.5).astype(BF16)

    ta_k = jnp.where(lane < NOPE, 0.0, ta)
    kr = misc * ta_k + pltpu.roll(misc, MISC_W - ROPE_DIM, 1) * tb
    kk = _bdot(ckv_b, wk_ref[...]) + jnp.concatenate([kr] * H_A, axis=1)
    k_ref[...] = kk.astype(BF16)
    v_ref[...] = _bdot(ckv_b, wv_ref[...]).astype(BF16)


def _stage_a(x, mods, mod_index, pre1, wa, gbias, qn, kvn, wq, wk, wv, ta, tb, table_index, tm):
    n = x.shape[0]
    tile = lambda w: pl.BlockSpec((tm, w), lambda i: (i, 0))
    bf = lambda w: jax.ShapeDtypeStruct((n, w), BF16)
    f32 = lambda w: jax.ShapeDtypeStruct((n, w), F32)
    return pl.pallas_call(
        _stage_a_kernel,
        grid=(n // tm,),
        in_specs=[tile(D_MODEL),
                  pl.BlockSpec((1, N_MOD, D_MODEL), lambda i: (mod_index(i), 0, 0)),
                  _const_spec((1, D_MODEL)),
                  _const_spec((D_MODEL, NA_COLS)),
                  _const_spec((1, MISC_W)),
                  _const_spec((1, Q_LORA)),
                  _const_spec((1, KV_LORA)),
                  _const_spec((Q_LORA, H_A * HEAD_PAD)),
                  _const_spec((KV_LORA, H_A * HEAD_PAD)),
                  _const_spec((KV_LORA, H_A * V_DIM)),
                  pl.BlockSpec((tm, LANE), lambda i: (table_index(i), 0)),
                  pl.BlockSpec((tm, LANE), lambda i: (table_index(i), 0))],
        out_specs=[tile(MLSTM_W), tile(MLSTM_W), tile(MLSTM_W), tile(MLSTM_W),
                   tile(D_MODEL), tile(D_MODEL), tile(MISC_W),
                   tile(H_A * HEAD_PAD), tile(H_A * HEAD_PAD), tile(H_A * V_DIM),
                   tile(KV_LORA), tile(ROPE_DIM)],
        out_shape=[bf(MLSTM_W), bf(MLSTM_W), bf(MLSTM_W), bf(MLSTM_W),
                   bf(D_MODEL), bf(D_MODEL), f32(MISC_W),
                   bf(H_A * HEAD_PAD), bf(H_A * HEAD_PAD), bf(H_A * V_DIM),
                   f32(KV_LORA), f32(ROPE_DIM)],
        compiler_params=_cparams(("arbitrary",)),
        name="stage_a",
    )(x, mods, pre1, wa, gbias, qn, kvn, wq, wk, wv, ta, tb)


def _kv_cache_kernel(ckv_ref, kr_ref, wk_ref, wv_ref, k_ref, v_ref):
    ckv_b = ckv_ref[...].astype(BF16)
    kk = _bdot(ckv_b, wk_ref[...]) + jnp.concatenate([kr_ref[...]] * H_A, axis=1)
    k_ref[...] = kk.astype(BF16)
    v_ref[...] = _bdot(ckv_b, wv_ref[...]).astype(BF16)


def _kv_cache(ckv, kr, wk, wv, tm):
    n = ckv.shape[0]
    return pl.pallas_call(
        _kv_cache_kernel,
        grid=(n // tm,),
        in_specs=[pl.BlockSpec((tm, KV_LORA), lambda i: (i, 0)),
                  pl.BlockSpec((tm, LANE), lambda i: (i, 0)),
                  _const_spec((KV_LORA, H_A * HEAD_PAD)),
                  _const_spec((KV_LORA, H_A * V_DIM))],
        out_specs=[pl.BlockSpec((tm, H_A * HEAD_PAD), lambda i: (i, 0)),
                   pl.BlockSpec((tm, H_A * V_DIM), lambda i: (i, 0))],
        out_shape=[jax.ShapeDtypeStruct((n, H_A * HEAD_PAD), BF16),
                   jax.ShapeDtypeStruct((n, H_A * V_DIM), BF16)],
        compiler_params=_cparams(("arbitrary",)),
        name="kv_cache",
    )(ckv, kr, wk, wv)


def _mlstm_kernel(qf_ref, kf_ref, vf_ref, gf_ref, qb_ref, kb_ref, vb_ref, gb_ref, c0_ref, n0_ref, m0_ref,
                  hf_ref, hb_ref, cn_ref, nn_ref, mn_ref, c_s, n_s, m_s):
    step = pl.program_id(1)

    @pl.when(step == 0)
    def _():
        c_s[...] = c0_ref[0]
        n_s[...] = n0_ref[0]
        m_s[...] = m0_ref[0]

    row = lax.broadcasted_iota(jnp.int32, (CHUNK, CHUNK), 0)
    col = lax.broadcasted_iota(jnp.int32, (CHUNK, CHUNK), 1)
    lower = (col <= row).astype(F32)
    upper = (col >= row).astype(F32)
    hi = lax.Precision.HIGHEST
    dirs = ((qf_ref, kf_ref, vf_ref, gf_ref, hf_ref), (qb_ref, kb_ref, vb_ref, gb_ref, hb_ref))
    for d, (q_ref, k_ref, v_ref, g_ref, h_ref) in enumerate(dirs):
        g = g_ref[...]
        g_t = g.T
        if d == 0:
            cum_c = jnp.dot(lower, g, preferred_element_type=F32, precision=hi)
            cum_r = jnp.dot(g_t, upper, preferred_element_type=F32, precision=hi)
            mask = col <= row
        else:
            cum_c = jnp.dot(upper, g, preferred_element_type=F32, precision=hi)
            cum_r = jnp.dot(g_t, lower, preferred_element_type=F32, precision=hi)
            mask = col >= row
        for hd in range(H_M):
            j = d * H_M + hd
            li = GATE_LANE0 + d * 2 * H_M + hd
            lf = li + H_M
            b_c = cum_c[:, lf:lf + 1]
            b_r = cum_r[lf:lf + 1, :]
            i_c = g[:, li:li + 1]
            i_r = g_t[li:li + 1, :]
            m = m_s[j:j + 1, 0:1]
            sl = slice(hd * DH_M, (hd + 1) * DH_M)
            q = q_ref[:, sl]
            k = k_ref[:, sl]
            v = v_ref[:, sl]
            c_st = c_s[j]
            n_st = n_s[j:j + 1, :]

            dmat = jnp.where(mask, b_c - b_r + i_r, -jnp.inf)
            inter = b_c + m
            m_t = jnp.maximum(inter, jnp.max(dmat, axis=1, keepdims=True))
            w = jnp.exp(dmat - m_t)
            s_inter = jnp.exp(inter - m_t)
            qk = lax.dot_general(q, k, (((1,), (1,)), ((), ())), preferred_element_type=F32) * w
            num = s_inter * _bdot(q, c_st.astype(BF16)) + _bdot(qk.astype(BF16), v)
            den = (s_inter * jnp.sum(q.astype(F32) * n_st, axis=1, keepdims=True)
                   + jnp.sum(qk, axis=1, keepdims=True))
            h_ref[:, sl] = num / jnp.maximum(jnp.abs(den), jnp.exp(-m_t))

            b_l = b_c[CHUNK - 1:CHUNK, :] if d == 0 else b_c[0:1, :]
            g_col = b_l - b_c + i_c
            m_new = jnp.maximum(b_l + m, jnp.max(g_col, axis=0, keepdims=True))
            decay = jnp.exp(b_l + m - m_new)
            wk = jnp.exp(g_col - m_new) * k.astype(F32)
            c_s[j] = decay * c_st + lax.dot_general(wk.astype(BF16), v, (((0,), (0,)), ((), ())),
                                                    preferred_element_type=F32)
            n_s[j:j + 1, :] = decay * n_st + jnp.sum(wk, axis=0, keepdims=True)
            m_s[j:j + 1, :] = jnp.broadcast_to(m_new, (1, LANE))

    @pl.when(step == pl.num_programs(1) - 1)
    def _():
        cn_ref[0] = c_s[...]
        nn_ref[0] = n_s[...]
        mn_ref[0] = m_s[...]


def _mlstm(mq, mk, mv, gates, c0, n0, m0, batch, seq):
    nc = seq // CHUNK
    n = batch * seq
    nj = 2 * H_M
    fwd = lambda w: pl.BlockSpec((CHUNK, w), lambda b, c: (b * nc + c, 0))
    bwd = lambda w: pl.BlockSpec((CHUNK, w), lambda b, c: (b * nc + nc - 1 - c, 0))
    st3 = pl.BlockSpec((1, nj, LANE), lambda b, c: (b, 0, 0))
    st4 = pl.BlockSpec((1, nj, DH_M, DH_M), lambda b, c: (b, 0, 0, 0))
    return pl.pallas_call(
        _mlstm_kernel,
        grid=(batch, nc),
        in_specs=[fwd(MLSTM_W), fwd(MLSTM_W), fwd(MLSTM_W), fwd(MISC_W),
                  bwd(MLSTM_W), bwd(MLSTM_W), bwd(MLSTM_W), bwd(MISC_W),
                  st4, st3, st3],
        out_specs=[fwd(MLSTM_W), bwd(MLSTM_W), st4, st3, st3],
        out_shape=[jax.ShapeDtypeStruct((n, MLSTM_W), F32), jax.ShapeDtypeStruct((n, MLSTM_W), F32),
                   jax.ShapeDtypeStruct((batch, nj, DH_M, DH_M), F32),
                   jax.ShapeDtypeStruct((batch, nj, LANE), F32),
                   jax.ShapeDtypeStruct((batch, nj, LANE), F32)],
        scratch_shapes=[pltpu.VMEM((nj, DH_M, DH_M), F32), pltpu.VMEM((nj, LANE), F32),
                        pltpu.VMEM((nj, LANE), F32)],
        compiler_params=_cparams(("arbitrary", "arbitrary")),
        name="mlstm",
    )(mq, mk, mv, gates, mq, mk, mv, gates, c0, n0, m0)


def _attn_kernel(*refs, n_main, tk, has_cache):
    if has_cache:
        q_ref, k_ref, v_ref, kc_ref, vc_ref, o_ref = refs
    else:
        q_ref, k_ref, v_ref, o_ref = refs
    tq = q_ref.shape[0]
    lane = lax.broadcasted_iota(jnp.int32, (tq, LANE), 1)

    def update(q, ks, vs, carry):
        m, l, acc = carry
        s = lax.dot_general(q, ks, (((1,), (1,)), ((), ())), preferred_element_type=F32)
        m_new = jnp.maximum(m, jnp.max(s, axis=1, keepdims=True))
        alpha = jnp.exp(m - m_new)
        p = jnp.exp(s - m_new)
        l = alpha * l + jnp.sum(p, axis=1, keepdims=True)
        acc = alpha * acc + _bdot(p.astype(BF16), vs)
        return m_new, l, acc

    for hp in range(H_A // 2):
        vsl = slice(hp * LANE, (hp + 1) * LANE)
        outs = []
        for hh in range(2):
            head = 2 * hp + hh
            hsl = slice(head * HEAD_PAD, (head + 1) * HEAD_PAD)
            q = q_ref[:, hsl]

            def body(t, carry):
                rows = pl.ds(pl.multiple_of(t * tk, tk), tk)
                return update(q, k_ref[rows, hsl], v_ref[rows, vsl], carry)

            carry = (jnp.full((tq, 1), -jnp.inf, F32), jnp.zeros((tq, 1), F32), jnp.zeros((tq, LANE), F32))
            carry = lax.fori_loop(0, n_main, body, carry)
            if has_cache:
                carry = update(q, kc_ref[:, hsl], vc_ref[:, vsl], carry)
            _, l, acc = carry
            outs.append(acc / l)
        o_ref[:, vsl] = jnp.where(lane < V_DIM, outs[0], outs[1]).astype(BF16)


def _attention(q, k, v, kc, vc, batch, seq, tq, tk):
    nq = seq // tq
    has_cache = kc is not None
    in_specs = [pl.BlockSpec((tq, H_A * HEAD_PAD), lambda b, i: (b * nq + i, 0)),
                pl.BlockSpec((seq, H_A * HEAD_PAD), lambda b, i: (b, 0)),
                pl.BlockSpec((seq, H_A * V_DIM), lambda b, i: (b, 0))]
    args = [q, k, v]
    if has_cache:
        past = kc.shape[0] // batch
        in_specs += [pl.BlockSpec((past, H_A * HEAD_PAD), lambda b, i: (b, 0)),
                     pl.BlockSpec((past, H_A * V_DIM), lambda b, i: (b, 0))]
        args += [kc, vc]
    return pl.pallas_call(
        functools.partial(_attn_kernel, n_main=seq // tk, tk=tk, has_cache=has_cache),
        grid=(batch, nq),
        in_specs=in_specs,
        out_specs=pl.BlockSpec((tq, H_A * V_DIM), lambda b, i: (b * nq + i, 0)),
        out_shape=jax.ShapeDtypeStruct((batch * seq, H_A * V_DIM), BF16),
        compiler_params=_cparams(("arbitrary", "arbitrary")),
        name="mla_attention",
    )(*args)


def _stage_c_kernel(x_ref, mod_ref, hf_ref, hb_ref, og_ref, attn_ref, ga_ref, gb_ref, hn_ref,
                    post1_ref, pre2_ref, post2_ref, wmo_ref, wao_ref, wout_ref, w1_ref, w2_ref, y_ref):
    mod = mod_ref[0]
    gate1, shift2, scale2, gate2 = mod[2:3], mod[3:4], mod[4:5], mod[5:6]
    hm = hf_ref[...] + hb_ref[...]
    hn = hn_ref[...]
    heads = []
    for hd in range(H_M):
        sl = slice(hd * DH_M, (hd + 1) * DH_M)
        heads.append(_rms(hm[:, sl], hn[:, sl]))
    hm = jnp.concatenate(heads, axis=1) * og_ref[...].astype(F32)
    y_m = _bdot(hm.astype(BF16), wmo_ref[...])
    y_a = _bdot(attn_ref[...], wao_ref[...])
    merged = ga_ref[...].astype(F32) * y_m + gb_ref[...].astype(F32) * y_a
    mix = _bdot(merged.astype(BF16), wout_ref[...])
    x1 = x_ref[...] + gate1 * _rms(mix, post1_ref[...])
    h2 = (_rms(x1, pre2_ref[...]) * (1.0 + scale2) + shift2).astype(BF16)
    ff = jnp.zeros(x1.shape, F32)
    for cidx in range(D_FF // D_MODEL):
        sl = slice(cidx * D_MODEL, (cidx + 1) * D_MODEL)
        a = jnp.maximum(_bdot(h2, w1_ref[:, sl]), 0.0)
        ff = ff + _bdot((a * a).astype(BF16), w2_ref[sl, :])
    y_ref[...] = x1 + gate2 * _rms(ff, post2_ref[...])


def _stage_c(x, mods, mod_index, hf, hb, og, attn, ga, gb, hn, post1, pre2, post2, wmo, wao, wout, w1, w2, tm):
    n = x.shape[0]
    tile = lambda w: pl.BlockSpec((tm, w), lambda i: (i, 0))
    return pl.pallas_call(
        _stage_c_kernel,
        grid=(n // tm,),
        in_specs=[tile(D_MODEL),
                  pl.BlockSpec((1, N_MOD, D_MODEL), lambda i: (mod_index(i), 0, 0)),
                  tile(MLSTM_W), tile(MLSTM_W), tile(MLSTM_W), tile(H_A * V_DIM),
                  tile(D_MODEL), tile(D_MODEL),
                  _const_spec((1, MLSTM_W)), _const_spec((1, D_MODEL)), _const_spec((1, D_MODEL)),
                  _const_spec((1, D_MODEL)),
                  _const_spec((MLSTM_W, D_MODEL)), _const_spec((H_A * V_DIM, D_MODEL)),
                  _const_spec((D_MODEL, D_MODEL)), _const_spec((D_MODEL, D_FF)), _const_spec((D_FF, D_MODEL))],
        out_specs=tile(D_MODEL),
        out_shape=jax.ShapeDtypeStruct((n, D_MODEL), F32),
        compiler_params=_cparams(("arbitrary",)),
        name="stage_c",
    )(x, mods, hf, hb, og, attn, ga, gb, hn, post1, pre2, post2, wmo, wao, wout, w1, w2)


def _rope_tables(n_tokens):
    pos = np.arange(n_tokens)
    row = (pos // GRID_W).astype(np.float32)
    col = (pos % GRID_W).astype(np.float32)
    inv = (ROPE_BASE ** (-np.arange(0, AX_DIM, 2, dtype=np.float32) / AX_DIM)).astype(np.float32)
    ang = jnp.concatenate([jnp.asarray(row)[:, None] * inv, jnp.asarray(col)[:, None] * inv], axis=-1)
    cos, sin = jnp.cos(ang), jnp.sin(ang)
    ones = jnp.ones((n_tokens, NOPE), F32)
    zeros = jnp.zeros((n_tokens, NOPE), F32)
    pad = jnp.zeros((n_tokens, HEAD_PAD - NOPE - ROPE_DIM), F32)
    ta = jnp.concatenate([ones, cos, cos, pad], axis=-1)
    tb = jnp.concatenate([zeros, -sin, sin, pad], axis=-1)
    return ta, tb


def _plain_tables(n_tokens):
    ones = jnp.ones((n_tokens, NOPE + ROPE_DIM), F32)
    pad = jnp.zeros((n_tokens, HEAD_PAD - NOPE - ROPE_DIM), F32)
    return jnp.concatenate([ones, pad], axis=-1), jnp.zeros((n_tokens, HEAD_PAD), F32)


def kernel(x_prompt, x_sample, cache_mla_ckv, cache_mla_krope, state_mlstm_C, state_mlstm_n, state_mlstm_m,
           c, c_ctx, w_ada, b_ada, norm_pre1, norm_post1, norm_pre2, norm_post2, w_in, mlstm_gate_b,
           mla_q_norm, mla_kv_norm, w_uq, w_ukv, w_mla_o, mlstm_head_norm, w_mlstm_o, w_out, w_mlp1, w_mlp2):
    bp, sp, _ = x_prompt.shape
    bs, ss, _ = x_sample.shape
    depth = w_in.shape[0]
    past = cache_mla_ckv.shape[2]
    nj = 2 * H_M
    even = np.arange(0, ROPE_DIM, 2)
    odd = np.arange(1, ROPE_DIM, 2)
    perm = np.concatenate([even, odd])
    perm_sw = np.concatenate([odd, even])

    xp = x_prompt.reshape(bp * sp, D_MODEL)
    xs = x_sample.reshape(bs * ss, D_MODEL)
    cc = jnp.zeros((8, D_MODEL), F32).at[:bs].set(c).at[bs].set(c_ctx)
    ta_lat, tb_lat = _rope_tables(ss)
    tm_ctx, tm_lat = 512, 512
    ta_ctx, tb_ctx = _plain_tables(tm_ctx)

    new_ckv, new_krope, new_c, new_n, new_m = [], [], [], [], []
    for l in range(depth):
        cols = np.cumsum((MLSTM_W,) * 4 + (N_GATES, Q_LORA, KV_LORA, ROPE_DIM))
        w_mq, w_mk, w_mv, w_mo, w_g, w_cq, w_ckv, w_kr, w_merge = jnp.split(w_in[l], cols.tolist(), axis=1)
        w_misc = jnp.concatenate([w_kr, w_g, jnp.zeros((D_MODEL, MISC_W - 3 * ROPE_DIM - N_GATES), F32),
                                  w_kr[:, perm], w_kr[:, perm_sw]], axis=1)
        wa = jnp.concatenate([w_mq, w_mk, w_mv, w_mo, w_merge, w_cq, w_ckv, w_misc], axis=1).astype(BF16)
        gbias = jnp.zeros((1, MISC_W), F32).at[0, GATE_LANE0:GATE_LANE0 + N_GATES].set(
            mlstm_gate_b[l].reshape(N_GATES))
        uq = w_uq[l]
        wq = jnp.concatenate([uq[..., :NOPE], uq[..., NOPE:][..., perm], uq[..., NOPE:][..., perm_sw]],
                             axis=-1).reshape(Q_LORA, H_A * HEAD_PAD).astype(BF16)
        ukv = w_ukv[l]
        wk = jnp.concatenate([ukv[..., :NOPE], jnp.zeros((KV_LORA, H_A, HEAD_PAD - NOPE), F32)],
                             axis=-1).reshape(KV_LORA, H_A * HEAD_PAD).astype(BF16)
        wv = ukv[..., NOPE:].reshape(KV_LORA, H_A * V_DIM).astype(BF16)
        wmo = w_mlstm_o[l].astype(BF16)
        wao = w_mla_o[l].astype(BF16)
        wout = w_out[l].astype(BF16)
        w1 = w_mlp1[l].astype(BF16)
        w2 = w_mlp2[l].astype(BF16)
        pre1, post1 = norm_pre1[l][None], norm_post1[l][None]
        pre2, post2 = norm_pre2[l][None], norm_post2[l][None]
        qn, kvn = mla_q_norm[l][None], mla_kv_norm[l][None]
        hn = mlstm_head_norm[l].reshape(1, MLSTM_W)

        mods = _modulation(cc, w_ada[l], b_ada[l][None]).reshape(8, N_MOD, D_MODEL)

        ctx_mod = lambda i: bs
        a = _stage_a(xp, mods, ctx_mod, pre1, wa, gbias, qn, kvn, wq, wk, wv, ta_ctx, tb_ctx,
                     lambda i: 0, tm_ctx)
        mq, mk, mv, og, ga, gb, gates, q, k, v, ckv_n, kro = a
        c0 = jnp.zeros((bp, nj, DH_M, DH_M), F32)
        n0 = jnp.zeros((bp, nj, LANE), F32)
        m0 = jnp.full((bp, nj, LANE), -1e30, F32)
        hf, hb, c_fin, n_fin, m_fin = _mlstm(mq, mk, mv, gates, c0, n0, m0, bp, sp)
        attn = _attention(q, k, v, None, None, bp, sp, sp, sp)
        xp = _stage_c(xp, mods, ctx_mod, hf, hb, og, attn, ga, gb, hn, post1, pre2, post2,
                      wmo, wao, wout, w1, w2, tm_ctx)
        new_ckv.append(ckv_n.reshape(bp, sp, KV_LORA))
        new_krope.append(kro.reshape(bp, sp, ROPE_DIM))
        new_c.append(c_fin.reshape(bp, 2, H_M, DH_M, DH_M))
        new_n.append(n_fin.reshape(bp, 2, H_M, DH_M))
        new_m.append(m_fin[:, :, 0].reshape(bp, 2, H_M))

        tiles_per_seq = ss // tm_lat
        lat_mod = lambda i: i // tiles_per_seq
        a = _stage_a(xs, mods, lat_mod, pre1, wa, gbias, qn, kvn, wq, wk, wv, ta_lat, tb_lat,
                     lambda i: i % tiles_per_seq, tm_lat)
        mq, mk, mv, og, ga, gb, gates, q, k, v, _, _ = a
        kr_cache = jnp.zeros((bs * past, LANE), F32).at[:, NOPE:NOPE + ROPE_DIM].set(
            cache_mla_krope[:, l].reshape(bs * past, ROPE_DIM)[:, perm])
        kc, vc = _kv_cache(cache_mla_ckv[:, l].reshape(bs * past, KV_LORA), kr_cache, wk, wv, past)
        c0 = state_mlstm_C[:, l].reshape(bs, nj, DH_M, DH_M)
        n0 = state_mlstm_n[:, l].reshape(bs, nj, DH_M)
        m0 = jnp.broadcast_to(state_mlstm_m[:, l].reshape(bs, nj, 1), (bs, nj, LANE))
        hf, hb, _, _, _ = _mlstm(mq, mk, mv, gates, c0, n0, m0, bs, ss)
        attn = _attention(q, k, v, kc, vc, bs, ss, 256, 512)
        xs = _stage_c(xs, mods, lat_mod, hf, hb, og, attn, ga, gb, hn, post1, pre2, post2,
                      wmo, wao, wout, w1, w2, tm_lat)

    return (xp.reshape(bp, sp, D_MODEL), xs.reshape(bs, ss, D_MODEL),
            jnp.stack(new_ckv, axis=1), jnp.stack(new_krope, axis=1), jnp.stack(new_c, axis=1),
            jnp.stack(new_n, axis=1), jnp.stack(new_m, axis=1))
```

```python
import functools

import numpy as np
import jax
import jax.numpy as jnp
from jax import lax
from jax.experimental import pallas as pl
from jax.experimental.pallas import tpu as pltpu

F32 = jnp.float32
BF16 = jnp.bfloat16

D_MODEL = 1024
H_M = 4
DH_M = 128
MLSTM_W = H_M * DH_M
CHUNK = 128
H_A = 8
NOPE = 64
ROPE_DIM = 32
V_DIM = 64
Q_LORA = 384
KV_LORA = 256
AX_DIM = ROPE_DIM // 2
ROPE_BASE = 10000.0
GRID_W = 64
D_FF = 4 * D_MODEL
N_MOD = 6
EPS = 1e-6
N_GATES = 4 * H_M

LANE = 128
HEAD_PAD = LANE
MISC_W = LANE
GATE_LANE0 = ROPE_DIM
OFF_MQ, OFF_MK, OFF_MV, OFF_MO = 0, MLSTM_W, 2 * MLSTM_W, 3 * MLSTM_W
OFF_GA = 4 * MLSTM_W
OFF_GB = OFF_GA + D_MODEL
OFF_CQ = OFF_GB + D_MODEL
OFF_CKV = OFF_CQ + Q_LORA
OFF_MISC = OFF_CKV + KV_LORA
NA_COLS = OFF_MISC + MISC_W

VMEM_LIMIT = 56 * 1024 * 1024
QK_SCALE = float((NOPE + ROPE_DIM) ** -0.5 * np.log2(np.e))
QK_LOOKAHEAD = 3
ONES_ROWS = 16


def _cparams(sem):
    return pltpu.CompilerParams(dimension_semantics=sem, vmem_limit_bytes=VMEM_LIMIT)


def _const_spec(shape):
    nd = len(shape)
    return pl.BlockSpec(shape, lambda *_: (0,) * nd, pipeline_mode=pl.Buffered(1))


def _rms(x, w):
    return x * lax.rsqrt(jnp.mean(x * x, axis=-1, keepdims=True) + EPS) * w


def _sigmoid(x):
    return 1.0 / (1.0 + jnp.exp(-x))


def _log_sigmoid(x):
    return jnp.minimum(x, 0.0) - jnp.log(1.0 + jnp.exp(-jnp.abs(x)))


def _bdot(a, b):
    return jnp.dot(a, b, preferred_element_type=F32)


def _mod_kernel(c_ref, w_ref, b_ref, o_ref):
    c = c_ref[...]
    s = c * _sigmoid(c)
    o_ref[...] = jnp.dot(s, w_ref[...], preferred_element_type=F32,
                         precision=lax.Precision.HIGHEST) + b_ref[...]


def _modulation(cc, w_ada, b_ada):
    n_out = w_ada.shape[1]
    tn = 1536
    return pl.pallas_call(
        _mod_kernel,
        grid=(n_out // tn,),
        in_specs=[pl.BlockSpec((8, D_MODEL), lambda j: (0, 0)),
                  pl.BlockSpec((D_MODEL, tn), lambda j: (0, j)),
                  pl.BlockSpec((1, tn), lambda j: (0, j))],
        out_specs=pl.BlockSpec((8, tn), lambda j: (0, j)),
        out_shape=jax.ShapeDtypeStruct((8, n_out), F32),
        compiler_params=_cparams(("arbitrary",)),
        name="modulation",
    )(cc, w_ada, b_ada)


def _stage_a_kernel(x_ref, mod_ref, pre1_ref, wa_ref, gbias_ref, qn_ref, kvn_ref, wq_ref, wk_ref, wv_ref,
                    ta_ref, tb_ref,
                    mq_ref, mk_ref, mv_ref, og_ref, ga_ref, gb_ref, gates_ref, q_ref, k_ref, v_ref,
                    ckv_ref, kro_ref):
    x = x_ref[...]
    mod = mod_ref[0]
    shift1, scale1 = mod[0:1], mod[1:2]
    h = _rms(x, pre1_ref[...]) * (1.0 + scale1) + shift1
    hb = h.astype(BF16)

    def proj(lo, hi):
        return _bdot(hb, wa_ref[:, lo:hi])

    mq_ref[...] = (proj(OFF_MQ, OFF_MK) * DH_M ** -0.5).astype(BF16)
    mk_ref[...] = proj(OFF_MK, OFF_MV).astype(BF16)
    mv_ref[...] = proj(OFF_MV, OFF_MO).astype(BF16)
    og_ref[...] = _sigmoid(proj(OFF_MO, OFF_GA)).astype(BF16)
    ga_ref[...] = _sigmoid(proj(OFF_GA, OFF_GB)).astype(BF16)
    gb_ref[...] = _sigmoid(proj(OFF_GB, OFF_CQ)).astype(BF16)
    cq = proj(OFF_CQ, OFF_CKV)
    ckv = proj(OFF_CKV, OFF_MISC)
    misc = proj(OFF_MISC, NA_COLS)

    lane = lax.broadcasted_iota(jnp.int32, misc.shape, 1)
    g = misc + gbias_ref[...]
    is_f = ((lane - GATE_LANE0) // H_M) % 2 == 1
    gates_ref[...] = jnp.where(is_f, _log_sigmoid(g), g)
    kro_ref[...] = misc[:, 0:ROPE_DIM]

    ckv_n = _rms(ckv, kvn_ref[...])
    ckv_ref[...] = ckv_n
    ckv_b = ckv_n.astype(BF16)

    ta = ta_ref[...]
    tb = tb_ref[...]
    qf = _bdot(_rms(cq, qn_ref[...]).astype(BF16), wq_ref[...])
    ta8 = jnp.concatenate([ta] * H_A, axis=1)
    tb8 = jnp.concatenate([tb] * H_A, axis=1)
    q = qf * ta8 + pltpu.roll(qf, H_A * HEAD_PAD - ROPE_DIM, 1) * tb8
    q_ref[...] = (q * QK_SCALE).astype(BF16)

    ta_k = jnp.where(lane < NOPE, 0.0, ta)
    kr = misc * ta_k + pltpu.roll(misc, MISC_W - ROPE_DIM, 1) * tb
    kk = _bdot(ckv_b, wk_ref[...]) + jnp.concatenate([kr] * H_A, axis=1)
    k_ref[...] = kk.astype(BF16)
    vt = _bdot(wv_ref[...], ckv_n.T.astype(BF16)).astype(BF16)
    tkv = v_ref.shape[2]
    for j in range(v_ref.shape[0]):
        v_ref[j] = vt[:, j * tkv:(j + 1) * tkv]


def _stage_a(x, mods, mod_index, pre1, wa, gbias, qn, kvn, wq, wk, wv, ta, tb, table_index, tm, tkv):
    n = x.shape[0]
    tile = lambda w: pl.BlockSpec((tm, w), lambda i: (i, 0))
    bf = lambda w: jax.ShapeDtypeStruct((n, w), BF16)
    f32 = lambda w: jax.ShapeDtypeStruct((n, w), F32)
    return pl.pallas_call(
        _stage_a_kernel,
        grid=(n // tm,),
        in_specs=[tile(D_MODEL),
                  pl.BlockSpec((1, N_MOD, D_MODEL), lambda i: (mod_index(i), 0, 0)),
                  _const_spec((1, D_MODEL)),
                  _const_spec((D_MODEL, NA_COLS)),
                  _const_spec((1, MISC_W)),
                  _const_spec((1, Q_LORA)),
                  _const_spec((1, KV_LORA)),
                  _const_spec((Q_LORA, H_A * HEAD_PAD)),
                  _const_spec((KV_LORA, H_A * HEAD_PAD)),
                  _const_spec((H_A * V_DIM, KV_LORA)),
                  pl.BlockSpec((tm, LANE), lambda i: (table_index(i), 0)),
                  pl.BlockSpec((tm, LANE), lambda i: (table_index(i), 0))],
        out_specs=[tile(MLSTM_W), tile(MLSTM_W), tile(MLSTM_W), tile(MLSTM_W),
                   tile(D_MODEL), tile(D_MODEL), tile(MISC_W),
                   tile(H_A * HEAD_PAD), tile(H_A * HEAD_PAD),
                   pl.BlockSpec((tm // tkv, H_A * V_DIM, tkv), lambda i: (i, 0, 0)),
                   tile(KV_LORA), tile(ROPE_DIM)],
        out_shape=[bf(MLSTM_W), bf(MLSTM_W), bf(MLSTM_W), bf(MLSTM_W),
                   bf(D_MODEL), bf(D_MODEL), f32(MISC_W),
                   bf(H_A * HEAD_PAD), bf(H_A * HEAD_PAD),
                   jax.ShapeDtypeStruct((n // tkv, H_A * V_DIM, tkv), BF16),
                   f32(KV_LORA), f32(ROPE_DIM)],
        compiler_params=_cparams(("arbitrary",)),
        name="stage_a",
    )(x, mods, pre1, wa, gbias, qn, kvn, wq, wk, wv, ta, tb)


def _kv_cache_kernel(ckv_ref, kr_ref, wk_ref, wv_ref, k_ref, v_ref):
    ckv_b = ckv_ref[...].astype(BF16)
    kk = _bdot(ckv_b, wk_ref[...]) + jnp.concatenate([kr_ref[...]] * H_A, axis=1)
    k_ref[...] = kk.astype(BF16)
    v_ref[0] = _bdot(wv_ref[...], ckv_ref[...].T.astype(BF16)).astype(BF16)


def _kv_cache(ckv, kr, wk, wv, tm):
    n = ckv.shape[0]
    return pl.pallas_call(
        _kv_cache_kernel,
        grid=(n // tm,),
        in_specs=[pl.BlockSpec((tm, KV_LORA), lambda i: (i, 0)),
                  pl.BlockSpec((tm, LANE), lambda i: (i, 0)),
                  _const_spec((KV_LORA, H_A * HEAD_PAD)),
                  _const_spec((H_A * V_DIM, KV_LORA))],
        out_specs=[pl.BlockSpec((tm, H_A * HEAD_PAD), lambda i: (i, 0)),
                   pl.BlockSpec((1, H_A * V_DIM, tm), lambda i: (i, 0, 0))],
        out_shape=[jax.ShapeDtypeStruct((n, H_A * HEAD_PAD), BF16),
                   jax.ShapeDtypeStruct((n // tm, H_A * V_DIM, tm), BF16)],
        compiler_params=_cparams(("arbitrary",)),
        name="kv_cache",
    )(ckv, kr, wk, wv)


def _mlstm_kernel(qf_ref, kf_ref, vf_ref, gf_ref, qb_ref, kb_ref, vb_ref, gb_ref, c0_ref, n0_ref, m0_ref,
                  hf_ref, hb_ref, cn_ref, nn_ref, mn_ref, c_s, n_s, m_s):
    step = pl.program_id(1)

    @pl.when(step == 0)
    def _():
        c_s[...] = c0_ref[0]
        n_s[...] = n0_ref[0]
        m_s[...] = m0_ref[0]

    row = lax.broadcasted_iota(jnp.int32, (CHUNK, CHUNK), 0)
    col = lax.broadcasted_iota(jnp.int32, (CHUNK, CHUNK), 1)
    lower = (col <= row).astype(F32)
    upper = (col >= row).astype(F32)
    hi = lax.Precision.HIGHEST
    dirs = ((qf_ref, kf_ref, vf_ref, gf_ref, hf_ref), (qb_ref, kb_ref, vb_ref, gb_ref, hb_ref))
    for d, (q_ref, k_ref, v_ref, g_ref, h_ref) in enumerate(dirs):
        g = g_ref[...]
        g_t = g.T
        if d == 0:
            cum_c = jnp.dot(lower, g, preferred_element_type=F32, precision=hi)
            cum_r = jnp.dot(g_t, upper, preferred_element_type=F32, precision=hi)
            mask = col <= row
        else:
            cum_c = jnp.dot(upper, g, preferred_element_type=F32, precision=hi)
            cum_r = jnp.dot(g_t, lower, preferred_element_type=F32, precision=hi)
            mask = col >= row
        for hd in range(H_M):
            j = d * H_M + hd
            li = GATE_LANE0 + d * 2 * H_M + hd
            lf = li + H_M
            b_c = cum_c[:, lf:lf + 1]
            b_r = cum_r[lf:lf + 1, :]
            i_c = g[:, li:li + 1]
            i_r = g_t[li:li + 1, :]
            m = m_s[j:j + 1, 0:1]
            sl = slice(hd * DH_M, (hd + 1) * DH_M)
            q = q_ref[:, sl]
            k = k_ref[:, sl]
            v = v_ref[:, sl]
            c_st = c_s[j]
            n_st = n_s[j:j + 1, :]

            dmat = jnp.where(mask, b_c - b_r + i_r, -jnp.inf)
            inter = b_c + m
            m_t = jnp.maximum(inter, jnp.max(dmat, axis=1, keepdims=True))
            w = jnp.exp(dmat - m_t)
            s_inter = jnp.exp(inter - m_t)
            qk = lax.dot_general(q, k, (((1,), (1,)), ((), ())), preferred_element_type=F32) * w
            num = s_inter * _bdot(q, c_st.astype(BF16)) + _bdot(qk.astype(BF16), v)
            den = (s_inter * jnp.sum(q.astype(F32) * n_st, axis=1, keepdims=True)
                   + jnp.sum(qk, axis=1, keepdims=True))
            h_ref[:, sl] = num / jnp.maximum(jnp.abs(den), jnp.exp(-m_t))

            b_l = b_c[CHUNK - 1:CHUNK, :] if d == 0 else b_c[0:1, :]
            g_col = b_l - b_c + i_c
            m_new = jnp.maximum(b_l + m, jnp.max(g_col, axis=0, keepdims=True))
            decay = jnp.exp(b_l + m - m_new)
            wk = jnp.exp(g_col - m_new) * k.astype(F32)
            c_s[j] = decay * c_st + lax.dot_general(wk.astype(BF16), v, (((0,), (0,)), ((), ())),
                                                    preferred_element_type=F32)
            n_s[j:j + 1, :] = decay * n_st + jnp.sum(wk, axis=0, keepdims=True)
            m_s[j:j + 1, :] = jnp.broadcast_to(m_new, (1, LANE))

    @pl.when(step == pl.num_programs(1) - 1)
    def _():
        cn_ref[0] = c_s[...]
        nn_ref[0] = n_s[...]
        mn_ref[0] = m_s[...]


def _mlstm(mq, mk, mv, gates, c0, n0, m0, batch, seq):
    nc = seq // CHUNK
    n = batch * seq
    nj = 2 * H_M
    fwd = lambda w: pl.BlockSpec((CHUNK, w), lambda b, c: (b * nc + c, 0))
    bwd = lambda w: pl.BlockSpec((CHUNK, w), lambda b, c: (b * nc + nc - 1 - c, 0))
    st3 = pl.BlockSpec((1, nj, LANE), lambda b, c: (b, 0, 0))
    st4 = pl.BlockSpec((1, nj, DH_M, DH_M), lambda b, c: (b, 0, 0, 0))
    return pl.pallas_call(
        _mlstm_kernel,
        grid=(batch, nc),
        in_specs=[fwd(MLSTM_W), fwd(MLSTM_W), fwd(MLSTM_W), fwd(MISC_W),
                  bwd(MLSTM_W), bwd(MLSTM_W), bwd(MLSTM_W), bwd(MISC_W),
                  st4, st3, st3],
        out_specs=[fwd(MLSTM_W), bwd(MLSTM_W), st4, st3, st3],
        out_shape=[jax.ShapeDtypeStruct((n, MLSTM_W), F32), jax.ShapeDtypeStruct((n, MLSTM_W), F32),
                   jax.ShapeDtypeStruct((batch, nj, DH_M, DH_M), F32),
                   jax.ShapeDtypeStruct((batch, nj, LANE), F32),
                   jax.ShapeDtypeStruct((batch, nj, LANE), F32)],
        scratch_shapes=[pltpu.VMEM((nj, DH_M, DH_M), F32), pltpu.VMEM((nj, LANE), F32),
                        pltpu.VMEM((nj, LANE), F32)],
        compiler_params=_cparams(("arbitrary", "arbitrary")),
        name="mlstm",
    )(mq, mk, mv, gates, mq, mk, mv, gates, c0, n0, m0)


def _attn_kernel(*refs, n_main, tk, has_cache):
    if has_cache:
        q_ref, k_ref, vt_ref, kc_ref, vct_ref, o_ref, qt_s, acc_s, m_s, ot_s = refs
    else:
        q_ref, k_ref, vt_ref, o_ref, qt_s, acc_s, m_s, ot_s = refs
    qt_s[...] = q_ref[...].astype(F32).T.astype(BF16)
    acc_s[...] = jnp.zeros(acc_s.shape, F32)
    m_s[...] = jnp.full(m_s.shape, -jnp.inf, F32)

    def update(get_k, get_vt, width):
        ones = jnp.ones((ONES_ROWS, width), BF16)

        def scores(head):
            return _bdot(get_k(head), qt_s[head * HEAD_PAD:(head + 1) * HEAD_PAD, :])

        pending = [scores(h) for h in range(QK_LOOKAHEAD)]
        for head in range(H_A):
            s = pending.pop(0)
            if head + QK_LOOKAHEAD < H_A:
                pending.append(scores(head + QK_LOOKAHEAD))
            m_old = m_s[head:head + 1, :]
            m_new = jnp.maximum(m_old, jnp.max(s, axis=0, keepdims=True))
            alpha = jnp.exp2(m_old - m_new)
            p = jnp.exp2((s - m_new).astype(BF16))
            lhs = jnp.concatenate([get_vt(head), ones], axis=0)
            acc_s[head] = alpha * acc_s[head] + _bdot(lhs, p)
            m_s[head:head + 1, :] = m_new

    def body(t, carry):
        rows = pl.ds(pl.multiple_of(t * tk, tk), tk)
        update(lambda h: k_ref[rows, h * HEAD_PAD:(h + 1) * HEAD_PAD],
               lambda h: vt_ref[t, h * V_DIM:(h + 1) * V_DIM, :], tk)
        return carry

    lax.fori_loop(0, n_main, body, 0)
    if has_cache:
        update(lambda h: kc_ref[:, h * HEAD_PAD:(h + 1) * HEAD_PAD],
               lambda h: vct_ref[0, h * V_DIM:(h + 1) * V_DIM, :], kc_ref.shape[0])
    for head in range(H_A):
        acc = acc_s[head]
        ot_s[head * V_DIM:(head + 1) * V_DIM, :] = acc[0:V_DIM] * (1.0 / acc[V_DIM:V_DIM + 1])
    o_ref[...] = ot_s[...].T.astype(BF16)


def _attention(q, k, vt, kc, vct, batch, seq, tq, tk):
    nq = seq // tq
    has_cache = kc is not None
    in_specs = [pl.BlockSpec((tq, H_A * HEAD_PAD), lambda b, i: (b * nq + i, 0)),
                pl.BlockSpec((seq, H_A * HEAD_PAD), lambda b, i: (b, 0)),
                pl.BlockSpec((seq // tk, H_A * V_DIM, tk), lambda b, i: (b, 0, 0))]
    args = [q, k, vt]
    if has_cache:
        past = kc.shape[0] // batch
        in_specs += [pl.BlockSpec((past, H_A * HEAD_PAD), lambda b, i: (b, 0)),
                     pl.BlockSpec((1, H_A * V_DIM, past), lambda b, i: (b, 0, 0))]
        args += [kc, vct]
    return pl.pallas_call(
        functools.partial(_attn_kernel, n_main=seq // tk, tk=tk, has_cache=has_cache),
        grid=(batch, nq),
        in_specs=in_specs,
        out_specs=pl.BlockSpec((tq, H_A * V_DIM), lambda b, i: (b * nq + i, 0)),
        out_shape=jax.ShapeDtypeStruct((batch * seq, H_A * V_DIM), BF16),
        scratch_shapes=[pltpu.VMEM((H_A * HEAD_PAD, tq), BF16),
                        pltpu.VMEM((H_A, V_DIM + ONES_ROWS, tq), F32),
                        pltpu.VMEM((H_A, tq), F32),
                        pltpu.VMEM((H_A * V_DIM, tq), F32)],
        compiler_params=_cparams(("arbitrary", "arbitrary")),
        name="mla_attention",
    )(*args)


def _stage_c_kernel(x_ref, mod_ref, hf_ref, hb_ref, og_ref, attn_ref, ga_ref, gb_ref, hn_ref,
                    post1_ref, pre2_ref, post2_ref, wmo_ref, wao_ref, wout_ref, w1_ref, w2_ref, y_ref):
    mod = mod_ref[0]
    gate1, shift2, scale2, gate2 = mod[2:3], mod[3:4], mod[4:5], mod[5:6]
    hm = hf_ref[...] + hb_ref[...]
    hn = hn_ref[...]
    heads = []
    for hd in range(H_M):
        sl = slice(hd * DH_M, (hd + 1) * DH_M)
        heads.append(_rms(hm[:, sl], hn[:, sl]))
    hm = jnp.concatenate(heads, axis=1) * og_ref[...].astype(F32)
    y_m = _bdot(hm.astype(BF16), wmo_ref[...])
    y_a = _bdot(attn_ref[...], wao_ref[...])
    merged = ga_ref[...].astype(F32) * y_m + gb_ref[...].astype(F32) * y_a
    mix = _bdot(merged.astype(BF16), wout_ref[...])
    x1 = x_ref[...] + gate1 * _rms(mix, post1_ref[...])
    h2 = (_rms(x1, pre2_ref[...]) * (1.0 + scale2) + shift2).astype(BF16)
    ff = jnp.zeros(x1.shape, F32)
    for cidx in range(D_FF // D_MODEL):
        sl = slice(cidx * D_MODEL, (cidx + 1) * D_MODEL)
        a = jnp.maximum(_bdot(h2, w1_ref[:, sl]), 0.0)
        ff = ff + _bdot((a * a).astype(BF16), w2_ref[sl, :])
    y_ref[...] = x1 + gate2 * _rms(ff, post2_ref[...])


def _stage_c(x, mods, mod_index, hf, hb, og, attn, ga, gb, hn, post1, pre2, post2, wmo, wao, wout, w1, w2, tm):
    n = x.shape[0]
    tile = lambda w: pl.BlockSpec((tm, w), lambda i: (i, 0))
    return pl.pallas_call(
        _stage_c_kernel,
        grid=(n // tm,),
        in_specs=[tile(D_MODEL),
                  pl.BlockSpec((1, N_MOD, D_MODEL), lambda i: (mod_index(i), 0, 0)),
                  tile(MLSTM_W), tile(MLSTM_W), tile(MLSTM_W), tile(H_A * V_DIM),
                  tile(D_MODEL), tile(D_MODEL),
                  _const_spec((1, MLSTM_W)), _const_spec((1, D_MODEL)), _const_spec((1, D_MODEL)),
                  _const_spec((1, D_MODEL)),
                  _const_spec((MLSTM_W, D_MODEL)), _const_spec((H_A * V_DIM, D_MODEL)),
                  _const_spec((D_MODEL, D_MODEL)), _const_spec((D_MODEL, D_FF)), _const_spec((D_FF, D_MODEL))],
        out_specs=tile(D_MODEL),
        out_shape=jax.ShapeDtypeStruct((n, D_MODEL), F32),
        compiler_params=_cparams(("arbitrary",)),
        name="stage_c",
    )(x, mods, hf, hb, og, attn, ga, gb, hn, post1, pre2, post2, wmo, wao, wout, w1, w2)


def _rope_tables(n_tokens):
    pos = np.arange(n_tokens)
    row = (pos // GRID_W).astype(np.float32)
    col = (pos % GRID_W).astype(np.float32)
    inv = (ROPE_BASE ** (-np.arange(0, AX_DIM, 2, dtype=np.float32) / AX_DIM)).astype(np.float32)
    ang = jnp.concatenate([jnp.asarray(row)[:, None] * inv, jnp.asarray(col)[:, None] * inv], axis=-1)
    cos, sin = jnp.cos(ang), jnp.sin(ang)
    ones = jnp.ones((n_tokens, NOPE), F32)
    zeros = jnp.zeros((n_tokens, NOPE), F32)
    pad = jnp.zeros((n_tokens, HEAD_PAD - NOPE - ROPE_DIM), F32)
    ta = jnp.concatenate([ones, cos, cos, pad], axis=-1)
    tb = jnp.concatenate([zeros, -sin, sin, pad], axis=-1)
    return ta, tb


def _plain_tables(n_tokens):
    ones = jnp.ones((n_tokens, NOPE + ROPE_DIM), F32)
    pad = jnp.zeros((n_tokens, HEAD_PAD - NOPE - ROPE_DIM), F32)
    return jnp.concatenate([ones, pad], axis=-1), jnp.zeros((n_tokens, HEAD_PAD), F32)


def kernel(x_prompt, x_sample, cache_mla_ckv, cache_mla_krope, state_mlstm_C, state_mlstm_n, state_mlstm_m,
           c, c_ctx, w_ada, b_ada, norm_pre1, norm_post1, norm_pre2, norm_post2, w_in, mlstm_gate_b,
           mla_q_norm, mla_kv_norm, w_uq, w_ukv, w_mla_o, mlstm_head_norm, w_mlstm_o, w_out, w_mlp1, w_mlp2):
    bp, sp, _ = x_prompt.shape
    bs, ss, _ = x_sample.shape
    depth = w_in.shape[0]
    past = cache_mla_ckv.shape[2]
    nj = 2 * H_M
    even = np.arange(0, ROPE_DIM, 2)
    odd = np.arange(1, ROPE_DIM, 2)
    perm = np.concatenate([even, odd])
    perm_sw = np.concatenate([odd, even])

    xp = x_prompt.reshape(bp * sp, D_MODEL)
    xs = x_sample.reshape(bs * ss, D_MODEL)
    cc = jnp.zeros((8, D_MODEL), F32).at[:bs].set(c).at[bs].set(c_ctx)
    ta_lat, tb_lat = _rope_tables(ss)
    tm_ctx, tm_lat = 512, 512
    tk_lat = 512
    ta_ctx, tb_ctx = _plain_tables(tm_ctx)

    new_ckv, new_krope, new_c, new_n, new_m = [], [], [], [], []
    for l in range(depth):
        cols = np.cumsum((MLSTM_W,) * 4 + (N_GATES, Q_LORA, KV_LORA, ROPE_DIM))
        w_mq, w_mk, w_mv, w_mo, w_g, w_cq, w_ckv, w_kr, w_merge = jnp.split(w_in[l], cols.tolist(), axis=1)
        w_misc = jnp.concatenate([w_kr, w_g, jnp.zeros((D_MODEL, MISC_W - 3 * ROPE_DIM - N_GATES), F32),
                                  w_kr[:, perm], w_kr[:, perm_sw]], axis=1)
        wa = jnp.concatenate([w_mq, w_mk, w_mv, w_mo, w_merge, w_cq, w_ckv, w_misc], axis=1).astype(BF16)
        gbias = jnp.zeros((1, MISC_W), F32).at[0, GATE_LANE0:GATE_LANE0 + N_GATES].set(
            mlstm_gate_b[l].reshape(N_GATES))
        uq = w_uq[l]
        wq = jnp.concatenate([uq[..., :NOPE], uq[..., NOPE:][..., perm], uq[..., NOPE:][..., perm_sw]],
                             axis=-1).reshape(Q_LORA, H_A * HEAD_PAD).astype(BF16)
        ukv = w_ukv[l]
        wk = jnp.concatenate([ukv[..., :NOPE], jnp.zeros((KV_LORA, H_A, HEAD_PAD - NOPE), F32)],
                             axis=-1).reshape(KV_LORA, H_A * HEAD_PAD).astype(BF16)
        wv = ukv[..., NOPE:].reshape(KV_LORA, H_A * V_DIM).T.astype(BF16)
        wmo = w_mlstm_o[l].astype(BF16)
        wao = w_mla_o[l].astype(BF16)
        wout = w_out[l].astype(BF16)
        w1 = w_mlp1[l].astype(BF16)
        w2 = w_mlp2[l].astype(BF16)
        pre1, post1 = norm_pre1[l][None], norm_post1[l][None]
        pre2, post2 = norm_pre2[l][None], norm_post2[l][None]
        qn, kvn = mla_q_norm[l][None], mla_kv_norm[l][None]
        hn = mlstm_head_norm[l].reshape(1, MLSTM_W)

        mods = _modulation(cc, w_ada[l], b_ada[l][None]).reshape(8, N_MOD, D_MODEL)

        ctx_mod = lambda i: bs
        a = _stage_a(xp, mods, ctx_mod, pre1, wa, gbias, qn, kvn, wq, wk, wv, ta_ctx, tb_ctx,
                     lambda i: 0, tm_ctx, sp)
        mq, mk, mv, og, ga, gb, gates, q, k, v, ckv_n, kro = a
        c0 = jnp.zeros((bp, nj, DH_M, DH_M), F32)
        n0 = jnp.zeros((bp, nj, LANE), F32)
        m0 = jnp.full((bp, nj, LANE), -1e30, F32)
        hf, hb, c_fin, n_fin, m_fin = _mlstm(mq, mk, mv, gates, c0, n0, m0, bp, sp)
        attn = _attention(q, k, v, None, None, bp, sp, sp, sp)
        xp = _stage_c(xp, mods, ctx_mod, hf, hb, og, attn, ga, gb, hn, post1, pre2, post2,
                      wmo, wao, wout, w1, w2, tm_ctx)
        new_ckv.append(ckv_n.reshape(bp, sp, KV_LORA))
        new_krope.append(kro.reshape(bp, sp, ROPE_DIM))
        new_c.append(c_fin.reshape(bp, 2, H_M, DH_M, DH_M))
        new_n.append(n_fin.reshape(bp, 2, H_M, DH_M))
        new_m.append(m_fin[:, :, 0].reshape(bp, 2, H_M))

        tiles_per_seq = ss // tm_lat
        lat_mod = lambda i: i // tiles_per_seq
        a = _stage_a(xs, mods, lat_mod, pre1, wa, gbias, qn, kvn, wq, wk, wv, ta_lat, tb_lat,
                     lambda i: i % tiles_per_seq, tm_lat, tk_lat)
        mq, mk, mv, og, ga, gb, gates, q, k, v, _, _ = a
        kr_cache = jnp.zeros((bs * past, LANE), F32).at[:, NOPE:NOPE + ROPE_DIM].set(
            cache_mla_krope[:, l].reshape(bs * past, ROPE_DIM)[:, perm])
        kc, vc = _kv_cache(cache_mla_ckv[:, l].reshape(bs * past, KV_LORA), kr_cache, wk, wv, past)
        c0 = state_mlstm_C[:, l].reshape(bs, nj, DH_M, DH_M)
        n0 = state_mlstm_n[:, l].reshape(bs, nj, DH_M)
        m0 = jnp.broadcast_to(state_mlstm_m[:, l].reshape(bs, nj, 1), (bs, nj, LANE))
        hf, hb, _, _, _ = _mlstm(mq, mk, mv, gates, c0, n0, m0, bs, ss)
        attn = _attention(q, k, v, kc, vc, bs, ss, 256, tk_lat)
        xs = _stage_c(xs, mods, lat_mod, hf, hb, og, attn, ga, gb, hn, post1, pre2, post2,
                      wmo, wao, wout, w1, w2, tm_lat)

    return (xp.reshape(bp, sp, D_MODEL), xs.reshape(bs, ss, D_MODEL),
            jnp.stack(new_ckv, axis=1), jnp.stack(new_krope, axis=1), jnp.stack(new_c, axis=1),
            jnp.stack(new_n, axis=1), jnp.stack(new_m, axis=1))
```

```python
import functools

import numpy as np
import jax
import jax.numpy as jnp
from jax import lax
from jax.experimental import pallas as pl
from jax.experimental.pallas import tpu as pltpu

F32 = jnp.float32
BF16 = jnp.bfloat16

D_MODEL = 1024
H_M = 4
DH_M = 128
MLSTM_W = H_M * DH_M
CHUNK = 128
H_A = 8
NOPE = 64
ROPE_DIM = 32
V_DIM = 64
Q_LORA = 384
KV_LORA = 256
AX_DIM = ROPE_DIM // 2
ROPE_BASE = 10000.0
GRID_W = 64
D_FF = 4 * D_MODEL
N_MOD = 6
EPS = 1e-6
N_GATES = 4 * H_M

LANE = 128
HEAD_PAD = LANE
MISC_W = LANE
GATE_LANE0 = ROPE_DIM
OFF_MQ, OFF_MK, OFF_MV, OFF_MO = 0, MLSTM_W, 2 * MLSTM_W, 3 * MLSTM_W
OFF_GA = 4 * MLSTM_W
OFF_GB = OFF_GA + D_MODEL
OFF_CQ = OFF_GB + D_MODEL
OFF_CKV = OFF_CQ + Q_LORA
OFF_MISC = OFF_CKV + KV_LORA
NA_COLS = OFF_MISC + MISC_W

VMEM_LIMIT = 56 * 1024 * 1024
QK_SCALE = float((NOPE + ROPE_DIM) ** -0.5 * np.log2(np.e))
QK_LOOKAHEAD = 3
ONES_ROWS = 16


def _cparams(sem):
    return pltpu.CompilerParams(dimension_semantics=sem, vmem_limit_bytes=VMEM_LIMIT)


def _const_spec(shape):
    nd = len(shape)
    return pl.BlockSpec(shape, lambda *_: (0,) * nd, pipeline_mode=pl.Buffered(1))


def _rms(x, w):
    return x * lax.rsqrt(jnp.mean(x * x, axis=-1, keepdims=True) + EPS) * w


def _sigmoid(x):
    return 1.0 / (1.0 + jnp.exp(-x))


def _log_sigmoid(x):
    return jnp.minimum(x, 0.0) - jnp.log(1.0 + jnp.exp(-jnp.abs(x)))


def _bdot(a, b):
    return jnp.dot(a, b, preferred_element_type=F32)


def _mod_kernel(c_ref, w_ref, b_ref, o_ref):
    c = c_ref[...]
    s = c * _sigmoid(c)
    o_ref[...] = jnp.dot(s, w_ref[...], preferred_element_type=F32,
                         precision=lax.Precision.HIGHEST) + b_ref[...]


def _modulation(cc, w_ada, b_ada):
    n_out = w_ada.shape[1]
    tn = 1536
    return pl.pallas_call(
        _mod_kernel,
        grid=(n_out // tn,),
        in_specs=[pl.BlockSpec((8, D_MODEL), lambda j: (0, 0)),
                  pl.BlockSpec((D_MODEL, tn), lambda j: (0, j)),
                  pl.BlockSpec((1, tn), lambda j: (0, j))],
        out_specs=pl.BlockSpec((8, tn), lambda j: (0, j)),
        out_shape=jax.ShapeDtypeStruct((8, n_out), F32),
        compiler_params=_cparams(("arbitrary",)),
        name="modulation",
    )(cc, w_ada, b_ada)


def _stage_a_kernel(x_ref, mod_ref, pre1_ref, wa_ref, gbias_ref, qn_ref, kvn_ref, wq_ref, wk_ref, wv_ref,
                    ta_ref, tb_ref,
                    mq_ref, mk_ref, mv_ref, og_ref, ga_ref, gb_ref, gates_ref, q_ref, k_ref, v_ref,
                    ckv_ref, kro_ref):
    x = x_ref[...]
    mod = mod_ref[0]
    shift1, scale1 = mod[0:1], mod[1:2]
    h = _rms(x, pre1_ref[...]) * (1.0 + scale1) + shift1
    hb = h.astype(BF16)

    def proj(lo, hi):
        return _bdot(hb, wa_ref[:, lo:hi])

    mq_ref[...] = (proj(OFF_MQ, OFF_MK) * DH_M ** -0.5).astype(BF16)
    mk_ref[...] = proj(OFF_MK, OFF_MV).T.astype(BF16)
    mv_ref[...] = proj(OFF_MV, OFF_MO).astype(BF16)
    og_ref[...] = _sigmoid(proj(OFF_MO, OFF_GA)).astype(BF16)
    ga_ref[...] = _sigmoid(proj(OFF_GA, OFF_GB)).astype(BF16)
    gb_ref[...] = _sigmoid(proj(OFF_GB, OFF_CQ)).astype(BF16)
    cq = proj(OFF_CQ, OFF_CKV)
    ckv = proj(OFF_CKV, OFF_MISC)
    misc = proj(OFF_MISC, NA_COLS)

    lane = lax.broadcasted_iota(jnp.int32, misc.shape, 1)
    g = misc + gbias_ref[...]
    is_f = (lane >= GATE_LANE0 + 2 * H_M) & (lane < GATE_LANE0 + N_GATES)
    gates_ref[...] = jnp.where(is_f, _log_sigmoid(g), g)
    kro_ref[...] = misc[:, 0:ROPE_DIM]

    ckv_n = _rms(ckv, kvn_ref[...])
    ckv_ref[...] = ckv_n
    ckv_b = ckv_n.astype(BF16)

    ta = ta_ref[...]
    tb = tb_ref[...]
    qf = _bdot(_rms(cq, qn_ref[...]).astype(BF16), wq_ref[...])
    ta8 = jnp.concatenate([ta] * H_A, axis=1)
    tb8 = jnp.concatenate([tb] * H_A, axis=1)
    q = qf * ta8 + pltpu.roll(qf, H_A * HEAD_PAD - ROPE_DIM, 1) * tb8
    q_ref[...] = (q * QK_SCALE).astype(BF16)

    ta_k = jnp.where(lane < NOPE, 0.0, ta)
    kr = misc * ta_k + pltpu.roll(misc, MISC_W - ROPE_DIM, 1) * tb
    kk = _bdot(ckv_b, wk_ref[...]) + jnp.concatenate([kr] * H_A, axis=1)
    k_ref[...] = kk.astype(BF16)
    vt = _bdot(wv_ref[...], ckv_n.T.astype(BF16)).astype(BF16)
    tkv = v_ref.shape[2]
    for j in range(v_ref.shape[0]):
        v_ref[j] = vt[:, j * tkv:(j + 1) * tkv]


def _stage_a(x, mods, mod_index, pre1, wa, gbias, qn, kvn, wq, wk, wv, ta, tb, table_index, tm, tkv):
    n = x.shape[0]
    tile = lambda w: pl.BlockSpec((tm, w), lambda i: (i, 0))
    bf = lambda w: jax.ShapeDtypeStruct((n, w), BF16)
    f32 = lambda w: jax.ShapeDtypeStruct((n, w), F32)
    return pl.pallas_call(
        _stage_a_kernel,
        grid=(n // tm,),
        in_specs=[tile(D_MODEL),
                  pl.BlockSpec((1, N_MOD, D_MODEL), lambda i: (mod_index(i), 0, 0)),
                  _const_spec((1, D_MODEL)),
                  _const_spec((D_MODEL, NA_COLS)),
                  _const_spec((1, MISC_W)),
                  _const_spec((1, Q_LORA)),
                  _const_spec((1, KV_LORA)),
                  _const_spec((Q_LORA, H_A * HEAD_PAD)),
                  _const_spec((KV_LORA, H_A * HEAD_PAD)),
                  _const_spec((H_A * V_DIM, KV_LORA)),
                  pl.BlockSpec((tm, LANE), lambda i: (table_index(i), 0)),
                  pl.BlockSpec((tm, LANE), lambda i: (table_index(i), 0))],
        out_specs=[tile(MLSTM_W), pl.BlockSpec((MLSTM_W, tm), lambda i: (0, i)), tile(MLSTM_W), tile(MLSTM_W),
                   tile(D_MODEL), tile(D_MODEL), tile(MISC_W),
                   tile(H_A * HEAD_PAD), tile(H_A * HEAD_PAD),
                   pl.BlockSpec((tm // tkv, H_A * V_DIM, tkv), lambda i: (i, 0, 0)),
                   tile(KV_LORA), tile(ROPE_DIM)],
        out_shape=[bf(MLSTM_W), jax.ShapeDtypeStruct((MLSTM_W, n), BF16), bf(MLSTM_W), bf(MLSTM_W),
                   bf(D_MODEL), bf(D_MODEL), f32(MISC_W),
                   bf(H_A * HEAD_PAD), bf(H_A * HEAD_PAD),
                   jax.ShapeDtypeStruct((n // tkv, H_A * V_DIM, tkv), BF16),
                   f32(KV_LORA), f32(ROPE_DIM)],
        compiler_params=_cparams(("arbitrary",)),
        name="stage_a",
    )(x, mods, pre1, wa, gbias, qn, kvn, wq, wk, wv, ta, tb)


def _kv_cache_kernel(ckv_ref, kr_ref, wk_ref, wv_ref, k_ref, v_ref):
    ckv_b = ckv_ref[...].astype(BF16)
    kk = _bdot(ckv_b, wk_ref[...]) + jnp.concatenate([kr_ref[...]] * H_A, axis=1)
    k_ref[...] = kk.astype(BF16)
    v_ref[0] = _bdot(wv_ref[...], ckv_ref[...].T.astype(BF16)).astype(BF16)


def _kv_cache(ckv, kr, wk, wv, tm):
    n = ckv.shape[0]
    return pl.pallas_call(
        _kv_cache_kernel,
        grid=(n // tm,),
        in_specs=[pl.BlockSpec((tm, KV_LORA), lambda i: (i, 0)),
                  pl.BlockSpec((tm, LANE), lambda i: (i, 0)),
                  _const_spec((KV_LORA, H_A * HEAD_PAD)),
                  _const_spec((H_A * V_DIM, KV_LORA))],
        out_specs=[pl.BlockSpec((tm, H_A * HEAD_PAD), lambda i: (i, 0)),
                   pl.BlockSpec((1, H_A * V_DIM, tm), lambda i: (i, 0, 0))],
        out_shape=[jax.ShapeDtypeStruct((n, H_A * HEAD_PAD), BF16),
                   jax.ShapeDtypeStruct((n // tm, H_A * V_DIM, tm), BF16)],
        compiler_params=_cparams(("arbitrary",)),
        name="kv_cache",
    )(ckv, kr, wk, wv)


def _split3(x):
    x1 = x.astype(BF16).astype(F32)
    r = x - x1
    x2 = r.astype(BF16).astype(F32)
    x3 = (r - x2).astype(BF16).astype(F32)
    return x1, x2, x3


def _mlstm_kernel(qf_ref, ktf_ref, vf_ref, gf_ref, qb_ref, ktb_ref, vb_ref, gb_ref, c0_ref, m0_ref,
                  hf_ref, hb_ref, cn_ref, mn_ref, c_s, m_s):
    step = pl.program_id(1)

    @pl.when(step == 0)
    def _():
        c_s[...] = c0_ref[0]
        m_s[...] = m0_ref[0]

    nj = 2 * H_M
    row = lax.broadcasted_iota(jnp.int32, (CHUNK, CHUNK), 0)
    col = lax.broadcasted_iota(jnp.int32, (CHUNK, CHUNK), 1)
    lower = (col <= row).astype(F32)
    upper = (col >= row).astype(F32)
    hi = lax.Precision.HIGHEST
    row8 = lax.broadcasted_iota(jnp.int32, (nj, CHUNK), 0)
    lane8 = lax.broadcasted_iota(jnp.int32, (nj, CHUNK), 1)
    is_fwd = row8 < H_M

    gt_f = gf_ref[...].T
    gt_b = gb_ref[...].T
    gi, gf = GATE_LANE0, GATE_LANE0 + nj
    i8 = jnp.where(is_fwd, gt_f[gi:gi + nj], gt_b[gi:gi + nj])
    f8 = jnp.where(is_fwd, gt_f[gf:gf + nj], gt_b[gf:gf + nj])
    b8 = jnp.where(is_fwd, jnp.dot(f8, upper, preferred_element_type=F32, precision=hi),
                   jnp.dot(f8, lower, preferred_element_type=F32, precision=hi))
    a8 = i8 - b8
    cm8 = a8
    shift = 1
    while shift < CHUNK:
        y_f = jnp.where(lane8 >= shift, pltpu.roll(cm8, shift, 1), -jnp.inf)
        y_b = jnp.where(lane8 < CHUNK - shift, pltpu.roll(cm8, CHUNK - shift, 1), -jnp.inf)
        cm8 = jnp.maximum(cm8, jnp.where(is_fwd, y_f, y_b))
        shift *= 2
    m8 = m_s[...]
    mx8 = jnp.maximum(m8, jnp.max(a8, axis=1, keepdims=True))
    e8 = jnp.exp(a8 - mx8)
    decay8 = jnp.exp(m8 - mx8)
    u8 = jnp.maximum(m8, cm8)
    ma8 = m8 - a8
    m_s[...] = jnp.sum(f8, axis=1, keepdims=True) + mx8

    ones8 = jnp.ones((nj, CHUNK), F32)
    zeros8 = jnp.zeros((nj, CHUNK), F32)
    p_rows = [ones8] * 3 + list(_split3(u8)) + list(_split3(b8))
    p_t = jnp.concatenate(p_rows + [zeros8] * (CHUNK // nj - len(p_rows)), axis=0).T
    q_s = list(_split3(a8)) + [-ones8] * 3
    q_s = jnp.concatenate(q_s + [zeros8] * (CHUNK // nj - len(q_s)), axis=0)
    q_m = jnp.concatenate([zeros8] * 3 + [-ones8] * 6 + [zeros8] * (CHUNK // nj - 9), axis=0)
    q_full = jnp.concatenate([q_s, q_m], axis=1).astype(BF16)
    ones_v = jnp.ones((CHUNK, DH_M), BF16)

    dirs = ((qf_ref, ktf_ref, vf_ref, hf_ref), (qb_ref, ktb_ref, vb_ref, hb_ref))
    chains = [(d, hd) for d in range(2) for hd in range(H_M)]
    first = []
    for d, hd in chains:
        q_ref, kt_ref, v_ref, _ = dirs[d]
        j = d * H_M + hd
        sl = slice(hd * DH_M, (hd + 1) * DH_M)
        q = q_ref[:, sl]
        k_t = kt_ref[sl, :]
        v_aug = jnp.concatenate([v_ref[:, sl], ones_v], axis=1)
        dm = _bdot(jnp.where(col % nj == j, p_t, 0.0).astype(BF16), q_full)
        qk = _bdot(q, k_t)
        c_st = c_s[j]
        qc = _bdot(q, c_st.astype(BF16))
        upd = _bdot((k_t.astype(F32) * e8[j:j + 1, :]).astype(BF16), v_aug)
        dec = decay8[j:j + 1, :]
        c_s[j] = jnp.concatenate([dec, dec], axis=1) * c_st + upd
        first.append((dm, qk, qc, v_aug))
    for (d, hd), (dm, qk, qc, v_aug) in zip(chains, first):
        h_ref = dirs[d][3]
        j = d * H_M + hd
        sl = slice(hd * DH_M, (hd + 1) * DH_M)
        mask = (col <= row) if d == 0 else (col >= row)
        d1 = dm[:, :CHUNK]
        w = jnp.exp(jnp.where(mask, d1, -jnp.inf))
        s_inter = jnp.exp(d1 + ma8[j:j + 1, :])
        intra = _bdot((qk * w).astype(BF16), v_aug)
        num = s_inter * qc[:, :DH_M] + intra[:, :DH_M]
        den = s_inter * qc[:, DH_M:] + intra[:, DH_M:]
        h_ref[:, sl] = num / jnp.maximum(jnp.abs(den), jnp.exp(dm[:, CHUNK:]))

    @pl.when(step == pl.num_programs(1) - 1)
    def _():
        cn_ref[0] = c_s[...]
        mn_ref[0] = m_s[...]


def _mlstm(mq, mkt, mv, gates, c0, m0, batch, seq):
    nc = seq // CHUNK
    n = batch * seq
    nj = 2 * H_M
    fwd = lambda w: pl.BlockSpec((CHUNK, w), lambda b, c: (b * nc + c, 0))
    bwd = lambda w: pl.BlockSpec((CHUNK, w), lambda b, c: (b * nc + nc - 1 - c, 0))
    fwd_t = pl.BlockSpec((MLSTM_W, CHUNK), lambda b, c: (0, b * nc + c))
    bwd_t = pl.BlockSpec((MLSTM_W, CHUNK), lambda b, c: (0, b * nc + nc - 1 - c))
    st_m = pl.BlockSpec((1, nj, LANE), lambda b, c: (b, 0, 0))
    st_c = pl.BlockSpec((1, nj, DH_M, 2 * DH_M), lambda b, c: (b, 0, 0, 0))
    return pl.pallas_call(
        _mlstm_kernel,
        grid=(batch, nc),
        in_specs=[fwd(MLSTM_W), fwd_t, fwd(MLSTM_W), fwd(MISC_W),
                  bwd(MLSTM_W), bwd_t, bwd(MLSTM_W), bwd(MISC_W),
                  st_c, st_m],
        out_specs=[fwd(MLSTM_W), bwd(MLSTM_W), st_c, st_m],
        out_shape=[jax.ShapeDtypeStruct((n, MLSTM_W), F32), jax.ShapeDtypeStruct((n, MLSTM_W), F32),
                   jax.ShapeDtypeStruct((batch, nj, DH_M, 2 * DH_M), F32),
                   jax.ShapeDtypeStruct((batch, nj, LANE), F32)],
        scratch_shapes=[pltpu.VMEM((nj, DH_M, 2 * DH_M), F32), pltpu.VMEM((nj, LANE), F32)],
        compiler_params=_cparams(("arbitrary", "arbitrary")),
        name="mlstm",
    )(mq, mkt, mv, gates, mq, mkt, mv, gates, c0, m0)


def _attn_kernel(*refs, n_main, tk, has_cache):
    if has_cache:
        q_ref, k_ref, vt_ref, kc_ref, vct_ref, o_ref, qt_s, acc_s, m_s, ot_s = refs
    else:
        q_ref, k_ref, vt_ref, o_ref, qt_s, acc_s, m_s, ot_s = refs
    qt_s[...] = q_ref[...].astype(F32).T.astype(BF16)
    acc_s[...] = jnp.zeros(acc_s.shape, F32)
    m_s[...] = jnp.full(m_s.shape, -jnp.inf, F32)

    def update(get_k, get_vt, width):
        ones = jnp.ones((ONES_ROWS, width), BF16)

        def scores(head):
            return _bdot(get_k(head), qt_s[head * HEAD_PAD:(head + 1) * HEAD_PAD, :])

        pending = [scores(h) for h in range(QK_LOOKAHEAD)]
        for head in range(H_A):
            s = pending.pop(0)
            if head + QK_LOOKAHEAD < H_A:
                pending.append(scores(head + QK_LOOKAHEAD))
            m_old = m_s[head:head + 1, :]
            m_new = jnp.maximum(m_old, jnp.max(s, axis=0, keepdims=True))
            alpha = jnp.exp2(m_old - m_new)
            p = jnp.exp2((s - m_new).astype(BF16))
            lhs = jnp.concatenate([get_vt(head), ones], axis=0)
            acc_s[head] = alpha * acc_s[head] + _bdot(lhs, p)
            m_s[head:head + 1, :] = m_new

    def body(t, carry):
        rows = pl.ds(pl.multiple_of(t * tk, tk), tk)
        update(lambda h: k_ref[rows, h * HEAD_PAD:(h + 1) * HEAD_PAD],
               lambda h: vt_ref[t, h * V_DIM:(h + 1) * V_DIM, :], tk)
        return carry

    lax.fori_loop(0, n_main, body, 0)
    if has_cache:
        update(lambda h: kc_ref[:, h * HEAD_PAD:(h + 1) * HEAD_PAD],
               lambda h: vct_ref[0, h * V_DIM:(h + 1) * V_DIM, :], kc_ref.shape[0])
    for head in range(H_A):
        acc = acc_s[head]
        ot_s[head * V_DIM:(head + 1) * V_DIM, :] = acc[0:V_DIM] * (1.0 / acc[V_DIM:V_DIM + 1])
    o_ref[...] = ot_s[...].T.astype(BF16)


def _attention(q, k, vt, kc, vct, batch, seq, tq, tk):
    nq = seq // tq
    has_cache = kc is not None
    in_specs = [pl.BlockSpec((tq, H_A * HEAD_PAD), lambda b, i: (b * nq + i, 0)),
                pl.BlockSpec((seq, H_A * HEAD_PAD), lambda b, i: (b, 0)),
                pl.BlockSpec((seq // tk, H_A * V_DIM, tk), lambda b, i: (b, 0, 0))]
    args = [q, k, vt]
    if has_cache:
        past = kc.shape[0] // batch
        in_specs += [pl.BlockSpec((past, H_A * HEAD_PAD), lambda b, i: (b, 0)),
                     pl.BlockSpec((1, H_A * V_DIM, past), lambda b, i: (b, 0, 0))]
        args += [kc, vct]
    return pl.pallas_call(
        functools.partial(_attn_kernel, n_main=seq // tk, tk=tk, has_cache=has_cache),
        grid=(batch, nq),
        in_specs=in_specs,
        out_specs=pl.BlockSpec((tq, H_A * V_DIM), lambda b, i: (b * nq + i, 0)),
        out_shape=jax.ShapeDtypeStruct((batch * seq, H_A * V_DIM), BF16),
        scratch_shapes=[pltpu.VMEM((H_A * HEAD_PAD, tq), BF16),
                        pltpu.VMEM((H_A, V_DIM + ONES_ROWS, tq), F32),
                        pltpu.VMEM((H_A, tq), F32),
                        pltpu.VMEM((H_A * V_DIM, tq), F32)],
        compiler_params=_cparams(("arbitrary", "arbitrary")),
        name="mla_attention",
    )(*args)


def _stage_c_kernel(x_ref, mod_ref, hf_ref, hb_ref, og_ref, attn_ref, ga_ref, gb_ref, hn_ref,
                    post1_ref, pre2_ref, post2_ref, wmo_ref, wao_ref, wout_ref, w1_ref, w2_ref, y_ref):
    mod = mod_ref[0]
    gate1, shift2, scale2, gate2 = mod[2:3], mod[3:4], mod[4:5], mod[5:6]
    hm = hf_ref[...] + hb_ref[...]
    hn = hn_ref[...]
    heads = []
    for hd in range(H_M):
        sl = slice(hd * DH_M, (hd + 1) * DH_M)
        heads.append(_rms(hm[:, sl], hn[:, sl]))
    hm = jnp.concatenate(heads, axis=1) * og_ref[...].astype(F32)
    y_m = _bdot(hm.astype(BF16), wmo_ref[...])
    y_a = _bdot(attn_ref[...], wao_ref[...])
    merged = ga_ref[...].astype(F32) * y_m + gb_ref[...].astype(F32) * y_a
    mix = _bdot(merged.astype(BF16), wout_ref[...])
    x1 = x_ref[...] + gate1 * _rms(mix, post1_ref[...])
    h2 = (_rms(x1, pre2_ref[...]) * (1.0 + scale2) + shift2).astype(BF16)
    ff = jnp.zeros(x1.shape, F32)
    for cidx in range(D_FF // D_MODEL):
        sl = slice(cidx * D_MODEL, (cidx + 1) * D_MODEL)
        a = jnp.maximum(_bdot(h2, w1_ref[:, sl]), 0.0)
        ff = ff + _bdot((a * a).astype(BF16), w2_ref[sl, :])
    y_ref[...] = x1 + gate2 * _rms(ff, post2_ref[...])


def _stage_c(x, mods, mod_index, hf, hb, og, attn, ga, gb, hn, post1, pre2, post2, wmo, wao, wout, w1, w2, tm):
    n = x.shape[0]
    tile = lambda w: pl.BlockSpec((tm, w), lambda i: (i, 0))
    return pl.pallas_call(
        _stage_c_kernel,
        grid=(n // tm,),
        in_specs=[tile(D_MODEL),
                  pl.BlockSpec((1, N_MOD, D_MODEL), lambda i: (mod_index(i), 0, 0)),
                  tile(MLSTM_W), tile(MLSTM_W), tile(MLSTM_W), tile(H_A * V_DIM),
                  tile(D_MODEL), tile(D_MODEL),
                  _const_spec((1, MLSTM_W)), _const_spec((1, D_MODEL)), _const_spec((1, D_MODEL)),
                  _const_spec((1, D_MODEL)),
                  _const_spec((MLSTM_W, D_MODEL)), _const_spec((H_A * V_DIM, D_MODEL)),
                  _const_spec((D_MODEL, D_MODEL)), _const_spec((D_MODEL, D_FF)), _const_spec((D_FF, D_MODEL))],
        out_specs=tile(D_MODEL),
        out_shape=jax.ShapeDtypeStruct((n, D_MODEL), F32),
        compiler_params=_cparams(("arbitrary",)),
        name="stage_c",
    )(x, mods, hf, hb, og, attn, ga, gb, hn, post1, pre2, post2, wmo, wao, wout, w1, w2)


def _rope_tables(n_tokens):
    pos = np.arange(n_tokens)
    row = (pos // GRID_W).astype(np.float32)
    col = (pos % GRID_W).astype(np.float32)
    inv = (ROPE_BASE ** (-np.arange(0, AX_DIM, 2, dtype=np.float32) / AX_DIM)).astype(np.float32)
    ang = jnp.concatenate([jnp.asarray(row)[:, None] * inv, jnp.asarray(col)[:, None] * inv], axis=-1)
    cos, sin = jnp.cos(ang), jnp.sin(ang)
    ones = jnp.ones((n_tokens, NOPE), F32)
    zeros = jnp.zeros((n_tokens, NOPE), F32)
    pad = jnp.zeros((n_tokens, HEAD_PAD - NOPE - ROPE_DIM), F32)
    ta = jnp.concatenate([ones, cos, cos, pad], axis=-1)
    tb = jnp.concatenate([zeros, -sin, sin, pad], axis=-1)
    return ta, tb


def _plain_tables(n_tokens):
    ones = jnp.ones((n_tokens, NOPE + ROPE_DIM), F32)
    pad = jnp.zeros((n_tokens, HEAD_PAD - NOPE - ROPE_DIM), F32)
    return jnp.concatenate([ones, pad], axis=-1), jnp.zeros((n_tokens, HEAD_PAD), F32)


def kernel(x_prompt, x_sample, cache_mla_ckv, cache_mla_krope, state_mlstm_C, state_mlstm_n, state_mlstm_m,
           c, c_ctx, w_ada, b_ada, norm_pre1, norm_post1, norm_pre2, norm_post2, w_in, mlstm_gate_b,
           mla_q_norm, mla_kv_norm, w_uq, w_ukv, w_mla_o, mlstm_head_norm, w_mlstm_o, w_out, w_mlp1, w_mlp2):
    bp, sp, _ = x_prompt.shape
    bs, ss, _ = x_sample.shape
    depth = w_in.shape[0]
    past = cache_mla_ckv.shape[2]
    nj = 2 * H_M
    even = np.arange(0, ROPE_DIM, 2)
    odd = np.arange(1, ROPE_DIM, 2)
    perm = np.concatenate([even, odd])
    perm_sw = np.concatenate([odd, even])

    xp = x_prompt.reshape(bp * sp, D_MODEL)
    xs = x_sample.reshape(bs * ss, D_MODEL)
    cc = jnp.zeros((8, D_MODEL), F32).at[:bs].set(c).at[bs].set(c_ctx)
    ta_lat, tb_lat = _rope_tables(ss)
    tm_ctx, tm_lat = 512, 512
    tk_lat = 512
    ta_ctx, tb_ctx = _plain_tables(tm_ctx)

    new_ckv, new_krope, new_c, new_n, new_m = [], [], [], [], []
    for l in range(depth):
        cols = np.cumsum((MLSTM_W,) * 4 + (N_GATES, Q_LORA, KV_LORA, ROPE_DIM))
        w_mq, w_mk, w_mv, w_mo, w_g, w_cq, w_ckv, w_kr, w_merge = jnp.split(w_in[l], cols.tolist(), axis=1)
        w_g = w_g.reshape(D_MODEL, 2, 2, H_M).transpose(0, 2, 1, 3).reshape(D_MODEL, N_GATES)
        w_misc = jnp.concatenate([w_kr, w_g, jnp.zeros((D_MODEL, MISC_W - 3 * ROPE_DIM - N_GATES), F32),
                                  w_kr[:, perm], w_kr[:, perm_sw]], axis=1)
        wa = jnp.concatenate([w_mq, w_mk, w_mv, w_mo, w_merge, w_cq, w_ckv, w_misc], axis=1).astype(BF16)
        gbias = jnp.zeros((1, MISC_W), F32).at[0, GATE_LANE0:GATE_LANE0 + N_GATES].set(
            mlstm_gate_b[l].transpose(1, 0, 2).reshape(N_GATES))
        uq = w_uq[l]
        wq = jnp.concatenate([uq[..., :NOPE], uq[..., NOPE:][..., perm], uq[..., NOPE:][..., perm_sw]],
                             axis=-1).reshape(Q_LORA, H_A * HEAD_PAD).astype(BF16)
        ukv = w_ukv[l]
        wk = jnp.concatenate([ukv[..., :NOPE], jnp.zeros((KV_LORA, H_A, HEAD_PAD - NOPE), F32)],
                             axis=-1).reshape(KV_LORA, H_A * HEAD_PAD).astype(BF16)
        wv = ukv[..., NOPE:].reshape(KV_LORA, H_A * V_DIM).T.astype(BF16)
        wmo = w_mlstm_o[l].astype(BF16)
        wao = w_mla_o[l].astype(BF16)
        wout = w_out[l].astype(BF16)
        w1 = w_mlp1[l].astype(BF16)
        w2 = w_mlp2[l].astype(BF16)
        pre1, post1 = norm_pre1[l][None], norm_post1[l][None]
        pre2, post2 = norm_pre2[l][None], norm_post2[l][None]
        qn, kvn = mla_q_norm[l][None], mla_kv_norm[l][None]
        hn = mlstm_head_norm[l].reshape(1, MLSTM_W)

        mods = _modulation(cc, w_ada[l], b_ada[l][None]).reshape(8, N_MOD, D_MODEL)

        ctx_mod = lambda i: bs
        a = _stage_a(xp, mods, ctx_mod, pre1, wa, gbias, qn, kvn, wq, wk, wv, ta_ctx, tb_ctx,
                     lambda i: 0, tm_ctx, sp)
        mq, mk, mv, og, ga, gb, gates, q, k, v, ckv_n, kro = a
        c0 = jnp.zeros((bp, nj, DH_M, 2 * DH_M), F32)
        m0 = jnp.full((bp, nj, LANE), -1e30, F32)
        hf, hb, cn_fin, m_fin = _mlstm(mq, mk, mv, gates, c0, m0, bp, sp)
        c_fin, n_fin = cn_fin[..., :DH_M], cn_fin[..., DH_M]
        attn = _attention(q, k, v, None, None, bp, sp, sp, sp)
        xp = _stage_c(xp, mods, ctx_mod, hf, hb, og, attn, ga, gb, hn, post1, pre2, post2,
                      wmo, wao, wout, w1, w2, tm_ctx)
        new_ckv.append(ckv_n.reshape(bp, sp, KV_LORA))
        new_krope.append(kro.reshape(bp, sp, ROPE_DIM))
        new_c.append(c_fin.reshape(bp, 2, H_M, DH_M, DH_M))
        new_n.append(n_fin.reshape(bp, 2, H_M, DH_M))
        new_m.append(m_fin[:, :, 0].reshape(bp, 2, H_M))

        tiles_per_seq = ss // tm_lat
        lat_mod = lambda i: i // tiles_per_seq
        a = _stage_a(xs, mods, lat_mod, pre1, wa, gbias, qn, kvn, wq, wk, wv, ta_lat, tb_lat,
                     lambda i: i % tiles_per_seq, tm_lat, tk_lat)
        mq, mk, mv, og, ga, gb, gates, q, k, v, _, _ = a
        kr_cache = jnp.zeros((bs * past, LANE), F32).at[:, NOPE:NOPE + ROPE_DIM].set(
            cache_mla_krope[:, l].reshape(bs * past, ROPE_DIM)[:, perm])
        kc, vc = _kv_cache(cache_mla_ckv[:, l].reshape(bs * past, KV_LORA), kr_cache, wk, wv, past)
        n0 = state_mlstm_n[:, l].reshape(bs, nj, DH_M, 1)
        c0 = jnp.concatenate([state_mlstm_C[:, l].reshape(bs, nj, DH_M, DH_M),
                              jnp.broadcast_to(n0, (bs, nj, DH_M, DH_M))], axis=-1)
        m0 = jnp.broadcast_to(state_mlstm_m[:, l].reshape(bs, nj, 1), (bs, nj, LANE))
        hf, hb, _, _ = _mlstm(mq, mk, mv, gates, c0, m0, bs, ss)
        attn = _attention(q, k, v, kc, vc, bs, ss, 256, tk_lat)
        xs = _stage_c(xs, mods, lat_mod, hf, hb, og, attn, ga, gb, hn, post1, pre2, post2,
                      wmo, wao, wout, w1, w2, tm_lat)

    return (xp.reshape(bp, sp, D_MODEL), xs.reshape(bs, ss, D_MODEL),
            jnp.stack(new_ckv, axis=1), jnp.stack(new_krope, axis=1), jnp.stack(new_c, axis=1),
            jnp.stack(new_n, axis=1), jnp.stack(new_m, axis=1))
```

```python
import functools

import numpy as np
import jax
import jax.numpy as jnp
from jax import lax
from jax.experimental import pallas as pl
from jax.experimental.pallas import tpu as pltpu

F32 = jnp.float32
BF16 = jnp.bfloat16

D_MODEL = 1024
H_M = 4
DH_M = 128
MLSTM_W = H_M * DH_M
CHUNK = 128
H_A = 8
NOPE = 64
ROPE_DIM = 32
V_DIM = 64
Q_LORA = 384
KV_LORA = 256
AX_DIM = ROPE_DIM // 2
ROPE_BASE = 10000.0
GRID_W = 64
D_FF = 4 * D_MODEL
N_MOD = 6
EPS = 1e-6
N_GATES = 4 * H_M
M_INIT = -1e30

LANE = 128
HEAD_PAD = LANE
MISC_W = LANE
GATE_LANE0 = ROPE_DIM
OFF_MQ, OFF_MK, OFF_MV, OFF_MO = 0, MLSTM_W, 2 * MLSTM_W, 3 * MLSTM_W
OFF_GA = 4 * MLSTM_W
OFF_GB = OFF_GA + D_MODEL
OFF_CQ = OFF_GB + D_MODEL
OFF_CKV = OFF_CQ + Q_LORA
OFF_MISC = OFF_CKV + KV_LORA
NA_COLS = OFF_MISC + MISC_W

VMEM_LIMIT = 56 * 1024 * 1024
QK_SCALE = float((NOPE + ROPE_DIM) ** -0.5 * np.log2(np.e))
QK_LOOKAHEAD = 3
KEY_TILES_PER_STEP = 3
ONES_ROWS = 16


def _cparams(sem):
    return pltpu.CompilerParams(dimension_semantics=sem, vmem_limit_bytes=VMEM_LIMIT)


def _const_spec(shape):
    nd = len(shape)
    return pl.BlockSpec(shape, lambda *_: (0,) * nd, pipeline_mode=pl.Buffered(1))


def _rms(x, w):
    return x * lax.rsqrt(jnp.mean(x * x, axis=-1, keepdims=True) + EPS) * w


def _sigmoid(x):
    return 1.0 / (1.0 + jnp.exp(-x))


def _log_sigmoid(x):
    return jnp.minimum(x, 0.0) - jnp.log(1.0 + jnp.exp(-jnp.abs(x)))


def _bdot(a, b):
    return jnp.dot(a, b, preferred_element_type=F32)


def _mod_kernel(c_ref, w_ref, b_ref, o_ref):
    c = c_ref[...]
    s = c * _sigmoid(c)
    o_ref[...] = jnp.dot(s, w_ref[...], preferred_element_type=F32,
                         precision=lax.Precision.HIGHEST) + b_ref[...]


def _modulation(cc, w_ada, b_ada):
    n_out = w_ada.shape[1]
    tn = 1536
    return pl.pallas_call(
        _mod_kernel,
        grid=(n_out // tn,),
        in_specs=[pl.BlockSpec((8, D_MODEL), lambda j: (0, 0)),
                  pl.BlockSpec((D_MODEL, tn), lambda j: (0, j)),
                  pl.BlockSpec((1, tn), lambda j: (0, j))],
        out_specs=pl.BlockSpec((8, tn), lambda j: (0, j)),
        out_shape=jax.ShapeDtypeStruct((8, n_out), F32),
        compiler_params=_cparams(("arbitrary",)),
        name="modulation",
    )(cc, w_ada, b_ada)


def _stage_a_kernel(x_ref, mod_ref, pre1_ref, wa_ref, gbias_ref, qn_ref, kvn_ref, wq_ref, wk_ref, wv_ref,
                    ta_ref, tb_ref,
                    mq_ref, mk_ref, mv_ref, og_ref, ga_ref, gb_ref, gates_ref, q_ref, k_ref, v_ref,
                    ckv_ref, kro_ref):
    x = x_ref[...]
    mod = mod_ref[0]
    shift1, scale1 = mod[0:1], mod[1:2]
    h = _rms(x, pre1_ref[...]) * (1.0 + scale1) + shift1
    hb = h.astype(BF16)

    def proj(lo, hi):
        return _bdot(hb, wa_ref[:, lo:hi])

    mq_ref[...] = (proj(OFF_MQ, OFF_MK) * DH_M ** -0.5).astype(BF16)
    mk_ref[...] = proj(OFF_MK, OFF_MV).T.astype(BF16)
    mv_ref[...] = proj(OFF_MV, OFF_MO).astype(BF16)
    og_ref[...] = _sigmoid(proj(OFF_MO, OFF_GA)).astype(BF16)
    ga_ref[...] = _sigmoid(proj(OFF_GA, OFF_GB)).astype(BF16)
    gb_ref[...] = _sigmoid(proj(OFF_GB, OFF_CQ)).astype(BF16)
    cq = proj(OFF_CQ, OFF_CKV)
    ckv = proj(OFF_CKV, OFF_MISC)
    misc = proj(OFF_MISC, NA_COLS)

    lane = lax.broadcasted_iota(jnp.int32, misc.shape, 1)
    g = misc + gbias_ref[...]
    is_f = (lane >= GATE_LANE0 + 2 * H_M) & (lane < GATE_LANE0 + N_GATES)
    gates_ref[...] = jnp.where(is_f, _log_sigmoid(g), g)
    kro_ref[...] = misc[:, 0:ROPE_DIM]

    ckv_n = _rms(ckv, kvn_ref[...])
    ckv_ref[...] = ckv_n
    ckv_b = ckv_n.astype(BF16)

    ta = ta_ref[...]
    tb = tb_ref[...]
    qf = _bdot(_rms(cq, qn_ref[...]).astype(BF16), wq_ref[...])
    ta8 = jnp.concatenate([ta] * H_A, axis=1)
    tb8 = jnp.concatenate([tb] * H_A, axis=1)
    q = qf * ta8 + pltpu.roll(qf, H_A * HEAD_PAD - ROPE_DIM, 1) * tb8
    q_ref[...] = (q * QK_SCALE).astype(BF16)

    ta_k = jnp.where(lane < NOPE, 0.0, ta)
    kr = misc * ta_k + pltpu.roll(misc, MISC_W - ROPE_DIM, 1) * tb
    kk = _bdot(ckv_b, wk_ref[...]) + jnp.concatenate([kr] * H_A, axis=1)
    k_ref[...] = kk.astype(BF16)
    vt = _bdot(wv_ref[...], ckv_n.T.astype(BF16)).astype(BF16)
    tkv = v_ref.shape[2]
    for j in range(v_ref.shape[0]):
        v_ref[j] = vt[:, j * tkv:(j + 1) * tkv]


def _stage_a(x, mods, mod_index, pre1, wa, gbias, qn, kvn, wq, wk, wv, ta, tb, table_index, tm, tkv,
             kv_index=lambda i: i, n_kv=None):
    n = x.shape[0]
    n_kv = n if n_kv is None else n_kv
    tile = lambda w: pl.BlockSpec((tm, w), lambda i: (i, 0))
    bf = lambda w: jax.ShapeDtypeStruct((n, w), BF16)
    f32 = lambda w: jax.ShapeDtypeStruct((n, w), F32)
    return pl.pallas_call(
        _stage_a_kernel,
        grid=(n // tm,),
        in_specs=[tile(D_MODEL),
                  pl.BlockSpec((1, N_MOD, D_MODEL), lambda i: (mod_index(i), 0, 0)),
                  _const_spec((1, D_MODEL)),
                  _const_spec((D_MODEL, NA_COLS)),
                  _const_spec((1, MISC_W)),
                  _const_spec((1, Q_LORA)),
                  _const_spec((1, KV_LORA)),
                  _const_spec((Q_LORA, H_A * HEAD_PAD)),
                  _const_spec((KV_LORA, H_A * HEAD_PAD)),
                  _const_spec((H_A * V_DIM, KV_LORA)),
                  pl.BlockSpec((tm, LANE), lambda i: (table_index(i), 0)),
                  pl.BlockSpec((tm, LANE), lambda i: (table_index(i), 0))],
        out_specs=[tile(MLSTM_W), pl.BlockSpec((MLSTM_W, tm), lambda i: (0, i)), tile(MLSTM_W), tile(MLSTM_W),
                   tile(D_MODEL), tile(D_MODEL), tile(MISC_W),
                   tile(H_A * HEAD_PAD), pl.BlockSpec((tm, H_A * HEAD_PAD), lambda i: (kv_index(i), 0)),
                   pl.BlockSpec((tm // tkv, H_A * V_DIM, tkv), lambda i: (kv_index(i), 0, 0)),
                   tile(KV_LORA), tile(ROPE_DIM)],
        out_shape=[bf(MLSTM_W), jax.ShapeDtypeStruct((MLSTM_W, n), BF16), bf(MLSTM_W), bf(MLSTM_W),
                   bf(D_MODEL), bf(D_MODEL), f32(MISC_W),
                   bf(H_A * HEAD_PAD), jax.ShapeDtypeStruct((n_kv, H_A * HEAD_PAD), BF16),
                   jax.ShapeDtypeStruct((n_kv // tkv, H_A * V_DIM, tkv), BF16),
                   f32(KV_LORA), f32(ROPE_DIM)],
        compiler_params=_cparams(("arbitrary",)),
        name="stage_a",
    )(x, mods, pre1, wa, gbias, qn, kvn, wq, wk, wv, ta, tb)


def _kv_cache_kernel(ckv_ref, kr_ref, wk_ref, wv_ref, k_hbm, v_hbm, k_ref, v_ref):
    del k_hbm, v_hbm
    ckv_b = ckv_ref[...].astype(BF16)
    kk = _bdot(ckv_b, wk_ref[...]) + jnp.concatenate([kr_ref[...]] * H_A, axis=1)
    k_ref[...] = kk.astype(BF16)
    v_ref[0] = _bdot(wv_ref[...], ckv_ref[...].T.astype(BF16)).astype(BF16)


def _kv_cache(ckv, kr, wk, wv, k_full, vt_full, tm, kv_index):
    n = ckv.shape[0]
    return pl.pallas_call(
        _kv_cache_kernel,
        grid=(n // tm,),
        in_specs=[pl.BlockSpec((tm, KV_LORA), lambda i: (i, 0)),
                  pl.BlockSpec((tm, LANE), lambda i: (i, 0)),
                  _const_spec((KV_LORA, H_A * HEAD_PAD)),
                  _const_spec((H_A * V_DIM, KV_LORA)),
                  pl.BlockSpec(memory_space=pl.ANY),
                  pl.BlockSpec(memory_space=pl.ANY)],
        out_specs=[pl.BlockSpec((tm, H_A * HEAD_PAD), lambda i: (kv_index(i), 0)),
                   pl.BlockSpec((1, H_A * V_DIM, tm), lambda i: (kv_index(i), 0, 0))],
        out_shape=[jax.ShapeDtypeStruct(k_full.shape, BF16),
                   jax.ShapeDtypeStruct(vt_full.shape, BF16)],
        input_output_aliases={4: 0, 5: 1},
        compiler_params=_cparams(("arbitrary",)),
        name="kv_cache",
    )(ckv, kr, wk, wv, k_full, vt_full)


def _split3(x):
    x1 = x.astype(BF16).astype(F32)
    r = x - x1
    x2 = r.astype(BF16).astype(F32)
    x3 = (r - x2).astype(BF16).astype(F32)
    return x1, x2, x3


def _mlstm_kernel(*refs, has_state):
    if has_state:
        (qf_ref, ktf_ref, vf_ref, gf_ref, qb_ref, ktb_ref, vb_ref, gb_ref, c0_ref, m0_ref,
         hf_ref, hb_ref, cn_ref, nn_ref, mn_ref, c_s, m_s) = refs
    else:
        (qf_ref, ktf_ref, vf_ref, gf_ref, qb_ref, ktb_ref, vb_ref, gb_ref,
         hf_ref, hb_ref, cn_ref, nn_ref, mn_ref, c_s, m_s) = refs
    step = pl.program_id(1)

    @pl.when(step == 0)
    def _():
        if has_state:
            c_s[...] = c0_ref[0]
            m_s[...] = m0_ref[0]
        else:
            c_s[...] = jnp.zeros(c_s.shape, F32)
            m_s[...] = jnp.full(m_s.shape, M_INIT, F32)

    nj = 2 * H_M
    row = lax.broadcasted_iota(jnp.int32, (CHUNK, CHUNK), 0)
    col = lax.broadcasted_iota(jnp.int32, (CHUNK, CHUNK), 1)
    lower = (col <= row).astype(F32)
    upper = (col >= row).astype(F32)
    hi = lax.Precision.HIGHEST
    row8 = lax.broadcasted_iota(jnp.int32, (nj, CHUNK), 0)
    lane8 = lax.broadcasted_iota(jnp.int32, (nj, CHUNK), 1)
    is_fwd = row8 < H_M

    gt_f = gf_ref[...].T
    gt_b = gb_ref[...].T
    gi, gf = GATE_LANE0, GATE_LANE0 + nj
    i8 = jnp.where(is_fwd, gt_f[gi:gi + nj], gt_b[gi:gi + nj])
    f8 = jnp.where(is_fwd, gt_f[gf:gf + nj], gt_b[gf:gf + nj])
    b8 = jnp.where(is_fwd, jnp.dot(f8, upper, preferred_element_type=F32, precision=hi),
                   jnp.dot(f8, lower, preferred_element_type=F32, precision=hi))
    a8 = i8 - b8
    cm8 = a8
    shift = 1
    while shift < CHUNK:
        y_f = jnp.where(lane8 >= shift, pltpu.roll(cm8, shift, 1), -jnp.inf)
        y_b = jnp.where(lane8 < CHUNK - shift, pltpu.roll(cm8, CHUNK - shift, 1), -jnp.inf)
        cm8 = jnp.maximum(cm8, jnp.where(is_fwd, y_f, y_b))
        shift *= 2
    m8 = m_s[...]
    mx8 = jnp.maximum(m8, jnp.max(a8, axis=1, keepdims=True))
    e8 = jnp.exp(a8 - mx8)
    decay8 = jnp.exp(m8 - mx8)
    u8 = jnp.maximum(m8, cm8)
    ma8 = m8 - a8
    m_s[...] = jnp.sum(f8, axis=1, keepdims=True) + mx8

    ones8 = jnp.ones((nj, CHUNK), F32)
    zeros8 = jnp.zeros((nj, CHUNK), F32)
    p_rows = [ones8] * 3 + list(_split3(u8)) + list(_split3(b8))
    p_t = jnp.concatenate(p_rows + [zeros8] * (CHUNK // nj - len(p_rows)), axis=0).T
    q_s = list(_split3(a8)) + [-ones8] * 3
    q_s = jnp.concatenate(q_s + [zeros8] * (CHUNK // nj - len(q_s)), axis=0)
    q_m = jnp.concatenate([zeros8] * 3 + [-ones8] * 6 + [zeros8] * (CHUNK // nj - 9), axis=0)
    q_full = jnp.concatenate([q_s, q_m], axis=1).astype(BF16)
    ones_v = jnp.ones((CHUNK, DH_M), BF16)

    dirs = ((qf_ref, ktf_ref, vf_ref, hf_ref), (qb_ref, ktb_ref, vb_ref, hb_ref))
    chains = [(d, hd) for d in range(2) for hd in range(H_M)]
    first = []
    for d, hd in chains:
        q_ref, kt_ref, v_ref, _ = dirs[d]
        j = d * H_M + hd
        sl = slice(hd * DH_M, (hd + 1) * DH_M)
        q = q_ref[:, sl]
        k_t = kt_ref[sl, :]
        v_aug = jnp.concatenate([v_ref[:, sl], ones_v], axis=1)
        dm = _bdot(jnp.where(col % nj == j, p_t, 0.0).astype(BF16), q_full)
        qk = _bdot(q, k_t)
        c_st = c_s[j]
        qc = _bdot(q, c_st.astype(BF16))
        upd = _bdot((k_t.astype(F32) * e8[j:j + 1, :]).astype(BF16), v_aug)
        dec = decay8[j:j + 1, :]
        c_s[j] = jnp.concatenate([dec, dec], axis=1) * c_st + upd
        first.append((dm, qk, qc, v_aug))
    for (d, hd), (dm, qk, qc, v_aug) in zip(chains, first):
        h_ref = dirs[d][3]
        j = d * H_M + hd
        sl = slice(hd * DH_M, (hd + 1) * DH_M)
        mask = (col <= row) if d == 0 else (col >= row)
        d1 = dm[:, :CHUNK]
        w = jnp.exp(jnp.where(mask, d1, -jnp.inf))
        s_inter = jnp.exp(d1 + ma8[j:j + 1, :])
        intra = _bdot((qk * w).astype(BF16), v_aug)
        num = s_inter * qc[:, :DH_M] + intra[:, :DH_M]
        den = s_inter * qc[:, DH_M:] + intra[:, DH_M:]
        h_ref[:, sl] = num / jnp.maximum(jnp.abs(den), jnp.exp(dm[:, CHUNK:]))

    @pl.when(step == pl.num_programs(1) - 1)
    def _():
        for j in range(nj):
            c_aug = c_s[j]
            cn_ref[0, j] = c_aug[:, :DH_M]
            nn_ref[0, j:j + 1, :] = c_aug[:, DH_M:].T[0:1, :]
        mn_ref[0] = m_s[...]


def _mlstm(mq, mkt, mv, gates, c0, m0, batch, seq):
    has_state = c0 is not None
    nc = seq // CHUNK
    n = batch * seq
    nj = 2 * H_M
    fwd = lambda w: pl.BlockSpec((CHUNK, w), lambda b, c: (b * nc + c, 0))
    bwd = lambda w: pl.BlockSpec((CHUNK, w), lambda b, c: (b * nc + nc - 1 - c, 0))
    fwd_t = pl.BlockSpec((MLSTM_W, CHUNK), lambda b, c: (0, b * nc + c))
    bwd_t = pl.BlockSpec((MLSTM_W, CHUNK), lambda b, c: (0, b * nc + nc - 1 - c))
    st_m = pl.BlockSpec((1, nj, LANE), lambda b, c: (b, 0, 0))
    st_c = pl.BlockSpec((1, nj, DH_M, 2 * DH_M), lambda b, c: (b, 0, 0, 0))
    in_specs = [fwd(MLSTM_W), fwd_t, fwd(MLSTM_W), fwd(MISC_W),
                bwd(MLSTM_W), bwd_t, bwd(MLSTM_W), bwd(MISC_W)]
    args = [mq, mkt, mv, gates, mq, mkt, mv, gates]
    if has_state:
        in_specs += [st_c, st_m]
        args += [c0, m0]
    return pl.pallas_call(
        functools.partial(_mlstm_kernel, has_state=has_state),
        grid=(batch, nc),
        in_specs=in_specs,
        out_specs=[fwd(MLSTM_W), bwd(MLSTM_W),
                   pl.BlockSpec((1, nj, DH_M, DH_M), lambda b, c: (b, 0, 0, 0)), st_m, st_m],
        out_shape=[jax.ShapeDtypeStruct((n, MLSTM_W), F32), jax.ShapeDtypeStruct((n, MLSTM_W), F32),
                   jax.ShapeDtypeStruct((batch, nj, DH_M, DH_M), F32),
                   jax.ShapeDtypeStruct((batch, nj, DH_M), F32),
                   jax.ShapeDtypeStruct((batch, nj, LANE), F32)],
        scratch_shapes=[pltpu.VMEM((nj, DH_M, 2 * DH_M), F32), pltpu.VMEM((nj, LANE), F32)],
        compiler_params=_cparams(("arbitrary", "arbitrary")),
        name="mlstm",
    )(*args)


def _attn_kernel(q_ref, k_ref, vt_ref, o_ref, qt_s, acc_s, m_s, ot_s, *, tiles_per_step):
    n_tiles, _, tk = vt_ref.shape
    qt_s[...] = q_ref[...].astype(F32).T.astype(BF16)
    acc_s[...] = jnp.zeros(acc_s.shape, F32)
    m_s[...] = jnp.full(m_s.shape, -jnp.inf, F32)

    def update(tiles):
        items = [(tile, head) for tile in tiles for head in range(H_A)]

        def scores(item):
            (get_k, _, _), head = item
            return _bdot(get_k(head), qt_s[head * HEAD_PAD:(head + 1) * HEAD_PAD, :])

        pending = [scores(it) for it in items[:QK_LOOKAHEAD]]
        for idx, ((_, get_vt, width), head) in enumerate(items):
            s = pending.pop(0)
            if idx + QK_LOOKAHEAD < len(items):
                pending.append(scores(items[idx + QK_LOOKAHEAD]))
            m_old = m_s[head:head + 1, :]
            m_new = jnp.maximum(m_old, jnp.max(s, axis=0, keepdims=True))
            alpha = jnp.exp2(m_old - m_new)
            p = jnp.exp2((s - m_new).astype(BF16))
            ones = jnp.ones((ONES_ROWS, width), BF16)
            lhs = jnp.concatenate([get_vt(head), ones], axis=0)
            acc_s[head] = alpha * acc_s[head] + _bdot(lhs, p)
            m_s[head:head + 1, :] = m_new

    def main_tile(t):
        rows = pl.ds(pl.multiple_of(t * tk, tk), tk)
        return (lambda h: k_ref[rows, h * HEAD_PAD:(h + 1) * HEAD_PAD],
                lambda h: vt_ref[t, h * V_DIM:(h + 1) * V_DIM, :], tk)

    def body(i, carry):
        update([main_tile(i * tiles_per_step + u) for u in range(tiles_per_step)])
        return carry

    lax.fori_loop(0, n_tiles // tiles_per_step, body, 0)
    for head in range(H_A):
        acc = acc_s[head]
        ot_s[head * V_DIM:(head + 1) * V_DIM, :] = acc[0:V_DIM] * (1.0 / acc[V_DIM:V_DIM + 1])
    o_ref[...] = ot_s[...].T.astype(BF16)


def _attention(q, k, vt, batch, seq, tq, tiles_per_step):
    nq = seq // tq
    n_keys = k.shape[0] // batch
    tk = vt.shape[2]
    n_tiles = n_keys // tk
    assert n_tiles % tiles_per_step == 0
    in_specs = [pl.BlockSpec((tq, H_A * HEAD_PAD), lambda b, i: (b * nq + i, 0)),
                pl.BlockSpec((n_keys, H_A * HEAD_PAD), lambda b, i: (b, 0)),
                pl.BlockSpec((n_tiles, H_A * V_DIM, tk), lambda b, i: (b, 0, 0))]
    return pl.pallas_call(
        functools.partial(_attn_kernel, tiles_per_step=tiles_per_step),
        grid=(batch, nq),
        in_specs=in_specs,
        out_specs=pl.BlockSpec((tq, H_A * V_DIM), lambda b, i: (b * nq + i, 0)),
        out_shape=jax.ShapeDtypeStruct((batch * seq, H_A * V_DIM), BF16),
        scratch_shapes=[pltpu.VMEM((H_A * HEAD_PAD, tq), BF16),
                        pltpu.VMEM((H_A, V_DIM + ONES_ROWS, tq), F32),
                        pltpu.VMEM((H_A, tq), F32),
                        pltpu.VMEM((H_A * V_DIM, tq), F32)],
        compiler_params=_cparams(("arbitrary", "arbitrary")),
        name="mla_attention",
    )(q, k, vt)


def _stage_c_kernel(x_ref, mod_ref, hf_ref, hb_ref, og_ref, attn_ref, ga_ref, gb_ref, hn_ref,
                    post1_ref, pre2_ref, post2_ref, wmo_ref, wao_ref, wout_ref, w1_ref, w2_ref, y_ref):
    mod = mod_ref[0]
    gate1, shift2, scale2, gate2 = mod[2:3], mod[3:4], mod[4:5], mod[5:6]
    hm = hf_ref[...] + hb_ref[...]
    hn = hn_ref[...]
    heads = []
    for hd in range(H_M):
        sl = slice(hd * DH_M, (hd + 1) * DH_M)
        heads.append(_rms(hm[:, sl], hn[:, sl]))
    hm = jnp.concatenate(heads, axis=1) * og_ref[...].astype(F32)
    y_m = _bdot(hm.astype(BF16), wmo_ref[...])
    y_a = _bdot(attn_ref[...], wao_ref[...])
    merged = ga_ref[...].astype(F32) * y_m + gb_ref[...].astype(F32) * y_a
    mix = _bdot(merged.astype(BF16), wout_ref[...])
    x1 = x_ref[...] + gate1 * _rms(mix, post1_ref[...])
    h2 = (_rms(x1, pre2_ref[...]) * (1.0 + scale2) + shift2).astype(BF16)
    ff = jnp.zeros(x1.shape, F32)
    for cidx in range(D_FF // D_MODEL):
        sl = slice(cidx * D_MODEL, (cidx + 1) * D_MODEL)
        a = jnp.maximum(_bdot(h2, w1_ref[:, sl]), 0.0)
        ff = ff + _bdot((a * a).astype(BF16), w2_ref[sl, :])
    y_ref[...] = x1 + gate2 * _rms(ff, post2_ref[...])


def _stage_c(x, mods, mod_index, hf, hb, og, attn, ga, gb, hn, post1, pre2, post2, wmo, wao, wout, w1, w2, tm):
    n = x.shape[0]
    tile = lambda w: pl.BlockSpec((tm, w), lambda i: (i, 0))
    return pl.pallas_call(
        _stage_c_kernel,
        grid=(n // tm,),
        in_specs=[tile(D_MODEL),
                  pl.BlockSpec((1, N_MOD, D_MODEL), lambda i: (mod_index(i), 0, 0)),
                  tile(MLSTM_W), tile(MLSTM_W), tile(MLSTM_W), tile(H_A * V_DIM),
                  tile(D_MODEL), tile(D_MODEL),
                  _const_spec((1, MLSTM_W)), _const_spec((1, D_MODEL)), _const_spec((1, D_MODEL)),
                  _const_spec((1, D_MODEL)),
                  _const_spec((MLSTM_W, D_MODEL)), _const_spec((H_A * V_DIM, D_MODEL)),
                  _const_spec((D_MODEL, D_MODEL)), _const_spec((D_MODEL, D_FF)), _const_spec((D_FF, D_MODEL))],
        out_specs=tile(D_MODEL),
        out_shape=jax.ShapeDtypeStruct((n, D_MODEL), F32),
        compiler_params=_cparams(("arbitrary",)),
        name="stage_c",
    )(x, mods, hf, hb, og, attn, ga, gb, hn, post1, pre2, post2, wmo, wao, wout, w1, w2)


def _rope_tables(n_tokens):
    pos = np.arange(n_tokens)
    row = (pos // GRID_W).astype(np.float32)
    col = (pos % GRID_W).astype(np.float32)
    inv = (ROPE_BASE ** (-np.arange(0, AX_DIM, 2, dtype=np.float32) / AX_DIM)).astype(np.float32)
    ang = jnp.concatenate([jnp.asarray(row)[:, None] * inv, jnp.asarray(col)[:, None] * inv], axis=-1)
    cos, sin = jnp.cos(ang), jnp.sin(ang)
    ones = jnp.ones((n_tokens, NOPE), F32)
    zeros = jnp.zeros((n_tokens, NOPE), F32)
    pad = jnp.zeros((n_tokens, HEAD_PAD - NOPE - ROPE_DIM), F32)
    ta = jnp.concatenate([ones, cos, cos, pad], axis=-1)
    tb = jnp.concatenate([zeros, -sin, sin, pad], axis=-1)
    return ta, tb


def _plain_tables(n_tokens):
    ones = jnp.ones((n_tokens, NOPE + ROPE_DIM), F32)
    pad = jnp.zeros((n_tokens, HEAD_PAD - NOPE - ROPE_DIM), F32)
    return jnp.concatenate([ones, pad], axis=-1), jnp.zeros((n_tokens, HEAD_PAD), F32)


def kernel(x_prompt, x_sample, cache_mla_ckv, cache_mla_krope, state_mlstm_C, state_mlstm_n, state_mlstm_m,
           c, c_ctx, w_ada, b_ada, norm_pre1, norm_post1, norm_pre2, norm_post2, w_in, mlstm_gate_b,
           mla_q_norm, mla_kv_norm, w_uq, w_ukv, w_mla_o, mlstm_head_norm, w_mlstm_o, w_out, w_mlp1, w_mlp2):
    bp, sp, _ = x_prompt.shape
    bs, ss, _ = x_sample.shape
    depth = w_in.shape[0]
    past = cache_mla_ckv.shape[2]
    nj = 2 * H_M
    even = np.arange(0, ROPE_DIM, 2)
    odd = np.arange(1, ROPE_DIM, 2)
    perm = np.concatenate([even, odd])
    perm_sw = np.concatenate([odd, even])

    xp = x_prompt.reshape(bp * sp, D_MODEL)
    xs = x_sample.reshape(bs * ss, D_MODEL)
    cc = jnp.zeros((8, D_MODEL), F32).at[:bs].set(c).at[bs].set(c_ctx)
    ta_lat, tb_lat = _rope_tables(ss)
    tm_ctx, tm_lat = 512, 512
    assert past == tm_lat
    ta_ctx, tb_ctx = _plain_tables(tm_ctx)

    new_ckv, new_krope, new_c, new_n, new_m = [], [], [], [], []
    for l in range(depth):
        cols = np.cumsum((MLSTM_W,) * 4 + (N_GATES, Q_LORA, KV_LORA, ROPE_DIM))
        w_mq, w_mk, w_mv, w_mo, w_g, w_cq, w_ckv, w_kr, w_merge = jnp.split(
            w_in[l].astype(BF16), cols.tolist(), axis=1)
        w_g = w_g.reshape(D_MODEL, 2, 2, H_M).transpose(0, 2, 1, 3).reshape(D_MODEL, N_GATES)
        w_misc = jnp.concatenate([w_kr, w_g, jnp.zeros((D_MODEL, MISC_W - 3 * ROPE_DIM - N_GATES), BF16),
                                  w_kr[:, perm], w_kr[:, perm_sw]], axis=1)
        wa = jnp.concatenate([w_mq, w_mk, w_mv, w_mo, w_merge, w_cq, w_ckv, w_misc], axis=1)
        gbias = jnp.zeros((1, MISC_W), F32).at[0, GATE_LANE0:GATE_LANE0 + N_GATES].set(
            mlstm_gate_b[l].transpose(1, 0, 2).reshape(N_GATES))
        uq = w_uq[l]
        wq = jnp.concatenate([uq[..., :NOPE], uq[..., NOPE:][..., perm], uq[..., NOPE:][..., perm_sw]],
                             axis=-1).reshape(Q_LORA, H_A * HEAD_PAD).astype(BF16)
        ukv = w_ukv[l]
        wk = jnp.concatenate([ukv[..., :NOPE], jnp.zeros((KV_LORA, H_A, HEAD_PAD - NOPE), F32)],
                             axis=-1).reshape(KV_LORA, H_A * HEAD_PAD).astype(BF16)
        wv = ukv[..., NOPE:].reshape(KV_LORA, H_A * V_DIM).T.astype(BF16)
        wmo = w_mlstm_o[l].astype(BF16)
        wao = w_mla_o[l].astype(BF16)
        wout = w_out[l].astype(BF16)
        w1 = w_mlp1[l].astype(BF16)
        w2 = w_mlp2[l].astype(BF16)
        pre1, post1 = norm_pre1[l][None], norm_post1[l][None]
        pre2, post2 = norm_pre2[l][None], norm_post2[l][None]
        qn, kvn = mla_q_norm[l][None], mla_kv_norm[l][None]
        hn = mlstm_head_norm[l].reshape(1, MLSTM_W)

        mods = _modulation(cc, w_ada[l], b_ada[l][None]).reshape(8, N_MOD, D_MODEL)

        ctx_mod = lambda i: bs
        a = _stage_a(xp, mods, ctx_mod, pre1, wa, gbias, qn, kvn, wq, wk, wv, ta_ctx, tb_ctx,
                     lambda i: 0, tm_ctx, sp)
        mq, mk, mv, og, ga, gb, gates, q, k, v, ckv_n, kro = a
        hf, hb, c_fin, n_fin, m_fin = _mlstm(mq, mk, mv, gates, None, None, bp, sp)
        attn = _attention(q, k, v, bp, sp, sp, 1)
        xp = _stage_c(xp, mods, ctx_mod, hf, hb, og, attn, ga, gb, hn, post1, pre2, post2,
                      wmo, wao, wout, w1, w2, tm_ctx)
        new_ckv.append(ckv_n.reshape(bp, sp, KV_LORA))
        new_krope.append(kro.reshape(bp, sp, ROPE_DIM))
        new_c.append(c_fin.reshape(bp, 2, H_M, DH_M, DH_M))
        new_n.append(n_fin.reshape(bp, 2, H_M, DH_M))
        new_m.append(m_fin[:, :, 0].reshape(bp, 2, H_M))

        tiles_per_seq = ss // tm_lat
        lat_mod = lambda i: i // tiles_per_seq
        kv_tiles = tiles_per_seq + past // tm_lat
        a = _stage_a(xs, mods, lat_mod, pre1, wa, gbias, qn, kvn, wq, wk, wv, ta_lat, tb_lat,
                     lambda i: i % tiles_per_seq, tm_lat, tm_lat,
                     kv_index=lambda i: (i // tiles_per_seq) * kv_tiles + i % tiles_per_seq,
                     n_kv=bs * kv_tiles * tm_lat)
        mq, mk, mv, og, ga, gb, gates, q, k, v, _, _ = a
        kr_cache = jnp.zeros((bs * past, LANE), F32).at[:, NOPE:NOPE + ROPE_DIM].set(
            cache_mla_krope[:, l].reshape(bs * past, ROPE_DIM)[:, perm])
        k, v = _kv_cache(cache_mla_ckv[:, l].reshape(bs * past, KV_LORA), kr_cache, wk, wv, k, v, past,
                         lambda i: i * kv_tiles + tiles_per_seq)
        n0 = state_mlstm_n[:, l].reshape(bs, nj, DH_M, 1)
        c0 = jnp.concatenate([state_mlstm_C[:, l].reshape(bs, nj, DH_M, DH_M),
                              jnp.broadcast_to(n0, (bs, nj, DH_M, DH_M))], axis=-1)
        m0 = jnp.broadcast_to(state_mlstm_m[:, l].reshape(bs, nj, 1), (bs, nj, LANE))
        hf, hb, _, _, _ = _mlstm(mq, mk, mv, gates, c0, m0, bs, ss)
        attn = _attention(q, k, v, bs, ss, 256, KEY_TILES_PER_STEP)
        xs = _stage_c(xs, mods, lat_mod, hf, hb, og, attn, ga, gb, hn, post1, pre2, post2,
                      wmo, wao, wout, w1, w2, tm_lat)

    return (xp.reshape(bp, sp, D_MODEL), xs.reshape(bs, ss, D_MODEL),
            jnp.stack(new_ckv, axis=1), jnp.stack(new_krope, axis=1), jnp.stack(new_c, axis=1),
            jnp.stack(new_n, axis=1), jnp.stack(new_m, axis=1))
```

```python
import functools

import numpy as np
import jax
import jax.numpy as jnp
from jax import lax
from jax.experimental import pallas as pl
from jax.experimental.pallas import tpu as pltpu

F32 = jnp.float32
BF16 = jnp.bfloat16

D_MODEL = 1024
H_M = 4
DH_M = 128
MLSTM_W = H_M * DH_M
CHUNK = 128
H_A = 8
NOPE = 64
ROPE_DIM = 32
V_DIM = 64
Q_LORA = 384
KV_LORA = 256
AX_DIM = ROPE_DIM // 2
ROPE_BASE = 10000.0
GRID_W = 64
D_FF = 4 * D_MODEL
N_MOD = 6
EPS = 1e-6
N_GATES = 4 * H_M
M_INIT = -1e30

LANE = 128
HEAD_PAD = LANE
MISC_W = LANE
GATE_LANE0 = ROPE_DIM
OFF_MQ, OFF_MK, OFF_MV, OFF_MO = 0, MLSTM_W, 2 * MLSTM_W, 3 * MLSTM_W
OFF_GA = 4 * MLSTM_W
OFF_GB = OFF_GA + D_MODEL
OFF_CQ = OFF_GB + D_MODEL
OFF_CKV = OFF_CQ + Q_LORA
OFF_MISC = OFF_CKV + KV_LORA
NA_COLS = OFF_MISC + MISC_W

VMEM_LIMIT = 60 * 1024 * 1024
QK_SCALE = float((NOPE + ROPE_DIM) ** -0.5 * np.log2(np.e))
QK_LOOKAHEAD = 4
ROW_SPLITS = 2
KEY_TILES_PER_STEP = 3
ONES_ROWS = 16


def _cparams(sem):
    return pltpu.CompilerParams(dimension_semantics=sem, vmem_limit_bytes=VMEM_LIMIT)


def _const_spec(shape):
    nd = len(shape)
    return pl.BlockSpec(shape, lambda *_: (0,) * nd, pipeline_mode=pl.Buffered(1))


def _rms(x, w):
    return x * lax.rsqrt(jnp.mean(x * x, axis=-1, keepdims=True) + EPS) * w


def _sigmoid(x):
    return 1.0 / (1.0 + jnp.exp(-x))


def _log_sigmoid(x):
    return jnp.minimum(x, 0.0) - jnp.log(1.0 + jnp.exp(-jnp.abs(x)))


def _bdot(a, b):
    return jnp.dot(a, b, preferred_element_type=F32)


def _mod_kernel(c_ref, w_ref, b_ref, o_ref):
    c = c_ref[...]
    s = c * _sigmoid(c)
    o_ref[...] = jnp.dot(s, w_ref[...], preferred_element_type=F32,
                         precision=lax.Precision.HIGHEST) + b_ref[...]


def _modulation(cc, w_ada, b_ada):
    n_out = w_ada.shape[1]
    tn = 1536
    return pl.pallas_call(
        _mod_kernel,
        grid=(n_out // tn,),
        in_specs=[pl.BlockSpec((8, D_MODEL), lambda j: (0, 0)),
                  pl.BlockSpec((D_MODEL, tn), lambda j: (0, j)),
                  pl.BlockSpec((1, tn), lambda j: (0, j))],
        out_specs=pl.BlockSpec((8, tn), lambda j: (0, j)),
        out_shape=jax.ShapeDtypeStruct((8, n_out), F32),
        compiler_params=_cparams(("arbitrary",)),
        name="modulation",
    )(cc, w_ada, b_ada)


def _stage_a_kernel(x_ref, mod_ref, pre1_ref, wa_ref, gbias_ref, qn_ref, kvn_ref, wq_ref, wk_ref, wv_ref,
                    ta_ref, tb_ref,
                    mq_ref, mk_ref, mv_ref, og_ref, ga_ref, gb_ref, gates_ref, q_ref, k_ref, v_ref,
                    ckv_ref, kro_ref):
    mod = mod_ref[0]
    shift1, scale1 = mod[0:1], mod[1:2]
    tm = x_ref.shape[0]
    rows = tm // ROW_SPLITS
    tkv = v_ref.shape[2]

    def normed(rs):
        return (_rms(x_ref[rs, :], pre1_ref[...]) * (1.0 + scale1) + shift1).astype(BF16)

    def small_proj(hb):
        return tuple(_bdot(hb, wa_ref[:, lo:hi])
                     for lo, hi in ((OFF_CQ, OFF_CKV), (OFF_CKV, OFF_MISC), (OFF_MISC, NA_COLS)))

    def big_proj(hb, rs):
        proj = lambda lo, hi: _bdot(hb, wa_ref[:, lo:hi])
        mq_ref[rs, :] = (proj(OFF_MQ, OFF_MK) * DH_M ** -0.5).astype(BF16)
        mk_ref[:, rs] = proj(OFF_MK, OFF_MV).T.astype(BF16)
        mv_ref[rs, :] = proj(OFF_MV, OFF_MO).astype(BF16)
        og_ref[rs, :] = _sigmoid(proj(OFF_MO, OFF_GA)).astype(BF16)
        ga_ref[rs, :] = _sigmoid(proj(OFF_GA, OFF_GB)).astype(BF16)
        gb_ref[rs, :] = _sigmoid(proj(OFF_GB, OFF_CQ)).astype(BF16)

    def mla_build(cq, ckv, misc, rs):
        lane = lax.broadcasted_iota(jnp.int32, misc.shape, 1)
        g = misc + gbias_ref[...]
        is_f = (lane >= GATE_LANE0 + 2 * H_M) & (lane < GATE_LANE0 + N_GATES)
        gates_ref[rs, :] = jnp.where(is_f, _log_sigmoid(g), g)
        kro_ref[rs, :] = misc[:, 0:ROPE_DIM]

        ckv_n = _rms(ckv, kvn_ref[...])
        ckv_ref[rs, :] = ckv_n
        ckv_b = ckv_n.astype(BF16)

        ta = ta_ref[rs, :]
        tb = tb_ref[rs, :]
        qf = _bdot(_rms(cq, qn_ref[...]).astype(BF16), wq_ref[...])
        ta8 = jnp.concatenate([ta] * H_A, axis=1)
        tb8 = jnp.concatenate([tb] * H_A, axis=1)
        q = qf * ta8 + pltpu.roll(qf, H_A * HEAD_PAD - ROPE_DIM, 1) * tb8
        q_ref[rs, :] = (q * QK_SCALE).astype(BF16)

        ta_k = jnp.where(lane < NOPE, 0.0, ta)
        kr = misc * ta_k + pltpu.roll(misc, MISC_W - ROPE_DIM, 1) * tb
        kk = _bdot(ckv_b, wk_ref[...]) + jnp.concatenate([kr] * H_A, axis=1)
        k_ref[rs, :] = kk.astype(BF16)
        vt = _bdot(wv_ref[...], ckv_n.T.astype(BF16)).astype(BF16)
        step = min(rows, tkv)
        for off in range(0, rows, step):
            lo = rs.start + off
            v_ref[lo // tkv, :, lo % tkv:lo % tkv + step] = vt[:, off:off + step]

    groups = [slice(r * rows, (r + 1) * rows) for r in range(ROW_SPLITS)]
    hbs = [normed(rs) for rs in groups]
    for hb, rs in zip(hbs, groups):
        small = small_proj(hb)
        big_proj(hb, rs)
        mla_build(*small, rs)


def _stage_a(x, mods, mod_index, pre1, wa, gbias, qn, kvn, wq, wk, wv, ta, tb, table_index, tm, tkv,
             kv_index=lambda i: i, n_kv=None):
    n = x.shape[0]
    n_kv = n if n_kv is None else n_kv
    tile = lambda w: pl.BlockSpec((tm, w), lambda i: (i, 0))
    bf = lambda w: jax.ShapeDtypeStruct((n, w), BF16)
    f32 = lambda w: jax.ShapeDtypeStruct((n, w), F32)
    return pl.pallas_call(
        _stage_a_kernel,
        grid=(n // tm,),
        in_specs=[tile(D_MODEL),
                  pl.BlockSpec((1, N_MOD, D_MODEL), lambda i: (mod_index(i), 0, 0)),
                  _const_spec((1, D_MODEL)),
                  _const_spec((D_MODEL, NA_COLS)),
                  _const_spec((1, MISC_W)),
                  _const_spec((1, Q_LORA)),
                  _const_spec((1, KV_LORA)),
                  _const_spec((Q_LORA, H_A * HEAD_PAD)),
                  _const_spec((KV_LORA, H_A * HEAD_PAD)),
                  _const_spec((H_A * V_DIM, KV_LORA)),
                  pl.BlockSpec((tm, LANE), lambda i: (table_index(i), 0)),
                  pl.BlockSpec((tm, LANE), lambda i: (table_index(i), 0))],
        out_specs=[tile(MLSTM_W), pl.BlockSpec((MLSTM_W, tm), lambda i: (0, i)), tile(MLSTM_W), tile(MLSTM_W),
                   tile(D_MODEL), tile(D_MODEL), tile(MISC_W),
                   tile(H_A * HEAD_PAD), pl.BlockSpec((tm, H_A * HEAD_PAD), lambda i: (kv_index(i), 0)),
                   pl.BlockSpec((tm // tkv, H_A * V_DIM, tkv), lambda i: (kv_index(i), 0, 0)),
                   tile(KV_LORA), tile(ROPE_DIM)],
        out_shape=[bf(MLSTM_W), jax.ShapeDtypeStruct((MLSTM_W, n), BF16), bf(MLSTM_W), bf(MLSTM_W),
                   bf(D_MODEL), bf(D_MODEL), f32(MISC_W),
                   bf(H_A * HEAD_PAD), jax.ShapeDtypeStruct((n_kv, H_A * HEAD_PAD), BF16),
                   jax.ShapeDtypeStruct((n_kv // tkv, H_A * V_DIM, tkv), BF16),
                   f32(KV_LORA), f32(ROPE_DIM)],
        compiler_params=_cparams(("arbitrary",)),
        name="stage_a",
    )(x, mods, pre1, wa, gbias, qn, kvn, wq, wk, wv, ta, tb)


def _kv_cache_kernel(ckv_ref, kr_ref, wk_ref, wv_ref, k_hbm, v_hbm, k_ref, v_ref):
    del k_hbm, v_hbm
    ckv_b = ckv_ref[...].astype(BF16)
    kk = _bdot(ckv_b, wk_ref[...]) + jnp.concatenate([kr_ref[...]] * H_A, axis=1)
    k_ref[...] = kk.astype(BF16)
    v_ref[0] = _bdot(wv_ref[...], ckv_ref[...].T.astype(BF16)).astype(BF16)


def _kv_cache(ckv, kr, wk, wv, k_full, vt_full, tm, kv_index):
    n = ckv.shape[0]
    return pl.pallas_call(
        _kv_cache_kernel,
        grid=(n // tm,),
        in_specs=[pl.BlockSpec((tm, KV_LORA), lambda i: (i, 0)),
                  pl.BlockSpec((tm, LANE), lambda i: (i, 0)),
                  _const_spec((KV_LORA, H_A * HEAD_PAD)),
                  _const_spec((H_A * V_DIM, KV_LORA)),
                  pl.BlockSpec(memory_space=pl.ANY),
                  pl.BlockSpec(memory_space=pl.ANY)],
        out_specs=[pl.BlockSpec((tm, H_A * HEAD_PAD), lambda i: (kv_index(i), 0)),
                   pl.BlockSpec((1, H_A * V_DIM, tm), lambda i: (kv_index(i), 0, 0))],
        out_shape=[jax.ShapeDtypeStruct(k_full.shape, BF16),
                   jax.ShapeDtypeStruct(vt_full.shape, BF16)],
        input_output_aliases={4: 0, 5: 1},
        compiler_params=_cparams(("arbitrary",)),
        name="kv_cache",
    )(ckv, kr, wk, wv, k_full, vt_full)


def _split3(x):
    x1 = x.astype(BF16).astype(F32)
    r = x - x1
    x2 = r.astype(BF16).astype(F32)
    x3 = (r - x2).astype(BF16).astype(F32)
    return x1, x2, x3


def _mlstm_kernel(*refs, has_state):
    if has_state:
        (qf_ref, ktf_ref, vf_ref, gf_ref, qb_ref, ktb_ref, vb_ref, gb_ref, c0_ref, m0_ref,
         hf_ref, hb_ref, cn_ref, nn_ref, mn_ref, c_s, m_s) = refs
    else:
        (qf_ref, ktf_ref, vf_ref, gf_ref, qb_ref, ktb_ref, vb_ref, gb_ref,
         hf_ref, hb_ref, cn_ref, nn_ref, mn_ref, c_s, m_s) = refs
    step = pl.program_id(1)

    @pl.when(step == 0)
    def _():
        if has_state:
            c_s[...] = c0_ref[0]
            m_s[...] = m0_ref[0]
        else:
            c_s[...] = jnp.zeros(c_s.shape, F32)
            m_s[...] = jnp.full(m_s.shape, M_INIT, F32)

    nj = 2 * H_M
    row = lax.broadcasted_iota(jnp.int32, (CHUNK, CHUNK), 0)
    col = lax.broadcasted_iota(jnp.int32, (CHUNK, CHUNK), 1)
    lower = (col <= row).astype(F32)
    upper = (col >= row).astype(F32)
    hi = lax.Precision.HIGHEST
    row8 = lax.broadcasted_iota(jnp.int32, (nj, CHUNK), 0)
    lane8 = lax.broadcasted_iota(jnp.int32, (nj, CHUNK), 1)
    is_fwd = row8 < H_M

    gt_f = gf_ref[...].T
    gt_b = gb_ref[...].T
    gi, gf = GATE_LANE0, GATE_LANE0 + nj
    i8 = jnp.where(is_fwd, gt_f[gi:gi + nj], gt_b[gi:gi + nj])
    f8 = jnp.where(is_fwd, gt_f[gf:gf + nj], gt_b[gf:gf + nj])
    b8 = jnp.where(is_fwd, jnp.dot(f8, upper, preferred_element_type=F32, precision=hi),
                   jnp.dot(f8, lower, preferred_element_type=F32, precision=hi))
    a8 = i8 - b8
    cm8 = a8
    shift = 1
    while shift < CHUNK:
        y_f = jnp.where(lane8 >= shift, pltpu.roll(cm8, shift, 1), -jnp.inf)
        y_b = jnp.where(lane8 < CHUNK - shift, pltpu.roll(cm8, CHUNK - shift, 1), -jnp.inf)
        cm8 = jnp.maximum(cm8, jnp.where(is_fwd, y_f, y_b))
        shift *= 2
    m8 = m_s[...]
    mx8 = jnp.maximum(m8, jnp.max(a8, axis=1, keepdims=True))
    e8 = jnp.exp(a8 - mx8)
    decay8 = jnp.exp(m8 - mx8)
    u8 = jnp.maximum(m8, cm8)
    ma8 = m8 - a8
    m_s[...] = jnp.sum(f8, axis=1, keepdims=True) + mx8

    ones8 = jnp.ones((nj, CHUNK), F32)
    zeros8 = jnp.zeros((nj, CHUNK), F32)
    p_rows = [ones8] * 3 + list(_split3(u8)) + list(_split3(b8))
    p_t = jnp.concatenate(p_rows + [zeros8] * (CHUNK // nj - len(p_rows)), axis=0).T
    q_s = list(_split3(a8)) + [-ones8] * 3
    q_s = jnp.concatenate(q_s + [zeros8] * (CHUNK // nj - len(q_s)), axis=0)
    q_m = jnp.concatenate([zeros8] * 3 + [-ones8] * 6 + [zeros8] * (CHUNK // nj - 9), axis=0)
    q_full = jnp.concatenate([q_s, q_m], axis=1).astype(BF16)
    ones_v = jnp.ones((CHUNK, DH_M), BF16)

    dirs = ((qf_ref, ktf_ref, vf_ref, hf_ref), (qb_ref, ktb_ref, vb_ref, hb_ref))
    chains = [(d, hd) for d in range(2) for hd in range(H_M)]
    first = []
    for d, hd in chains:
        q_ref, kt_ref, v_ref, _ = dirs[d]
        j = d * H_M + hd
        sl = slice(hd * DH_M, (hd + 1) * DH_M)
        q = q_ref[:, sl]
        k_t = kt_ref[sl, :]
        v_aug = jnp.concatenate([v_ref[:, sl], ones_v], axis=1)
        dm = _bdot(jnp.where(col % nj == j, p_t, 0.0).astype(BF16), q_full)
        qk = _bdot(q, k_t)
        c_st = c_s[j]
        qc = _bdot(q, c_st.astype(BF16))
        upd = _bdot((k_t.astype(F32) * e8[j:j + 1, :]).astype(BF16), v_aug)
        dec = decay8[j:j + 1, :]
        c_s[j] = jnp.concatenate([dec, dec], axis=1) * c_st + upd
        first.append((dm, qk, qc, v_aug))
    for (d, hd), (dm, qk, qc, v_aug) in zip(chains, first):
        h_ref = dirs[d][3]
        j = d * H_M + hd
        sl = slice(hd * DH_M, (hd + 1) * DH_M)
        mask = (col <= row) if d == 0 else (col >= row)
        d1 = dm[:, :CHUNK]
        w = jnp.exp(jnp.where(mask, d1, -jnp.inf))
        s_inter = jnp.exp(d1 + ma8[j:j + 1, :])
        intra = _bdot((qk * w).astype(BF16), v_aug)
        num = s_inter * qc[:, :DH_M] + intra[:, :DH_M]
        den = s_inter * qc[:, DH_M:] + intra[:, DH_M:]
        h_ref[:, sl] = num / jnp.maximum(jnp.abs(den), jnp.exp(dm[:, CHUNK:]))

    @pl.when(step == pl.num_programs(1) - 1)
    def _():
        for j in range(nj):
            c_aug = c_s[j]
            cn_ref[0, j] = c_aug[:, :DH_M]
            nn_ref[0, j:j + 1, :] = c_aug[:, DH_M:].T[0:1, :]
        mn_ref[0] = m_s[...]


def _mlstm(mq, mkt, mv, gates, c0, m0, batch, seq):
    has_state = c0 is not None
    nc = seq // CHUNK
    n = batch * seq
    nj = 2 * H_M
    fwd = lambda w: pl.BlockSpec((CHUNK, w), lambda b, c: (b * nc + c, 0))
    bwd = lambda w: pl.BlockSpec((CHUNK, w), lambda b, c: (b * nc + nc - 1 - c, 0))
    fwd_t = pl.BlockSpec((MLSTM_W, CHUNK), lambda b, c: (0, b * nc + c))
    bwd_t = pl.BlockSpec((MLSTM_W, CHUNK), lambda b, c: (0, b * nc + nc - 1 - c))
    st_m = pl.BlockSpec((1, nj, LANE), lambda b, c: (b, 0, 0))
    st_c = pl.BlockSpec((1, nj, DH_M, 2 * DH_M), lambda b, c: (b, 0, 0, 0))
    in_specs = [fwd(MLSTM_W), fwd_t, fwd(MLSTM_W), fwd(MISC_W),
                bwd(MLSTM_W), bwd_t, bwd(MLSTM_W), bwd(MISC_W)]
    args = [mq, mkt, mv, gates, mq, mkt, mv, gates]
    if has_state:
        in_specs += [st_c, st_m]
        args += [c0, m0]
    return pl.pallas_call(
        functools.partial(_mlstm_kernel, has_state=has_state),
        grid=(batch, nc),
        in_specs=in_specs,
        out_specs=[fwd(MLSTM_W), bwd(MLSTM_W),
                   pl.BlockSpec((1, nj, DH_M, DH_M), lambda b, c: (b, 0, 0, 0)), st_m, st_m],
        out_shape=[jax.ShapeDtypeStruct((n, MLSTM_W), F32), jax.ShapeDtypeStruct((n, MLSTM_W), F32),
                   jax.ShapeDtypeStruct((batch, nj, DH_M, DH_M), F32),
                   jax.ShapeDtypeStruct((batch, nj, DH_M), F32),
                   jax.ShapeDtypeStruct((batch, nj, LANE), F32)],
        scratch_shapes=[pltpu.VMEM((nj, DH_M, 2 * DH_M), F32), pltpu.VMEM((nj, LANE), F32)],
        compiler_params=_cparams(("arbitrary", "arbitrary")),
        name="mlstm",
    )(*args)


def _attn_kernel(q_ref, k_ref, vt_ref, o_ref, qt_s, acc_s, m_s, ot_s, *, tiles_per_step):
    n_tiles, _, tk = vt_ref.shape
    qt_s[...] = q_ref[...].astype(F32).T.astype(BF16)
    acc_s[...] = jnp.zeros(acc_s.shape, F32)
    m_s[...] = jnp.full(m_s.shape, -jnp.inf, F32)

    def update(tiles):
        items = [(tile, head) for tile in tiles for head in range(H_A)]

        def scores(item):
            (get_k, _, _), head = item
            return _bdot(get_k(head), qt_s[head * HEAD_PAD:(head + 1) * HEAD_PAD, :])

        pending = [scores(it) for it in items[:QK_LOOKAHEAD]]
        for idx, ((_, get_vt, width), head) in enumerate(items):
            s = pending.pop(0)
            if idx + QK_LOOKAHEAD < len(items):
                pending.append(scores(items[idx + QK_LOOKAHEAD]))
            m_old = m_s[head:head + 1, :]
            m_new = jnp.maximum(m_old, jnp.max(s, axis=0, keepdims=True))
            alpha = jnp.exp2(m_old - m_new)
            p = jnp.exp2((s - m_new).astype(BF16))
            ones = jnp.ones((ONES_ROWS, width), BF16)
            lhs = jnp.concatenate([get_vt(head), ones], axis=0)
            acc_s[head] = alpha * acc_s[head] + _bdot(lhs, p)
            m_s[head:head + 1, :] = m_new

    def main_tile(t):
        rows = pl.ds(pl.multiple_of(t * tk, tk), tk)
        return (lambda h: k_ref[rows, h * HEAD_PAD:(h + 1) * HEAD_PAD],
                lambda h: vt_ref[t, h * V_DIM:(h + 1) * V_DIM, :], tk)

    def body(i, carry):
        update([main_tile(i * tiles_per_step + u) for u in range(tiles_per_step)])
        return carry

    lax.fori_loop(0, n_tiles // tiles_per_step, body, 0)
    for head in range(H_A):
        acc = acc_s[head]
        ot_s[head * V_DIM:(head + 1) * V_DIM, :] = acc[0:V_DIM] * (1.0 / acc[V_DIM:V_DIM + 1])
    o_ref[...] = ot_s[...].T.astype(BF16)


def _attention(q, k, vt, batch, seq, tq, tiles_per_step):
    nq = seq // tq
    n_keys = k.shape[0] // batch
    tk = vt.shape[2]
    n_tiles = n_keys // tk
    assert n_tiles % tiles_per_step == 0
    in_specs = [pl.BlockSpec((tq, H_A * HEAD_PAD), lambda b, i: (b * nq + i, 0)),
                pl.BlockSpec((n_keys, H_A * HEAD_PAD), lambda b, i: (b, 0)),
                pl.BlockSpec((n_tiles, H_A * V_DIM, tk), lambda b, i: (b, 0, 0))]
    return pl.pallas_call(
        functools.partial(_attn_kernel, tiles_per_step=tiles_per_step),
        grid=(batch, nq),
        in_specs=in_specs,
        out_specs=pl.BlockSpec((tq, H_A * V_DIM), lambda b, i: (b * nq + i, 0)),
        out_shape=jax.ShapeDtypeStruct((batch * seq, H_A * V_DIM), BF16),
        scratch_shapes=[pltpu.VMEM((H_A * HEAD_PAD, tq), BF16),
                        pltpu.VMEM((H_A, V_DIM + ONES_ROWS, tq), F32),
                        pltpu.VMEM((H_A, tq), F32),
                        pltpu.VMEM((H_A * V_DIM, tq), F32)],
        compiler_params=_cparams(("arbitrary", "arbitrary")),
        name="mla_attention",
    )(q, k, vt)


def _stage_c_kernel(x_ref, mod_ref, hf_ref, hb_ref, og_ref, attn_ref, ga_ref, gb_ref, hn_ref,
                    post1_ref, pre2_ref, post2_ref, wmo_ref, wao_ref, wout_ref, w1_ref, w2_ref, y_ref):
    mod = mod_ref[0]
    gate1, shift2, scale2, gate2 = mod[2:3], mod[3:4], mod[4:5], mod[5:6]
    hn = hn_ref[...]
    rows = x_ref.shape[0] // ROW_SPLITS
    groups = [slice(r * rows, (r + 1) * rows) for r in range(ROW_SPLITS)]
    n_ff = D_FF // D_MODEL

    def mixer_in(rs):
        hm = hf_ref[rs, :] + hb_ref[rs, :]
        heads = [_rms(hm[:, hd * DH_M:(hd + 1) * DH_M], hn[:, hd * DH_M:(hd + 1) * DH_M]) for hd in range(H_M)]
        return (jnp.concatenate(heads, axis=1) * og_ref[rs, :].astype(F32)).astype(BF16)

    def branch_out(hm, rs):
        return _bdot(hm, wmo_ref[...]), _bdot(attn_ref[rs, :], wao_ref[...])

    def merge(y_m, y_a, rs):
        return (ga_ref[rs, :].astype(F32) * y_m + gb_ref[rs, :].astype(F32) * y_a).astype(BF16)

    def mid(mix, rs):
        x1 = x_ref[rs, :] + gate1 * _rms(mix, post1_ref[...])
        return x1, (_rms(x1, pre2_ref[...]) * (1.0 + scale2) + shift2).astype(BF16)

    def mlp(h2):
        up = lambda c: _bdot(h2, w1_ref[:, c * D_MODEL:(c + 1) * D_MODEL])
        ff = jnp.zeros((rows, D_MODEL), F32)
        nxt = up(0)
        for c in range(n_ff):
            a = jnp.maximum(nxt, 0.0)
            if c + 1 < n_ff:
                nxt = up(c + 1)
            ff = ff + _bdot((a * a).astype(BF16), w2_ref[c * D_MODEL:(c + 1) * D_MODEL, :])
        return ff

    hms = [mixer_in(rs) for rs in groups]
    ys = [branch_out(hm, rs) for hm, rs in zip(hms, groups)]
    mixes = [_bdot(merge(*y, rs), wout_ref[...]) for y, rs in zip(ys, groups)]
    mids = [mid(mix, rs) for mix, rs in zip(mixes, groups)]
    ffs = [mlp(h2) for _, h2 in mids]
    for (x1, _), ff, rs in zip(mids, ffs, groups):
        y_ref[rs, :] = x1 + gate2 * _rms(ff, post2_ref[...])


def _stage_c(x, mods, mod_index, hf, hb, og, attn, ga, gb, hn, post1, pre2, post2, wmo, wao, wout, w1, w2, tm):
    n = x.shape[0]
    tile = lambda w: pl.BlockSpec((tm, w), lambda i: (i, 0))
    return pl.pallas_call(
        _stage_c_kernel,
        grid=(n // tm,),
        in_specs=[tile(D_MODEL),
                  pl.BlockSpec((1, N_MOD, D_MODEL), lambda i: (mod_index(i), 0, 0)),
                  tile(MLSTM_W), tile(MLSTM_W), tile(MLSTM_W), tile(H_A * V_DIM),
                  tile(D_MODEL), tile(D_MODEL),
                  _const_spec((1, MLSTM_W)), _const_spec((1, D_MODEL)), _const_spec((1, D_MODEL)),
                  _const_spec((1, D_MODEL)),
                  _const_spec((MLSTM_W, D_MODEL)), _const_spec((H_A * V_DIM, D_MODEL)),
                  _const_spec((D_MODEL, D_MODEL)), _const_spec((D_MODEL, D_FF)), _const_spec((D_FF, D_MODEL))],
        out_specs=tile(D_MODEL),
        out_shape=jax.ShapeDtypeStruct((n, D_MODEL), F32),
        compiler_params=_cparams(("arbitrary",)),
        name="stage_c",
    )(x, mods, hf, hb, og, attn, ga, gb, hn, post1, pre2, post2, wmo, wao, wout, w1, w2)


def _rope_tables(n_tokens):
    pos = np.arange(n_tokens)
    row = (pos // GRID_W).astype(np.float32)
    col = (pos % GRID_W).astype(np.float32)
    inv = (ROPE_BASE ** (-np.arange(0, AX_DIM, 2, dtype=np.float32) / AX_DIM)).astype(np.float32)
    ang = jnp.concatenate([jnp.asarray(row)[:, None] * inv, jnp.asarray(col)[:, None] * inv], axis=-1)
    cos, sin = jnp.cos(ang), jnp.sin(ang)
    ones = jnp.ones((n_tokens, NOPE), F32)
    zeros = jnp.zeros((n_tokens, NOPE), F32)
    pad = jnp.zeros((n_tokens, HEAD_PAD - NOPE - ROPE_DIM), F32)
    ta = jnp.concatenate([ones, cos, cos, pad], axis=-1)
    tb = jnp.concatenate([zeros, -sin, sin, pad], axis=-1)
    return ta, tb


def _plain_tables(n_tokens):
    ones = jnp.ones((n_tokens, NOPE + ROPE_DIM), F32)
    pad = jnp.zeros((n_tokens, HEAD_PAD - NOPE - ROPE_DIM), F32)
    return jnp.concatenate([ones, pad], axis=-1), jnp.zeros((n_tokens, HEAD_PAD), F32)


def kernel(x_prompt, x_sample, cache_mla_ckv, cache_mla_krope, state_mlstm_C, state_mlstm_n, state_mlstm_m,
           c, c_ctx, w_ada, b_ada, norm_pre1, norm_post1, norm_pre2, norm_post2, w_in, mlstm_gate_b,
           mla_q_norm, mla_kv_norm, w_uq, w_ukv, w_mla_o, mlstm_head_norm, w_mlstm_o, w_out, w_mlp1, w_mlp2):
    bp, sp, _ = x_prompt.shape
    bs, ss, _ = x_sample.shape
    depth = w_in.shape[0]
    past = cache_mla_ckv.shape[2]
    nj = 2 * H_M
    even = np.arange(0, ROPE_DIM, 2)
    odd = np.arange(1, ROPE_DIM, 2)
    perm = np.concatenate([even, odd])
    perm_sw = np.concatenate([odd, even])

    xp = x_prompt.reshape(bp * sp, D_MODEL)
    xs = x_sample.reshape(bs * ss, D_MODEL)
    cc = jnp.zeros((8, D_MODEL), F32).at[:bs].set(c).at[bs].set(c_ctx)
    ta_lat, tb_lat = _rope_tables(ss)
    tm_ctx, tm_lat = 512, 512
    assert past == tm_lat
    ta_ctx, tb_ctx = _plain_tables(tm_ctx)

    new_ckv, new_krope, new_c, new_n, new_m = [], [], [], [], []
    for l in range(depth):
        cols = np.cumsum((MLSTM_W,) * 4 + (N_GATES, Q_LORA, KV_LORA, ROPE_DIM))
        w_mq, w_mk, w_mv, w_mo, w_g, w_cq, w_ckv, w_kr, w_merge = jnp.split(
            w_in[l].astype(BF16), cols.tolist(), axis=1)
        w_g = w_g.reshape(D_MODEL, 2, 2, H_M).transpose(0, 2, 1, 3).reshape(D_MODEL, N_GATES)
        w_misc = jnp.concatenate([w_kr, w_g, jnp.zeros((D_MODEL, MISC_W - 3 * ROPE_DIM - N_GATES), BF16),
                                  w_kr[:, perm], w_kr[:, perm_sw]], axis=1)
        wa = jnp.concatenate([w_mq, w_mk, w_mv, w_mo, w_merge, w_cq, w_ckv, w_misc], axis=1)
        gbias = jnp.zeros((1, MISC_W), F32).at[0, GATE_LANE0:GATE_LANE0 + N_GATES].set(
            mlstm_gate_b[l].transpose(1, 0, 2).reshape(N_GATES))
        uq = w_uq[l]
        wq = jnp.concatenate([uq[..., :NOPE], uq[..., NOPE:][..., perm], uq[..., NOPE:][..., perm_sw]],
                             axis=-1).reshape(Q_LORA, H_A * HEAD_PAD).astype(BF16)
        ukv = w_ukv[l]
        wk = jnp.concatenate([ukv[..., :NOPE], jnp.zeros((KV_LORA, H_A, HEAD_PAD - NOPE), F32)],
                             axis=-1).reshape(KV_LORA, H_A * HEAD_PAD).astype(BF16)
        wv = ukv[..., NOPE:].reshape(KV_LORA, H_A * V_DIM).T.astype(BF16)
        wmo = w_mlstm_o[l].astype(BF16)
        wao = w_mla_o[l].astype(BF16)
        wout = w_out[l].astype(BF16)
        w1 = w_mlp1[l].astype(BF16)
        w2 = w_mlp2[l].astype(BF16)
        pre1, post1 = norm_pre1[l][None], norm_post1[l][None]
        pre2, post2 = norm_pre2[l][None], norm_post2[l][None]
        qn, kvn = mla_q_norm[l][None], mla_kv_norm[l][None]
        hn = mlstm_head_norm[l].reshape(1, MLSTM_W)

        mods = _modulation(cc, w_ada[l], b_ada[l][None]).reshape(8, N_MOD, D_MODEL)

        ctx_mod = lambda i: bs
        a = _stage_a(xp, mods, ctx_mod, pre1, wa, gbias, qn, kvn, wq, wk, wv, ta_ctx, tb_ctx,
                     lambda i: 0, tm_ctx, sp)
        mq, mk, mv, og, ga, gb, gates, q, k, v, ckv_n, kro = a
        hf, hb, c_fin, n_fin, m_fin = _mlstm(mq, mk, mv, gates, None, None, bp, sp)
        attn = _attention(q, k, v, bp, sp, sp, 1)
        xp = _stage_c(xp, mods, ctx_mod, hf, hb, og, attn, ga, gb, hn, post1, pre2, post2,
                      wmo, wao, wout, w1, w2, tm_ctx)
        new_ckv.append(ckv_n.reshape(bp, sp, KV_LORA))
        new_krope.append(kro.reshape(bp, sp, ROPE_DIM))
        new_c.append(c_fin.reshape(bp, 2, H_M, DH_M, DH_M))
        new_n.append(n_fin.reshape(bp, 2, H_M, DH_M))
        new_m.append(m_fin[:, :, 0].reshape(bp, 2, H_M))

        tiles_per_seq = ss // tm_lat
        lat_mod = lambda i: i // tiles_per_seq
        kv_tiles = tiles_per_seq + past // tm_lat
        a = _stage_a(xs, mods, lat_mod, pre1, wa, gbias, qn, kvn, wq, wk, wv, ta_lat, tb_lat,
                     lambda i: i % tiles_per_seq, tm_lat, tm_lat,
                     kv_index=lambda i: (i // tiles_per_seq) * kv_tiles + i % tiles_per_seq,
                     n_kv=bs * kv_tiles * tm_lat)
        mq, mk, mv, og, ga, gb, gates, q, k, v, _, _ = a
        kr_cache = jnp.zeros((bs * past, LANE), F32).at[:, NOPE:NOPE + ROPE_DIM].set(
            cache_mla_krope[:, l].reshape(bs * past, ROPE_DIM)[:, perm])
        k, v = _kv_cache(cache_mla_ckv[:, l].reshape(bs * past, KV_LORA), kr_cache, wk, wv, k, v, past,
                         lambda i: i * kv_tiles + tiles_per_seq)
        n0 = state_mlstm_n[:, l].reshape(bs, nj, DH_M, 1)
        c0 = jnp.concatenate([state_mlstm_C[:, l].reshape(bs, nj, DH_M, DH_M),
                              jnp.broadcast_to(n0, (bs, nj, DH_M, DH_M))], axis=-1)
        m0 = jnp.broadcast_to(state_mlstm_m[:, l].reshape(bs, nj, 1), (bs, nj, LANE))
        hf, hb, _, _, _ = _mlstm(mq, mk, mv, gates, c0, m0, bs, ss)
        attn = _attention(q, k, v, bs, ss, 256, KEY_TILES_PER_STEP)
        xs = _stage_c(xs, mods, lat_mod, hf, hb, og, attn, ga, gb, hn, post1, pre2, post2,
                      wmo, wao, wout, w1, w2, tm_lat)

    return (xp.reshape(bp, sp, D_MODEL), xs.reshape(bs, ss, D_MODEL),
            jnp.stack(new_ckv, axis=1), jnp.stack(new_krope, axis=1), jnp.stack(new_c, axis=1),
            jnp.stack(new_n, axis=1), jnp.stack(new_m, axis=1))
```

```python
import functools

import numpy as np
import jax
import jax.numpy as jnp
from jax import lax
from jax.experimental import pallas as pl
from jax.experimental.pallas import tpu as pltpu

F32 = jnp.float32
BF16 = jnp.bfloat16

D_MODEL = 1024
H_M = 4
DH_M = 128
MLSTM_W = H_M * DH_M
CHUNK = 128
H_A = 8
NOPE = 64
ROPE_DIM = 32
V_DIM = 64
Q_LORA = 384
KV_LORA = 256
AX_DIM = ROPE_DIM // 2
ROPE_BASE = 10000.0
GRID_W = 64
D_FF = 4 * D_MODEL
N_MOD = 6
EPS = 1e-6
N_GATES = 4 * H_M
M_INIT = -1e30

LANE = 128
HEAD_PAD = LANE
MISC_W = LANE
GATE_LANE0 = ROPE_DIM
OFF_MQ, OFF_MK, OFF_MV, OFF_MO, W_BIG_COLS = 0, MLSTM_W, 2 * MLSTM_W, 3 * MLSTM_W, 4 * MLSTM_W
OFF_CQ, OFF_CKV, OFF_MISC = 0, Q_LORA, Q_LORA + KV_LORA
W_SMALL_COLS = OFF_MISC + MISC_W

VMEM_LIMIT = 60 * 1024 * 1024
QK_SCALE = float((NOPE + ROPE_DIM) ** -0.5 * np.log2(np.e))
QK_LOOKAHEAD = 4
ROW_SPLITS = 2
KEY_TILES_PER_STEP = 3
ONES_ROWS = 16


def _cparams(sem):
    return pltpu.CompilerParams(dimension_semantics=sem, vmem_limit_bytes=VMEM_LIMIT)


def _const_spec(shape):
    nd = len(shape)
    return pl.BlockSpec(shape, lambda *_: (0,) * nd, pipeline_mode=pl.Buffered(1))


def _rms(x, w):
    return x * lax.rsqrt(jnp.mean(x * x, axis=-1, keepdims=True) + EPS) * w


def _sigmoid(x):
    return 1.0 / (1.0 + jnp.exp(-x))


def _log_sigmoid(x):
    return jnp.minimum(x, 0.0) - jnp.log(1.0 + jnp.exp(-jnp.abs(x)))


def _bdot(a, b):
    return jnp.dot(a, b, preferred_element_type=F32)


def _mod_kernel(c_ref, w_ref, b_ref, o_ref):
    c = c_ref[...]
    s = c * _sigmoid(c)
    o_ref[...] = jnp.dot(s, w_ref[...], preferred_element_type=F32,
                         precision=lax.Precision.HIGHEST) + b_ref[...]


def _modulation(cc, w_ada, b_ada):
    n_out = w_ada.shape[1]
    tn = 1536
    return pl.pallas_call(
        _mod_kernel,
        grid=(n_out // tn,),
        in_specs=[pl.BlockSpec((8, D_MODEL), lambda j: (0, 0)),
                  pl.BlockSpec((D_MODEL, tn), lambda j: (0, j)),
                  pl.BlockSpec((1, tn), lambda j: (0, j))],
        out_specs=pl.BlockSpec((8, tn), lambda j: (0, j)),
        out_shape=jax.ShapeDtypeStruct((8, n_out), F32),
        compiler_params=_cparams(("arbitrary",)),
        name="modulation",
    )(cc, w_ada, b_ada)


def _stage_a_kernel(x_ref, mod_ref, pre1_ref, wbig_ref, wmerge_ref, wsmall_ref, gbias_ref, qn_ref, kvn_ref,
                    wq_ref, wk_ref, wv_ref,
                    ta_ref, tb_ref,
                    mq_ref, mk_ref, mv_ref, og_ref, ga_ref, gb_ref, gates_ref, q_ref, k_ref, v_ref,
                    ckv_ref, kro_ref):
    mod = mod_ref[0]
    shift1, scale1 = mod[0:1], mod[1:2]
    tm = x_ref.shape[0]
    rows = tm // ROW_SPLITS
    tkv = v_ref.shape[2]

    def normed(rs):
        return (_rms(x_ref[rs, :], pre1_ref[...]) * (1.0 + scale1) + shift1).astype(BF16)

    def small_proj(hb):
        return tuple(_bdot(hb, wsmall_ref[:, lo:hi])
                     for lo, hi in ((OFF_CQ, OFF_CKV), (OFF_CKV, OFF_MISC), (OFF_MISC, W_SMALL_COLS)))

    def big_proj(hb, rs):
        proj = lambda lo, hi: _bdot(hb, wbig_ref[:, lo:hi])
        mq_ref[rs, :] = (proj(OFF_MQ, OFF_MK) * DH_M ** -0.5).astype(BF16)
        mk_ref[:, rs] = proj(OFF_MK, OFF_MV).T.astype(BF16)
        mv_ref[rs, :] = proj(OFF_MV, OFF_MO).astype(BF16)
        og_ref[rs, :] = _sigmoid(proj(OFF_MO, W_BIG_COLS)).astype(BF16)
        ga_ref[rs, :] = _sigmoid(_bdot(hb, wmerge_ref[:, :D_MODEL])).astype(BF16)
        gb_ref[rs, :] = _sigmoid(_bdot(hb, wmerge_ref[:, D_MODEL:])).astype(BF16)

    def mla_build(cq, ckv, misc, rs):
        lane = lax.broadcasted_iota(jnp.int32, misc.shape, 1)
        g = misc + gbias_ref[...]
        is_f = (lane >= GATE_LANE0 + 2 * H_M) & (lane < GATE_LANE0 + N_GATES)
        gates_ref[rs, :] = jnp.where(is_f, _log_sigmoid(g), g)
        kro_ref[rs, :] = misc[:, 0:ROPE_DIM]

        ckv_n = _rms(ckv, kvn_ref[...])
        ckv_ref[rs, :] = ckv_n
        ckv_b = ckv_n.astype(BF16)

        ta = ta_ref[rs, :]
        tb = tb_ref[rs, :]
        qf = _bdot(_rms(cq, qn_ref[...]).astype(BF16), wq_ref[...])
        ta8 = jnp.concatenate([ta] * H_A, axis=1)
        tb8 = jnp.concatenate([tb] * H_A, axis=1)
        q = qf * ta8 + pltpu.roll(qf, H_A * HEAD_PAD - ROPE_DIM, 1) * tb8
        q_ref[rs, :] = (q * QK_SCALE).astype(BF16)

        ta_k = jnp.where(lane < NOPE, 0.0, ta)
        kr = misc * ta_k + pltpu.roll(misc, MISC_W - ROPE_DIM, 1) * tb
        kk = _bdot(ckv_b, wk_ref[...]) + jnp.concatenate([kr] * H_A, axis=1)
        k_ref[rs, :] = kk.astype(BF16)
        vt = _bdot(wv_ref[...], ckv_n.T.astype(BF16)).astype(BF16)
        step = min(rows, tkv)
        for off in range(0, rows, step):
            lo = rs.start + off
            v_ref[lo // tkv, :, lo % tkv:lo % tkv + step] = vt[:, off:off + step]

    groups = [slice(r * rows, (r + 1) * rows) for r in range(ROW_SPLITS)]
    hbs = [normed(rs) for rs in groups]
    for hb, rs in zip(hbs, groups):
        small = small_proj(hb)
        big_proj(hb, rs)
        mla_build(*small, rs)


def _stage_a(x, mods, mod_index, pre1, wa, gbias, qn, kvn, wq, wk, wv, ta, tb, table_index, tm, tkv):
    n = x.shape[0]
    tile = lambda w: pl.BlockSpec((tm, w), lambda i: (i, 0))
    bf = lambda w: jax.ShapeDtypeStruct((n, w), BF16)
    f32 = lambda w: jax.ShapeDtypeStruct((n, w), F32)
    return pl.pallas_call(
        _stage_a_kernel,
        grid=(n // tm,),
        in_specs=[tile(D_MODEL),
                  pl.BlockSpec((1, N_MOD, D_MODEL), lambda i: (mod_index(i), 0, 0)),
                  _const_spec((1, D_MODEL)),
                  _const_spec((D_MODEL, W_BIG_COLS)),
                  _const_spec((D_MODEL, 2 * D_MODEL)),
                  _const_spec((D_MODEL, W_SMALL_COLS)),
                  _const_spec((1, MISC_W)),
                  _const_spec((1, Q_LORA)),
                  _const_spec((1, KV_LORA)),
                  _const_spec((Q_LORA, H_A * HEAD_PAD)),
                  _const_spec((KV_LORA, H_A * HEAD_PAD)),
                  _const_spec((H_A * V_DIM, KV_LORA)),
                  pl.BlockSpec((tm, LANE), lambda i: (table_index(i), 0)),
                  pl.BlockSpec((tm, LANE), lambda i: (table_index(i), 0))],
        out_specs=[tile(MLSTM_W), pl.BlockSpec((MLSTM_W, tm), lambda i: (0, i)), tile(MLSTM_W), tile(MLSTM_W),
                   tile(D_MODEL), tile(D_MODEL), tile(MISC_W),
                   tile(H_A * HEAD_PAD), tile(H_A * HEAD_PAD),
                   pl.BlockSpec((tm // tkv, H_A * V_DIM, tkv), lambda i: (i, 0, 0)),
                   tile(KV_LORA), tile(ROPE_DIM)],
        out_shape=[bf(MLSTM_W), jax.ShapeDtypeStruct((MLSTM_W, n), BF16), bf(MLSTM_W), bf(MLSTM_W),
                   bf(D_MODEL), bf(D_MODEL), f32(MISC_W),
                   bf(H_A * HEAD_PAD), bf(H_A * HEAD_PAD),
                   jax.ShapeDtypeStruct((n // tkv, H_A * V_DIM, tkv), BF16),
                   f32(KV_LORA), f32(ROPE_DIM)],
        compiler_params=_cparams(("arbitrary",)),
        name="stage_a",
    )(x, mods, pre1, *wa, gbias, qn, kvn, wq, wk, wv, ta, tb)


def _kv_cache_kernel(ckv_ref, kr_ref, wk_ref, wv_ref, k_ref, v_ref):
    ckv_b = ckv_ref[...].astype(BF16)
    kk = _bdot(ckv_b, wk_ref[...]) + jnp.concatenate([kr_ref[...]] * H_A, axis=1)
    k_ref[...] = kk.astype(BF16)
    v_ref[0] = _bdot(wv_ref[...], ckv_ref[...].T.astype(BF16)).astype(BF16)


def _kv_cache(ckv, kr, wk, wv, tm):
    n = ckv.shape[0]
    return pl.pallas_call(
        _kv_cache_kernel,
        grid=(n // tm,),
        in_specs=[pl.BlockSpec((tm, KV_LORA), lambda i: (i, 0)),
                  pl.BlockSpec((tm, LANE), lambda i: (i, 0)),
                  _const_spec((KV_LORA, H_A * HEAD_PAD)),
                  _const_spec((H_A * V_DIM, KV_LORA))],
        out_specs=[pl.BlockSpec((tm, H_A * HEAD_PAD), lambda i: (i, 0)),
                   pl.BlockSpec((1, H_A * V_DIM, tm), lambda i: (i, 0, 0))],
        out_shape=[jax.ShapeDtypeStruct((n, H_A * HEAD_PAD), BF16),
                   jax.ShapeDtypeStruct((n // tm, H_A * V_DIM, tm), BF16)],
        compiler_params=_cparams(("arbitrary",)),
        name="kv_cache",
    )(ckv, kr, wk, wv)


def _split3(x):
    x1 = x.astype(BF16).astype(F32)
    r = x - x1
    x2 = r.astype(BF16).astype(F32)
    x3 = (r - x2).astype(BF16).astype(F32)
    return x1, x2, x3


GATE_ROWS = 32
PIECE_GROUPS = 6
GATE_CHUNKS_PER_STEP = 8


def _gate_rows_kernel(g_ref, pt_ref, rows_ref):
    nj = 2 * H_M
    nch = g_ref.shape[0] // CHUNK
    n_rows = nch * nj
    row = lax.broadcasted_iota(jnp.int32, (CHUNK, CHUNK), 0)
    col = lax.broadcasted_iota(jnp.int32, (CHUNK, CHUNK), 1)
    lower = (col <= row).astype(F32)
    upper = (col >= row).astype(F32)
    hi = lax.Precision.HIGHEST
    rowi = lax.broadcasted_iota(jnp.int32, (n_rows, CHUNK), 0)
    lane = lax.broadcasted_iota(jnp.int32, (n_rows, CHUNK), 1)
    is_fwd = rowi % nj < H_M

    gi, gf = GATE_LANE0, GATE_LANE0 + nj
    i_rows, f_rows = [], []
    for c in range(nch):
        g_t = g_ref[c * CHUNK:(c + 1) * CHUNK, :].T
        i_rows.append(g_t[gi:gi + nj])
        f_rows.append(g_t[gf:gf + nj])
    i_all = jnp.concatenate(i_rows, axis=0)
    f_all = jnp.concatenate(f_rows, axis=0)
    b = jnp.where(is_fwd, jnp.dot(f_all, upper, preferred_element_type=F32, precision=hi),
                  jnp.dot(f_all, lower, preferred_element_type=F32, precision=hi))
    a = i_all - b
    cm = a
    shift = 1
    while shift < CHUNK:
        y_f = jnp.where(lane >= shift, pltpu.roll(cm, shift, 1), -jnp.inf)
        y_b = jnp.where(lane < CHUNK - shift, pltpu.roll(cm, CHUNK - shift, 1), -jnp.inf)
        cm = jnp.maximum(cm, jnp.where(is_fwd, y_f, y_b))
        shift *= 2
    a_max = jnp.broadcast_to(jnp.max(a, axis=1, keepdims=True), a.shape)
    f_sum = jnp.broadcast_to(jnp.sum(f_all, axis=1, keepdims=True), a.shape)
    pieces = list(_split3(cm)) + list(_split3(b))
    zeros8 = jnp.zeros((nj, CHUNK), F32)
    for c in range(nch):
        sl = slice(c * nj, (c + 1) * nj)
        stack = jnp.concatenate([p[sl] for p in pieces] + [zeros8] * (CHUNK // nj - PIECE_GROUPS), axis=0)
        pt_ref[c * CHUNK:(c + 1) * CHUNK, :] = stack.T.astype(BF16)
        rows_ref[c] = jnp.concatenate([a[sl], a_max[sl], f_sum[sl], zeros8], axis=0)


def _gate_rows(gates, chunks_per_step):
    n = gates.shape[0]
    tm = chunks_per_step * CHUNK
    return pl.pallas_call(
        _gate_rows_kernel,
        grid=(n // tm,),
        in_specs=[pl.BlockSpec((tm, MISC_W), lambda i: (i, 0))],
        out_specs=[pl.BlockSpec((tm, LANE), lambda i: (i, 0)),
                   pl.BlockSpec((chunks_per_step, GATE_ROWS, CHUNK), lambda i: (i, 0, 0))],
        out_shape=[jax.ShapeDtypeStruct((n, LANE), BF16),
                   jax.ShapeDtypeStruct((n // CHUNK, GATE_ROWS, CHUNK), F32)],
        compiler_params=_cparams(("arbitrary",)),
        name="gate_rows",
    )(gates)


def _mlstm_kernel(*refs, has_state):
    if has_state:
        (qf_ref, ktf_ref, vf_ref, ptf_ref, rf_ref, qb_ref, ktb_ref, vb_ref, ptb_ref, rb_ref, c0_ref, m0_ref,
         hf_ref, hb_ref, cn_ref, nn_ref, mn_ref, c_s, m_s) = refs
    else:
        (qf_ref, ktf_ref, vf_ref, ptf_ref, rf_ref, qb_ref, ktb_ref, vb_ref, ptb_ref, rb_ref,
         hf_ref, hb_ref, cn_ref, nn_ref, mn_ref, c_s, m_s) = refs
    step = pl.program_id(1)

    @pl.when(step == 0)
    def _():
        if has_state:
            c_s[...] = c0_ref[0]
            m_s[...] = m0_ref[0]
        else:
            c_s[...] = jnp.zeros(c_s.shape, F32)
            m_s[...] = jnp.full(m_s.shape, M_INIT, F32)

    nj = 2 * H_M
    row = lax.broadcasted_iota(jnp.int32, (CHUNK, CHUNK), 0)
    col = lax.broadcasted_iota(jnp.int32, (CHUNK, CHUNK), 1)
    is_fwd = lax.broadcasted_iota(jnp.int32, (nj, CHUNK), 0) < H_M
    rows_f, rows_b = rf_ref[0], rb_ref[0]
    pick = lambda g: jnp.where(is_fwd, rows_f[g * nj:(g + 1) * nj], rows_b[g * nj:(g + 1) * nj])
    a8, a_max8, f_sum8 = pick(0), pick(1), pick(2)
    m8 = m_s[...]
    mx8 = jnp.maximum(m8, a_max8)
    e8 = jnp.exp(a8 - mx8)
    decay8 = jnp.exp(m8 - mx8)
    m_s[...] = f_sum8 + mx8

    krow = lax.broadcasted_iota(jnp.int32, (CHUNK, 2 * CHUNK), 0)
    kcol = lax.broadcasted_iota(jnp.int32, (CHUNK, 2 * CHUNK), 1)
    cm_rows = krow < (PIECE_GROUPS // 2) * nj
    b_rows = (krow >= (PIECE_GROUPS // 2) * nj) & (krow < PIECE_GROUPS * nj)
    plane_sel = jnp.where((cm_rows & (kcol < CHUNK)) | (b_rows & (kcol >= CHUNK)), 1.0, 0.0).astype(BF16)
    ones_v = jnp.ones((CHUNK, DH_M), BF16)

    dirs = ((qf_ref, ktf_ref, vf_ref, ptf_ref, hf_ref), (qb_ref, ktb_ref, vb_ref, ptb_ref, hb_ref))
    chains = [(d, hd) for d in range(2) for hd in range(H_M)]
    first = []
    for d, hd in chains:
        q_ref, kt_ref, v_ref, pt_ref, _ = dirs[d]
        j = d * H_M + hd
        sl = slice(hd * DH_M, (hd + 1) * DH_M)
        q = q_ref[:, sl]
        k_t = kt_ref[sl, :]
        v_aug = jnp.concatenate([v_ref[:, sl], ones_v], axis=1)
        planes = _bdot(jnp.where(col % nj == j, pt_ref[...].astype(F32), 0.0).astype(BF16), plane_sel)
        qk = _bdot(q, k_t)
        c_st = c_s[j]
        qc = _bdot(q, c_st.astype(BF16))
        upd = _bdot((k_t.astype(F32) * e8[j:j + 1, :]).astype(BF16), v_aug)
        dec = decay8[j:j + 1, :]
        c_s[j] = jnp.concatenate([dec, dec], axis=1) * c_st + upd
        first.append((planes, qk, qc, v_aug))
    for (d, hd), (planes, qk, qc, v_aug) in zip(chains, first):
        h_ref = dirs[d][4]
        j = d * H_M + hd
        sl = slice(hd * DH_M, (hd + 1) * DH_M)
        mask = (col <= row) if d == 0 else (col >= row)
        m_row = m8[j:j + 1, :]
        u = jnp.maximum(m_row, planes[:, :CHUNK])
        w = jnp.exp(jnp.where(mask, a8[j:j + 1, :] - u, -jnp.inf))
        s_inter = jnp.exp(m_row - u)
        intra = _bdot((qk * w).astype(BF16), v_aug)
        num = s_inter * qc[:, :DH_M] + intra[:, :DH_M]
        den = s_inter * qc[:, DH_M:] + intra[:, DH_M:]
        h_ref[:, sl] = num / jnp.maximum(jnp.abs(den), jnp.exp(-(planes[:, CHUNK:] + u)))

    @pl.when(step == pl.num_programs(1) - 1)
    def _():
        for j in range(nj):
            c_aug = c_s[j]
            cn_ref[0, j] = c_aug[:, :DH_M]
            nn_ref[0, j:j + 1, :] = c_aug[:, DH_M:].T[0:1, :]
        mn_ref[0] = m_s[...]


def _mlstm(mq, mkt, mv, gate_pt, gate_rows, c0, m0, batch, seq):
    has_state = c0 is not None
    nc = seq // CHUNK
    n = batch * seq
    nj = 2 * H_M
    fwd = lambda w: pl.BlockSpec((CHUNK, w), lambda b, c: (b * nc + c, 0))
    bwd = lambda w: pl.BlockSpec((CHUNK, w), lambda b, c: (b * nc + nc - 1 - c, 0))
    fwd_t = pl.BlockSpec((MLSTM_W, CHUNK), lambda b, c: (0, b * nc + c))
    bwd_t = pl.BlockSpec((MLSTM_W, CHUNK), lambda b, c: (0, b * nc + nc - 1 - c))
    st_m = pl.BlockSpec((1, nj, LANE), lambda b, c: (b, 0, 0))
    st_c = pl.BlockSpec((1, nj, DH_M, 2 * DH_M), lambda b, c: (b, 0, 0, 0))
    fwd_r = pl.BlockSpec((1, GATE_ROWS, CHUNK), lambda b, c: (b * nc + c, 0, 0))
    bwd_r = pl.BlockSpec((1, GATE_ROWS, CHUNK), lambda b, c: (b * nc + nc - 1 - c, 0, 0))
    in_specs = [fwd(MLSTM_W), fwd_t, fwd(MLSTM_W), fwd(LANE), fwd_r,
                bwd(MLSTM_W), bwd_t, bwd(MLSTM_W), bwd(LANE), bwd_r]
    args = [mq, mkt, mv, gate_pt, gate_rows, mq, mkt, mv, gate_pt, gate_rows]
    if has_state:
        in_specs += [st_c, st_m]
        args += [c0, m0]
    return pl.pallas_call(
        functools.partial(_mlstm_kernel, has_state=has_state),
        grid=(batch, nc),
        in_specs=in_specs,
        out_specs=[fwd(MLSTM_W), bwd(MLSTM_W),
                   pl.BlockSpec((1, nj, DH_M, DH_M), lambda b, c: (b, 0, 0, 0)), st_m, st_m],
        out_shape=[jax.ShapeDtypeStruct((n, MLSTM_W), F32), jax.ShapeDtypeStruct((n, MLSTM_W), F32),
                   jax.ShapeDtypeStruct((batch, nj, DH_M, DH_M), F32),
                   jax.ShapeDtypeStruct((batch, nj, DH_M), F32),
                   jax.ShapeDtypeStruct((batch, nj, LANE), F32)],
        scratch_shapes=[pltpu.VMEM((nj, DH_M, 2 * DH_M), F32), pltpu.VMEM((nj, LANE), F32)],
        compiler_params=_cparams(("arbitrary", "arbitrary")),
        name="mlstm",
    )(*args)


def _attn_kernel(*refs, tiles_per_step, has_cache):
    if has_cache:
        q_ref, k_ref, vt_ref, kc_ref, vct_ref, o_ref, qt_s, acc_s, m_s, ot_s = refs
    else:
        q_ref, k_ref, vt_ref, o_ref, qt_s, acc_s, m_s, ot_s = refs
    n_tiles, _, tk = vt_ref.shape
    qt_s[...] = q_ref[...].astype(F32).T.astype(BF16)
    acc_s[...] = jnp.zeros(acc_s.shape, F32)
    m_s[...] = jnp.full(m_s.shape, -jnp.inf, F32)

    def update(tiles):
        items = [(tile, head) for tile in tiles for head in range(H_A)]

        def scores(item):
            (get_k, _, _), head = item
            return _bdot(get_k(head), qt_s[head * HEAD_PAD:(head + 1) * HEAD_PAD, :])

        pending = [scores(it) for it in items[:QK_LOOKAHEAD]]
        for idx, ((_, get_vt, width), head) in enumerate(items):
            s = pending.pop(0)
            if idx + QK_LOOKAHEAD < len(items):
                pending.append(scores(items[idx + QK_LOOKAHEAD]))
            m_old = m_s[head:head + 1, :]
            m_new = jnp.maximum(m_old, jnp.max(s, axis=0, keepdims=True))
            alpha = jnp.exp2(m_old - m_new)
            p = jnp.exp2((s - m_new).astype(BF16))
            ones = jnp.ones((ONES_ROWS, width), BF16)
            lhs = jnp.concatenate([get_vt(head), ones], axis=0)
            acc_s[head] = alpha * acc_s[head] + _bdot(lhs, p)
            m_s[head:head + 1, :] = m_new

    def main_tile(t):
        start = t * tk if isinstance(t, int) else pl.multiple_of(t * tk, tk)
        rows = pl.ds(start, tk)
        return (lambda h: k_ref[rows, h * HEAD_PAD:(h + 1) * HEAD_PAD],
                lambda h: vt_ref[t, h * V_DIM:(h + 1) * V_DIM, :], tk)

    def body(i, carry):
        update([main_tile(i * tiles_per_step + u) for u in range(tiles_per_step)])
        return carry

    tail = []
    if has_cache:
        tail = [(lambda h: kc_ref[:, h * HEAD_PAD:(h + 1) * HEAD_PAD],
                 lambda h: vct_ref[0, h * V_DIM:(h + 1) * V_DIM, :], kc_ref.shape[0])]
    n_loop = (n_tiles + len(tail) - tiles_per_step) // tiles_per_step
    tail = [main_tile(t) for t in range(n_loop * tiles_per_step, n_tiles)] + tail
    if n_loop:
        lax.fori_loop(0, n_loop, body, 0)
    update(tail)
    for head in range(H_A):
        acc = acc_s[head]
        ot_s[head * V_DIM:(head + 1) * V_DIM, :] = acc[0:V_DIM] * (1.0 / acc[V_DIM:V_DIM + 1])
    o_ref[...] = ot_s[...].T.astype(BF16)


def _attention(q, k, vt, kc, vct, batch, seq, tq, tiles_per_step):
    nq = seq // tq
    tk = vt.shape[2]
    n_tiles = seq // tk
    has_cache = kc is not None
    assert (n_tiles + has_cache) % tiles_per_step == 0
    in_specs = [pl.BlockSpec((tq, H_A * HEAD_PAD), lambda b, i: (b * nq + i, 0)),
                pl.BlockSpec((seq, H_A * HEAD_PAD), lambda b, i: (b, 0)),
                pl.BlockSpec((n_tiles, H_A * V_DIM, tk), lambda b, i: (b, 0, 0))]
    args = [q, k, vt]
    if has_cache:
        past = kc.shape[0] // batch
        in_specs += [pl.BlockSpec((past, H_A * HEAD_PAD), lambda b, i: (b, 0)),
                     pl.BlockSpec((1, H_A * V_DIM, past), lambda b, i: (b, 0, 0))]
        args += [kc, vct]
    return pl.pallas_call(
        functools.partial(_attn_kernel, tiles_per_step=tiles_per_step, has_cache=has_cache),
        grid=(batch, nq),
        in_specs=in_specs,
        out_specs=pl.BlockSpec((tq, H_A * V_DIM), lambda b, i: (b * nq + i, 0)),
        out_shape=jax.ShapeDtypeStruct((batch * seq, H_A * V_DIM), BF16),
        scratch_shapes=[pltpu.VMEM((H_A * HEAD_PAD, tq), BF16),
                        pltpu.VMEM((H_A, V_DIM + ONES_ROWS, tq), F32),
                        pltpu.VMEM((H_A, tq), F32),
                        pltpu.VMEM((H_A * V_DIM, tq), F32)],
        compiler_params=_cparams(("arbitrary", "arbitrary")),
        name="mla_attention",
    )(*args)


def _stage_c_kernel(x_ref, mod_ref, hf_ref, hb_ref, og_ref, attn_ref, ga_ref, gb_ref, hn_ref,
                    post1_ref, pre2_ref, post2_ref, wmo_ref, wao_ref, wout_ref, w1_ref, w2_ref, y_ref):
    mod = mod_ref[0]
    gate1, shift2, scale2, gate2 = mod[2:3], mod[3:4], mod[4:5], mod[5:6]
    hn = hn_ref[...]
    rows = x_ref.shape[0] // ROW_SPLITS
    groups = [slice(r * rows, (r + 1) * rows) for r in range(ROW_SPLITS)]
    n_ff = D_FF // D_MODEL

    def mixer_in(rs):
        hm = hf_ref[rs, :] + hb_ref[rs, :]
        heads = [_rms(hm[:, hd * DH_M:(hd + 1) * DH_M], hn[:, hd * DH_M:(hd + 1) * DH_M]) for hd in range(H_M)]
        return (jnp.concatenate(heads, axis=1) * og_ref[rs, :].astype(F32)).astype(BF16)

    def branch_out(hm, rs):
        return _bdot(hm, wmo_ref[...]), _bdot(attn_ref[rs, :], wao_ref[...])

    def merge(y_m, y_a, rs):
        return (ga_ref[rs, :].astype(F32) * y_m + gb_ref[rs, :].astype(F32) * y_a).astype(BF16)

    def mid(mix, rs):
        x1 = x_ref[rs, :] + gate1 * _rms(mix, post1_ref[...])
        return x1, (_rms(x1, pre2_ref[...]) * (1.0 + scale2) + shift2).astype(BF16)

    def mlp(h2):
        up = lambda c: _bdot(h2, w1_ref[:, c * D_MODEL:(c + 1) * D_MODEL])
        ff = jnp.zeros((rows, D_MODEL), F32)
        nxt = up(0)
        for c in range(n_ff):
            a = jnp.maximum(nxt, 0.0)
            if c + 1 < n_ff:
                nxt = up(c + 1)
            ff = ff + _bdot((a * a).astype(BF16), w2_ref[c * D_MODEL:(c + 1) * D_MODEL, :])
        return ff

    hms = [mixer_in(rs) for rs in groups]
    ys = [branch_out(hm, rs) for hm, rs in zip(hms, groups)]
    mixes = [_bdot(merge(*y, rs), wout_ref[...]) for y, rs in zip(ys, groups)]
    mids = [mid(mix, rs) for mix, rs in zip(mixes, groups)]
    ffs = [mlp(h2) for _, h2 in mids]
    for (x1, _), ff, rs in zip(mids, ffs, groups):
        y_ref[rs, :] = x1 + gate2 * _rms(ff, post2_ref[...])


def _stage_c(x, mods, mod_index, hf, hb, og, attn, ga, gb, hn, post1, pre2, post2, wmo, wao, wout, w1, w2, tm):
    n = x.shape[0]
    tile = lambda w: pl.BlockSpec((tm, w), lambda i: (i, 0))
    return pl.pallas_call(
        _stage_c_kernel,
        grid=(n // tm,),
        in_specs=[tile(D_MODEL),
                  pl.BlockSpec((1, N_MOD, D_MODEL), lambda i: (mod_index(i), 0, 0)),
                  tile(MLSTM_W), tile(MLSTM_W), tile(MLSTM_W), tile(H_A * V_DIM),
                  tile(D_MODEL), tile(D_MODEL),
                  _const_spec((1, MLSTM_W)), _const_spec((1, D_MODEL)), _const_spec((1, D_MODEL)),
                  _const_spec((1, D_MODEL)),
                  _const_spec((MLSTM_W, D_MODEL)), _const_spec((H_A * V_DIM, D_MODEL)),
                  _const_spec((D_MODEL, D_MODEL)), _const_spec((D_MODEL, D_FF)), _const_spec((D_FF, D_MODEL))],
        out_specs=tile(D_MODEL),
        out_shape=jax.ShapeDtypeStruct((n, D_MODEL), F32),
        compiler_params=_cparams(("arbitrary",)),
        name="stage_c",
    )(x, mods, hf, hb, og, attn, ga, gb, hn, post1, pre2, post2, wmo, wao, wout, w1, w2)


def _rope_tables(n_tokens):
    pos = np.arange(n_tokens)
    row = (pos // GRID_W).astype(np.float32)
    col = (pos % GRID_W).astype(np.float32)
    inv = (ROPE_BASE ** (-np.arange(0, AX_DIM, 2, dtype=np.float32) / AX_DIM)).astype(np.float32)
    ang = jnp.concatenate([jnp.asarray(row)[:, None] * inv, jnp.asarray(col)[:, None] * inv], axis=-1)
    cos, sin = jnp.cos(ang), jnp.sin(ang)
    ones = jnp.ones((n_tokens, NOPE), F32)
    zeros = jnp.zeros((n_tokens, NOPE), F32)
    pad = jnp.zeros((n_tokens, HEAD_PAD - NOPE - ROPE_DIM), F32)
    ta = jnp.concatenate([ones, cos, cos, pad], axis=-1)
    tb = jnp.concatenate([zeros, -sin, sin, pad], axis=-1)
    return ta, tb


def _plain_tables(n_tokens):
    ones = jnp.ones((n_tokens, NOPE + ROPE_DIM), F32)
    pad = jnp.zeros((n_tokens, HEAD_PAD - NOPE - ROPE_DIM), F32)
    return jnp.concatenate([ones, pad], axis=-1), jnp.zeros((n_tokens, HEAD_PAD), F32)


def kernel(x_prompt, x_sample, cache_mla_ckv, cache_mla_krope, state_mlstm_C, state_mlstm_n, state_mlstm_m,
           c, c_ctx, w_ada, b_ada, norm_pre1, norm_post1, norm_pre2, norm_post2, w_in, mlstm_gate_b,
           mla_q_norm, mla_kv_norm, w_uq, w_ukv, w_mla_o, mlstm_head_norm, w_mlstm_o, w_out, w_mlp1, w_mlp2):
    bp, sp, _ = x_prompt.shape
    bs, ss, _ = x_sample.shape
    depth = w_in.shape[0]
    past = cache_mla_ckv.shape[2]
    nj = 2 * H_M
    even = np.arange(0, ROPE_DIM, 2)
    odd = np.arange(1, ROPE_DIM, 2)
    perm = np.concatenate([even, odd])
    perm_sw = np.concatenate([odd, even])

    xp = x_prompt.reshape(bp * sp, D_MODEL)
    xs = x_sample.reshape(bs * ss, D_MODEL)
    cc = jnp.zeros((8, D_MODEL), F32).at[:bs].set(c).at[bs].set(c_ctx)
    ta_lat, tb_lat = _rope_tables(ss)
    tm_ctx, tm_lat = 512, 512
    assert past == tm_lat
    ta_ctx, tb_ctx = _plain_tables(tm_ctx)

    new_ckv, new_krope, new_c, new_n, new_m = [], [], [], [], []
    for l in range(depth):
        cols = np.cumsum((W_BIG_COLS, N_GATES, Q_LORA, KV_LORA, ROPE_DIM))
        w_big, w_g, w_cq, w_ckv, w_kr, w_merge = jnp.split(w_in[l].astype(BF16), cols.tolist(), axis=1)
        w_g = w_g.reshape(D_MODEL, 2, 2, H_M).transpose(0, 2, 1, 3).reshape(D_MODEL, N_GATES)
        w_misc = jnp.concatenate([w_kr, w_g, jnp.zeros((D_MODEL, MISC_W - 3 * ROPE_DIM - N_GATES), BF16),
                                  w_kr[:, perm], w_kr[:, perm_sw]], axis=1)
        wa = (w_big, w_merge, jnp.concatenate([w_cq, w_ckv, w_misc], axis=1))
        gbias = jnp.zeros((1, MISC_W), F32).at[0, GATE_LANE0:GATE_LANE0 + N_GATES].set(
            mlstm_gate_b[l].transpose(1, 0, 2).reshape(N_GATES))
        uq = w_uq[l]
        wq = jnp.concatenate([uq[..., :NOPE], uq[..., NOPE:][..., perm], uq[..., NOPE:][..., perm_sw]],
                             axis=-1).reshape(Q_LORA, H_A * HEAD_PAD).astype(BF16)
        ukv = w_ukv[l]
        wk = jnp.concatenate([ukv[..., :NOPE], jnp.zeros((KV_LORA, H_A, HEAD_PAD - NOPE), F32)],
                             axis=-1).reshape(KV_LORA, H_A * HEAD_PAD).astype(BF16)
        wv = ukv[..., NOPE:].reshape(KV_LORA, H_A * V_DIM).T.astype(BF16)
        wmo = w_mlstm_o[l].astype(BF16)
        wao = w_mla_o[l].astype(BF16)
        wout = w_out[l].astype(BF16)
        w1 = w_mlp1[l].astype(BF16)
        w2 = w_mlp2[l].astype(BF16)
        pre1, post1 = norm_pre1[l][None], norm_post1[l][None]
        pre2, post2 = norm_pre2[l][None], norm_post2[l][None]
        qn, kvn = mla_q_norm[l][None], mla_kv_norm[l][None]
        hn = mlstm_head_norm[l].reshape(1, MLSTM_W)

        mods = _modulation(cc, w_ada[l], b_ada[l][None]).reshape(8, N_MOD, D_MODEL)

        ctx_mod = lambda i: bs
        a = _stage_a(xp, mods, ctx_mod, pre1, wa, gbias, qn, kvn, wq, wk, wv, ta_ctx, tb_ctx,
                     lambda i: 0, tm_ctx, sp)
        mq, mk, mv, og, ga, gb, gates, q, k, v, ckv_n, kro = a
        hf, hb, c_fin, n_fin, m_fin = _mlstm(mq, mk, mv, *_gate_rows(gates, GATE_CHUNKS_PER_STEP),
                                             None, None, bp, sp)
        attn = _attention(q, k, v, None, None, bp, sp, sp, 1)
        xp = _stage_c(xp, mods, ctx_mod, hf, hb, og, attn, ga, gb, hn, post1, pre2, post2,
                      wmo, wao, wout, w1, w2, tm_ctx)
        new_ckv.append(ckv_n.reshape(bp, sp, KV_LORA))
        new_krope.append(kro.reshape(bp, sp, ROPE_DIM))
        new_c.append(c_fin.reshape(bp, 2, H_M, DH_M, DH_M))
        new_n.append(n_fin.reshape(bp, 2, H_M, DH_M))
        new_m.append(m_fin[:, :, 0].reshape(bp, 2, H_M))

        tiles_per_seq = ss // tm_lat
        lat_mod = lambda i: i // tiles_per_seq
        a = _stage_a(xs, mods, lat_mod, pre1, wa, gbias, qn, kvn, wq, wk, wv, ta_lat, tb_lat,
                     lambda i: i % tiles_per_seq, tm_lat, tm_lat)
        mq, mk, mv, og, ga, gb, gates, q, k, v, _, _ = a
        kr_cache = jnp.zeros((bs * past, LANE), F32).at[:, NOPE:NOPE + ROPE_DIM].set(
            cache_mla_krope[:, l].reshape(bs * past, ROPE_DIM)[:, perm])
        kc, vc = _kv_cache(cache_mla_ckv[:, l].reshape(bs * past, KV_LORA), kr_cache, wk, wv, past)
        n0 = state_mlstm_n[:, l].reshape(bs, nj, DH_M, 1)
        c0 = jnp.concatenate([state_mlstm_C[:, l].reshape(bs, nj, DH_M, DH_M),
                              jnp.broadcast_to(n0, (bs, nj, DH_M, DH_M))], axis=-1)
        m0 = jnp.broadcast_to(state_mlstm_m[:, l].reshape(bs, nj, 1), (bs, nj, LANE))
        hf, hb, _, _, _ = _mlstm(mq, mk, mv, *_gate_rows(gates, GATE_CHUNKS_PER_STEP), c0, m0, bs, ss)
        attn = _attention(q, k, v, kc, vc, bs, ss, 256, KEY_TILES_PER_STEP)
        xs = _stage_c(xs, mods, lat_mod, hf, hb, og, attn, ga, gb, hn, post1, pre2, post2,
                      wmo, wao, wout, w1, w2, tm_lat)

    return (xp.reshape(bp, sp, D_MODEL), xs.reshape(bs, ss, D_MODEL),
            jnp.stack(new_ckv, axis=1), jnp.stack(new_krope, axis=1), jnp.stack(new_c, axis=1),
            jnp.stack(new_n, axis=1), jnp.stack(new_m, axis=1))
```

```python
import functools

import numpy as np
import jax
import jax.numpy as jnp
from jax import lax
from jax.experimental import pallas as pl
from jax.experimental.pallas import tpu as pltpu

F32 = jnp.float32
BF16 = jnp.bfloat16

D_MODEL = 1024
H_M = 4
DH_M = 128
MLSTM_W = H_M * DH_M
CHUNK = 128
H_A = 8
NOPE = 64
ROPE_DIM = 32
V_DIM = 64
Q_LORA = 384
KV_LORA = 256
AX_DIM = ROPE_DIM // 2
ROPE_BASE = 10000.0
GRID_W = 64
D_FF = 4 * D_MODEL
N_MOD = 6
EPS = 1e-6
N_GATES = 4 * H_M
M_INIT = -1e30

LANE = 128
HEAD_PAD = LANE
MISC_W = LANE
GATE_LANE0 = ROPE_DIM
OFF_MQ, OFF_MK, OFF_MV, OFF_MO, W_BIG_COLS = 0, MLSTM_W, 2 * MLSTM_W, 3 * MLSTM_W, 4 * MLSTM_W
OFF_CQ, OFF_CKV, OFF_MISC = 0, Q_LORA, Q_LORA + KV_LORA
W_SMALL_COLS = OFF_MISC + MISC_W

VMEM_LIMIT = 60 * 1024 * 1024
QK_SCALE = float((NOPE + ROPE_DIM) ** -0.5 * np.log2(np.e))
QK_LOOKAHEAD = 3
ATTN_KEY_TILE = 512
ROW_SPLITS = 2
KEY_TILES_PER_STEP = 3
ONES_ROWS = 16


def _cparams(sem):
    return pltpu.CompilerParams(dimension_semantics=sem, vmem_limit_bytes=VMEM_LIMIT)


def _const_spec(shape):
    nd = len(shape)
    return pl.BlockSpec(shape, lambda *_: (0,) * nd, pipeline_mode=pl.Buffered(1))


def _rms(x, w):
    return x * lax.rsqrt(jnp.mean(x * x, axis=-1, keepdims=True) + EPS) * w


def _sigmoid(x):
    return 1.0 / (1.0 + jnp.exp(-x))


def _log_sigmoid(x):
    return jnp.minimum(x, 0.0) - jnp.log(1.0 + jnp.exp(-jnp.abs(x)))


def _bdot(a, b):
    return jnp.dot(a, b, preferred_element_type=F32)


def _mod_kernel(c_ref, w_ref, b_ref, o_ref):
    c = c_ref[...]
    s = c * _sigmoid(c)
    o_ref[...] = jnp.dot(s, w_ref[...], preferred_element_type=F32,
                         precision=lax.Precision.HIGHEST) + b_ref[...]


def _modulation(cc, w_ada, b_ada):
    n_out = w_ada.shape[1]
    tn = 1536
    return pl.pallas_call(
        _mod_kernel,
        grid=(n_out // tn,),
        in_specs=[pl.BlockSpec((8, D_MODEL), lambda j: (0, 0)),
                  pl.BlockSpec((D_MODEL, tn), lambda j: (0, j)),
                  pl.BlockSpec((1, tn), lambda j: (0, j))],
        out_specs=pl.BlockSpec((8, tn), lambda j: (0, j)),
        out_shape=jax.ShapeDtypeStruct((8, n_out), F32),
        compiler_params=_cparams(("arbitrary",)),
        name="modulation",
    )(cc, w_ada, b_ada)


def _stage_a_kernel(x_ref, mod_ref, pre1_ref, wbig_ref, wmerge_ref, wsmall_ref, gbias_ref, qn_ref, kvn_ref,
                    wq_ref, wk_ref, wv_ref,
                    ta_ref, tb_ref,
                    mq_ref, mk_ref, mv_ref, og_ref, ga_ref, gb_ref, gates_ref, q_ref, k_ref, v_ref,
                    ckv_ref, kro_ref):
    mod = mod_ref[0]
    shift1, scale1 = mod[0:1], mod[1:2]
    tm = x_ref.shape[0]
    rows = tm // ROW_SPLITS
    tkv = v_ref.shape[2]

    def normed(rs):
        return (_rms(x_ref[rs, :], pre1_ref[...]) * (1.0 + scale1) + shift1).astype(BF16)

    def small_proj(hb):
        return tuple(_bdot(hb, wsmall_ref[:, lo:hi])
                     for lo, hi in ((OFF_CQ, OFF_CKV), (OFF_CKV, OFF_MISC), (OFF_MISC, W_SMALL_COLS)))

    def big_proj(hb, rs):
        proj = lambda lo, hi: _bdot(hb, wbig_ref[:, lo:hi])
        mq_ref[rs, :] = (proj(OFF_MQ, OFF_MK) * DH_M ** -0.5).astype(BF16)
        mk_ref[:, rs] = proj(OFF_MK, OFF_MV).T.astype(BF16)
        mv_ref[rs, :] = proj(OFF_MV, OFF_MO).astype(BF16)
        og_ref[rs, :] = _sigmoid(proj(OFF_MO, W_BIG_COLS)).astype(BF16)
        ga_ref[rs, :] = _sigmoid(_bdot(hb, wmerge_ref[:, :D_MODEL])).astype(BF16)
        gb_ref[rs, :] = _sigmoid(_bdot(hb, wmerge_ref[:, D_MODEL:])).astype(BF16)

    def mla_build(cq, ckv, misc, rs):
        lane = lax.broadcasted_iota(jnp.int32, misc.shape, 1)
        g = misc + gbias_ref[...]
        is_f = (lane >= GATE_LANE0 + 2 * H_M) & (lane < GATE_LANE0 + N_GATES)
        gates_ref[rs, :] = jnp.where(is_f, _log_sigmoid(g), g)
        kro_ref[rs, :] = misc[:, 0:ROPE_DIM]

        ckv_n = _rms(ckv, kvn_ref[...])
        ckv_ref[rs, :] = ckv_n
        ckv_b = ckv_n.astype(BF16)

        ta = ta_ref[rs, :]
        tb = tb_ref[rs, :]
        qf = _bdot(_rms(cq, qn_ref[...]).astype(BF16), wq_ref[...])
        ta8 = jnp.concatenate([ta] * H_A, axis=1)
        tb8 = jnp.concatenate([tb] * H_A, axis=1)
        q = qf * ta8 + pltpu.roll(qf, H_A * HEAD_PAD - ROPE_DIM, 1) * tb8
        q_ref[rs, :] = (q * QK_SCALE).astype(BF16)

        ta_k = jnp.where(lane < NOPE, 0.0, ta)
        kr = misc * ta_k + pltpu.roll(misc, MISC_W - ROPE_DIM, 1) * tb
        kk = _bdot(ckv_b, wk_ref[...]) + jnp.concatenate([kr] * H_A, axis=1)
        k_ref[rs, :] = kk.astype(BF16)
        vt = _bdot(wv_ref[...], ckv_n.T.astype(BF16)).astype(BF16)
        step = min(rows, tkv)
        for off in range(0, rows, step):
            lo = rs.start + off
            v_ref[lo // tkv, :, lo % tkv:lo % tkv + step] = vt[:, off:off + step]

    groups = [slice(r * rows, (r + 1) * rows) for r in range(ROW_SPLITS)]
    hbs = [normed(rs) for rs in groups]
    for hb, rs in zip(hbs, groups):
        small = small_proj(hb)
        big_proj(hb, rs)
        mla_build(*small, rs)


def _stage_a(x, mods, mod_index, pre1, wa, gbias, qn, kvn, wq, wk, wv, ta, tb, table_index, tm, tkv):
    n = x.shape[0]
    tile = lambda w: pl.BlockSpec((tm, w), lambda i: (i, 0))
    bf = lambda w: jax.ShapeDtypeStruct((n, w), BF16)
    f32 = lambda w: jax.ShapeDtypeStruct((n, w), F32)
    return pl.pallas_call(
        _stage_a_kernel,
        grid=(n // tm,),
        in_specs=[tile(D_MODEL),
                  pl.BlockSpec((1, N_MOD, D_MODEL), lambda i: (mod_index(i), 0, 0)),
                  _const_spec((1, D_MODEL)),
                  _const_spec((D_MODEL, W_BIG_COLS)),
                  _const_spec((D_MODEL, 2 * D_MODEL)),
                  _const_spec((D_MODEL, W_SMALL_COLS)),
                  _const_spec((1, MISC_W)),
                  _const_spec((1, Q_LORA)),
                  _const_spec((1, KV_LORA)),
                  _const_spec((Q_LORA, H_A * HEAD_PAD)),
                  _const_spec((KV_LORA, H_A * HEAD_PAD)),
                  _const_spec((H_A * V_DIM, KV_LORA)),
                  pl.BlockSpec((tm, LANE), lambda i: (table_index(i), 0)),
                  pl.BlockSpec((tm, LANE), lambda i: (table_index(i), 0))],
        out_specs=[tile(MLSTM_W), pl.BlockSpec((MLSTM_W, tm), lambda i: (0, i)), tile(MLSTM_W), tile(MLSTM_W),
                   tile(D_MODEL), tile(D_MODEL), tile(MISC_W),
                   tile(H_A * HEAD_PAD), tile(H_A * HEAD_PAD),
                   pl.BlockSpec((tm // tkv, H_A * V_DIM, tkv), lambda i: (i, 0, 0)),
                   tile(KV_LORA), tile(ROPE_DIM)],
        out_shape=[bf(MLSTM_W), jax.ShapeDtypeStruct((MLSTM_W, n), BF16), bf(MLSTM_W), bf(MLSTM_W),
                   bf(D_MODEL), bf(D_MODEL), f32(MISC_W),
                   bf(H_A * HEAD_PAD), bf(H_A * HEAD_PAD),
                   jax.ShapeDtypeStruct((n // tkv, H_A * V_DIM, tkv), BF16),
                   f32(KV_LORA), f32(ROPE_DIM)],
        compiler_params=_cparams(("arbitrary",)),
        name="stage_a",
    )(x, mods, pre1, *wa, gbias, qn, kvn, wq, wk, wv, ta, tb)


def _kv_cache_kernel(ckv_ref, kr_ref, wk_ref, wv_ref, k_ref, v_ref):
    ckv_b = ckv_ref[...].astype(BF16)
    kk = _bdot(ckv_b, wk_ref[...]) + jnp.concatenate([kr_ref[...]] * H_A, axis=1)
    k_ref[...] = kk.astype(BF16)
    v_ref[0] = _bdot(wv_ref[...], ckv_ref[...].T.astype(BF16)).astype(BF16)


def _kv_cache(ckv, kr, wk, wv, tm):
    n = ckv.shape[0]
    return pl.pallas_call(
        _kv_cache_kernel,
        grid=(n // tm,),
        in_specs=[pl.BlockSpec((tm, KV_LORA), lambda i: (i, 0)),
                  pl.BlockSpec((tm, LANE), lambda i: (i, 0)),
                  _const_spec((KV_LORA, H_A * HEAD_PAD)),
                  _const_spec((H_A * V_DIM, KV_LORA))],
        out_specs=[pl.BlockSpec((tm, H_A * HEAD_PAD), lambda i: (i, 0)),
                   pl.BlockSpec((1, H_A * V_DIM, tm), lambda i: (i, 0, 0))],
        out_shape=[jax.ShapeDtypeStruct((n, H_A * HEAD_PAD), BF16),
                   jax.ShapeDtypeStruct((n // tm, H_A * V_DIM, tm), BF16)],
        compiler_params=_cparams(("arbitrary",)),
        name="kv_cache",
    )(ckv, kr, wk, wv)


def _split3(x):
    x1 = x.astype(BF16).astype(F32)
    r = x - x1
    x2 = r.astype(BF16).astype(F32)
    x3 = (r - x2).astype(BF16).astype(F32)
    return x1, x2, x3


GATE_ROWS = 32
PIECE_GROUPS = 6
GATE_CHUNKS_PER_STEP = 8


def _gate_rows_kernel(g_ref, pt_ref, rows_ref):
    nj = 2 * H_M
    nch = g_ref.shape[0] // CHUNK
    n_rows = nch * nj
    row = lax.broadcasted_iota(jnp.int32, (CHUNK, CHUNK), 0)
    col = lax.broadcasted_iota(jnp.int32, (CHUNK, CHUNK), 1)
    lower = (col <= row).astype(F32)
    upper = (col >= row).astype(F32)
    hi = lax.Precision.HIGHEST
    rowi = lax.broadcasted_iota(jnp.int32, (n_rows, CHUNK), 0)
    lane = lax.broadcasted_iota(jnp.int32, (n_rows, CHUNK), 1)
    is_fwd = rowi % nj < H_M

    gi, gf = GATE_LANE0, GATE_LANE0 + nj
    i_rows, f_rows = [], []
    for c in range(nch):
        g_t = g_ref[c * CHUNK:(c + 1) * CHUNK, :].T
        i_rows.append(g_t[gi:gi + nj])
        f_rows.append(g_t[gf:gf + nj])
    i_all = jnp.concatenate(i_rows, axis=0)
    f_all = jnp.concatenate(f_rows, axis=0)
    b = jnp.where(is_fwd, jnp.dot(f_all, upper, preferred_element_type=F32, precision=hi),
                  jnp.dot(f_all, lower, preferred_element_type=F32, precision=hi))
    a = i_all - b
    cm = a
    shift = 1
    while shift < CHUNK:
        y_f = jnp.where(lane >= shift, pltpu.roll(cm, shift, 1), -jnp.inf)
        y_b = jnp.where(lane < CHUNK - shift, pltpu.roll(cm, CHUNK - shift, 1), -jnp.inf)
        cm = jnp.maximum(cm, jnp.where(is_fwd, y_f, y_b))
        shift *= 2
    a_max = jnp.broadcast_to(jnp.max(a, axis=1, keepdims=True), a.shape)
    f_sum = jnp.broadcast_to(jnp.sum(f_all, axis=1, keepdims=True), a.shape)
    pieces = list(_split3(cm)) + list(_split3(b))
    zeros8 = jnp.zeros((nj, CHUNK), F32)
    for c in range(nch):
        sl = slice(c * nj, (c + 1) * nj)
        stack = jnp.concatenate([p[sl] for p in pieces] + [zeros8] * (CHUNK // nj - PIECE_GROUPS), axis=0)
        pt_ref[c * CHUNK:(c + 1) * CHUNK, :] = stack.T.astype(BF16)
        rows_ref[c] = jnp.concatenate([a[sl], a_max[sl], f_sum[sl], zeros8], axis=0)


def _gate_rows(gates, chunks_per_step):
    n = gates.shape[0]
    tm = chunks_per_step * CHUNK
    return pl.pallas_call(
        _gate_rows_kernel,
        grid=(n // tm,),
        in_specs=[pl.BlockSpec((tm, MISC_W), lambda i: (i, 0))],
        out_specs=[pl.BlockSpec((tm, LANE), lambda i: (i, 0)),
                   pl.BlockSpec((chunks_per_step, GATE_ROWS, CHUNK), lambda i: (i, 0, 0))],
        out_shape=[jax.ShapeDtypeStruct((n, LANE), BF16),
                   jax.ShapeDtypeStruct((n // CHUNK, GATE_ROWS, CHUNK), F32)],
        compiler_params=_cparams(("arbitrary",)),
        name="gate_rows",
    )(gates)


def _mlstm_kernel(*refs, has_state):
    if has_state:
        (qf_ref, ktf_ref, vf_ref, ptf_ref, rf_ref, qb_ref, ktb_ref, vb_ref, ptb_ref, rb_ref, c0_ref, m0_ref,
         hf_ref, hb_ref, cn_ref, nn_ref, mn_ref, c_s, m_s) = refs
    else:
        (qf_ref, ktf_ref, vf_ref, ptf_ref, rf_ref, qb_ref, ktb_ref, vb_ref, ptb_ref, rb_ref,
         hf_ref, hb_ref, cn_ref, nn_ref, mn_ref, c_s, m_s) = refs
    step = pl.program_id(1)

    @pl.when(step == 0)
    def _():
        if has_state:
            c_s[...] = c0_ref[0]
            m_s[...] = m0_ref[0]
        else:
            c_s[...] = jnp.zeros(c_s.shape, F32)
            m_s[...] = jnp.full(m_s.shape, M_INIT, F32)

    nj = 2 * H_M
    row = lax.broadcasted_iota(jnp.int32, (CHUNK, CHUNK), 0)
    col = lax.broadcasted_iota(jnp.int32, (CHUNK, CHUNK), 1)
    is_fwd = lax.broadcasted_iota(jnp.int32, (nj, CHUNK), 0) < H_M
    rows_f, rows_b = rf_ref[0], rb_ref[0]
    pick = lambda g: jnp.where(is_fwd, rows_f[g * nj:(g + 1) * nj], rows_b[g * nj:(g + 1) * nj])
    a8, a_max8, f_sum8 = pick(0), pick(1), pick(2)
    m8 = m_s[...]
    mx8 = jnp.maximum(m8, a_max8)
    e8 = jnp.exp(a8 - mx8)
    decay8 = jnp.exp(m8 - mx8)
    m_s[...] = f_sum8 + mx8

    krow = lax.broadcasted_iota(jnp.int32, (CHUNK, 2 * CHUNK), 0)
    kcol = lax.broadcasted_iota(jnp.int32, (CHUNK, 2 * CHUNK), 1)
    cm_rows = krow < (PIECE_GROUPS // 2) * nj
    b_rows = (krow >= (PIECE_GROUPS // 2) * nj) & (krow < PIECE_GROUPS * nj)
    plane_sel = jnp.where((cm_rows & (kcol < CHUNK)) | (b_rows & (kcol >= CHUNK)), 1.0, 0.0).astype(BF16)
    ones_v = jnp.ones((CHUNK, DH_M), BF16)

    dirs = ((qf_ref, ktf_ref, vf_ref, ptf_ref, hf_ref), (qb_ref, ktb_ref, vb_ref, ptb_ref, hb_ref))
    chains = [(d, hd) for d in range(2) for hd in range(H_M)]
    first = []
    for d, hd in chains:
        q_ref, kt_ref, v_ref, pt_ref, _ = dirs[d]
        j = d * H_M + hd
        sl = slice(hd * DH_M, (hd + 1) * DH_M)
        q = q_ref[:, sl]
        k_t = kt_ref[sl, :]
        v_aug = jnp.concatenate([v_ref[:, sl], ones_v], axis=1)
        planes = _bdot(jnp.where(col % nj == j, pt_ref[...].astype(F32), 0.0).astype(BF16), plane_sel)
        qk = _bdot(q, k_t)
        c_st = c_s[j]
        qc = _bdot(q, c_st.astype(BF16))
        upd = _bdot((k_t.astype(F32) * e8[j:j + 1, :]).astype(BF16), v_aug)
        dec = decay8[j:j + 1, :]
        c_s[j] = jnp.concatenate([dec, dec], axis=1) * c_st + upd
        first.append((planes, qk, qc, v_aug))
    for (d, hd), (planes, qk, qc, v_aug) in zip(chains, first):
        h_ref = dirs[d][4]
        j = d * H_M + hd
        sl = slice(hd * DH_M, (hd + 1) * DH_M)
        mask = (col <= row) if d == 0 else (col >= row)
        m_row = m8[j:j + 1, :]
        u = jnp.maximum(m_row, planes[:, :CHUNK])
        w = jnp.exp(jnp.where(mask, a8[j:j + 1, :] - u, -jnp.inf))
        s_inter = jnp.exp(m_row - u)
        intra = _bdot((qk * w).astype(BF16), v_aug)
        num = s_inter * qc[:, :DH_M] + intra[:, :DH_M]
        den = s_inter * qc[:, DH_M:] + intra[:, DH_M:]
        h_ref[:, sl] = num / jnp.maximum(jnp.abs(den), jnp.exp(-(planes[:, CHUNK:] + u)))

    @pl.when(step == pl.num_programs(1) - 1)
    def _():
        for j in range(nj):
            c_aug = c_s[j]
            cn_ref[0, j] = c_aug[:, :DH_M]
            nn_ref[0, j:j + 1, :] = c_aug[:, DH_M:].T[0:1, :]
        mn_ref[0] = m_s[...]


def _mlstm(mq, mkt, mv, gate_pt, gate_rows, c0, m0, batch, seq):
    has_state = c0 is not None
    nc = seq // CHUNK
    n = batch * seq
    nj = 2 * H_M
    fwd = lambda w: pl.BlockSpec((CHUNK, w), lambda b, c: (b * nc + c, 0))
    bwd = lambda w: pl.BlockSpec((CHUNK, w), lambda b, c: (b * nc + nc - 1 - c, 0))
    fwd_t = pl.BlockSpec((MLSTM_W, CHUNK), lambda b, c: (0, b * nc + c))
    bwd_t = pl.BlockSpec((MLSTM_W, CHUNK), lambda b, c: (0, b * nc + nc - 1 - c))
    st_m = pl.BlockSpec((1, nj, LANE), lambda b, c: (b, 0, 0))
    st_c = pl.BlockSpec((1, nj, DH_M, 2 * DH_M), lambda b, c: (b, 0, 0, 0))
    fwd_r = pl.BlockSpec((1, GATE_ROWS, CHUNK), lambda b, c: (b * nc + c, 0, 0))
    bwd_r = pl.BlockSpec((1, GATE_ROWS, CHUNK), lambda b, c: (b * nc + nc - 1 - c, 0, 0))
    in_specs = [fwd(MLSTM_W), fwd_t, fwd(MLSTM_W), fwd(LANE), fwd_r,
                bwd(MLSTM_W), bwd_t, bwd(MLSTM_W), bwd(LANE), bwd_r]
    args = [mq, mkt, mv, gate_pt, gate_rows, mq, mkt, mv, gate_pt, gate_rows]
    if has_state:
        in_specs += [st_c, st_m]
        args += [c0, m0]
    return pl.pallas_call(
        functools.partial(_mlstm_kernel, has_state=has_state),
        grid=(batch, nc),
        in_specs=in_specs,
        out_specs=[fwd(MLSTM_W), bwd(MLSTM_W),
                   pl.BlockSpec((1, nj, DH_M, DH_M), lambda b, c: (b, 0, 0, 0)), st_m, st_m],
        out_shape=[jax.ShapeDtypeStruct((n, MLSTM_W), F32), jax.ShapeDtypeStruct((n, MLSTM_W), F32),
                   jax.ShapeDtypeStruct((batch, nj, DH_M, DH_M), F32),
                   jax.ShapeDtypeStruct((batch, nj, DH_M), F32),
                   jax.ShapeDtypeStruct((batch, nj, LANE), F32)],
        scratch_shapes=[pltpu.VMEM((nj, DH_M, 2 * DH_M), F32), pltpu.VMEM((nj, LANE), F32)],
        compiler_params=_cparams(("arbitrary", "arbitrary")),
        name="mlstm",
    )(*args)


def _attn_kernel(*refs, tiles_per_step, has_cache):
    if has_cache:
        q_ref, k_ref, vt_ref, kc_ref, vct_ref, o_ref, qt_s, acc_s, m_s, ot_s, s_scr = refs
    else:
        q_ref, k_ref, vt_ref, o_ref, qt_s, acc_s, m_s, ot_s, s_scr = refs
    n_tiles, _, tk = vt_ref.shape
    qt_s[...] = q_ref[...].astype(F32).T.astype(BF16)
    acc_s[...] = jnp.zeros(acc_s.shape, F32)
    m_s[...] = jnp.full(m_s.shape, -jnp.inf, F32)

    def update(tiles):
        items = [(tile, head) for tile in tiles for head in range(H_A)]

        def scores(idx):
            (get_k, _, width), head = items[idx]
            s = _bdot(get_k(head), qt_s[head * HEAD_PAD:(head + 1) * HEAD_PAD, :])
            s_scr[idx % s_scr.shape[0], 0:width, :] = s
            return jnp.max(s, axis=0, keepdims=True)

        pending = [scores(i) for i in range(QK_LOOKAHEAD)]
        for idx, ((_, get_vt, width), head) in enumerate(items):
            m_tile = pending.pop(0)
            if idx + QK_LOOKAHEAD < len(items):
                pending.append(scores(idx + QK_LOOKAHEAD))
            m_old = m_s[head:head + 1, :]
            m_new = jnp.maximum(m_old, m_tile)
            alpha = jnp.exp2(m_old - m_new)
            p = jnp.exp2((s_scr[idx % s_scr.shape[0], 0:width, :] - m_new).astype(BF16))
            ones = jnp.ones((ONES_ROWS, width), BF16)
            lhs = jnp.concatenate([get_vt(head), ones], axis=0)
            acc_s[head] = alpha * acc_s[head] + _bdot(lhs, p)
            m_s[head:head + 1, :] = m_new

    def main_tile(t):
        start = t * tk if isinstance(t, int) else pl.multiple_of(t * tk, tk)
        rows = pl.ds(start, tk)
        return (lambda h: k_ref[rows, h * HEAD_PAD:(h + 1) * HEAD_PAD],
                lambda h: vt_ref[t, h * V_DIM:(h + 1) * V_DIM, :], tk)

    def body(i, carry):
        update([main_tile(i * tiles_per_step + u) for u in range(tiles_per_step)])
        return carry

    tail = []
    if has_cache:
        tail = [(lambda h: kc_ref[:, h * HEAD_PAD:(h + 1) * HEAD_PAD],
                 lambda h: vct_ref[0, h * V_DIM:(h + 1) * V_DIM, :], kc_ref.shape[0])]
    n_loop = (n_tiles + len(tail) - 1) // tiles_per_step
    tail = [main_tile(t) for t in range(n_loop * tiles_per_step, n_tiles)] + tail
    if n_loop:
        lax.fori_loop(0, n_loop, body, 0)
    update(tail)
    for head in range(H_A):
        acc = acc_s[head]
        ot_s[head * V_DIM:(head + 1) * V_DIM, :] = acc[0:V_DIM] * (1.0 / acc[V_DIM:V_DIM + 1])
    o_ref[...] = ot_s[...].T.astype(BF16)


def _attention(q, k, vt, kc, vct, batch, seq, tq, tiles_per_step):
    nq = seq // tq
    tk = vt.shape[2]
    n_tiles = seq // tk
    has_cache = kc is not None
    max_width = max(tk, kc.shape[0] // batch) if has_cache else tk
    in_specs = [pl.BlockSpec((tq, H_A * HEAD_PAD), lambda b, i: (b * nq + i, 0)),
                pl.BlockSpec((seq, H_A * HEAD_PAD), lambda b, i: (b, 0)),
                pl.BlockSpec((n_tiles, H_A * V_DIM, tk), lambda b, i: (b, 0, 0))]
    args = [q, k, vt]
    if has_cache:
        past = kc.shape[0] // batch
        in_specs += [pl.BlockSpec((past, H_A * HEAD_PAD), lambda b, i: (b, 0)),
                     pl.BlockSpec((1, H_A * V_DIM, past), lambda b, i: (b, 0, 0))]
        args += [kc, vct]
    return pl.pallas_call(
        functools.partial(_attn_kernel, tiles_per_step=tiles_per_step, has_cache=has_cache),
        grid=(batch, nq),
        in_specs=in_specs,
        out_specs=pl.BlockSpec((tq, H_A * V_DIM), lambda b, i: (b * nq + i, 0)),
        out_shape=jax.ShapeDtypeStruct((batch * seq, H_A * V_DIM), BF16),
        scratch_shapes=[pltpu.VMEM((H_A * HEAD_PAD, tq), BF16),
                        pltpu.VMEM((H_A, V_DIM + ONES_ROWS, tq), F32),
                        pltpu.VMEM((H_A, tq), F32),
                        pltpu.VMEM((H_A * V_DIM, tq), F32),
                        pltpu.VMEM((QK_LOOKAHEAD + 1, max_width, tq), F32)],
        compiler_params=_cparams(("arbitrary", "arbitrary")),
        name="mla_attention",
    )(*args)


def _stage_c_kernel(x_ref, mod_ref, hf_ref, hb_ref, og_ref, attn_ref, ga_ref, gb_ref, hn_ref,
                    post1_ref, pre2_ref, post2_ref, wmo_ref, wao_ref, wout_ref, w1_ref, w2_ref, y_ref):
    mod = mod_ref[0]
    gate1, shift2, scale2, gate2 = mod[2:3], mod[3:4], mod[4:5], mod[5:6]
    hn = hn_ref[...]
    rows = x_ref.shape[0] // ROW_SPLITS
    groups = [slice(r * rows, (r + 1) * rows) for r in range(ROW_SPLITS)]
    n_ff = D_FF // D_MODEL

    def mixer_in(rs):
        hm = hf_ref[rs, :] + hb_ref[rs, :]
        heads = [_rms(hm[:, hd * DH_M:(hd + 1) * DH_M], hn[:, hd * DH_M:(hd + 1) * DH_M]) for hd in range(H_M)]
        return (jnp.concatenate(heads, axis=1) * og_ref[rs, :].astype(F32)).astype(BF16)

    def branch_out(hm, rs):
        return _bdot(hm, wmo_ref[...]), _bdot(attn_ref[rs, :], wao_ref[...])

    def merge(y_m, y_a, rs):
        return (ga_ref[rs, :].astype(F32) * y_m + gb_ref[rs, :].astype(F32) * y_a).astype(BF16)

    def mid(mix, rs):
        x1 = x_ref[rs, :] + gate1 * _rms(mix, post1_ref[...])
        return x1, (_rms(x1, pre2_ref[...]) * (1.0 + scale2) + shift2).astype(BF16)

    def mlp(h2):
        up = lambda c: _bdot(h2, w1_ref[:, c * D_MODEL:(c + 1) * D_MODEL])
        ff = jnp.zeros((rows, D_MODEL), F32)
        nxt = up(0)
        for c in range(n_ff):
            a = jnp.maximum(nxt, 0.0)
            if c + 1 < n_ff:
                nxt = up(c + 1)
            ff = ff + _bdot((a * a).astype(BF16), w2_ref[c * D_MODEL:(c + 1) * D_MODEL, :])
        return ff

    hms = [mixer_in(rs) for rs in groups]
    ys = [branch_out(hm, rs) for hm, rs in zip(hms, groups)]
    mixes = [_bdot(merge(*y, rs), wout_ref[...]) for y, rs in zip(ys, groups)]
    mids = [mid(mix, rs) for mix, rs in zip(mixes, groups)]
    ffs = [mlp(h2) for _, h2 in mids]
    for (x1, _), ff, rs in zip(mids, ffs, groups):
        y_ref[rs, :] = x1 + gate2 * _rms(ff, post2_ref[...])


def _stage_c(x, mods, mod_index, hf, hb, og, attn, ga, gb, hn, post1, pre2, post2, wmo, wao, wout, w1, w2, tm):
    n = x.shape[0]
    tile = lambda w: pl.BlockSpec((tm, w), lambda i: (i, 0))
    return pl.pallas_call(
        _stage_c_kernel,
        grid=(n // tm,),
        in_specs=[tile(D_MODEL),
                  pl.BlockSpec((1, N_MOD, D_MODEL), lambda i: (mod_index(i), 0, 0)),
                  tile(MLSTM_W), tile(MLSTM_W), tile(MLSTM_W), tile(H_A * V_DIM),
                  tile(D_MODEL), tile(D_MODEL),
                  _const_spec((1, MLSTM_W)), _const_spec((1, D_MODEL)), _const_spec((1, D_MODEL)),
                  _const_spec((1, D_MODEL)),
                  _const_spec((MLSTM_W, D_MODEL)), _const_spec((H_A * V_DIM, D_MODEL)),
                  _const_spec((D_MODEL, D_MODEL)), _const_spec((D_MODEL, D_FF)), _const_spec((D_FF, D_MODEL))],
        out_specs=tile(D_MODEL),
        out_shape=jax.ShapeDtypeStruct((n, D_MODEL), F32),
        compiler_params=_cparams(("arbitrary",)),
        name="stage_c",
    )(x, mods, hf, hb, og, attn, ga, gb, hn, post1, pre2, post2, wmo, wao, wout, w1, w2)


def _rope_tables(n_tokens):
    pos = np.arange(n_tokens)
    row = (pos // GRID_W).astype(np.float32)
    col = (pos % GRID_W).astype(np.float32)
    inv = (ROPE_BASE ** (-np.arange(0, AX_DIM, 2, dtype=np.float32) / AX_DIM)).astype(np.float32)
    ang = np.concatenate([row[:, None] * inv, col[:, None] * inv], axis=-1)
    cos = np.cos(ang.astype(np.float64)).astype(np.float32)
    sin = np.sin(ang.astype(np.float64)).astype(np.float32)
    ones = np.ones((n_tokens, NOPE), np.float32)
    zeros = np.zeros((n_tokens, NOPE), np.float32)
    pad = np.zeros((n_tokens, HEAD_PAD - NOPE - ROPE_DIM), np.float32)
    ta = np.concatenate([ones, cos, cos, pad], axis=-1)
    tb = np.concatenate([zeros, -sin, sin, pad], axis=-1)
    return jnp.asarray(ta), jnp.asarray(tb)


def _plain_tables(n_tokens):
    ones = jnp.ones((n_tokens, NOPE + ROPE_DIM), F32)
    pad = jnp.zeros((n_tokens, HEAD_PAD - NOPE - ROPE_DIM), F32)
    return jnp.concatenate([ones, pad], axis=-1), jnp.zeros((n_tokens, HEAD_PAD), F32)


def kernel(x_prompt, x_sample, cache_mla_ckv, cache_mla_krope, state_mlstm_C, state_mlstm_n, state_mlstm_m,
           c, c_ctx, w_ada, b_ada, norm_pre1, norm_post1, norm_pre2, norm_post2, w_in, mlstm_gate_b,
           mla_q_norm, mla_kv_norm, w_uq, w_ukv, w_mla_o, mlstm_head_norm, w_mlstm_o, w_out, w_mlp1, w_mlp2):
    bp, sp, _ = x_prompt.shape
    bs, ss, _ = x_sample.shape
    depth = w_in.shape[0]
    past = cache_mla_ckv.shape[2]
    nj = 2 * H_M
    even = np.arange(0, ROPE_DIM, 2)
    odd = np.arange(1, ROPE_DIM, 2)
    perm = np.concatenate([even, odd])
    perm_sw = np.concatenate([odd, even])

    xp = x_prompt.reshape(bp * sp, D_MODEL)
    xs = x_sample.reshape(bs * ss, D_MODEL)
    cc = jnp.zeros((8, D_MODEL), F32).at[:bs].set(c).at[bs].set(c_ctx)
    ta_lat, tb_lat = _rope_tables(ss)
    tm_ctx, tm_lat = 512, 512
    assert past == tm_lat
    ta_ctx, tb_ctx = _plain_tables(tm_ctx)

    new_ckv, new_krope, new_c, new_n, new_m = [], [], [], [], []
    for l in range(depth):
        cols = np.cumsum((W_BIG_COLS, N_GATES, Q_LORA, KV_LORA, ROPE_DIM))
        w_big, w_g, w_cq, w_ckv, w_kr, w_merge = jnp.split(w_in[l].astype(BF16), cols.tolist(), axis=1)
        w_g = w_g.reshape(D_MODEL, 2, 2, H_M).transpose(0, 2, 1, 3).reshape(D_MODEL, N_GATES)
        w_misc = jnp.concatenate([w_kr, w_g, jnp.zeros((D_MODEL, MISC_W - 3 * ROPE_DIM - N_GATES), BF16),
                                  w_kr[:, perm], w_kr[:, perm_sw]], axis=1)
        wa = (w_big, w_merge, jnp.concatenate([w_cq, w_ckv, w_misc], axis=1))
        gbias = jnp.zeros((1, MISC_W), F32).at[0, GATE_LANE0:GATE_LANE0 + N_GATES].set(
            mlstm_gate_b[l].transpose(1, 0, 2).reshape(N_GATES))
        uq = w_uq[l]
        wq = jnp.concatenate([uq[..., :NOPE], uq[..., NOPE:][..., perm], uq[..., NOPE:][..., perm_sw]],
                             axis=-1).reshape(Q_LORA, H_A * HEAD_PAD).astype(BF16)
        ukv = w_ukv[l]
        wk = jnp.concatenate([ukv[..., :NOPE], jnp.zeros((KV_LORA, H_A, HEAD_PAD - NOPE), F32)],
                             axis=-1).reshape(KV_LORA, H_A * HEAD_PAD).astype(BF16)
        wv = ukv[..., NOPE:].reshape(KV_LORA, H_A * V_DIM).T.astype(BF16)
        wmo = w_mlstm_o[l].astype(BF16)
        wao = w_mla_o[l].astype(BF16)
        wout = w_out[l].astype(BF16)
        w1 = w_mlp1[l].astype(BF16)
        w2 = w_mlp2[l].astype(BF16)
        pre1, post1 = norm_pre1[l][None], norm_post1[l][None]
        pre2, post2 = norm_pre2[l][None], norm_post2[l][None]
        qn, kvn = mla_q_norm[l][None], mla_kv_norm[l][None]
        hn = mlstm_head_norm[l].reshape(1, MLSTM_W)

        mods = _modulation(cc, w_ada[l], b_ada[l][None]).reshape(8, N_MOD, D_MODEL)

        ctx_mod = lambda i: bs
        a = _stage_a(xp, mods, ctx_mod, pre1, wa, gbias, qn, kvn, wq, wk, wv, ta_ctx, tb_ctx,
                     lambda i: 0, tm_ctx, sp)
        mq, mk, mv, og, ga, gb, gates, q, k, v, ckv_n, kro = a
        hf, hb, c_fin, n_fin, m_fin = _mlstm(mq, mk, mv, *_gate_rows(gates, GATE_CHUNKS_PER_STEP),
                                             None, None, bp, sp)
        attn = _attention(q, k, v, None, None, bp, sp, sp, 1)
        xp = _stage_c(xp, mods, ctx_mod, hf, hb, og, attn, ga, gb, hn, post1, pre2, post2,
                      wmo, wao, wout, w1, w2, tm_ctx)
        new_ckv.append(ckv_n.reshape(bp, sp, KV_LORA))
        new_krope.append(kro.reshape(bp, sp, ROPE_DIM))
        new_c.append(c_fin.reshape(bp, 2, H_M, DH_M, DH_M))
        new_n.append(n_fin.reshape(bp, 2, H_M, DH_M))
        new_m.append(m_fin[:, :, 0].reshape(bp, 2, H_M))

        tiles_per_seq = ss // tm_lat
        lat_mod = lambda i: i // tiles_per_seq
        a = _stage_a(xs, mods, lat_mod, pre1, wa, gbias, qn, kvn, wq, wk, wv, ta_lat, tb_lat,
                     lambda i: i % tiles_per_seq, tm_lat, ATTN_KEY_TILE)
        mq, mk, mv, og, ga, gb, gates, q, k, v, _, _ = a
        kr_cache = jnp.zeros((bs * past, LANE), F32).at[:, NOPE:NOPE + ROPE_DIM].set(
            cache_mla_krope[:, l].reshape(bs * past, ROPE_DIM)[:, perm])
        kc, vc = _kv_cache(cache_mla_ckv[:, l].reshape(bs * past, KV_LORA), kr_cache, wk, wv, past)
        n0 = state_mlstm_n[:, l].reshape(bs, nj, DH_M, 1)
        c0 = jnp.concatenate([state_mlstm_C[:, l].reshape(bs, nj, DH_M, DH_M),
                              jnp.broadcast_to(n0, (bs, nj, DH_M, DH_M))], axis=-1)
        m0 = jnp.broadcast_to(state_mlstm_m[:, l].reshape(bs, nj, 1), (bs, nj, LANE))
        hf, hb, _, _, _ = _mlstm(mq, mk, mv, *_gate_rows(gates, GATE_CHUNKS_PER_STEP), c0, m0, bs, ss)
        attn = _attention(q, k, v, kc, vc, bs, ss, 256, KEY_TILES_PER_STEP)
        xs = _stage_c(xs, mods, lat_mod, hf, hb, og, attn, ga, gb, hn, post1, pre2, post2,
                      wmo, wao, wout, w1, w2, tm_lat)

    return (xp.reshape(bp, sp, D_MODEL), xs.reshape(bs, ss, D_MODEL),
            jnp.stack(new_ckv, axis=1), jnp.stack(new_krope, axis=1), jnp.stack(new_c, axis=1),
            jnp.stack(new_n, axis=1), jnp.stack(new_m, axis=1))
```

```python
import functools

import numpy as np
import jax
import jax.numpy as jnp
from jax import lax
from jax.experimental import pallas as pl
from jax.experimental.pallas import tpu as pltpu

F32 = jnp.float32
BF16 = jnp.bfloat16

D_MODEL = 1024
H_M = 4
DH_M = 128
MLSTM_W = H_M * DH_M
CHUNK = 128
H_A = 8
NOPE = 64
ROPE_DIM = 32
V_DIM = 64
Q_LORA = 384
KV_LORA = 256
AX_DIM = ROPE_DIM // 2
ROPE_BASE = 10000.0
GRID_W = 64
D_FF = 4 * D_MODEL
N_MOD = 6
EPS = 1e-6
N_GATES = 4 * H_M
M_INIT = -1e30

LANE = 128
HEAD_PAD = LANE
MISC_W = LANE
GATE_LANE0 = ROPE_DIM
OFF_MQ, OFF_MK, OFF_MV, OFF_MO, W_BIG_COLS = 0, MLSTM_W, 2 * MLSTM_W, 3 * MLSTM_W, 4 * MLSTM_W
OFF_CQ, OFF_CKV, OFF_MISC = 0, Q_LORA, Q_LORA + KV_LORA
W_SMALL_COLS = OFF_MISC + MISC_W

VMEM_LIMIT = 60 * 1024 * 1024
QK_SCALE = float((NOPE + ROPE_DIM) ** -0.5 * np.log2(np.e))
QK_LOOKAHEAD = 3
ATTN_KEY_TILE = 512
ROW_SPLITS = 2
KEY_TILES_PER_STEP = 3
ONES_ROWS = 16


def _cparams(sem):
    return pltpu.CompilerParams(dimension_semantics=sem, vmem_limit_bytes=VMEM_LIMIT)


def _const_spec(shape):
    nd = len(shape)
    return pl.BlockSpec(shape, lambda *_: (0,) * nd, pipeline_mode=pl.Buffered(1))


def _rms(x, w):
    return x * lax.rsqrt(jnp.mean(x * x, axis=-1, keepdims=True) + EPS) * w


def _sigmoid(x):
    return 1.0 / (1.0 + jnp.exp(-x))


def _log_sigmoid(x):
    return jnp.minimum(x, 0.0) - jnp.log(1.0 + jnp.exp(-jnp.abs(x)))


def _bdot(a, b):
    return jnp.dot(a, b, preferred_element_type=F32)


def _mod_kernel(c_ref, w_ref, b_ref, o_ref):
    c = c_ref[...]
    s = c * _sigmoid(c)
    o_ref[...] = jnp.dot(s, w_ref[...], preferred_element_type=F32,
                         precision=lax.Precision.HIGHEST) + b_ref[...]


def _modulation(cc, w_ada, b_ada):
    n_out = w_ada.shape[1]
    tn = 1536
    return pl.pallas_call(
        _mod_kernel,
        grid=(n_out // tn,),
        in_specs=[pl.BlockSpec((8, D_MODEL), lambda j: (0, 0)),
                  pl.BlockSpec((D_MODEL, tn), lambda j: (0, j)),
                  pl.BlockSpec((1, tn), lambda j: (0, j))],
        out_specs=pl.BlockSpec((8, tn), lambda j: (0, j)),
        out_shape=jax.ShapeDtypeStruct((8, n_out), F32),
        compiler_params=_cparams(("arbitrary",)),
        name="modulation",
    )(cc, w_ada, b_ada)


def _stage_a_kernel(x_ref, mod_ref, pre1_ref, wbig_ref, wmerge_ref, wsmall_ref, gbias_ref, qn_ref, kvn_ref,
                    wq_ref, wk_ref, wv_ref,
                    ta_ref, tb_ref,
                    mq_ref, mk_ref, mv_ref, og_ref, ga_ref, gb_ref, gates_ref, q_ref, k_ref, v_ref,
                    ckv_ref, kro_ref):
    mod = mod_ref[0]
    shift1, scale1 = mod[0:1], mod[1:2]
    tm = x_ref.shape[0]
    rows = tm // ROW_SPLITS
    tkv = v_ref.shape[2]

    def normed(rs):
        return (_rms(x_ref[rs, :], pre1_ref[...]) * (1.0 + scale1) + shift1).astype(BF16)

    def small_proj(hb):
        return tuple(_bdot(hb, wsmall_ref[:, lo:hi])
                     for lo, hi in ((OFF_CQ, OFF_CKV), (OFF_CKV, OFF_MISC), (OFF_MISC, W_SMALL_COLS)))

    def big_proj(hb, rs):
        proj = lambda lo, hi: _bdot(hb, wbig_ref[:, lo:hi])
        mq_ref[rs, :] = (proj(OFF_MQ, OFF_MK) * DH_M ** -0.5).astype(BF16)
        mk_ref[:, rs] = proj(OFF_MK, OFF_MV).T.astype(BF16)
        mv_ref[rs, :] = proj(OFF_MV, OFF_MO).astype(BF16)
        og_ref[rs, :] = _sigmoid(proj(OFF_MO, W_BIG_COLS)).astype(BF16)
        ga_ref[rs, :] = _sigmoid(_bdot(hb, wmerge_ref[:, :D_MODEL])).astype(BF16)
        gb_ref[rs, :] = _sigmoid(_bdot(hb, wmerge_ref[:, D_MODEL:])).astype(BF16)

    def mla_build(cq, ckv, misc, rs):
        lane = lax.broadcasted_iota(jnp.int32, misc.shape, 1)
        g = misc + gbias_ref[...]
        is_f = (lane >= GATE_LANE0 + 2 * H_M) & (lane < GATE_LANE0 + N_GATES)
        gates_ref[rs, :] = jnp.where(is_f, _log_sigmoid(g), g)
        kro_ref[rs, :] = misc[:, 0:ROPE_DIM]

        ckv_n = _rms(ckv, kvn_ref[...])
        ckv_ref[rs, :] = ckv_n
        ckv_b = ckv_n.astype(BF16)

        ta = ta_ref[rs, :]
        tb = tb_ref[rs, :]
        qf = _bdot(_rms(cq, qn_ref[...]).astype(BF16), wq_ref[...])
        ta8 = jnp.concatenate([ta] * H_A, axis=1)
        tb8 = jnp.concatenate([tb] * H_A, axis=1)
        q = qf * ta8 + pltpu.roll(qf, H_A * HEAD_PAD - ROPE_DIM, 1) * tb8
        q_ref[rs, :] = (q * QK_SCALE).astype(BF16)

        ta_k = jnp.where(lane < NOPE, 0.0, ta)
        kr = misc * ta_k + pltpu.roll(misc, MISC_W - ROPE_DIM, 1) * tb
        kk = _bdot(ckv_b, wk_ref[...]) + jnp.concatenate([kr] * H_A, axis=1)
        k_ref[rs, :] = kk.astype(BF16)
        vt = _bdot(wv_ref[...], ckv_n.T.astype(BF16)).astype(BF16)
        step = min(rows, tkv)
        for off in range(0, rows, step):
            lo = rs.start + off
            v_ref[lo // tkv, :, lo % tkv:lo % tkv + step] = vt[:, off:off + step]

    groups = [slice(r * rows, (r + 1) * rows) for r in range(ROW_SPLITS)]
    hbs = [normed(rs) for rs in groups]
    for hb, rs in zip(hbs, groups):
        small = small_proj(hb)
        big_proj(hb, rs)
        mla_build(*small, rs)


def _stage_a(x, mods, mod_index, pre1, wa, gbias, qn, kvn, wq, wk, wv, ta, tb, table_index, tm, tkv):
    n = x.shape[0]
    tile = lambda w: pl.BlockSpec((tm, w), lambda i: (i, 0))
    bf = lambda w: jax.ShapeDtypeStruct((n, w), BF16)
    f32 = lambda w: jax.ShapeDtypeStruct((n, w), F32)
    return pl.pallas_call(
        _stage_a_kernel,
        grid=(n // tm,),
        in_specs=[tile(D_MODEL),
                  pl.BlockSpec((1, N_MOD, D_MODEL), lambda i: (mod_index(i), 0, 0)),
                  _const_spec((1, D_MODEL)),
                  _const_spec((D_MODEL, W_BIG_COLS)),
                  _const_spec((D_MODEL, 2 * D_MODEL)),
                  _const_spec((D_MODEL, W_SMALL_COLS)),
                  _const_spec((1, MISC_W)),
                  _const_spec((1, Q_LORA)),
                  _const_spec((1, KV_LORA)),
                  _const_spec((Q_LORA, H_A * HEAD_PAD)),
                  _const_spec((KV_LORA, H_A * HEAD_PAD)),
                  _const_spec((H_A * V_DIM, KV_LORA)),
                  pl.BlockSpec((tm, LANE), lambda i: (table_index(i), 0)),
                  pl.BlockSpec((tm, LANE), lambda i: (table_index(i), 0))],
        out_specs=[tile(MLSTM_W), pl.BlockSpec((MLSTM_W, tm), lambda i: (0, i)), tile(MLSTM_W), tile(MLSTM_W),
                   tile(D_MODEL), tile(D_MODEL), tile(MISC_W),
                   tile(H_A * HEAD_PAD), tile(H_A * HEAD_PAD),
                   pl.BlockSpec((tm // tkv, H_A * V_DIM, tkv), lambda i: (i, 0, 0)),
                   tile(KV_LORA), tile(ROPE_DIM)],
        out_shape=[bf(MLSTM_W), jax.ShapeDtypeStruct((MLSTM_W, n), BF16), bf(MLSTM_W), bf(MLSTM_W),
                   bf(D_MODEL), bf(D_MODEL), f32(MISC_W),
                   bf(H_A * HEAD_PAD), bf(H_A * HEAD_PAD),
                   jax.ShapeDtypeStruct((n // tkv, H_A * V_DIM, tkv), BF16),
                   f32(KV_LORA), f32(ROPE_DIM)],
        compiler_params=_cparams(("arbitrary",)),
        name="stage_a",
    )(x, mods, pre1, *wa, gbias, qn, kvn, wq, wk, wv, ta, tb)


def _kv_cache_kernel(ckv_ref, kr_ref, wk_ref, wv_ref, k_ref, v_ref):
    ckv_b = ckv_ref[...].astype(BF16)
    kk = _bdot(ckv_b, wk_ref[...]) + jnp.concatenate([kr_ref[...]] * H_A, axis=1)
    k_ref[...] = kk.astype(BF16)
    v_ref[0] = _bdot(wv_ref[...], ckv_ref[...].T.astype(BF16)).astype(BF16)


def _kv_cache(ckv, kr, wk, wv, tm):
    n = ckv.shape[0]
    return pl.pallas_call(
        _kv_cache_kernel,
        grid=(n // tm,),
        in_specs=[pl.BlockSpec((tm, KV_LORA), lambda i: (i, 0)),
                  pl.BlockSpec((tm, LANE), lambda i: (i, 0)),
                  _const_spec((KV_LORA, H_A * HEAD_PAD)),
                  _const_spec((H_A * V_DIM, KV_LORA))],
        out_specs=[pl.BlockSpec((tm, H_A * HEAD_PAD), lambda i: (i, 0)),
                   pl.BlockSpec((1, H_A * V_DIM, tm), lambda i: (i, 0, 0))],
        out_shape=[jax.ShapeDtypeStruct((n, H_A * HEAD_PAD), BF16),
                   jax.ShapeDtypeStruct((n // tm, H_A * V_DIM, tm), BF16)],
        compiler_params=_cparams(("arbitrary",)),
        name="kv_cache",
    )(ckv, kr, wk, wv)


def _split3(x):
    x1 = x.astype(BF16).astype(F32)
    r = x - x1
    x2 = r.astype(BF16).astype(F32)
    x3 = (r - x2).astype(BF16).astype(F32)
    return x1, x2, x3


GATE_ROWS = 32
PIECE_GROUPS = 6
GATE_CHUNKS_PER_STEP = 8


def _gate_rows_kernel(g_ref, pt_ref, rows_ref):
    nj = 2 * H_M
    nch = g_ref.shape[0] // CHUNK
    n_rows = nch * nj
    row = lax.broadcasted_iota(jnp.int32, (CHUNK, CHUNK), 0)
    col = lax.broadcasted_iota(jnp.int32, (CHUNK, CHUNK), 1)
    lower = (col <= row).astype(F32)
    upper = (col >= row).astype(F32)
    hi = lax.Precision.HIGHEST
    rowi = lax.broadcasted_iota(jnp.int32, (n_rows, CHUNK), 0)
    lane = lax.broadcasted_iota(jnp.int32, (n_rows, CHUNK), 1)
    is_fwd = rowi % nj < H_M

    gi, gf = GATE_LANE0, GATE_LANE0 + nj
    i_rows, f_rows = [], []
    for c in range(nch):
        g_t = g_ref[c * CHUNK:(c + 1) * CHUNK, :].T
        i_rows.append(g_t[gi:gi + nj])
        f_rows.append(g_t[gf:gf + nj])
    i_all = jnp.concatenate(i_rows, axis=0)
    f_all = jnp.concatenate(f_rows, axis=0)
    b = jnp.where(is_fwd, jnp.dot(f_all, upper, preferred_element_type=F32, precision=hi),
                  jnp.dot(f_all, lower, preferred_element_type=F32, precision=hi))
    a = i_all - b
    cm = a
    shift = 1
    while shift < CHUNK:
        y_f = jnp.where(lane >= shift, pltpu.roll(cm, shift, 1), -jnp.inf)
        y_b = jnp.where(lane < CHUNK - shift, pltpu.roll(cm, CHUNK - shift, 1), -jnp.inf)
        cm = jnp.maximum(cm, jnp.where(is_fwd, y_f, y_b))
        shift *= 2
    a_max = jnp.broadcast_to(jnp.max(a, axis=1, keepdims=True), a.shape)
    f_sum = jnp.broadcast_to(jnp.sum(f_all, axis=1, keepdims=True), a.shape)
    pieces = list(_split3(cm)) + list(_split3(b))
    zeros8 = jnp.zeros((nj, CHUNK), F32)
    for c in range(nch):
        sl = slice(c * nj, (c + 1) * nj)
        stack = jnp.concatenate([p[sl] for p in pieces] + [zeros8] * (CHUNK // nj - PIECE_GROUPS), axis=0)
        pt_ref[c * CHUNK:(c + 1) * CHUNK, :] = stack.T.astype(BF16)
        rows_ref[c] = jnp.concatenate([a[sl], a_max[sl], f_sum[sl], zeros8], axis=0)


def _gate_rows(gates, chunks_per_step):
    n = gates.shape[0]
    tm = chunks_per_step * CHUNK
    return pl.pallas_call(
        _gate_rows_kernel,
        grid=(n // tm,),
        in_specs=[pl.BlockSpec((tm, MISC_W), lambda i: (i, 0))],
        out_specs=[pl.BlockSpec((tm, LANE), lambda i: (i, 0)),
                   pl.BlockSpec((chunks_per_step, GATE_ROWS, CHUNK), lambda i: (i, 0, 0))],
        out_shape=[jax.ShapeDtypeStruct((n, LANE), BF16),
                   jax.ShapeDtypeStruct((n // CHUNK, GATE_ROWS, CHUNK), F32)],
        compiler_params=_cparams(("arbitrary",)),
        name="gate_rows",
    )(gates)


def _mlstm_kernel(*refs, has_state):
    if has_state:
        (qf_ref, ktf_ref, vf_ref, ptf_ref, rf_ref, qb_ref, ktb_ref, vb_ref, ptb_ref, rb_ref, c0_ref, m0_ref,
         hf_ref, hb_ref, cn_ref, nn_ref, mn_ref, c_s, m_s) = refs
    else:
        (qf_ref, ktf_ref, vf_ref, ptf_ref, rf_ref, qb_ref, ktb_ref, vb_ref, ptb_ref, rb_ref,
         hf_ref, hb_ref, cn_ref, nn_ref, mn_ref, c_s, m_s) = refs
    step = pl.program_id(1)

    @pl.when(step == 0)
    def _():
        if has_state:
            c_s[...] = c0_ref[0]
            m_s[...] = m0_ref[0]
        else:
            c_s[...] = jnp.zeros(c_s.shape, F32)
            m_s[...] = jnp.full(m_s.shape, M_INIT, F32)

    nj = 2 * H_M
    row = lax.broadcasted_iota(jnp.int32, (CHUNK, CHUNK), 0)
    col = lax.broadcasted_iota(jnp.int32, (CHUNK, CHUNK), 1)
    is_fwd = lax.broadcasted_iota(jnp.int32, (nj, CHUNK), 0) < H_M
    rows_f, rows_b = rf_ref[0], rb_ref[0]
    pick = lambda g: jnp.where(is_fwd, rows_f[g * nj:(g + 1) * nj], rows_b[g * nj:(g + 1) * nj])
    a8, a_max8, f_sum8 = pick(0), pick(1), pick(2)
    m8 = m_s[...]
    mx8 = jnp.maximum(m8, a_max8)
    e8 = jnp.exp(a8 - mx8)
    decay8 = jnp.exp(m8 - mx8)
    m_s[...] = f_sum8 + mx8

    krow = lax.broadcasted_iota(jnp.int32, (CHUNK, 2 * CHUNK), 0)
    kcol = lax.broadcasted_iota(jnp.int32, (CHUNK, 2 * CHUNK), 1)
    cm_rows = krow < (PIECE_GROUPS // 2) * nj
    b_rows = (krow >= (PIECE_GROUPS // 2) * nj) & (krow < PIECE_GROUPS * nj)
    plane_sel = jnp.where((cm_rows & (kcol < CHUNK)) | (b_rows & (kcol >= CHUNK)), 1.0, 0.0).astype(BF16)
    ones_v = jnp.ones((CHUNK, DH_M), BF16)

    dirs = ((qf_ref, ktf_ref, vf_ref, ptf_ref, hf_ref), (qb_ref, ktb_ref, vb_ref, ptb_ref, hb_ref))
    chains = [(d, hd) for d in range(2) for hd in range(H_M)]
    first = []
    for d, hd in chains:
        q_ref, kt_ref, v_ref, pt_ref, _ = dirs[d]
        j = d * H_M + hd
        sl = slice(hd * DH_M, (hd + 1) * DH_M)
        q = q_ref[:, sl]
        k_t = kt_ref[sl, :]
        v_aug = jnp.concatenate([v_ref[:, sl], ones_v], axis=1)
        planes = _bdot(jnp.where(col % nj == j, pt_ref[...].astype(F32), 0.0).astype(BF16), plane_sel)
        qk = _bdot(q, k_t)
        c_st = c_s[j]
        qc = _bdot(q, c_st.astype(BF16))
        upd = _bdot((k_t.astype(F32) * e8[j:j + 1, :]).astype(BF16), v_aug)
        dec = decay8[j:j + 1, :]
        c_s[j] = jnp.concatenate([dec, dec], axis=1) * c_st + upd
        first.append((planes, qk, qc, v_aug))
    for (d, hd), (planes, qk, qc, v_aug) in zip(chains, first):
        h_ref = dirs[d][4]
        j = d * H_M + hd
        sl = slice(hd * DH_M, (hd + 1) * DH_M)
        mask = (col <= row) if d == 0 else (col >= row)
        m_row = m8[j:j + 1, :]
        u = jnp.maximum(m_row, planes[:, :CHUNK])
        w = jnp.exp(jnp.where(mask, a8[j:j + 1, :] - u, -jnp.inf))
        s_inter = jnp.exp(m_row - u)
        intra = _bdot((qk * w).astype(BF16), v_aug)
        num = s_inter * qc[:, :DH_M] + intra[:, :DH_M]
        den = s_inter * qc[:, DH_M:] + intra[:, DH_M:]
        h_ref[:, sl] = num / jnp.maximum(jnp.abs(den), jnp.exp(-(planes[:, CHUNK:] + u)))

    @pl.when(step == pl.num_programs(1) - 1)
    def _():
        for j in range(nj):
            c_aug = c_s[j]
            cn_ref[0, j] = c_aug[:, :DH_M]
            nn_ref[0, j:j + 1, :] = c_aug[:, DH_M:].T[0:1, :]
        mn_ref[0] = m_s[...]


def _mlstm(mq, mkt, mv, gate_pt, gate_rows, c0, m0, batch, seq):
    has_state = c0 is not None
    nc = seq // CHUNK
    n = batch * seq
    nj = 2 * H_M
    fwd = lambda w: pl.BlockSpec((CHUNK, w), lambda b, c: (b * nc + c, 0))
    bwd = lambda w: pl.BlockSpec((CHUNK, w), lambda b, c: (b * nc + nc - 1 - c, 0))
    fwd_t = pl.BlockSpec((MLSTM_W, CHUNK), lambda b, c: (0, b * nc + c))
    bwd_t = pl.BlockSpec((MLSTM_W, CHUNK), lambda b, c: (0, b * nc + nc - 1 - c))
    st_m = pl.BlockSpec((1, nj, LANE), lambda b, c: (b, 0, 0))
    st_c = pl.BlockSpec((1, nj, DH_M, 2 * DH_M), lambda b, c: (b, 0, 0, 0))
    fwd_r = pl.BlockSpec((1, GATE_ROWS, CHUNK), lambda b, c: (b * nc + c, 0, 0))
    bwd_r = pl.BlockSpec((1, GATE_ROWS, CHUNK), lambda b, c: (b * nc + nc - 1 - c, 0, 0))
    in_specs = [fwd(MLSTM_W), fwd_t, fwd(MLSTM_W), fwd(LANE), fwd_r,
                bwd(MLSTM_W), bwd_t, bwd(MLSTM_W), bwd(LANE), bwd_r]
    args = [mq, mkt, mv, gate_pt, gate_rows, mq, mkt, mv, gate_pt, gate_rows]
    if has_state:
        in_specs += [st_c, st_m]
        args += [c0, m0]
    return pl.pallas_call(
        functools.partial(_mlstm_kernel, has_state=has_state),
        grid=(batch, nc),
        in_specs=in_specs,
        out_specs=[fwd(MLSTM_W), bwd(MLSTM_W),
                   pl.BlockSpec((1, nj, DH_M, DH_M), lambda b, c: (b, 0, 0, 0)), st_m, st_m],
        out_shape=[jax.ShapeDtypeStruct((n, MLSTM_W), F32), jax.ShapeDtypeStruct((n, MLSTM_W), F32),
                   jax.ShapeDtypeStruct((batch, nj, DH_M, DH_M), F32),
                   jax.ShapeDtypeStruct((batch, nj, DH_M), F32),
                   jax.ShapeDtypeStruct((batch, nj, LANE), F32)],
        scratch_shapes=[pltpu.VMEM((nj, DH_M, 2 * DH_M), F32), pltpu.VMEM((nj, LANE), F32)],
        compiler_params=_cparams(("arbitrary", "arbitrary")),
        name="mlstm",
    )(*args)


def _attn_kernel(*refs, tiles_per_step, has_cache):
    if has_cache:
        q_ref, k_ref, vt_ref, kc_ref, vct_ref, o_ref, qt_s, acc_s, m_s, ot_s, s_scr = refs
    else:
        q_ref, k_ref, vt_ref, o_ref, qt_s, acc_s, m_s, ot_s, s_scr = refs
    n_tiles, _, tk = vt_ref.shape
    qt_s[...] = q_ref[...].astype(F32).T.astype(BF16)
    acc_s[...] = jnp.zeros(acc_s.shape, F32)
    m_s[...] = jnp.full(m_s.shape, -jnp.inf, F32)

    n_slots = s_scr.shape[0]
    assert (tiles_per_step * H_A) % n_slots == 0

    def scores(tile, head, slot):
        get_k, _, width = tile
        s = _bdot(get_k(head), qt_s[head * HEAD_PAD:(head + 1) * HEAD_PAD, :])
        s_scr[slot, 0:width, :] = s
        return jnp.max(s, axis=0, keepdims=True)

    def update(tiles, pending, next_tile):
        items = [(tile, head) for tile in tiles for head in range(H_A)]
        ahead = items + ([(next_tile, h) for h in range(QK_LOOKAHEAD)] if next_tile is not None else [])
        pending = list(pending)
        for idx, ((_, get_vt, width), head) in enumerate(items):
            m_tile = pending.pop(0)
            if idx + QK_LOOKAHEAD < len(ahead):
                pending.append(scores(*ahead[idx + QK_LOOKAHEAD], (idx + QK_LOOKAHEAD) % n_slots))
            m_old = m_s[head:head + 1, :]
            m_new = jnp.maximum(m_old, m_tile)
            alpha = jnp.exp2(m_old - m_new)
            p = jnp.exp2((s_scr[idx % n_slots, 0:width, :] - m_new).astype(BF16))
            ones = jnp.ones((ONES_ROWS, width), BF16)
            lhs = jnp.concatenate([get_vt(head), ones], axis=0)
            acc_s[head] = alpha * acc_s[head] + _bdot(lhs, p)
            m_s[head:head + 1, :] = m_new
        return tuple(pending)

    def main_tile(t):
        start = t * tk if isinstance(t, int) else pl.multiple_of(t * tk, tk)
        rows = pl.ds(start, tk)
        return (lambda h: k_ref[rows, h * HEAD_PAD:(h + 1) * HEAD_PAD],
                lambda h: vt_ref[t, h * V_DIM:(h + 1) * V_DIM, :], tk)

    def body(i, pending):
        first = i * tiles_per_step
        return update([main_tile(first + u) for u in range(tiles_per_step)], pending,
                      main_tile(first + tiles_per_step))

    tail = []
    if has_cache:
        tail = [(lambda h: kc_ref[:, h * HEAD_PAD:(h + 1) * HEAD_PAD],
                 lambda h: vct_ref[0, h * V_DIM:(h + 1) * V_DIM, :], kc_ref.shape[0])]
    n_loop = (n_tiles - 1) // tiles_per_step
    tail = [main_tile(t) for t in range(n_loop * tiles_per_step, n_tiles)] + tail
    pending = tuple(scores(main_tile(0), h, h) for h in range(QK_LOOKAHEAD))
    if n_loop:
        pending = lax.fori_loop(0, n_loop, body, pending)
    update(tail, pending, None)
    for head in range(H_A):
        acc = acc_s[head]
        ot_s[head * V_DIM:(head + 1) * V_DIM, :] = acc[0:V_DIM] * (1.0 / acc[V_DIM:V_DIM + 1])
    o_ref[...] = ot_s[...].T.astype(BF16)


def _attention(q, k, vt, kc, vct, batch, seq, tq, tiles_per_step):
    nq = seq // tq
    tk = vt.shape[2]
    n_tiles = seq // tk
    has_cache = kc is not None
    max_width = max(tk, kc.shape[0] // batch) if has_cache else tk
    in_specs = [pl.BlockSpec((tq, H_A * HEAD_PAD), lambda b, i: (b * nq + i, 0)),
                pl.BlockSpec((seq, H_A * HEAD_PAD), lambda b, i: (b, 0)),
                pl.BlockSpec((n_tiles, H_A * V_DIM, tk), lambda b, i: (b, 0, 0))]
    args = [q, k, vt]
    if has_cache:
        past = kc.shape[0] // batch
        in_specs += [pl.BlockSpec((past, H_A * HEAD_PAD), lambda b, i: (b, 0)),
                     pl.BlockSpec((1, H_A * V_DIM, past), lambda b, i: (b, 0, 0))]
        args += [kc, vct]
    return pl.pallas_call(
        functools.partial(_attn_kernel, tiles_per_step=tiles_per_step, has_cache=has_cache),
        grid=(batch, nq),
        in_specs=in_specs,
        out_specs=pl.BlockSpec((tq, H_A * V_DIM), lambda b, i: (b * nq + i, 0)),
        out_shape=jax.ShapeDtypeStruct((batch * seq, H_A * V_DIM), BF16),
        scratch_shapes=[pltpu.VMEM((H_A * HEAD_PAD, tq), BF16),
                        pltpu.VMEM((H_A, V_DIM + ONES_ROWS, tq), F32),
                        pltpu.VMEM((H_A, tq), F32),
                        pltpu.VMEM((H_A * V_DIM, tq), F32),
                        pltpu.VMEM((QK_LOOKAHEAD + 1, max_width, tq), F32)],
        compiler_params=_cparams(("arbitrary", "arbitrary")),
        name="mla_attention",
    )(*args)


def _stage_c_kernel(x_ref, mod_ref, hf_ref, hb_ref, og_ref, attn_ref, ga_ref, gb_ref, hn_ref,
                    post1_ref, pre2_ref, post2_ref, wmo_ref, wao_ref, wout_ref, w1_ref, w2_ref, y_ref):
    mod = mod_ref[0]
    gate1, shift2, scale2, gate2 = mod[2:3], mod[3:4], mod[4:5], mod[5:6]
    hn = hn_ref[...]
    rows = x_ref.shape[0] // ROW_SPLITS
    groups = [slice(r * rows, (r + 1) * rows) for r in range(ROW_SPLITS)]
    n_ff = D_FF // D_MODEL

    def mixer_in(rs):
        hm = hf_ref[rs, :] + hb_ref[rs, :]
        heads = [_rms(hm[:, hd * DH_M:(hd + 1) * DH_M], hn[:, hd * DH_M:(hd + 1) * DH_M]) for hd in range(H_M)]
        return (jnp.concatenate(heads, axis=1) * og_ref[rs, :].astype(F32)).astype(BF16)

    def branch_out(hm, rs):
        return _bdot(hm, wmo_ref[...]), _bdot(attn_ref[rs, :], wao_ref[...])

    def merge(y_m, y_a, rs):
        return (ga_ref[rs, :].astype(F32) * y_m + gb_ref[rs, :].astype(F32) * y_a).astype(BF16)

    def mid(mix, rs):
        x1 = x_ref[rs, :] + gate1 * _rms(mix, post1_ref[...])
        return x1, (_rms(x1, pre2_ref[...]) * (1.0 + scale2) + shift2).astype(BF16)

    def mlp(h2):
        up = lambda c: _bdot(h2, w1_ref[:, c * D_MODEL:(c + 1) * D_MODEL])
        ff = jnp.zeros((rows, D_MODEL), F32)
        nxt = up(0)
        for c in range(n_ff):
            a = jnp.maximum(nxt, 0.0)
            if c + 1 < n_ff:
                nxt = up(c + 1)
            ff = ff + _bdot((a * a).astype(BF16), w2_ref[c * D_MODEL:(c + 1) * D_MODEL, :])
        return ff

    hms = [mixer_in(rs) for rs in groups]
    ys = [branch_out(hm, rs) for hm, rs in zip(hms, groups)]
    mixes = [_bdot(merge(*y, rs), wout_ref[...]) for y, rs in zip(ys, groups)]
    mids = [mid(mix, rs) for mix, rs in zip(mixes, groups)]
    ffs = [mlp(h2) for _, h2 in mids]
    for (x1, _), ff, rs in zip(mids, ffs, groups):
        y_ref[rs, :] = x1 + gate2 * _rms(ff, post2_ref[...])


def _stage_c(x, mods, mod_index, hf, hb, og, attn, ga, gb, hn, post1, pre2, post2, wmo, wao, wout, w1, w2, tm):
    n = x.shape[0]
    tile = lambda w: pl.BlockSpec((tm, w), lambda i: (i, 0))
    return pl.pallas_call(
        _stage_c_kernel,
        grid=(n // tm,),
        in_specs=[tile(D_MODEL),
                  pl.BlockSpec((1, N_MOD, D_MODEL), lambda i: (mod_index(i), 0, 0)),
                  tile(MLSTM_W), tile(MLSTM_W), tile(MLSTM_W), tile(H_A * V_DIM),
                  tile(D_MODEL), tile(D_MODEL),
                  _const_spec((1, MLSTM_W)), _const_spec((1, D_MODEL)), _const_spec((1, D_MODEL)),
                  _const_spec((1, D_MODEL)),
                  _const_spec((MLSTM_W, D_MODEL)), _const_spec((H_A * V_DIM, D_MODEL)),
                  _const_spec((D_MODEL, D_MODEL)), _const_spec((D_MODEL, D_FF)), _const_spec((D_FF, D_MODEL))],
        out_specs=tile(D_MODEL),
        out_shape=jax.ShapeDtypeStruct((n, D_MODEL), F32),
        compiler_params=_cparams(("arbitrary",)),
        name="stage_c",
    )(x, mods, hf, hb, og, attn, ga, gb, hn, post1, pre2, post2, wmo, wao, wout, w1, w2)


def _rope_tables(n_tokens):
    pos = np.arange(n_tokens)
    row = (pos // GRID_W).astype(np.float32)
    col = (pos % GRID_W).astype(np.float32)
    inv = (ROPE_BASE ** (-np.arange(0, AX_DIM, 2, dtype=np.float32) / AX_DIM)).astype(np.float32)
    ang = np.concatenate([row[:, None] * inv, col[:, None] * inv], axis=-1)
    cos = np.cos(ang.astype(np.float64)).astype(np.float32)
    sin = np.sin(ang.astype(np.float64)).astype(np.float32)
    ones = np.ones((n_tokens, NOPE), np.float32)
    zeros = np.zeros((n_tokens, NOPE), np.float32)
    pad = np.zeros((n_tokens, HEAD_PAD - NOPE - ROPE_DIM), np.float32)
    ta = np.concatenate([ones, cos, cos, pad], axis=-1)
    tb = np.concatenate([zeros, -sin, sin, pad], axis=-1)
    return jnp.asarray(ta), jnp.asarray(tb)


def _plain_tables(n_tokens):
    ones = jnp.ones((n_tokens, NOPE + ROPE_DIM), F32)
    pad = jnp.zeros((n_tokens, HEAD_PAD - NOPE - ROPE_DIM), F32)
    return jnp.concatenate([ones, pad], axis=-1), jnp.zeros((n_tokens, HEAD_PAD), F32)


def kernel(x_prompt, x_sample, cache_mla_ckv, cache_mla_krope, state_mlstm_C, state_mlstm_n, state_mlstm_m,
           c, c_ctx, w_ada, b_ada, norm_pre1, norm_post1, norm_pre2, norm_post2, w_in, mlstm_gate_b,
           mla_q_norm, mla_kv_norm, w_uq, w_ukv, w_mla_o, mlstm_head_norm, w_mlstm_o, w_out, w_mlp1, w_mlp2):
    bp, sp, _ = x_prompt.shape
    bs, ss, _ = x_sample.shape
    depth = w_in.shape[0]
    past = cache_mla_ckv.shape[2]
    nj = 2 * H_M
    even = np.arange(0, ROPE_DIM, 2)
    odd = np.arange(1, ROPE_DIM, 2)
    perm = np.concatenate([even, odd])
    perm_sw = np.concatenate([odd, even])

    xp = x_prompt.reshape(bp * sp, D_MODEL)
    xs = x_sample.reshape(bs * ss, D_MODEL)
    cc = jnp.zeros((8, D_MODEL), F32).at[:bs].set(c).at[bs].set(c_ctx)
    ta_lat, tb_lat = _rope_tables(ss)
    tm_ctx, tm_lat = 512, 512
    assert past == tm_lat
    ta_ctx, tb_ctx = _plain_tables(tm_ctx)

    new_ckv, new_krope, new_c, new_n, new_m = [], [], [], [], []
    for l in range(depth):
        cols = np.cumsum((W_BIG_COLS, N_GATES, Q_LORA, KV_LORA, ROPE_DIM))
        w_big, w_g, w_cq, w_ckv, w_kr, w_merge = jnp.split(w_in[l].astype(BF16), cols.tolist(), axis=1)
        w_g = w_g.reshape(D_MODEL, 2, 2, H_M).transpose(0, 2, 1, 3).reshape(D_MODEL, N_GATES)
        w_misc = jnp.concatenate([w_kr, w_g, jnp.zeros((D_MODEL, MISC_W - 3 * ROPE_DIM - N_GATES), BF16),
                                  w_kr[:, perm], w_kr[:, perm_sw]], axis=1)
        wa = (w_big, w_merge, jnp.concatenate([w_cq, w_ckv, w_misc], axis=1))
        gbias = jnp.zeros((1, MISC_W), F32).at[0, GATE_LANE0:GATE_LANE0 + N_GATES].set(
            mlstm_gate_b[l].transpose(1, 0, 2).reshape(N_GATES))
        uq = w_uq[l]
        wq = jnp.concatenate([uq[..., :NOPE], uq[..., NOPE:][..., perm], uq[..., NOPE:][..., perm_sw]],
                             axis=-1).reshape(Q_LORA, H_A * HEAD_PAD).astype(BF16)
        ukv = w_ukv[l]
        wk = jnp.concatenate([ukv[..., :NOPE], jnp.zeros((KV_LORA, H_A, HEAD_PAD - NOPE), F32)],
                             axis=-1).reshape(KV_LORA, H_A * HEAD_PAD).astype(BF16)
        wv = ukv[..., NOPE:].reshape(KV_LORA, H_A * V_DIM).T.astype(BF16)
        wmo = w_mlstm_o[l].astype(BF16)
        wao = w_mla_o[l].astype(BF16)
        wout = w_out[l].astype(BF16)
        w1 = w_mlp1[l].astype(BF16)
        w2 = w_mlp2[l].astype(BF16)
        pre1, post1 = norm_pre1[l][None], norm_post1[l][None]
        pre2, post2 = norm_pre2[l][None], norm_post2[l][None]
        qn, kvn = mla_q_norm[l][None], mla_kv_norm[l][None]
        hn = mlstm_head_norm[l].reshape(1, MLSTM_W)

        mods = _modulation(cc, w_ada[l], b_ada[l][None]).reshape(8, N_MOD, D_MODEL)

        ctx_mod = lambda i: bs
        a = _stage_a(xp, mods, ctx_mod, pre1, wa, gbias, qn, kvn, wq, wk, wv, ta_ctx, tb_ctx,
                     lambda i: 0, tm_ctx, sp)
        mq, mk, mv, og, ga, gb, gates, q, k, v, ckv_n, kro = a
        hf, hb, c_fin, n_fin, m_fin = _mlstm(mq, mk, mv, *_gate_rows(gates, GATE_CHUNKS_PER_STEP),
                                             None, None, bp, sp)
        attn = _attention(q, k, v, None, None, bp, sp, sp, 1)
        xp = _stage_c(xp, mods, ctx_mod, hf, hb, og, attn, ga, gb, hn, post1, pre2, post2,
                      wmo, wao, wout, w1, w2, tm_ctx)
        new_ckv.append(ckv_n.reshape(bp, sp, KV_LORA))
        new_krope.append(kro.reshape(bp, sp, ROPE_DIM))
        new_c.append(c_fin.reshape(bp, 2, H_M, DH_M, DH_M))
        new_n.append(n_fin.reshape(bp, 2, H_M, DH_M))
        new_m.append(m_fin[:, :, 0].reshape(bp, 2, H_M))

        tiles_per_seq = ss // tm_lat
        lat_mod = lambda i: i // tiles_per_seq
        a = _stage_a(xs, mods, lat_mod, pre1, wa, gbias, qn, kvn, wq, wk, wv, ta_lat, tb_lat,
                     lambda i: i % tiles_per_seq, tm_lat, ATTN_KEY_TILE)
        mq, mk, mv, og, ga, gb, gates, q, k, v, _, _ = a
        kr_cache = jnp.zeros((bs * past, LANE), F32).at[:, NOPE:NOPE + ROPE_DIM].set(
            cache_mla_krope[:, l].reshape(bs * past, ROPE_DIM)[:, perm])
        kc, vc = _kv_cache(cache_mla_ckv[:, l].reshape(bs * past, KV_LORA), kr_cache, wk, wv, past)
        n0 = state_mlstm_n[:, l].reshape(bs, nj, DH_M, 1)
        c0 = jnp.concatenate([state_mlstm_C[:, l].reshape(bs, nj, DH_M, DH_M),
                              jnp.broadcast_to(n0, (bs, nj, DH_M, DH_M))], axis=-1)
        m0 = jnp.broadcast_to(state_mlstm_m[:, l].reshape(bs, nj, 1), (bs, nj, LANE))
        hf, hb, _, _, _ = _mlstm(mq, mk, mv, *_gate_rows(gates, GATE_CHUNKS_PER_STEP), c0, m0, bs, ss)
        attn = _attention(q, k, v, kc, vc, bs, ss, 256, KEY_TILES_PER_STEP)
        xs = _stage_c(xs, mods, lat_mod, hf, hb, og, attn, ga, gb, hn, post1, pre2, post2,
                      wmo, wao, wout, w1, w2, tm_lat)

    return (xp.reshape(bp, sp, D_MODEL), xs.reshape(bs, ss, D_MODEL),
            jnp.stack(new_ckv, axis=1), jnp.stack(new_krope, axis=1), jnp.stack(new_c, axis=1),
            jnp.stack(new_n, axis=1), jnp.stack(new_m, axis=1))
```

```python
import functools

import numpy as np
import jax
import jax.numpy as jnp
from jax import lax
from jax.experimental import pallas as pl
from jax.experimental.pallas import tpu as pltpu

F32 = jnp.float32
BF16 = jnp.bfloat16

D_MODEL = 1024
H_M = 4
DH_M = 128
MLSTM_W = H_M * DH_M
CHUNK = 128
H_A = 8
NOPE = 64
ROPE_DIM = 32
V_DIM = 64
Q_LORA = 384
KV_LORA = 256
AX_DIM = ROPE_DIM // 2
ROPE_BASE = 10000.0
GRID_W = 64
D_FF = 4 * D_MODEL
N_MOD = 6
EPS = 1e-6
N_GATES = 4 * H_M
M_INIT = -1e30

LANE = 128
HEAD_PAD = LANE
MISC_W = LANE
GATE_LANE0 = ROPE_DIM
OFF_MQ, OFF_MK, OFF_MV, OFF_MO, W_BIG_COLS = 0, MLSTM_W, 2 * MLSTM_W, 3 * MLSTM_W, 4 * MLSTM_W
OFF_CQ, OFF_CKV, OFF_MISC = 0, Q_LORA, Q_LORA + KV_LORA
W_SMALL_COLS = OFF_MISC + MISC_W

VMEM_LIMIT = 60 * 1024 * 1024
QK_SCALE = float((NOPE + ROPE_DIM) ** -0.5 * np.log2(np.e))
QK_LOOKAHEAD = 3
ATTN_KEY_TILE = 512
ATTN_QUERY_TILE = 256
ROW_SPLITS = 2
KEY_TILES_PER_STEP = 3
ONES_ROWS = 16


def _cparams(sem):
    return pltpu.CompilerParams(dimension_semantics=sem, vmem_limit_bytes=VMEM_LIMIT)


def _const_spec(shape):
    nd = len(shape)
    return pl.BlockSpec(shape, lambda *_: (0,) * nd, pipeline_mode=pl.Buffered(1))


def _rms(x, w):
    return x * lax.rsqrt(jnp.mean(x * x, axis=-1, keepdims=True) + EPS) * w


def _sigmoid(x):
    return 1.0 / (1.0 + jnp.exp(-x))


def _log_sigmoid(x):
    return jnp.minimum(x, 0.0) - jnp.log(1.0 + jnp.exp(-jnp.abs(x)))


def _bdot(a, b):
    return jnp.dot(a, b, preferred_element_type=F32)


def _mod_kernel(c_ref, w_ref, b_ref, o_ref):
    c = c_ref[...]
    s = c * _sigmoid(c)
    o_ref[...] = jnp.dot(s, w_ref[...], preferred_element_type=F32,
                         precision=lax.Precision.HIGHEST) + b_ref[...]


def _modulation(cc, w_ada, b_ada):
    n_out = w_ada.shape[1]
    tn = 1536
    return pl.pallas_call(
        _mod_kernel,
        grid=(n_out // tn,),
        in_specs=[pl.BlockSpec((8, D_MODEL), lambda j: (0, 0)),
                  pl.BlockSpec((D_MODEL, tn), lambda j: (0, j)),
                  pl.BlockSpec((1, tn), lambda j: (0, j))],
        out_specs=pl.BlockSpec((8, tn), lambda j: (0, j)),
        out_shape=jax.ShapeDtypeStruct((8, n_out), F32),
        compiler_params=_cparams(("arbitrary",)),
        name="modulation",
    )(cc, w_ada, b_ada)


def _stage_a_kernel(x_ref, mod_ref, pre1_ref, wbig_ref, wmerge_ref, wsmall_ref, gbias_ref, qn_ref, kvn_ref,
                    wq_ref, wk_ref, wv_ref,
                    ta_ref, tb_ref,
                    mq_ref, mk_ref, mv_ref, og_ref, ga_ref, gb_ref, gates_ref, q_ref, k_ref, v_ref,
                    ckv_ref, kro_ref):
    mod = mod_ref[0]
    shift1, scale1 = mod[0:1], mod[1:2]
    tm = x_ref.shape[0]
    rows = tm // ROW_SPLITS
    tkv = v_ref.shape[2]

    def normed(rs):
        return (_rms(x_ref[rs, :], pre1_ref[...]) * (1.0 + scale1) + shift1).astype(BF16)

    def small_proj(hb):
        return tuple(_bdot(hb, wsmall_ref[:, lo:hi])
                     for lo, hi in ((OFF_CQ, OFF_CKV), (OFF_CKV, OFF_MISC), (OFF_MISC, W_SMALL_COLS)))

    def big_proj(hb, rs):
        proj = lambda lo, hi: _bdot(hb, wbig_ref[:, lo:hi])
        mq_ref[rs, :] = (proj(OFF_MQ, OFF_MK) * DH_M ** -0.5).astype(BF16)
        mk_ref[:, rs] = proj(OFF_MK, OFF_MV).T.astype(BF16)
        mv_ref[rs, :] = proj(OFF_MV, OFF_MO).astype(BF16)
        og_ref[rs, :] = _sigmoid(proj(OFF_MO, W_BIG_COLS)).astype(BF16)
        ga_ref[rs, :] = _sigmoid(_bdot(hb, wmerge_ref[:, :D_MODEL])).astype(BF16)
        gb_ref[rs, :] = _sigmoid(_bdot(hb, wmerge_ref[:, D_MODEL:])).astype(BF16)

    def mla_build(cq, ckv, misc, rs):
        lane = lax.broadcasted_iota(jnp.int32, misc.shape, 1)
        g = misc + gbias_ref[...]
        is_f = (lane >= GATE_LANE0 + 2 * H_M) & (lane < GATE_LANE0 + N_GATES)
        gates_ref[rs, :] = jnp.where(is_f, _log_sigmoid(g), g)
        kro_ref[rs, :] = misc[:, 0:ROPE_DIM]

        ckv_n = _rms(ckv, kvn_ref[...])
        ckv_ref[rs, :] = ckv_n
        ckv_b = ckv_n.astype(BF16)

        ta = ta_ref[rs, :]
        tb = tb_ref[rs, :]
        qf = _bdot(_rms(cq, qn_ref[...]).astype(BF16), wq_ref[...])
        ta8 = jnp.concatenate([ta] * H_A, axis=1)
        tb8 = jnp.concatenate([tb] * H_A, axis=1)
        q = qf * ta8 + pltpu.roll(qf, H_A * HEAD_PAD - ROPE_DIM, 1) * tb8
        q_ref[rs, :] = (q * QK_SCALE).astype(BF16)

        ta_k = jnp.where(lane < NOPE, 0.0, ta)
        kr = misc * ta_k + pltpu.roll(misc, MISC_W - ROPE_DIM, 1) * tb
        kk = _bdot(ckv_b, wk_ref[...]) + jnp.concatenate([kr] * H_A, axis=1)
        k_ref[rs, :] = kk.astype(BF16)
        vt = _bdot(wv_ref[...], ckv_n.T.astype(BF16)).astype(BF16)
        step = min(rows, tkv)
        for off in range(0, rows, step):
            lo = rs.start + off
            v_ref[lo // tkv, :, lo % tkv:lo % tkv + step] = vt[:, off:off + step]

    groups = [slice(r * rows, (r + 1) * rows) for r in range(ROW_SPLITS)]
    hbs = [normed(rs) for rs in groups]
    for hb, rs in zip(hbs, groups):
        small = small_proj(hb)
        big_proj(hb, rs)
        mla_build(*small, rs)


def _stage_a(x, mods, mod_index, pre1, wa, gbias, qn, kvn, wq, wk, wv, ta, tb, table_index, tm, tkv):
    n = x.shape[0]
    tile = lambda w: pl.BlockSpec((tm, w), lambda i: (i, 0))
    bf = lambda w: jax.ShapeDtypeStruct((n, w), BF16)
    f32 = lambda w: jax.ShapeDtypeStruct((n, w), F32)
    return pl.pallas_call(
        _stage_a_kernel,
        grid=(n // tm,),
        in_specs=[tile(D_MODEL),
                  pl.BlockSpec((1, N_MOD, D_MODEL), lambda i: (mod_index(i), 0, 0)),
                  _const_spec((1, D_MODEL)),
                  _const_spec((D_MODEL, W_BIG_COLS)),
                  _const_spec((D_MODEL, 2 * D_MODEL)),
                  _const_spec((D_MODEL, W_SMALL_COLS)),
                  _const_spec((1, MISC_W)),
                  _const_spec((1, Q_LORA)),
                  _const_spec((1, KV_LORA)),
                  _const_spec((Q_LORA, H_A * HEAD_PAD)),
                  _const_spec((KV_LORA, H_A * HEAD_PAD)),
                  _const_spec((H_A * V_DIM, KV_LORA)),
                  pl.BlockSpec((tm, LANE), lambda i: (table_index(i), 0)),
                  pl.BlockSpec((tm, LANE), lambda i: (table_index(i), 0))],
        out_specs=[tile(MLSTM_W), pl.BlockSpec((MLSTM_W, tm), lambda i: (0, i)), tile(MLSTM_W), tile(MLSTM_W),
                   tile(D_MODEL), tile(D_MODEL), tile(MISC_W),
                   tile(H_A * HEAD_PAD), tile(H_A * HEAD_PAD),
                   pl.BlockSpec((tm // tkv, H_A * V_DIM, tkv), lambda i: (i, 0, 0)),
                   tile(KV_LORA), tile(ROPE_DIM)],
        out_shape=[bf(MLSTM_W), jax.ShapeDtypeStruct((MLSTM_W, n), BF16), bf(MLSTM_W), bf(MLSTM_W),
                   bf(D_MODEL), bf(D_MODEL), f32(MISC_W),
                   bf(H_A * HEAD_PAD), bf(H_A * HEAD_PAD),
                   jax.ShapeDtypeStruct((n // tkv, H_A * V_DIM, tkv), BF16),
                   f32(KV_LORA), f32(ROPE_DIM)],
        compiler_params=_cparams(("arbitrary",)),
        name="stage_a",
    )(x, mods, pre1, *wa, gbias, qn, kvn, wq, wk, wv, ta, tb)


def _kv_cache_kernel(ckv_ref, kr_ref, wk_ref, wv_ref, k_ref, v_ref):
    ckv_b = ckv_ref[...].astype(BF16)
    kk = _bdot(ckv_b, wk_ref[...]) + jnp.concatenate([kr_ref[...]] * H_A, axis=1)
    k_ref[...] = kk.astype(BF16)
    v_ref[0] = _bdot(wv_ref[...], ckv_ref[...].T.astype(BF16)).astype(BF16)


def _kv_cache(ckv, kr, wk, wv, tm):
    n = ckv.shape[0]
    return pl.pallas_call(
        _kv_cache_kernel,
        grid=(n // tm,),
        in_specs=[pl.BlockSpec((tm, KV_LORA), lambda i: (i, 0)),
                  pl.BlockSpec((tm, LANE), lambda i: (i, 0)),
                  _const_spec((KV_LORA, H_A * HEAD_PAD)),
                  _const_spec((H_A * V_DIM, KV_LORA))],
        out_specs=[pl.BlockSpec((tm, H_A * HEAD_PAD), lambda i: (i, 0)),
                   pl.BlockSpec((1, H_A * V_DIM, tm), lambda i: (i, 0, 0))],
        out_shape=[jax.ShapeDtypeStruct((n, H_A * HEAD_PAD), BF16),
                   jax.ShapeDtypeStruct((n // tm, H_A * V_DIM, tm), BF16)],
        compiler_params=_cparams(("arbitrary",)),
        name="kv_cache",
    )(ckv, kr, wk, wv)


def _split3(x):
    x1 = x.astype(BF16).astype(F32)
    r = x - x1
    x2 = r.astype(BF16).astype(F32)
    x3 = (r - x2).astype(BF16).astype(F32)
    return x1, x2, x3


GATE_ROWS = 32
GATE_CHUNKS_PER_STEP = 16


def _gate_rows_kernel(g_ref, cols_ref, rows_ref):
    nj = 2 * H_M
    nch = g_ref.shape[0] // CHUNK
    n_rows = nch * nj
    row = lax.broadcasted_iota(jnp.int32, (CHUNK, CHUNK), 0)
    col = lax.broadcasted_iota(jnp.int32, (CHUNK, CHUNK), 1)
    lower = (col <= row).astype(F32)
    upper = (col >= row).astype(F32)
    hi = lax.Precision.HIGHEST
    rowi = lax.broadcasted_iota(jnp.int32, (n_rows, CHUNK), 0)
    lane = lax.broadcasted_iota(jnp.int32, (n_rows, CHUNK), 1)
    is_fwd = rowi % nj < H_M

    gi, gf = GATE_LANE0, GATE_LANE0 + nj
    i_rows, f_rows = [], []
    for c in range(nch):
        g_t = g_ref[c * CHUNK:(c + 1) * CHUNK, :].T
        i_rows.append(g_t[gi:gi + nj])
        f_rows.append(g_t[gf:gf + nj])
    i_all = jnp.concatenate(i_rows, axis=0)
    f_all = jnp.concatenate(f_rows, axis=0)
    b = jnp.where(is_fwd, jnp.dot(f_all, upper, preferred_element_type=F32, precision=hi),
                  jnp.dot(f_all, lower, preferred_element_type=F32, precision=hi))
    a = i_all - b
    cm = a
    shift = 1
    while shift < CHUNK:
        y_f = jnp.where(lane >= shift, pltpu.roll(cm, shift, 1), -jnp.inf)
        y_b = jnp.where(lane < CHUNK - shift, pltpu.roll(cm, CHUNK - shift, 1), -jnp.inf)
        cm = jnp.maximum(cm, jnp.where(is_fwd, y_f, y_b))
        shift *= 2
    a_max = jnp.broadcast_to(jnp.max(a, axis=1, keepdims=True), a.shape)
    f_sum = jnp.broadcast_to(jnp.sum(f_all, axis=1, keepdims=True), a.shape)
    zeros8 = jnp.zeros((nj, CHUNK), F32)
    for c in range(nch):
        sl = slice(c * nj, (c + 1) * nj)
        stack = jnp.concatenate([cm[sl], b[sl]] + [zeros8] * (CHUNK // nj - 2), axis=0)
        cols_ref[c * CHUNK:(c + 1) * CHUNK, :] = stack.T
        rows_ref[c] = jnp.concatenate([a[sl], a_max[sl], f_sum[sl], zeros8], axis=0)


def _gate_rows(gates, chunks_per_step):
    n = gates.shape[0]
    tm = chunks_per_step * CHUNK
    return pl.pallas_call(
        _gate_rows_kernel,
        grid=(n // tm,),
        in_specs=[pl.BlockSpec((tm, MISC_W), lambda i: (i, 0))],
        out_specs=[pl.BlockSpec((tm, LANE), lambda i: (i, 0)),
                   pl.BlockSpec((chunks_per_step, GATE_ROWS, CHUNK), lambda i: (i, 0, 0))],
        out_shape=[jax.ShapeDtypeStruct((n, LANE), F32),
                   jax.ShapeDtypeStruct((n // CHUNK, GATE_ROWS, CHUNK), F32)],
        compiler_params=_cparams(("arbitrary",)),
        name="gate_rows",
    )(gates)


def _mlstm_kernel(*refs, has_state):
    if has_state:
        (qf_ref, ktf_ref, vf_ref, ptf_ref, rf_ref, qb_ref, ktb_ref, vb_ref, ptb_ref, rb_ref, c0_ref, m0_ref,
         hf_ref, hb_ref, cn_ref, nn_ref, mn_ref, c_s, m_s) = refs
    else:
        (qf_ref, ktf_ref, vf_ref, ptf_ref, rf_ref, qb_ref, ktb_ref, vb_ref, ptb_ref, rb_ref,
         hf_ref, hb_ref, cn_ref, nn_ref, mn_ref, c_s, m_s) = refs
    step = pl.program_id(1)

    @pl.when(step == 0)
    def _():
        if has_state:
            c_s[...] = c0_ref[0]
            m_s[...] = m0_ref[0]
        else:
            c_s[...] = jnp.zeros(c_s.shape, F32)
            m_s[...] = jnp.full(m_s.shape, M_INIT, F32)

    nj = 2 * H_M
    row = lax.broadcasted_iota(jnp.int32, (CHUNK, CHUNK), 0)
    col = lax.broadcasted_iota(jnp.int32, (CHUNK, CHUNK), 1)
    is_fwd = lax.broadcasted_iota(jnp.int32, (nj, CHUNK), 0) < H_M
    rows_f, rows_b = rf_ref[0], rb_ref[0]
    pick = lambda g: jnp.where(is_fwd, rows_f[g * nj:(g + 1) * nj], rows_b[g * nj:(g + 1) * nj])
    a8, a_max8, f_sum8 = pick(0), pick(1), pick(2)
    m8 = m_s[...]
    mx8 = jnp.maximum(m8, a_max8)
    e8 = jnp.exp(a8 - mx8)
    decay8 = jnp.exp(m8 - mx8)
    m_s[...] = f_sum8 + mx8

    ones_v = jnp.ones((CHUNK, DH_M), BF16)
    dirs = ((qf_ref, ktf_ref, vf_ref, ptf_ref, hf_ref), (qb_ref, ktb_ref, vb_ref, ptb_ref, hb_ref))
    chains = [(d, hd) for d in range(2) for hd in range(H_M)]
    first = []
    for d, hd in chains:
        q_ref, kt_ref, v_ref, _, _ = dirs[d]
        j = d * H_M + hd
        sl = slice(hd * DH_M, (hd + 1) * DH_M)
        q = q_ref[:, sl]
        k_t = kt_ref[sl, :]
        v_aug = jnp.concatenate([v_ref[:, sl], ones_v], axis=1)
        qk = _bdot(q, k_t)
        c_st = c_s[j]
        qc = _bdot(q, c_st.astype(BF16))
        upd = _bdot((k_t.astype(F32) * e8[j:j + 1, :]).astype(BF16), v_aug)
        dec = decay8[j:j + 1, :]
        c_s[j] = jnp.concatenate([dec, dec], axis=1) * c_st + upd
        first.append((qk, qc, v_aug))
    for (d, hd), (qk, qc, v_aug) in zip(chains, first):
        cols_ref, h_ref = dirs[d][3], dirs[d][4]
        j = d * H_M + hd
        sl = slice(hd * DH_M, (hd + 1) * DH_M)
        mask = (col <= row) if d == 0 else (col >= row)
        m_row = m8[j:j + 1, :]
        u = jnp.maximum(m_row, cols_ref[:, j:j + 1])
        w = jnp.exp(jnp.where(mask, a8[j:j + 1, :] - u, -jnp.inf))
        s_inter = jnp.exp(m_row - u)
        intra = _bdot((qk * w).astype(BF16), v_aug)
        num = s_inter * qc[:, :DH_M] + intra[:, :DH_M]
        den = s_inter * qc[:, DH_M:] + intra[:, DH_M:]
        h_ref[:, sl] = num / jnp.maximum(jnp.abs(den), jnp.exp(-(cols_ref[:, nj + j:nj + j + 1] + u)))

    @pl.when(step == pl.num_programs(1) - 1)
    def _():
        for j in range(nj):
            c_aug = c_s[j]
            cn_ref[0, j] = c_aug[:, :DH_M]
            nn_ref[0, j:j + 1, :] = c_aug[:, DH_M:].T[0:1, :]
        mn_ref[0] = m_s[...]


def _mlstm(mq, mkt, mv, gate_pt, gate_rows, c0, m0, batch, seq):
    has_state = c0 is not None
    nc = seq // CHUNK
    n = batch * seq
    nj = 2 * H_M
    fwd = lambda w: pl.BlockSpec((CHUNK, w), lambda b, c: (b * nc + c, 0))
    bwd = lambda w: pl.BlockSpec((CHUNK, w), lambda b, c: (b * nc + nc - 1 - c, 0))
    fwd_t = pl.BlockSpec((MLSTM_W, CHUNK), lambda b, c: (0, b * nc + c))
    bwd_t = pl.BlockSpec((MLSTM_W, CHUNK), lambda b, c: (0, b * nc + nc - 1 - c))
    st_m = pl.BlockSpec((1, nj, LANE), lambda b, c: (b, 0, 0))
    st_c = pl.BlockSpec((1, nj, DH_M, 2 * DH_M), lambda b, c: (b, 0, 0, 0))
    fwd_r = pl.BlockSpec((1, GATE_ROWS, CHUNK), lambda b, c: (b * nc + c, 0, 0))
    bwd_r = pl.BlockSpec((1, GATE_ROWS, CHUNK), lambda b, c: (b * nc + nc - 1 - c, 0, 0))
    in_specs = [fwd(MLSTM_W), fwd_t, fwd(MLSTM_W), fwd(LANE), fwd_r,
                bwd(MLSTM_W), bwd_t, bwd(MLSTM_W), bwd(LANE), bwd_r]
    args = [mq, mkt, mv, gate_pt, gate_rows, mq, mkt, mv, gate_pt, gate_rows]
    if has_state:
        in_specs += [st_c, st_m]
        args += [c0, m0]
    return pl.pallas_call(
        functools.partial(_mlstm_kernel, has_state=has_state),
        grid=(batch, nc),
        in_specs=in_specs,
        out_specs=[fwd(MLSTM_W), bwd(MLSTM_W),
                   pl.BlockSpec((1, nj, DH_M, DH_M), lambda b, c: (b, 0, 0, 0)), st_m, st_m],
        out_shape=[jax.ShapeDtypeStruct((n, MLSTM_W), F32), jax.ShapeDtypeStruct((n, MLSTM_W), F32),
                   jax.ShapeDtypeStruct((batch, nj, DH_M, DH_M), F32),
                   jax.ShapeDtypeStruct((batch, nj, DH_M), F32),
                   jax.ShapeDtypeStruct((batch, nj, LANE), F32)],
        scratch_shapes=[pltpu.VMEM((nj, DH_M, 2 * DH_M), F32), pltpu.VMEM((nj, LANE), F32)],
        compiler_params=_cparams(("arbitrary", "arbitrary")),
        name="mlstm",
    )(*args)


def _attn_kernel(*refs, tiles_per_step, has_cache):
    if has_cache:
        q_ref, k_ref, vt_ref, kc_ref, vct_ref, o_ref, qt_s, acc_s, m_s, ot_s, s_scr = refs
    else:
        q_ref, k_ref, vt_ref, o_ref, qt_s, acc_s, m_s, ot_s, s_scr = refs
    n_tiles, _, tk = vt_ref.shape
    qt_s[...] = q_ref[...].astype(F32).T.astype(BF16)
    acc_s[...] = jnp.zeros(acc_s.shape, F32)
    m_s[...] = jnp.full(m_s.shape, -jnp.inf, F32)

    n_slots = s_scr.shape[0]
    assert (tiles_per_step * H_A) % n_slots == 0

    def scores(tile, head, slot):
        get_k, _, width = tile
        s = _bdot(get_k(head), qt_s[head * HEAD_PAD:(head + 1) * HEAD_PAD, :])
        s_scr[slot, 0:width, :] = s
        return jnp.max(s, axis=0, keepdims=True)

    def update(tiles, pending, next_tile):
        items = [(tile, head) for tile in tiles for head in range(H_A)]
        ahead = items + ([(next_tile, h) for h in range(QK_LOOKAHEAD)] if next_tile is not None else [])
        pending = list(pending)
        for idx, ((_, get_vt, width), head) in enumerate(items):
            m_tile = pending.pop(0)
            if idx + QK_LOOKAHEAD < len(ahead):
                pending.append(scores(*ahead[idx + QK_LOOKAHEAD], (idx + QK_LOOKAHEAD) % n_slots))
            m_old = m_s[head:head + 1, :]
            m_new = jnp.maximum(m_old, m_tile)
            alpha = jnp.exp2(m_old - m_new)
            p = jnp.exp2((s_scr[idx % n_slots, 0:width, :] - m_new).astype(BF16))
            ones = jnp.ones((ONES_ROWS, width), BF16)
            lhs = jnp.concatenate([get_vt(head), ones], axis=0)
            acc_s[head] = alpha * acc_s[head] + _bdot(lhs, p)
            m_s[head:head + 1, :] = m_new
        return tuple(pending)

    def main_tile(t):
        start = t * tk if isinstance(t, int) else pl.multiple_of(t * tk, tk)
        rows = pl.ds(start, tk)
        return (lambda h: k_ref[rows, h * HEAD_PAD:(h + 1) * HEAD_PAD],
                lambda h: vt_ref[t, h * V_DIM:(h + 1) * V_DIM, :], tk)

    def body(i, pending):
        first = i * tiles_per_step
        return update([main_tile(first + u) for u in range(tiles_per_step)], pending,
                      main_tile(first + tiles_per_step))

    tail = []
    if has_cache:
        tail = [(lambda h: kc_ref[:, h * HEAD_PAD:(h + 1) * HEAD_PAD],
                 lambda h: vct_ref[0, h * V_DIM:(h + 1) * V_DIM, :], kc_ref.shape[0])]
    n_loop = (n_tiles - 1) // tiles_per_step
    tail = [main_tile(t) for t in range(n_loop * tiles_per_step, n_tiles)] + tail
    pending = tuple(scores(main_tile(0), h, h) for h in range(QK_LOOKAHEAD))
    if n_loop:
        pending = lax.fori_loop(0, n_loop, body, pending)
    update(tail, pending, None)
    for head in range(H_A):
        acc = acc_s[head]
        ot_s[head * V_DIM:(head + 1) * V_DIM, :] = acc[0:V_DIM] * (1.0 / acc[V_DIM:V_DIM + 1])
    o_ref[...] = ot_s[...].T.astype(BF16)


def _attention(q, k, vt, kc, vct, batch, seq, tq, tiles_per_step):
    nq = seq // tq
    tk = vt.shape[2]
    n_tiles = seq // tk
    has_cache = kc is not None
    max_width = max(tk, kc.shape[0] // batch) if has_cache else tk
    in_specs = [pl.BlockSpec((tq, H_A * HEAD_PAD), lambda b, i: (b * nq + i, 0)),
                pl.BlockSpec((seq, H_A * HEAD_PAD), lambda b, i: (b, 0)),
                pl.BlockSpec((n_tiles, H_A * V_DIM, tk), lambda b, i: (b, 0, 0))]
    args = [q, k, vt]
    if has_cache:
        past = kc.shape[0] // batch
        in_specs += [pl.BlockSpec((past, H_A * HEAD_PAD), lambda b, i: (b, 0)),
                     pl.BlockSpec((1, H_A * V_DIM, past), lambda b, i: (b, 0, 0))]
        args += [kc, vct]
    return pl.pallas_call(
        functools.partial(_attn_kernel, tiles_per_step=tiles_per_step, has_cache=has_cache),
        grid=(batch, nq),
        in_specs=in_specs,
        out_specs=pl.BlockSpec((tq, H_A * V_DIM), lambda b, i: (b * nq + i, 0)),
        out_shape=jax.ShapeDtypeStruct((batch * seq, H_A * V_DIM), BF16),
        scratch_shapes=[pltpu.VMEM((H_A * HEAD_PAD, tq), BF16),
                        pltpu.VMEM((H_A, V_DIM + ONES_ROWS, tq), F32),
                        pltpu.VMEM((H_A, tq), F32),
                        pltpu.VMEM((H_A * V_DIM, tq), F32),
                        pltpu.VMEM((QK_LOOKAHEAD + 1, max_width, tq), F32)],
        compiler_params=_cparams(("arbitrary", "arbitrary")),
        name="mla_attention",
    )(*args)


def _stage_c_kernel(x_ref, mod_ref, hf_ref, hb_ref, og_ref, attn_ref, ga_ref, gb_ref, hn_ref,
                    post1_ref, pre2_ref, post2_ref, wmo_ref, wao_ref, wout_ref, w1_ref, w2_ref, y_ref):
    mod = mod_ref[0]
    gate1, shift2, scale2, gate2 = mod[2:3], mod[3:4], mod[4:5], mod[5:6]
    hn = hn_ref[...]
    rows = x_ref.shape[0] // ROW_SPLITS
    groups = [slice(r * rows, (r + 1) * rows) for r in range(ROW_SPLITS)]
    n_ff = D_FF // D_MODEL

    def mixer_in(rs):
        hm = hf_ref[rs, :] + hb_ref[rs, :]
        heads = [_rms(hm[:, hd * DH_M:(hd + 1) * DH_M], hn[:, hd * DH_M:(hd + 1) * DH_M]) for hd in range(H_M)]
        return (jnp.concatenate(heads, axis=1) * og_ref[rs, :].astype(F32)).astype(BF16)

    def branch_out(hm, rs):
        return _bdot(hm, wmo_ref[...]), _bdot(attn_ref[rs, :], wao_ref[...])

    def merge(y_m, y_a, rs):
        return (ga_ref[rs, :].astype(F32) * y_m + gb_ref[rs, :].astype(F32) * y_a).astype(BF16)

    def mid(mix, rs):
        x1 = x_ref[rs, :] + gate1 * _rms(mix, post1_ref[...])
        return x1, (_rms(x1, pre2_ref[...]) * (1.0 + scale2) + shift2).astype(BF16)

    def mlp(h2):
        up = lambda c: _bdot(h2, w1_ref[:, c * D_MODEL:(c + 1) * D_MODEL])
        ff = jnp.zeros((rows, D_MODEL), F32)
        nxt = up(0)
        for c in range(n_ff):
            a = jnp.maximum(nxt, 0.0)
            if c + 1 < n_ff:
                nxt = up(c + 1)
            ff = ff + _bdot((a * a).astype(BF16), w2_ref[c * D_MODEL:(c + 1) * D_MODEL, :])
        return ff

    hms = [mixer_in(rs) for rs in groups]
    ys = [branch_out(hm, rs) for hm, rs in zip(hms, groups)]
    mixes = [_bdot(merge(*y, rs), wout_ref[...]) for y, rs in zip(ys, groups)]
    mids = [mid(mix, rs) for mix, rs in zip(mixes, groups)]
    ffs = [mlp(h2) for _, h2 in mids]
    for (x1, _), ff, rs in zip(mids, ffs, groups):
        y_ref[rs, :] = x1 + gate2 * _rms(ff, post2_ref[...])


def _stage_c(x, mods, mod_index, hf, hb, og, attn, ga, gb, hn, post1, pre2, post2, wmo, wao, wout, w1, w2, tm):
    n = x.shape[0]
    tile = lambda w: pl.BlockSpec((tm, w), lambda i: (i, 0))
    return pl.pallas_call(
        _stage_c_kernel,
        grid=(n // tm,),
        in_specs=[tile(D_MODEL),
                  pl.BlockSpec((1, N_MOD, D_MODEL), lambda i: (mod_index(i), 0, 0)),
                  tile(MLSTM_W), tile(MLSTM_W), tile(MLSTM_W), tile(H_A * V_DIM),
                  tile(D_MODEL), tile(D_MODEL),
                  _const_spec((1, MLSTM_W)), _const_spec((1, D_MODEL)), _const_spec((1, D_MODEL)),
                  _const_spec((1, D_MODEL)),
                  _const_spec((MLSTM_W, D_MODEL)), _const_spec((H_A * V_DIM, D_MODEL)),
                  _const_spec((D_MODEL, D_MODEL)), _const_spec((D_MODEL, D_FF)), _const_spec((D_FF, D_MODEL))],
        out_specs=tile(D_MODEL),
        out_shape=jax.ShapeDtypeStruct((n, D_MODEL), F32),
        compiler_params=_cparams(("arbitrary",)),
        name="stage_c",
    )(x, mods, hf, hb, og, attn, ga, gb, hn, post1, pre2, post2, wmo, wao, wout, w1, w2)


def _rope_tables(n_tokens):
    pos = np.arange(n_tokens)
    row = (pos // GRID_W).astype(np.float32)
    col = (pos % GRID_W).astype(np.float32)
    inv = (ROPE_BASE ** (-np.arange(0, AX_DIM, 2, dtype=np.float32) / AX_DIM)).astype(np.float32)
    ang = np.concatenate([row[:, None] * inv, col[:, None] * inv], axis=-1)
    cos = np.cos(ang.astype(np.float64)).astype(np.float32)
    sin = np.sin(ang.astype(np.float64)).astype(np.float32)
    ones = np.ones((n_tokens, NOPE), np.float32)
    zeros = np.zeros((n_tokens, NOPE), np.float32)
    pad = np.zeros((n_tokens, HEAD_PAD - NOPE - ROPE_DIM), np.float32)
    ta = np.concatenate([ones, cos, cos, pad], axis=-1)
    tb = np.concatenate([zeros, -sin, sin, pad], axis=-1)
    return jnp.asarray(ta), jnp.asarray(tb)


def _plain_tables(n_tokens):
    ones = jnp.ones((n_tokens, NOPE + ROPE_DIM), F32)
    pad = jnp.zeros((n_tokens, HEAD_PAD - NOPE - ROPE_DIM), F32)
    return jnp.concatenate([ones, pad], axis=-1), jnp.zeros((n_tokens, HEAD_PAD), F32)


def kernel(x_prompt, x_sample, cache_mla_ckv, cache_mla_krope, state_mlstm_C, state_mlstm_n, state_mlstm_m,
           c, c_ctx, w_ada, b_ada, norm_pre1, norm_post1, norm_pre2, norm_post2, w_in, mlstm_gate_b,
           mla_q_norm, mla_kv_norm, w_uq, w_ukv, w_mla_o, mlstm_head_norm, w_mlstm_o, w_out, w_mlp1, w_mlp2):
    bp, sp, _ = x_prompt.shape
    bs, ss, _ = x_sample.shape
    depth = w_in.shape[0]
    past = cache_mla_ckv.shape[2]
    nj = 2 * H_M
    even = np.arange(0, ROPE_DIM, 2)
    odd = np.arange(1, ROPE_DIM, 2)
    perm = np.concatenate([even, odd])
    perm_sw = np.concatenate([odd, even])

    xp = x_prompt.reshape(bp * sp, D_MODEL)
    xs = x_sample.reshape(bs * ss, D_MODEL)
    cc = jnp.zeros((8, D_MODEL), F32).at[:bs].set(c).at[bs].set(c_ctx)
    ta_lat, tb_lat = _rope_tables(ss)
    tm_ctx, tm_lat = 512, 512
    assert past == tm_lat
    ta_ctx, tb_ctx = _plain_tables(tm_ctx)

    new_ckv, new_krope, new_c, new_n, new_m = [], [], [], [], []
    for l in range(depth):
        cols = np.cumsum((W_BIG_COLS, N_GATES, Q_LORA, KV_LORA, ROPE_DIM))
        w_big, w_g, w_cq, w_ckv, w_kr, w_merge = jnp.split(w_in[l].astype(BF16), cols.tolist(), axis=1)
        w_g = w_g.reshape(D_MODEL, 2, 2, H_M).transpose(0, 2, 1, 3).reshape(D_MODEL, N_GATES)
        w_misc = jnp.concatenate([w_kr, w_g, jnp.zeros((D_MODEL, MISC_W - 3 * ROPE_DIM - N_GATES), BF16),
                                  w_kr[:, perm], w_kr[:, perm_sw]], axis=1)
        wa = (w_big, w_merge, jnp.concatenate([w_cq, w_ckv, w_misc], axis=1))
        gbias = jnp.zeros((1, MISC_W), F32).at[0, GATE_LANE0:GATE_LANE0 + N_GATES].set(
            mlstm_gate_b[l].transpose(1, 0, 2).reshape(N_GATES))
        uq = w_uq[l]
        wq = jnp.concatenate([uq[..., :NOPE], uq[..., NOPE:][..., perm], uq[..., NOPE:][..., perm_sw]],
                             axis=-1).reshape(Q_LORA, H_A * HEAD_PAD).astype(BF16)
        ukv = w_ukv[l]
        wk = jnp.concatenate([ukv[..., :NOPE], jnp.zeros((KV_LORA, H_A, HEAD_PAD - NOPE), F32)],
                             axis=-1).reshape(KV_LORA, H_A * HEAD_PAD).astype(BF16)
        wv = ukv[..., NOPE:].reshape(KV_LORA, H_A * V_DIM).T.astype(BF16)
        wmo = w_mlstm_o[l].astype(BF16)
        wao = w_mla_o[l].astype(BF16)
        wout = w_out[l].astype(BF16)
        w1 = w_mlp1[l].astype(BF16)
        w2 = w_mlp2[l].astype(BF16)
        pre1, post1 = norm_pre1[l][None], norm_post1[l][None]
        pre2, post2 = norm_pre2[l][None], norm_post2[l][None]
        qn, kvn = mla_q_norm[l][None], mla_kv_norm[l][None]
        hn = mlstm_head_norm[l].reshape(1, MLSTM_W)

        mods = _modulation(cc, w_ada[l], b_ada[l][None]).reshape(8, N_MOD, D_MODEL)

        ctx_mod = lambda i: bs
        a = _stage_a(xp, mods, ctx_mod, pre1, wa, gbias, qn, kvn, wq, wk, wv, ta_ctx, tb_ctx,
                     lambda i: 0, tm_ctx, sp)
        mq, mk, mv, og, ga, gb, gates, q, k, v, ckv_n, kro = a
        hf, hb, c_fin, n_fin, m_fin = _mlstm(mq, mk, mv, *_gate_rows(gates, GATE_CHUNKS_PER_STEP),
                                             None, None, bp, sp)
        attn = _attention(q, k, v, None, None, bp, sp, sp, 1)
        xp = _stage_c(xp, mods, ctx_mod, hf, hb, og, attn, ga, gb, hn, post1, pre2, post2,
                      wmo, wao, wout, w1, w2, tm_ctx)
        new_ckv.append(ckv_n.reshape(bp, sp, KV_LORA))
        new_krope.append(kro.reshape(bp, sp, ROPE_DIM))
        new_c.append(c_fin.reshape(bp, 2, H_M, DH_M, DH_M))
        new_n.append(n_fin.reshape(bp, 2, H_M, DH_M))
        new_m.append(m_fin[:, :, 0].reshape(bp, 2, H_M))

        tiles_per_seq = ss // tm_lat
        lat_mod = lambda i: i // tiles_per_seq
        a = _stage_a(xs, mods, lat_mod, pre1, wa, gbias, qn, kvn, wq, wk, wv, ta_lat, tb_lat,
                     lambda i: i % tiles_per_seq, tm_lat, ATTN_KEY_TILE)
        mq, mk, mv, og, ga, gb, gates, q, k, v, _, _ = a
        kr_cache = jnp.zeros((bs * past, LANE), F32).at[:, NOPE:NOPE + ROPE_DIM].set(
            cache_mla_krope[:, l].reshape(bs * past, ROPE_DIM)[:, perm])
        kc, vc = _kv_cache(cache_mla_ckv[:, l].reshape(bs * past, KV_LORA), kr_cache, wk, wv, past)
        n0 = state_mlstm_n[:, l].reshape(bs, nj, DH_M, 1)
        c0 = jnp.concatenate([state_mlstm_C[:, l].reshape(bs, nj, DH_M, DH_M),
                              jnp.broadcast_to(n0, (bs, nj, DH_M, DH_M))], axis=-1)
        m0 = jnp.broadcast_to(state_mlstm_m[:, l].reshape(bs, nj, 1), (bs, nj, LANE))
        hf, hb, _, _, _ = _mlstm(mq, mk, mv, *_gate_rows(gates, GATE_CHUNKS_PER_STEP), c0, m0, bs, ss)
        attn = _attention(q, k, v, kc, vc, bs, ss, ATTN_QUERY_TILE, KEY_TILES_PER_STEP)
        xs = _stage_c(xs, mods, lat_mod, hf, hb, og, attn, ga, gb, hn, post1, pre2, post2,
                      wmo, wao, wout, w1, w2, tm_lat)

    return (xp.reshape(bp, sp, D_MODEL), xs.reshape(bs, ss, D_MODEL),
            jnp.stack(new_ckv, axis=1), jnp.stack(new_krope, axis=1), jnp.stack(new_c, axis=1),
            jnp.stack(new_n, axis=1), jnp.stack(new_m, axis=1))
```

```python
import functools

import numpy as np
import jax
import jax.numpy as jnp
from jax import lax
from jax.experimental import pallas as pl
from jax.experimental.pallas import tpu as pltpu

F32 = jnp.float32
BF16 = jnp.bfloat16

D_MODEL = 1024
H_M = 4
DH_M = 128
MLSTM_W = H_M * DH_M
CHUNK = 128
H_A = 8
NOPE = 64
ROPE_DIM = 32
V_DIM = 64
Q_LORA = 384
KV_LORA = 256
AX_DIM = ROPE_DIM // 2
ROPE_BASE = 10000.0
GRID_W = 64
D_FF = 4 * D_MODEL
N_MOD = 6
EPS = 1e-6
N_GATES = 4 * H_M
M_INIT = -1e30

LANE = 128
HEAD_PAD = LANE
MISC_W = LANE
GATE_LANE0 = ROPE_DIM
OFF_MQ, OFF_MK, OFF_MV, OFF_MO, W_BIG_COLS = 0, MLSTM_W, 2 * MLSTM_W, 3 * MLSTM_W, 4 * MLSTM_W
OFF_CQ, OFF_CKV, OFF_MISC = 0, Q_LORA, Q_LORA + KV_LORA
W_SMALL_COLS = OFF_MISC + MISC_W

VMEM_LIMIT = 60 * 1024 * 1024
QK_SCALE = float((NOPE + ROPE_DIM) ** -0.5 * np.log2(np.e))
QK_LOOKAHEAD = 3
ATTN_KEY_TILE = 512
ATTN_QUERY_TILE = 256
ROW_SPLITS = 2
KEY_TILES_PER_STEP = 3
ONES_ROWS = 16


def _cparams(sem):
    return pltpu.CompilerParams(dimension_semantics=sem, vmem_limit_bytes=VMEM_LIMIT)


def _const_spec(shape):
    nd = len(shape)
    return pl.BlockSpec(shape, lambda *_: (0,) * nd, pipeline_mode=pl.Buffered(1))


def _rms(x, w):
    return x * lax.rsqrt(jnp.mean(x * x, axis=-1, keepdims=True) + EPS) * w


def _sigmoid(x):
    return 1.0 / (1.0 + jnp.exp(-x))


def _log_sigmoid(x):
    return jnp.minimum(x, 0.0) - jnp.log(1.0 + jnp.exp(-jnp.abs(x)))


def _bdot(a, b):
    return jnp.dot(a, b, preferred_element_type=F32)


def _mod_kernel(c_ref, w_ref, b_ref, o_ref):
    c = c_ref[...]
    s = c * _sigmoid(c)
    o_ref[...] = jnp.dot(s, w_ref[...], preferred_element_type=F32,
                         precision=lax.Precision.HIGHEST) + b_ref[...]


def _modulation(cc, w_ada, b_ada):
    n_out = w_ada.shape[1]
    tn = 1536
    return pl.pallas_call(
        _mod_kernel,
        grid=(n_out // tn,),
        in_specs=[pl.BlockSpec((8, D_MODEL), lambda j: (0, 0)),
                  pl.BlockSpec((D_MODEL, tn), lambda j: (0, j)),
                  pl.BlockSpec((1, tn), lambda j: (0, j))],
        out_specs=pl.BlockSpec((8, tn), lambda j: (0, j)),
        out_shape=jax.ShapeDtypeStruct((8, n_out), F32),
        compiler_params=_cparams(("arbitrary",)),
        name="modulation",
    )(cc, w_ada, b_ada)


def _stage_a_kernel(x_ref, mod_ref, pre1_ref, wbig_ref, wmerge_ref, wsmall_ref, gbias_ref, qn_ref, kvn_ref,
                    wq_ref, wk_ref, wv_ref,
                    ta_ref, tb_ref,
                    mq_ref, mk_ref, mv_ref, og_ref, ga_ref, gb_ref, gates_ref, q_ref, k_ref, v_ref,
                    ckv_ref, kro_ref):
    mod = mod_ref[0]
    shift1, scale1 = mod[0:1], mod[1:2]
    tm = x_ref.shape[0]
    rows = tm // ROW_SPLITS
    tkv = v_ref.shape[2]

    def normed(rs):
        return (_rms(x_ref[rs, :], pre1_ref[...]) * (1.0 + scale1) + shift1).astype(BF16)

    def small_proj(hb):
        return tuple(_bdot(hb, wsmall_ref[:, lo:hi])
                     for lo, hi in ((OFF_CQ, OFF_CKV), (OFF_CKV, OFF_MISC), (OFF_MISC, W_SMALL_COLS)))

    def big_proj(hb, rs):
        proj = lambda lo, hi: _bdot(hb, wbig_ref[:, lo:hi])
        mq_ref[rs, :] = (proj(OFF_MQ, OFF_MK) * DH_M ** -0.5).astype(BF16)
        mk_ref[:, rs] = proj(OFF_MK, OFF_MV).T.astype(BF16)
        mv_ref[rs, :] = proj(OFF_MV, OFF_MO).astype(BF16)
        og_ref[rs, :] = _sigmoid(proj(OFF_MO, W_BIG_COLS)).astype(BF16)
        ga_ref[rs, :] = _sigmoid(_bdot(hb, wmerge_ref[:, :D_MODEL])).astype(BF16)
        gb_ref[rs, :] = _sigmoid(_bdot(hb, wmerge_ref[:, D_MODEL:])).astype(BF16)

    def mla_build(cq, ckv, misc, rs):
        lane = lax.broadcasted_iota(jnp.int32, misc.shape, 1)
        g = misc + gbias_ref[...]
        is_f = (lane >= GATE_LANE0 + 2 * H_M) & (lane < GATE_LANE0 + N_GATES)
        gates_ref[rs, :] = jnp.where(is_f, _log_sigmoid(g), g)
        kro_ref[rs, :] = misc[:, 0:ROPE_DIM]

        ckv_n = _rms(ckv, kvn_ref[...])
        ckv_ref[rs, :] = ckv_n
        ckv_b = ckv_n.astype(BF16)

        ta = ta_ref[rs, :]
        tb = tb_ref[rs, :]
        qf = _bdot(_rms(cq, qn_ref[...]).astype(BF16), wq_ref[...])
        ta8 = jnp.concatenate([ta] * H_A, axis=1)
        tb8 = jnp.concatenate([tb] * H_A, axis=1)
        q = qf * ta8 + pltpu.roll(qf, H_A * HEAD_PAD - ROPE_DIM, 1) * tb8
        q_ref[rs, :] = (q * QK_SCALE).astype(BF16)

        ta_k = jnp.where(lane < NOPE, 0.0, ta)
        kr = misc * ta_k + pltpu.roll(misc, MISC_W - ROPE_DIM, 1) * tb
        kk = _bdot(ckv_b, wk_ref[...]) + jnp.concatenate([kr] * H_A, axis=1)
        k_ref[rs, :] = kk.astype(BF16)
        vt = _bdot(wv_ref[...], ckv_n.T.astype(BF16)).astype(BF16)
        step = min(rows, tkv)
        for off in range(0, rows, step):
            lo = rs.start + off
            v_ref[lo // tkv, :, lo % tkv:lo % tkv + step] = vt[:, off:off + step]

    groups = [slice(r * rows, (r + 1) * rows) for r in range(ROW_SPLITS)]
    hbs = [normed(rs) for rs in groups]
    for hb, rs in zip(hbs, groups):
        small = small_proj(hb)
        big_proj(hb, rs)
        mla_build(*small, rs)


def _stage_a(x, mods, mod_index, pre1, wa, gbias, qn, kvn, wq, wk, wv, ta, tb, table_index, tm, tkv):
    n = x.shape[0]
    tile = lambda w: pl.BlockSpec((tm, w), lambda i: (i, 0))
    bf = lambda w: jax.ShapeDtypeStruct((n, w), BF16)
    f32 = lambda w: jax.ShapeDtypeStruct((n, w), F32)
    return pl.pallas_call(
        _stage_a_kernel,
        grid=(n // tm,),
        in_specs=[tile(D_MODEL),
                  pl.BlockSpec((1, N_MOD, D_MODEL), lambda i: (mod_index(i), 0, 0)),
                  _const_spec((1, D_MODEL)),
                  _const_spec((D_MODEL, W_BIG_COLS)),
                  _const_spec((D_MODEL, 2 * D_MODEL)),
                  _const_spec((D_MODEL, W_SMALL_COLS)),
                  _const_spec((1, MISC_W)),
                  _const_spec((1, Q_LORA)),
                  _const_spec((1, KV_LORA)),
                  _const_spec((Q_LORA, H_A * HEAD_PAD)),
                  _const_spec((KV_LORA, H_A * HEAD_PAD)),
                  _const_spec((H_A * V_DIM, KV_LORA)),
                  pl.BlockSpec((tm, LANE), lambda i: (table_index(i), 0)),
                  pl.BlockSpec((tm, LANE), lambda i: (table_index(i), 0))],
        out_specs=[tile(MLSTM_W), pl.BlockSpec((MLSTM_W, tm), lambda i: (0, i)), tile(MLSTM_W), tile(MLSTM_W),
                   tile(D_MODEL), tile(D_MODEL), tile(MISC_W),
                   tile(H_A * HEAD_PAD), tile(H_A * HEAD_PAD),
                   pl.BlockSpec((tm // tkv, H_A * V_DIM, tkv), lambda i: (i, 0, 0)),
                   tile(KV_LORA), tile(ROPE_DIM)],
        out_shape=[bf(MLSTM_W), jax.ShapeDtypeStruct((MLSTM_W, n), BF16), bf(MLSTM_W), bf(MLSTM_W),
                   bf(D_MODEL), bf(D_MODEL), f32(MISC_W),
                   bf(H_A * HEAD_PAD), bf(H_A * HEAD_PAD),
                   jax.ShapeDtypeStruct((n // tkv, H_A * V_DIM, tkv), BF16),
                   f32(KV_LORA), f32(ROPE_DIM)],
        compiler_params=_cparams(("arbitrary",)),
        name="stage_a",
    )(x, mods, pre1, *wa, gbias, qn, kvn, wq, wk, wv, ta, tb)


def _kv_cache_kernel(ckv_ref, kr_ref, wk_ref, wv_ref, k_ref, v_ref):
    ckv_b = ckv_ref[...].astype(BF16)
    kk = _bdot(ckv_b, wk_ref[...]) + jnp.concatenate([kr_ref[...]] * H_A, axis=1)
    k_ref[...] = kk.astype(BF16)
    v_ref[0] = _bdot(wv_ref[...], ckv_ref[...].T.astype(BF16)).astype(BF16)


def _kv_cache(ckv, kr, wk, wv, tm):
    n = ckv.shape[0]
    return pl.pallas_call(
        _kv_cache_kernel,
        grid=(n // tm,),
        in_specs=[pl.BlockSpec((tm, KV_LORA), lambda i: (i, 0)),
                  pl.BlockSpec((tm, LANE), lambda i: (i, 0)),
                  _const_spec((KV_LORA, H_A * HEAD_PAD)),
                  _const_spec((H_A * V_DIM, KV_LORA))],
        out_specs=[pl.BlockSpec((tm, H_A * HEAD_PAD), lambda i: (i, 0)),
                   pl.BlockSpec((1, H_A * V_DIM, tm), lambda i: (i, 0, 0))],
        out_shape=[jax.ShapeDtypeStruct((n, H_A * HEAD_PAD), BF16),
                   jax.ShapeDtypeStruct((n // tm, H_A * V_DIM, tm), BF16)],
        compiler_params=_cparams(("arbitrary",)),
        name="kv_cache",
    )(ckv, kr, wk, wv)


def _split3(x):
    x1 = x.astype(BF16).astype(F32)
    r = x - x1
    x2 = r.astype(BF16).astype(F32)
    x3 = (r - x2).astype(BF16).astype(F32)
    return x1, x2, x3


GATE_ROWS = 32
MLSTM_CHUNKS_PER_STEP = 4
GATE_CHUNKS_PER_STEP = 16


def _gate_rows_kernel(g_ref, cols_ref, rows_ref):
    nj = 2 * H_M
    nch = g_ref.shape[0] // CHUNK
    n_rows = nch * nj
    row = lax.broadcasted_iota(jnp.int32, (CHUNK, CHUNK), 0)
    col = lax.broadcasted_iota(jnp.int32, (CHUNK, CHUNK), 1)
    lower = (col <= row).astype(F32)
    upper = (col >= row).astype(F32)
    hi = lax.Precision.HIGHEST
    rowi = lax.broadcasted_iota(jnp.int32, (n_rows, CHUNK), 0)
    lane = lax.broadcasted_iota(jnp.int32, (n_rows, CHUNK), 1)
    is_fwd = rowi % nj < H_M

    gi, gf = GATE_LANE0, GATE_LANE0 + nj
    i_rows, f_rows = [], []
    for c in range(nch):
        g_t = g_ref[c * CHUNK:(c + 1) * CHUNK, :].T
        i_rows.append(g_t[gi:gi + nj])
        f_rows.append(g_t[gf:gf + nj])
    i_all = jnp.concatenate(i_rows, axis=0)
    f_all = jnp.concatenate(f_rows, axis=0)
    b = jnp.where(is_fwd, jnp.dot(f_all, upper, preferred_element_type=F32, precision=hi),
                  jnp.dot(f_all, lower, preferred_element_type=F32, precision=hi))
    a = i_all - b
    cm = a
    shift = 1
    while shift < CHUNK:
        y_f = jnp.where(lane >= shift, pltpu.roll(cm, shift, 1), -jnp.inf)
        y_b = jnp.where(lane < CHUNK - shift, pltpu.roll(cm, CHUNK - shift, 1), -jnp.inf)
        cm = jnp.maximum(cm, jnp.where(is_fwd, y_f, y_b))
        shift *= 2
    a_max = jnp.broadcast_to(jnp.max(a, axis=1, keepdims=True), a.shape)
    f_sum = jnp.broadcast_to(jnp.sum(f_all, axis=1, keepdims=True), a.shape)
    zeros8 = jnp.zeros((nj, CHUNK), F32)
    for c in range(nch):
        sl = slice(c * nj, (c + 1) * nj)
        stack = jnp.concatenate([cm[sl], b[sl]] + [zeros8] * (CHUNK // nj - 2), axis=0)
        cols_ref[c * CHUNK:(c + 1) * CHUNK, :] = stack.T
        rows_ref[c] = jnp.concatenate([a[sl], a_max[sl], f_sum[sl], zeros8], axis=0)


def _gate_rows(gates, chunks_per_step):
    n = gates.shape[0]
    tm = chunks_per_step * CHUNK
    return pl.pallas_call(
        _gate_rows_kernel,
        grid=(n // tm,),
        in_specs=[pl.BlockSpec((tm, MISC_W), lambda i: (i, 0))],
        out_specs=[pl.BlockSpec((tm, LANE), lambda i: (i, 0)),
                   pl.BlockSpec((chunks_per_step, GATE_ROWS, CHUNK), lambda i: (i, 0, 0))],
        out_shape=[jax.ShapeDtypeStruct((n, LANE), F32),
                   jax.ShapeDtypeStruct((n // CHUNK, GATE_ROWS, CHUNK), F32)],
        compiler_params=_cparams(("arbitrary",)),
        name="gate_rows",
    )(gates)


def _mlstm_kernel(*refs, has_state):
    if has_state:
        (qf_ref, ktf_ref, vf_ref, ptf_ref, rf_ref, qb_ref, ktb_ref, vb_ref, ptb_ref, rb_ref, c0_ref, m0_ref,
         hf_ref, hb_ref, cn_ref, nn_ref, mn_ref, c_s, m_s) = refs
    else:
        (qf_ref, ktf_ref, vf_ref, ptf_ref, rf_ref, qb_ref, ktb_ref, vb_ref, ptb_ref, rb_ref,
         hf_ref, hb_ref, cn_ref, nn_ref, mn_ref, c_s, m_s) = refs
    step = pl.program_id(1)

    @pl.when(step == 0)
    def _():
        if has_state:
            c_s[...] = c0_ref[0]
            m_s[...] = m0_ref[0]
        else:
            c_s[...] = jnp.zeros(c_s.shape, F32)
            m_s[...] = jnp.full(m_s.shape, M_INIT, F32)

    nj = 2 * H_M
    row = lax.broadcasted_iota(jnp.int32, (CHUNK, CHUNK), 0)
    col = lax.broadcasted_iota(jnp.int32, (CHUNK, CHUNK), 1)
    is_fwd = lax.broadcasted_iota(jnp.int32, (nj, CHUNK), 0) < H_M
    ones_v = jnp.ones((CHUNK, DH_M), BF16)
    dirs = ((qf_ref, ktf_ref, vf_ref, ptf_ref, hf_ref), (qb_ref, ktb_ref, vb_ref, ptb_ref, hb_ref))
    chains = [(d, hd) for d in range(2) for hd in range(H_M)]
    n_sub = rf_ref.shape[0]
    for sub in range(n_sub):
        sub_of = (sub, n_sub - 1 - sub)
        tok = [slice(s * CHUNK, (s + 1) * CHUNK) for s in sub_of]
        rows_f, rows_b = rf_ref[sub_of[0]], rb_ref[sub_of[1]]
        pick = lambda g: jnp.where(is_fwd, rows_f[g * nj:(g + 1) * nj], rows_b[g * nj:(g + 1) * nj])
        a8, a_max8, f_sum8 = pick(0), pick(1), pick(2)
        m8 = m_s[...]
        mx8 = jnp.maximum(m8, a_max8)
        e8 = jnp.exp(a8 - mx8)
        decay8 = jnp.exp(m8 - mx8)
        m_s[...] = f_sum8 + mx8

        first = []
        for d, hd in chains:
            q_ref, kt_ref, v_ref, _, _ = dirs[d]
            j = d * H_M + hd
            sl = slice(hd * DH_M, (hd + 1) * DH_M)
            q = q_ref[tok[d], sl]
            k_t = kt_ref[sl, tok[d]]
            v_aug = jnp.concatenate([v_ref[tok[d], sl], ones_v], axis=1)
            qk = _bdot(q, k_t)
            c_st = c_s[j]
            qc = _bdot(q, c_st.astype(BF16))
            upd = _bdot((k_t.astype(F32) * e8[j:j + 1, :]).astype(BF16), v_aug)
            dec = decay8[j:j + 1, :]
            c_s[j] = jnp.concatenate([dec, dec], axis=1) * c_st + upd
            first.append((qk, qc, v_aug))
        for (d, hd), (qk, qc, v_aug) in zip(chains, first):
            cols_ref, h_ref = dirs[d][3], dirs[d][4]
            j = d * H_M + hd
            sl = slice(hd * DH_M, (hd + 1) * DH_M)
            mask = (col <= row) if d == 0 else (col >= row)
            m_row = m8[j:j + 1, :]
            u = jnp.maximum(m_row, cols_ref[tok[d], j:j + 1])
            w = jnp.exp(jnp.where(mask, a8[j:j + 1, :] - u, -jnp.inf))
            s_inter = jnp.exp(m_row - u)
            intra = _bdot((qk * w).astype(BF16), v_aug)
            num = s_inter * qc[:, :DH_M] + intra[:, :DH_M]
            den = s_inter * qc[:, DH_M:] + intra[:, DH_M:]
            h_ref[tok[d], sl] = num / jnp.maximum(jnp.abs(den),
                                                  jnp.exp(-(cols_ref[tok[d], nj + j:nj + j + 1] + u)))

    @pl.when(step == pl.num_programs(1) - 1)
    def _():
        for j in range(nj):
            c_aug = c_s[j]
            cn_ref[0, j] = c_aug[:, :DH_M]
            nn_ref[0, j:j + 1, :] = c_aug[:, DH_M:].T[0:1, :]
        mn_ref[0] = m_s[...]


def _mlstm(mq, mkt, mv, gate_pt, gate_rows, c0, m0, batch, seq):
    has_state = c0 is not None
    n_sub = min(MLSTM_CHUNKS_PER_STEP, seq // CHUNK)
    tm = n_sub * CHUNK
    nc = seq // tm
    n = batch * seq
    nj = 2 * H_M
    fwd = lambda w: pl.BlockSpec((tm, w), lambda b, c: (b * nc + c, 0))
    bwd = lambda w: pl.BlockSpec((tm, w), lambda b, c: (b * nc + nc - 1 - c, 0))
    fwd_t = pl.BlockSpec((MLSTM_W, tm), lambda b, c: (0, b * nc + c))
    bwd_t = pl.BlockSpec((MLSTM_W, tm), lambda b, c: (0, b * nc + nc - 1 - c))
    st_m = pl.BlockSpec((1, nj, LANE), lambda b, c: (b, 0, 0))
    st_c = pl.BlockSpec((1, nj, DH_M, 2 * DH_M), lambda b, c: (b, 0, 0, 0))
    fwd_r = pl.BlockSpec((n_sub, GATE_ROWS, CHUNK), lambda b, c: (b * nc + c, 0, 0))
    bwd_r = pl.BlockSpec((n_sub, GATE_ROWS, CHUNK), lambda b, c: (b * nc + nc - 1 - c, 0, 0))
    in_specs = [fwd(MLSTM_W), fwd_t, fwd(MLSTM_W), fwd(LANE), fwd_r,
                bwd(MLSTM_W), bwd_t, bwd(MLSTM_W), bwd(LANE), bwd_r]
    args = [mq, mkt, mv, gate_pt, gate_rows, mq, mkt, mv, gate_pt, gate_rows]
    if has_state:
        in_specs += [st_c, st_m]
        args += [c0, m0]
    return pl.pallas_call(
        functools.partial(_mlstm_kernel, has_state=has_state),
        grid=(batch, nc),
        in_specs=in_specs,
        out_specs=[fwd(MLSTM_W), bwd(MLSTM_W),
                   pl.BlockSpec((1, nj, DH_M, DH_M), lambda b, c: (b, 0, 0, 0)), st_m, st_m],
        out_shape=[jax.ShapeDtypeStruct((n, MLSTM_W), F32), jax.ShapeDtypeStruct((n, MLSTM_W), F32),
                   jax.ShapeDtypeStruct((batch, nj, DH_M, DH_M), F32),
                   jax.ShapeDtypeStruct((batch, nj, DH_M), F32),
                   jax.ShapeDtypeStruct((batch, nj, LANE), F32)],
        scratch_shapes=[pltpu.VMEM((nj, DH_M, 2 * DH_M), F32), pltpu.VMEM((nj, LANE), F32)],
        compiler_params=_cparams(("arbitrary", "arbitrary")),
        name="mlstm",
    )(*args)


def _attn_kernel(*refs, tiles_per_step, has_cache):
    if has_cache:
        q_ref, k_ref, vt_ref, kc_ref, vct_ref, o_ref, qt_s, acc_s, m_s, ot_s, s_scr = refs
    else:
        q_ref, k_ref, vt_ref, o_ref, qt_s, acc_s, m_s, ot_s, s_scr = refs
    n_tiles, _, tk = vt_ref.shape
    qt_s[...] = q_ref[...].astype(F32).T.astype(BF16)
    acc_s[...] = jnp.zeros(acc_s.shape, F32)
    m_s[...] = jnp.full(m_s.shape, -jnp.inf, F32)

    n_slots = s_scr.shape[0]
    assert (tiles_per_step * H_A) % n_slots == 0

    def scores(tile, head, slot):
        get_k, _, width = tile
        s = _bdot(get_k(head), qt_s[head * HEAD_PAD:(head + 1) * HEAD_PAD, :])
        s_scr[slot, 0:width, :] = s
        return jnp.max(s, axis=0, keepdims=True)

    def update(tiles, pending, next_tile):
        items = [(tile, head) for tile in tiles for head in range(H_A)]
        ahead = items + ([(next_tile, h) for h in range(QK_LOOKAHEAD)] if next_tile is not None else [])
        pending = list(pending)
        for idx, ((_, get_vt, width), head) in enumerate(items):
            m_tile = pending.pop(0)
            if idx + QK_LOOKAHEAD < len(ahead):
                pending.append(scores(*ahead[idx + QK_LOOKAHEAD], (idx + QK_LOOKAHEAD) % n_slots))
            m_old = m_s[head:head + 1, :]
            m_new = jnp.maximum(m_old, m_tile)
            alpha = jnp.exp2(m_old - m_new)
            p = jnp.exp2((s_scr[idx % n_slots, 0:width, :] - m_new).astype(BF16))
            ones = jnp.ones((ONES_ROWS, width), BF16)
            lhs = jnp.concatenate([get_vt(head), ones], axis=0)
            acc_s[head] = alpha * acc_s[head] + _bdot(lhs, p)
            m_s[head:head + 1, :] = m_new
        return tuple(pending)

    def main_tile(t):
        start = t * tk if isinstance(t, int) else pl.multiple_of(t * tk, tk)
        rows = pl.ds(start, tk)
        return (lambda h: k_ref[rows, h * HEAD_PAD:(h + 1) * HEAD_PAD],
                lambda h: vt_ref[t, h * V_DIM:(h + 1) * V_DIM, :], tk)

    def body(i, pending):
        first = i * tiles_per_step
        return update([main_tile(first + u) for u in range(tiles_per_step)], pending,
                      main_tile(first + tiles_per_step))

    tail = []
    if has_cache:
        tail = [(lambda h: kc_ref[:, h * HEAD_PAD:(h + 1) * HEAD_PAD],
                 lambda h: vct_ref[0, h * V_DIM:(h + 1) * V_DIM, :], kc_ref.shape[0])]
    n_loop = (n_tiles - 1) // tiles_per_step
    tail = [main_tile(t) for t in range(n_loop * tiles_per_step, n_tiles)] + tail
    pending = tuple(scores(main_tile(0), h, h) for h in range(QK_LOOKAHEAD))
    if n_loop:
        pending = lax.fori_loop(0, n_loop, body, pending)
    update(tail, pending, None)
    for head in range(H_A):
        acc = acc_s[head]
        ot_s[head * V_DIM:(head + 1) * V_DIM, :] = acc[0:V_DIM] * (1.0 / acc[V_DIM:V_DIM + 1])
    o_ref[...] = ot_s[...].T.astype(BF16)


def _attention(q, k, vt, kc, vct, batch, seq, tq, tiles_per_step):
    nq = seq // tq
    tk = vt.shape[2]
    n_tiles = seq // tk
    has_cache = kc is not None
    max_width = max(tk, kc.shape[0] // batch) if has_cache else tk
    in_specs = [pl.BlockSpec((tq, H_A * HEAD_PAD), lambda b, i: (b * nq + i, 0)),
                pl.BlockSpec((seq, H_A * HEAD_PAD), lambda b, i: (b, 0)),
                pl.BlockSpec((n_tiles, H_A * V_DIM, tk), lambda b, i: (b, 0, 0))]
    args = [q, k, vt]
    if has_cache:
        past = kc.shape[0] // batch
        in_specs += [pl.BlockSpec((past, H_A * HEAD_PAD), lambda b, i: (b, 0)),
                     pl.BlockSpec((1, H_A * V_DIM, past), lambda b, i: (b, 0, 0))]
        args += [kc, vct]
    return pl.pallas_call(
        functools.partial(_attn_kernel, tiles_per_step=tiles_per_step, has_cache=has_cache),
        grid=(batch, nq),
        in_specs=in_specs,
        out_specs=pl.BlockSpec((tq, H_A * V_DIM), lambda b, i: (b * nq + i, 0)),
        out_shape=jax.ShapeDtypeStruct((batch * seq, H_A * V_DIM), BF16),
        scratch_shapes=[pltpu.VMEM((H_A * HEAD_PAD, tq), BF16),
                        pltpu.VMEM((H_A, V_DIM + ONES_ROWS, tq), F32),
                        pltpu.VMEM((H_A, tq), F32),
                        pltpu.VMEM((H_A * V_DIM, tq), F32),
                        pltpu.VMEM((QK_LOOKAHEAD + 1, max_width, tq), F32)],
        compiler_params=_cparams(("arbitrary", "arbitrary")),
        name="mla_attention",
    )(*args)


def _stage_c_kernel(x_ref, mod_ref, hf_ref, hb_ref, og_ref, attn_ref, ga_ref, gb_ref, hn_ref,
                    post1_ref, pre2_ref, post2_ref, wmo_ref, wao_ref, wout_ref, w1_ref, w2_ref, y_ref):
    mod = mod_ref[0]
    gate1, shift2, scale2, gate2 = mod[2:3], mod[3:4], mod[4:5], mod[5:6]
    hn = hn_ref[...]
    rows = x_ref.shape[0] // ROW_SPLITS
    groups = [slice(r * rows, (r + 1) * rows) for r in range(ROW_SPLITS)]
    n_ff = D_FF // D_MODEL

    def mixer_in(rs):
        hm = hf_ref[rs, :] + hb_ref[rs, :]
        heads = [_rms(hm[:, hd * DH_M:(hd + 1) * DH_M], hn[:, hd * DH_M:(hd + 1) * DH_M]) for hd in range(H_M)]
        return (jnp.concatenate(heads, axis=1) * og_ref[rs, :].astype(F32)).astype(BF16)

    def branch_out(hm, rs):
        return _bdot(hm, wmo_ref[...]), _bdot(attn_ref[rs, :], wao_ref[...])

    def merge(y_m, y_a, rs):
        return (ga_ref[rs, :].astype(F32) * y_m + gb_ref[rs, :].astype(F32) * y_a).astype(BF16)

    def mid(mix, rs):
        x1 = x_ref[rs, :] + gate1 * _rms(mix, post1_ref[...])
        return x1, (_rms(x1, pre2_ref[...]) * (1.0 + scale2) + shift2).astype(BF16)

    def mlp(h2):
        up = lambda c: _bdot(h2, w1_ref[:, c * D_MODEL:(c + 1) * D_MODEL])
        ff = jnp.zeros((rows, D_MODEL), F32)
        nxt = up(0)
        for c in range(n_ff):
            a = jnp.maximum(nxt, 0.0)
            if c + 1 < n_ff:
                nxt = up(c + 1)
            ff = ff + _bdot((a * a).astype(BF16), w2_ref[c * D_MODEL:(c + 1) * D_MODEL, :])
        return ff

    hms = [mixer_in(rs) for rs in groups]
    ys = [branch_out(hm, rs) for hm, rs in zip(hms, groups)]
    mixes = [_bdot(merge(*y, rs), wout_ref[...]) for y, rs in zip(ys, groups)]
    mids = [mid(mix, rs) for mix, rs in zip(mixes, groups)]
    ffs = [mlp(h2) for _, h2 in mids]
    for (x1, _), ff, rs in zip(mids, ffs, groups):
        y_ref[rs, :] = x1 + gate2 * _rms(ff, post2_ref[...])


def _stage_c(x, mods, mod_index, hf, hb, og, attn, ga, gb, hn, post1, pre2, post2, wmo, wao, wout, w1, w2, tm):
    n = x.shape[0]
    tile = lambda w: pl.BlockSpec((tm, w), lambda i: (i, 0))
    return pl.pallas_call(
        _stage_c_kernel,
        grid=(n // tm,),
        in_specs=[tile(D_MODEL),
                  pl.BlockSpec((1, N_MOD, D_MODEL), lambda i: (mod_index(i), 0, 0)),
                  tile(MLSTM_W), tile(MLSTM_W), tile(MLSTM_W), tile(H_A * V_DIM),
                  tile(D_MODEL), tile(D_MODEL),
                  _const_spec((1, MLSTM_W)), _const_spec((1, D_MODEL)), _const_spec((1, D_MODEL)),
                  _const_spec((1, D_MODEL)),
                  _const_spec((MLSTM_W, D_MODEL)), _const_spec((H_A * V_DIM, D_MODEL)),
                  _const_spec((D_MODEL, D_MODEL)), _const_spec((D_MODEL, D_FF)), _const_spec((D_FF, D_MODEL))],
        out_specs=tile(D_MODEL),
        out_shape=jax.ShapeDtypeStruct((n, D_MODEL), F32),
        compiler_params=_cparams(("arbitrary",)),
        name="stage_c",
    )(x, mods, hf, hb, og, attn, ga, gb, hn, post1, pre2, post2, wmo, wao, wout, w1, w2)


def _rope_tables(n_tokens):
    pos = np.arange(n_tokens)
    row = (pos // GRID_W).astype(np.float32)
    col = (pos % GRID_W).astype(np.float32)
    inv = (ROPE_BASE ** (-np.arange(0, AX_DIM, 2, dtype=np.float32) / AX_DIM)).astype(np.float32)
    ang = np.concatenate([row[:, None] * inv, col[:, None] * inv], axis=-1)
    cos = np.cos(ang.astype(np.float64)).astype(np.float32)
    sin = np.sin(ang.astype(np.float64)).astype(np.float32)
    ones = np.ones((n_tokens, NOPE), np.float32)
    zeros = np.zeros((n_tokens, NOPE), np.float32)
    pad = np.zeros((n_tokens, HEAD_PAD - NOPE - ROPE_DIM), np.float32)
    ta = np.concatenate([ones, cos, cos, pad], axis=-1)
    tb = np.concatenate([zeros, -sin, sin, pad], axis=-1)
    return jnp.asarray(ta), jnp.asarray(tb)


def _plain_tables(n_tokens):
    ones = jnp.ones((n_tokens, NOPE + ROPE_DIM), F32)
    pad = jnp.zeros((n_tokens, HEAD_PAD - NOPE - ROPE_DIM), F32)
    return jnp.concatenate([ones, pad], axis=-1), jnp.zeros((n_tokens, HEAD_PAD), F32)


def kernel(x_prompt, x_sample, cache_mla_ckv, cache_mla_krope, state_mlstm_C, state_mlstm_n, state_mlstm_m,
           c, c_ctx, w_ada, b_ada, norm_pre1, norm_post1, norm_pre2, norm_post2, w_in, mlstm_gate_b,
           mla_q_norm, mla_kv_norm, w_uq, w_ukv, w_mla_o, mlstm_head_norm, w_mlstm_o, w_out, w_mlp1, w_mlp2):
    bp, sp, _ = x_prompt.shape
    bs, ss, _ = x_sample.shape
    depth = w_in.shape[0]
    past = cache_mla_ckv.shape[2]
    nj = 2 * H_M
    even = np.arange(0, ROPE_DIM, 2)
    odd = np.arange(1, ROPE_DIM, 2)
    perm = np.concatenate([even, odd])
    perm_sw = np.concatenate([odd, even])

    xp = x_prompt.reshape(bp * sp, D_MODEL)
    xs = x_sample.reshape(bs * ss, D_MODEL)
    cc = jnp.zeros((8, D_MODEL), F32).at[:bs].set(c).at[bs].set(c_ctx)
    ta_lat, tb_lat = _rope_tables(ss)
    tm_ctx, tm_lat = 512, 512
    assert past == tm_lat
    ta_ctx, tb_ctx = _plain_tables(tm_ctx)

    new_ckv, new_krope, new_c, new_n, new_m = [], [], [], [], []
    for l in range(depth):
        cols = np.cumsum((W_BIG_COLS, N_GATES, Q_LORA, KV_LORA, ROPE_DIM))
        w_big, w_g, w_cq, w_ckv, w_kr, w_merge = jnp.split(w_in[l].astype(BF16), cols.tolist(), axis=1)
        w_g = w_g.reshape(D_MODEL, 2, 2, H_M).transpose(0, 2, 1, 3).reshape(D_MODEL, N_GATES)
        w_misc = jnp.concatenate([w_kr, w_g, jnp.zeros((D_MODEL, MISC_W - 3 * ROPE_DIM - N_GATES), BF16),
                                  w_kr[:, perm], w_kr[:, perm_sw]], axis=1)
        wa = (w_big, w_merge, jnp.concatenate([w_cq, w_ckv, w_misc], axis=1))
        gbias = jnp.zeros((1, MISC_W), F32).at[0, GATE_LANE0:GATE_LANE0 + N_GATES].set(
            mlstm_gate_b[l].transpose(1, 0, 2).reshape(N_GATES))
        uq = w_uq[l]
        wq = jnp.concatenate([uq[..., :NOPE], uq[..., NOPE:][..., perm], uq[..., NOPE:][..., perm_sw]],
                             axis=-1).reshape(Q_LORA, H_A * HEAD_PAD).astype(BF16)
        ukv = w_ukv[l]
        wk = jnp.concatenate([ukv[..., :NOPE], jnp.zeros((KV_LORA, H_A, HEAD_PAD - NOPE), F32)],
                             axis=-1).reshape(KV_LORA, H_A * HEAD_PAD).astype(BF16)
        wv = ukv[..., NOPE:].reshape(KV_LORA, H_A * V_DIM).T.astype(BF16)
        wmo = w_mlstm_o[l].astype(BF16)
        wao = w_mla_o[l].astype(BF16)
        wout = w_out[l].astype(BF16)
        w1 = w_mlp1[l].astype(BF16)
        w2 = w_mlp2[l].astype(BF16)
        pre1, post1 = norm_pre1[l][None], norm_post1[l][None]
        pre2, post2 = norm_pre2[l][None], norm_post2[l][None]
        qn, kvn = mla_q_norm[l][None], mla_kv_norm[l][None]
        hn = mlstm_head_norm[l].reshape(1, MLSTM_W)

        mods = _modulation(cc, w_ada[l], b_ada[l][None]).reshape(8, N_MOD, D_MODEL)

        ctx_mod = lambda i: bs
        a = _stage_a(xp, mods, ctx_mod, pre1, wa, gbias, qn, kvn, wq, wk, wv, ta_ctx, tb_ctx,
                     lambda i: 0, tm_ctx, sp)
        mq, mk, mv, og, ga, gb, gates, q, k, v, ckv_n, kro = a
        hf, hb, c_fin, n_fin, m_fin = _mlstm(mq, mk, mv, *_gate_rows(gates, GATE_CHUNKS_PER_STEP),
                                             None, None, bp, sp)
        attn = _attention(q, k, v, None, None, bp, sp, sp, 1)
        xp = _stage_c(xp, mods, ctx_mod, hf, hb, og, attn, ga, gb, hn, post1, pre2, post2,
                      wmo, wao, wout, w1, w2, tm_ctx)
        new_ckv.append(ckv_n.reshape(bp, sp, KV_LORA))
        new_krope.append(kro.reshape(bp, sp, ROPE_DIM))
        new_c.append(c_fin.reshape(bp, 2, H_M, DH_M, DH_M))
        new_n.append(n_fin.reshape(bp, 2, H_M, DH_M))
        new_m.append(m_fin[:, :, 0].reshape(bp, 2, H_M))

        tiles_per_seq = ss // tm_lat
        lat_mod = lambda i: i // tiles_per_seq
        a = _stage_a(xs, mods, lat_mod, pre1, wa, gbias, qn, kvn, wq, wk, wv, ta_lat, tb_lat,
                     lambda i: i % tiles_per_seq, tm_lat, ATTN_KEY_TILE)
        mq, mk, mv, og, ga, gb, gates, q, k, v, _, _ = a
        kr_cache = jnp.zeros((bs * past, LANE), F32).at[:, NOPE:NOPE + ROPE_DIM].set(
            cache_mla_krope[:, l].reshape(bs * past, ROPE_DIM)[:, perm])
        kc, vc = _kv_cache(cache_mla_ckv[:, l].reshape(bs * past, KV_LORA), kr_cache, wk, wv, past)
        n0 = state_mlstm_n[:, l].reshape(bs, nj, DH_M, 1)
        c0 = jnp.concatenate([state_mlstm_C[:, l].reshape(bs, nj, DH_M, DH_M),
                              jnp.broadcast_to(n0, (bs, nj, DH_M, DH_M))], axis=-1)
        m0 = jnp.broadcast_to(state_mlstm_m[:, l].reshape(bs, nj, 1), (bs, nj, LANE))
        hf, hb, _, _, _ = _mlstm(mq, mk, mv, *_gate_rows(gates, GATE_CHUNKS_PER_STEP), c0, m0, bs, ss)
        attn = _attention(q, k, v, kc, vc, bs, ss, ATTN_QUERY_TILE, KEY_TILES_PER_STEP)
        xs = _stage_c(xs, mods, lat_mod, hf, hb, og, attn, ga, gb, hn, post1, pre2, post2,
                      wmo, wao, wout, w1, w2, tm_lat)

    return (xp.reshape(bp, sp, D_MODEL), xs.reshape(bs, ss, D_MODEL),
            jnp.stack(new_ckv, axis=1), jnp.stack(new_krope, axis=1), jnp.stack(new_c, axis=1),
            jnp.stack(new_n, axis=1), jnp.stack(new_m, axis=1))
```

```python
import functools

import numpy as np
import jax
import jax.numpy as jnp
from jax import lax
from jax.experimental import pallas as pl
from jax.experimental.pallas import tpu as pltpu

F32 = jnp.float32
BF16 = jnp.bfloat16

D_MODEL = 1024
H_M = 4
DH_M = 128
MLSTM_W = H_M * DH_M
CHUNK = 128
H_A = 8
NOPE = 64
ROPE_DIM = 32
V_DIM = 64
Q_LORA = 384
KV_LORA = 256
AX_DIM = ROPE_DIM // 2
ROPE_BASE = 10000.0
GRID_W = 64
D_FF = 4 * D_MODEL
N_MOD = 6
EPS = 1e-6
N_GATES = 4 * H_M
M_INIT = -1e30

LANE = 128
HEAD_PAD = LANE
MISC_W = LANE
GATE_LANE0 = ROPE_DIM
OFF_MQ, OFF_MK, OFF_MV, OFF_MO, W_BIG_COLS = 0, MLSTM_W, 2 * MLSTM_W, 3 * MLSTM_W, 4 * MLSTM_W
OFF_CQ, OFF_CKV, OFF_MISC = 0, Q_LORA, Q_LORA + KV_LORA
W_SMALL_COLS = OFF_MISC + MISC_W

VMEM_LIMIT = 60 * 1024 * 1024
QK_SCALE = float((NOPE + ROPE_DIM) ** -0.5 * np.log2(np.e))
QK_LOOKAHEAD = 3
ATTN_KEY_TILE = 512
ATTN_SEQS_PER_STEP = 4
ATTN_QUERY_TILE = 256
ROW_SPLITS = 2
KEY_TILES_PER_STEP = 3
ONES_ROWS = 16


def _cparams(sem):
    return pltpu.CompilerParams(dimension_semantics=sem, vmem_limit_bytes=VMEM_LIMIT)


def _const_spec(shape):
    nd = len(shape)
    return pl.BlockSpec(shape, lambda *_: (0,) * nd, pipeline_mode=pl.Buffered(1))


def _rms(x, w):
    return x * lax.rsqrt(jnp.mean(x * x, axis=-1, keepdims=True) + EPS) * w


def _sigmoid(x):
    return 1.0 / (1.0 + jnp.exp(-x))


def _log_sigmoid(x):
    return jnp.minimum(x, 0.0) - jnp.log(1.0 + jnp.exp(-jnp.abs(x)))


def _bdot(a, b):
    return jnp.dot(a, b, preferred_element_type=F32)


def _mod_kernel(c_ref, w_ref, b_ref, o_ref):
    c = c_ref[...]
    s = c * _sigmoid(c)
    o_ref[...] = jnp.dot(s, w_ref[...], preferred_element_type=F32,
                         precision=lax.Precision.HIGHEST) + b_ref[...]


def _modulation(cc, w_ada, b_ada):
    n_out = w_ada.shape[1]
    tn = 1536
    return pl.pallas_call(
        _mod_kernel,
        grid=(n_out // tn,),
        in_specs=[pl.BlockSpec((8, D_MODEL), lambda j: (0, 0)),
                  pl.BlockSpec((D_MODEL, tn), lambda j: (0, j)),
                  pl.BlockSpec((1, tn), lambda j: (0, j))],
        out_specs=pl.BlockSpec((8, tn), lambda j: (0, j)),
        out_shape=jax.ShapeDtypeStruct((8, n_out), F32),
        compiler_params=_cparams(("arbitrary",)),
        name="modulation",
    )(cc, w_ada, b_ada)


def _stage_a_kernel(x_ref, mod_ref, pre1_ref, wbig_ref, wmerge_ref, wsmall_ref, gbias_ref, qn_ref, kvn_ref,
                    wq_ref, wk_ref, wv_ref,
                    ta_ref, tb_ref,
                    mq_ref, mk_ref, mv_ref, og_ref, ga_ref, gb_ref, gates_ref, q_ref, k_ref, v_ref,
                    ckv_ref, kro_ref):
    mod = mod_ref[0]
    shift1, scale1 = mod[0:1], mod[1:2]
    tm = x_ref.shape[0]
    rows = tm // ROW_SPLITS
    tkv = v_ref.shape[2]

    def normed(rs):
        return (_rms(x_ref[rs, :], pre1_ref[...]) * (1.0 + scale1) + shift1).astype(BF16)

    def small_proj(hb):
        return tuple(_bdot(hb, wsmall_ref[:, lo:hi])
                     for lo, hi in ((OFF_CQ, OFF_CKV), (OFF_CKV, OFF_MISC), (OFF_MISC, W_SMALL_COLS)))

    def big_proj(hb, rs):
        proj = lambda lo, hi: _bdot(hb, wbig_ref[:, lo:hi])
        mq_ref[rs, :] = (proj(OFF_MQ, OFF_MK) * DH_M ** -0.5).astype(BF16)
        mk_ref[:, rs] = proj(OFF_MK, OFF_MV).T.astype(BF16)
        mv_ref[rs, :] = proj(OFF_MV, OFF_MO).astype(BF16)
        og_ref[rs, :] = _sigmoid(proj(OFF_MO, W_BIG_COLS)).astype(BF16)
        ga_ref[rs, :] = _sigmoid(_bdot(hb, wmerge_ref[:, :D_MODEL])).astype(BF16)
        gb_ref[rs, :] = _sigmoid(_bdot(hb, wmerge_ref[:, D_MODEL:])).astype(BF16)

    def mla_build(cq, ckv, misc, rs):
        lane = lax.broadcasted_iota(jnp.int32, misc.shape, 1)
        g = misc + gbias_ref[...]
        is_f = (lane >= GATE_LANE0 + 2 * H_M) & (lane < GATE_LANE0 + N_GATES)
        gates_ref[rs, :] = jnp.where(is_f, _log_sigmoid(g), g)
        kro_ref[rs, :] = misc[:, 0:ROPE_DIM]

        ckv_n = _rms(ckv, kvn_ref[...])
        ckv_ref[rs, :] = ckv_n
        ckv_b = ckv_n.astype(BF16)

        ta = ta_ref[rs, :]
        tb = tb_ref[rs, :]
        qf = _bdot(_rms(cq, qn_ref[...]).astype(BF16), wq_ref[...])
        ta8 = jnp.concatenate([ta] * H_A, axis=1)
        tb8 = jnp.concatenate([tb] * H_A, axis=1)
        q = qf * ta8 + pltpu.roll(qf, H_A * HEAD_PAD - ROPE_DIM, 1) * tb8
        q_ref[rs, :] = (q * QK_SCALE).astype(BF16)

        ta_k = jnp.where(lane < NOPE, 0.0, ta)
        kr = misc * ta_k + pltpu.roll(misc, MISC_W - ROPE_DIM, 1) * tb
        kk = _bdot(ckv_b, wk_ref[...]) + jnp.concatenate([kr] * H_A, axis=1)
        k_ref[rs, :] = kk.astype(BF16)
        vt = _bdot(wv_ref[...], ckv_n.T.astype(BF16)).astype(BF16)
        step = min(rows, tkv)
        for off in range(0, rows, step):
            lo = rs.start + off
            v_ref[lo // tkv, :, lo % tkv:lo % tkv + step] = vt[:, off:off + step]

    groups = [slice(r * rows, (r + 1) * rows) for r in range(ROW_SPLITS)]
    hbs = [normed(rs) for rs in groups]
    for hb, rs in zip(hbs, groups):
        small = small_proj(hb)
        big_proj(hb, rs)
        mla_build(*small, rs)


def _stage_a(x, mods, mod_index, pre1, wa, gbias, qn, kvn, wq, wk, wv, ta, tb, table_index, tm, tkv):
    n = x.shape[0]
    tile = lambda w: pl.BlockSpec((tm, w), lambda i: (i, 0))
    bf = lambda w: jax.ShapeDtypeStruct((n, w), BF16)
    f32 = lambda w: jax.ShapeDtypeStruct((n, w), F32)
    return pl.pallas_call(
        _stage_a_kernel,
        grid=(n // tm,),
        in_specs=[tile(D_MODEL),
                  pl.BlockSpec((1, N_MOD, D_MODEL), lambda i: (mod_index(i), 0, 0)),
                  _const_spec((1, D_MODEL)),
                  _const_spec((D_MODEL, W_BIG_COLS)),
                  _const_spec((D_MODEL, 2 * D_MODEL)),
                  _const_spec((D_MODEL, W_SMALL_COLS)),
                  _const_spec((1, MISC_W)),
                  _const_spec((1, Q_LORA)),
                  _const_spec((1, KV_LORA)),
                  _const_spec((Q_LORA, H_A * HEAD_PAD)),
                  _const_spec((KV_LORA, H_A * HEAD_PAD)),
                  _const_spec((H_A * V_DIM, KV_LORA)),
                  pl.BlockSpec((tm, LANE), lambda i: (table_index(i), 0)),
                  pl.BlockSpec((tm, LANE), lambda i: (table_index(i), 0))],
        out_specs=[tile(MLSTM_W), pl.BlockSpec((MLSTM_W, tm), lambda i: (0, i)), tile(MLSTM_W), tile(MLSTM_W),
                   tile(D_MODEL), tile(D_MODEL), tile(MISC_W),
                   tile(H_A * HEAD_PAD), tile(H_A * HEAD_PAD),
                   pl.BlockSpec((tm // tkv, H_A * V_DIM, tkv), lambda i: (i, 0, 0)),
                   tile(KV_LORA), tile(ROPE_DIM)],
        out_shape=[bf(MLSTM_W), jax.ShapeDtypeStruct((MLSTM_W, n), BF16), bf(MLSTM_W), bf(MLSTM_W),
                   bf(D_MODEL), bf(D_MODEL), f32(MISC_W),
                   bf(H_A * HEAD_PAD), bf(H_A * HEAD_PAD),
                   jax.ShapeDtypeStruct((n // tkv, H_A * V_DIM, tkv), BF16),
                   f32(KV_LORA), f32(ROPE_DIM)],
        compiler_params=_cparams(("arbitrary",)),
        name="stage_a",
    )(x, mods, pre1, *wa, gbias, qn, kvn, wq, wk, wv, ta, tb)


def _kv_cache_kernel(ckv_ref, kr_ref, wk_ref, wv_ref, k_ref, v_ref):
    ckv_b = ckv_ref[...].astype(BF16)
    kk = _bdot(ckv_b, wk_ref[...]) + jnp.concatenate([kr_ref[...]] * H_A, axis=1)
    k_ref[...] = kk.astype(BF16)
    v_ref[0] = _bdot(wv_ref[...], ckv_ref[...].T.astype(BF16)).astype(BF16)


def _kv_cache(ckv, kr, wk, wv, tm):
    n = ckv.shape[0]
    return pl.pallas_call(
        _kv_cache_kernel,
        grid=(n // tm,),
        in_specs=[pl.BlockSpec((tm, KV_LORA), lambda i: (i, 0)),
                  pl.BlockSpec((tm, LANE), lambda i: (i, 0)),
                  _const_spec((KV_LORA, H_A * HEAD_PAD)),
                  _const_spec((H_A * V_DIM, KV_LORA))],
        out_specs=[pl.BlockSpec((tm, H_A * HEAD_PAD), lambda i: (i, 0)),
                   pl.BlockSpec((1, H_A * V_DIM, tm), lambda i: (i, 0, 0))],
        out_shape=[jax.ShapeDtypeStruct((n, H_A * HEAD_PAD), BF16),
                   jax.ShapeDtypeStruct((n // tm, H_A * V_DIM, tm), BF16)],
        compiler_params=_cparams(("arbitrary",)),
        name="kv_cache",
    )(ckv, kr, wk, wv)


def _split3(x):
    x1 = x.astype(BF16).astype(F32)
    r = x - x1
    x2 = r.astype(BF16).astype(F32)
    x3 = (r - x2).astype(BF16).astype(F32)
    return x1, x2, x3


GATE_ROWS = 32
MLSTM_SEQS_PER_STEP = 2
MLSTM_CHUNKS_PER_STEP = 4
GATE_CHUNKS_PER_STEP = 16


def _gate_rows_kernel(g_ref, cols_ref, rows_ref):
    nj = 2 * H_M
    nch = g_ref.shape[0] // CHUNK
    n_rows = nch * nj
    row = lax.broadcasted_iota(jnp.int32, (CHUNK, CHUNK), 0)
    col = lax.broadcasted_iota(jnp.int32, (CHUNK, CHUNK), 1)
    lower = (col <= row).astype(F32)
    upper = (col >= row).astype(F32)
    hi = lax.Precision.HIGHEST
    rowi = lax.broadcasted_iota(jnp.int32, (n_rows, CHUNK), 0)
    lane = lax.broadcasted_iota(jnp.int32, (n_rows, CHUNK), 1)
    is_fwd = rowi % nj < H_M

    gi, gf = GATE_LANE0, GATE_LANE0 + nj
    i_rows, f_rows = [], []
    for c in range(nch):
        g_t = g_ref[c * CHUNK:(c + 1) * CHUNK, :].T
        i_rows.append(g_t[gi:gi + nj])
        f_rows.append(g_t[gf:gf + nj])
    i_all = jnp.concatenate(i_rows, axis=0)
    f_all = jnp.concatenate(f_rows, axis=0)
    b = jnp.where(is_fwd, jnp.dot(f_all, upper, preferred_element_type=F32, precision=hi),
                  jnp.dot(f_all, lower, preferred_element_type=F32, precision=hi))
    a = i_all - b
    cm = a
    shift = 1
    while shift < CHUNK:
        y_f = jnp.where(lane >= shift, pltpu.roll(cm, shift, 1), -jnp.inf)
        y_b = jnp.where(lane < CHUNK - shift, pltpu.roll(cm, CHUNK - shift, 1), -jnp.inf)
        cm = jnp.maximum(cm, jnp.where(is_fwd, y_f, y_b))
        shift *= 2
    a_max = jnp.broadcast_to(jnp.max(a, axis=1, keepdims=True), a.shape)
    f_sum = jnp.broadcast_to(jnp.sum(f_all, axis=1, keepdims=True), a.shape)
    zeros8 = jnp.zeros((nj, CHUNK), F32)
    for c in range(nch):
        sl = slice(c * nj, (c + 1) * nj)
        stack = jnp.concatenate([cm[sl], b[sl]] + [zeros8] * (CHUNK // nj - 2), axis=0)
        cols_ref[c * CHUNK:(c + 1) * CHUNK, :] = stack.T
        rows_ref[c] = jnp.concatenate([a[sl], a_max[sl], f_sum[sl], zeros8], axis=0)


def _gate_rows(gates, chunks_per_step):
    n = gates.shape[0]
    tm = chunks_per_step * CHUNK
    return pl.pallas_call(
        _gate_rows_kernel,
        grid=(n // tm,),
        in_specs=[pl.BlockSpec((tm, MISC_W), lambda i: (i, 0))],
        out_specs=[pl.BlockSpec((tm, LANE), lambda i: (i, 0)),
                   pl.BlockSpec((chunks_per_step, GATE_ROWS, CHUNK), lambda i: (i, 0, 0))],
        out_shape=[jax.ShapeDtypeStruct((n, LANE), F32),
                   jax.ShapeDtypeStruct((n // CHUNK, GATE_ROWS, CHUNK), F32)],
        compiler_params=_cparams(("arbitrary",)),
        name="gate_rows",
    )(gates)


def _mlstm_kernel(*refs, has_state, n_seq):
    if has_state:
        (qf_ref, ktf_ref, vf_ref, ptf_ref, rf_ref, qb_ref, ktb_ref, vb_ref, ptb_ref, rb_ref, c0_ref, m0_ref,
         hf_ref, hb_ref, cn_ref, nn_ref, mn_ref, c_s, m_s) = refs
    else:
        (qf_ref, ktf_ref, vf_ref, ptf_ref, rf_ref, qb_ref, ktb_ref, vb_ref, ptb_ref, rb_ref,
         hf_ref, hb_ref, cn_ref, nn_ref, mn_ref, c_s, m_s) = refs
    step = pl.program_id(1)

    nj = 2 * H_M

    @pl.when(step == 0)
    def _():
        if has_state:
            for sq in range(n_seq):
                c_s[sq * nj:(sq + 1) * nj] = c0_ref[sq]
                m_s[sq * nj:(sq + 1) * nj] = m0_ref[sq]
        else:
            c_s[...] = jnp.zeros(c_s.shape, F32)
            m_s[...] = jnp.full(m_s.shape, M_INIT, F32)

    row = lax.broadcasted_iota(jnp.int32, (CHUNK, CHUNK), 0)
    col = lax.broadcasted_iota(jnp.int32, (CHUNK, CHUNK), 1)
    is_fwd = lax.broadcasted_iota(jnp.int32, (nj, CHUNK), 0) < H_M
    ones_v = jnp.ones((CHUNK, DH_M), BF16)
    dirs = ((qf_ref, ktf_ref, vf_ref, ptf_ref, hf_ref), (qb_ref, ktb_ref, vb_ref, ptb_ref, hb_ref))
    chains = [(sq, d, hd) for sq in range(n_seq) for d in range(2) for hd in range(H_M)]
    n_sub = rf_ref.shape[0] // n_seq
    for sub in range(n_sub):
        sub_of = (sub, n_sub - 1 - sub)
        gate = []
        for sq in range(n_seq):
            rows_f, rows_b = rf_ref[sq * n_sub + sub_of[0]], rb_ref[sq * n_sub + sub_of[1]]
            pick = lambda g: jnp.where(is_fwd, rows_f[g * nj:(g + 1) * nj], rows_b[g * nj:(g + 1) * nj])
            a8, a_max8, f_sum8 = pick(0), pick(1), pick(2)
            m8 = m_s[sq * nj:(sq + 1) * nj]
            mx8 = jnp.maximum(m8, a_max8)
            gate.append((a8, m8, jnp.exp(a8 - mx8), jnp.exp(m8 - mx8)))
            m_s[sq * nj:(sq + 1) * nj] = f_sum8 + mx8
        tok = lambda sq, d: slice((sq * n_sub + sub_of[d]) * CHUNK, (sq * n_sub + sub_of[d] + 1) * CHUNK)

        first = []
        for sq, d, hd in chains:
            q_ref, kt_ref, v_ref, _, _ = dirs[d]
            j = d * H_M + hd
            sl = slice(hd * DH_M, (hd + 1) * DH_M)
            q = q_ref[tok(sq, d), sl]
            k_t = kt_ref[sl, tok(sq, d)]
            v_aug = jnp.concatenate([v_ref[tok(sq, d), sl], ones_v], axis=1)
            qk = _bdot(q, k_t)
            c_st = c_s[sq * nj + j]
            qc = _bdot(q, c_st.astype(BF16))
            upd = _bdot((k_t.astype(F32) * gate[sq][2][j:j + 1, :]).astype(BF16), v_aug)
            dec = gate[sq][3][j:j + 1, :]
            c_s[sq * nj + j] = jnp.concatenate([dec, dec], axis=1) * c_st + upd
            first.append((qk, qc, v_aug))
        for (sq, d, hd), (qk, qc, v_aug) in zip(chains, first):
            cols_ref, h_ref = dirs[d][3], dirs[d][4]
            j = d * H_M + hd
            sl = slice(hd * DH_M, (hd + 1) * DH_M)
            mask = (col <= row) if d == 0 else (col >= row)
            a8, m8 = gate[sq][0], gate[sq][1]
            m_row = m8[j:j + 1, :]
            u = jnp.maximum(m_row, cols_ref[tok(sq, d), j:j + 1])
            w = jnp.exp(jnp.where(mask, a8[j:j + 1, :] - u, -jnp.inf))
            s_inter = jnp.exp(m_row - u)
            intra = _bdot((qk * w).astype(BF16), v_aug)
            num = s_inter * qc[:, :DH_M] + intra[:, :DH_M]
            den = s_inter * qc[:, DH_M:] + intra[:, DH_M:]
            h_ref[tok(sq, d), sl] = num / jnp.maximum(
                jnp.abs(den), jnp.exp(-(cols_ref[tok(sq, d), nj + j:nj + j + 1] + u)))

    @pl.when(step == pl.num_programs(1) - 1)
    def _():
        for sq in range(n_seq):
            for j in range(nj):
                c_aug = c_s[sq * nj + j]
                cn_ref[sq, j] = c_aug[:, :DH_M]
                nn_ref[sq, j:j + 1, :] = c_aug[:, DH_M:].T[0:1, :]
            mn_ref[sq] = m_s[sq * nj:(sq + 1) * nj]


def _mlstm(mq, mkt, mv, gate_pt, gate_rows, c0, m0, batch, seq):
    has_state = c0 is not None
    n_sub = min(MLSTM_CHUNKS_PER_STEP, seq // CHUNK)
    nc = seq // (n_sub * CHUNK)
    n_seq = MLSTM_SEQS_PER_STEP if nc == 1 and batch % MLSTM_SEQS_PER_STEP == 0 else 1
    tm = n_seq * n_sub * CHUNK
    n = batch * seq
    nj = 2 * H_M
    fwd = lambda w: pl.BlockSpec((tm, w), lambda b, c: (b * nc + c, 0))
    bwd = lambda w: pl.BlockSpec((tm, w), lambda b, c: (b * nc + nc - 1 - c, 0))
    fwd_t = pl.BlockSpec((MLSTM_W, tm), lambda b, c: (0, b * nc + c))
    bwd_t = pl.BlockSpec((MLSTM_W, tm), lambda b, c: (0, b * nc + nc - 1 - c))
    st_m = pl.BlockSpec((n_seq, nj, LANE), lambda b, c: (b, 0, 0))
    st_c = pl.BlockSpec((n_seq, nj, DH_M, 2 * DH_M), lambda b, c: (b, 0, 0, 0))
    fwd_r = pl.BlockSpec((n_seq * n_sub, GATE_ROWS, CHUNK), lambda b, c: (b * nc + c, 0, 0))
    bwd_r = pl.BlockSpec((n_seq * n_sub, GATE_ROWS, CHUNK), lambda b, c: (b * nc + nc - 1 - c, 0, 0))
    in_specs = [fwd(MLSTM_W), fwd_t, fwd(MLSTM_W), fwd(LANE), fwd_r,
                bwd(MLSTM_W), bwd_t, bwd(MLSTM_W), bwd(LANE), bwd_r]
    args = [mq, mkt, mv, gate_pt, gate_rows, mq, mkt, mv, gate_pt, gate_rows]
    if has_state:
        in_specs += [st_c, st_m]
        args += [c0, m0]
    return pl.pallas_call(
        functools.partial(_mlstm_kernel, has_state=has_state, n_seq=n_seq),
        grid=(batch // n_seq, nc),
        in_specs=in_specs,
        out_specs=[fwd(MLSTM_W), bwd(MLSTM_W),
                   pl.BlockSpec((n_seq, nj, DH_M, DH_M), lambda b, c: (b, 0, 0, 0)), st_m, st_m],
        out_shape=[jax.ShapeDtypeStruct((n, MLSTM_W), F32), jax.ShapeDtypeStruct((n, MLSTM_W), F32),
                   jax.ShapeDtypeStruct((batch, nj, DH_M, DH_M), F32),
                   jax.ShapeDtypeStruct((batch, nj, DH_M), F32),
                   jax.ShapeDtypeStruct((batch, nj, LANE), F32)],
        scratch_shapes=[pltpu.VMEM((n_seq * nj, DH_M, 2 * DH_M), F32), pltpu.VMEM((n_seq * nj, LANE), F32)],
        compiler_params=_cparams(("arbitrary", "arbitrary")),
        name="mlstm",
    )(*args)


def _attn_kernel(*refs, tiles_per_step, has_cache, n_seq):
    if has_cache:
        q_ref, k_ref, vt_ref, kc_ref, vct_ref, o_ref, qt_s, acc_s, m_s, ot_s, s_scr = refs
    else:
        q_ref, k_ref, vt_ref, o_ref, qt_s, acc_s, m_s, ot_s, s_scr = refs
    tk = vt_ref.shape[2]
    n_tiles = vt_ref.shape[0] // n_seq
    tq = q_ref.shape[0] // n_seq
    d_q, d_o = H_A * HEAD_PAD, H_A * V_DIM
    for sq in range(n_seq):
        qt_s[sq * d_q:(sq + 1) * d_q, :] = q_ref[sq * tq:(sq + 1) * tq, :].astype(F32).T.astype(BF16)
    acc_s[...] = jnp.zeros(acc_s.shape, F32)
    m_s[...] = jnp.full(m_s.shape, -jnp.inf, F32)

    n_slots = s_scr.shape[0]

    def scores(tile, head, slot):
        get_k, _, width, sq = tile
        stream = sq * H_A + head
        s = _bdot(get_k(head), qt_s[stream * HEAD_PAD:(stream + 1) * HEAD_PAD, :])
        s_scr[slot, 0:width, :] = s
        return jnp.max(s, axis=0, keepdims=True)

    def update(tiles, pending, next_tile):
        items = [(tile, head) for tile in tiles for head in range(H_A)]
        ahead = items + ([(next_tile, h) for h in range(QK_LOOKAHEAD)] if next_tile is not None else [])
        pending = list(pending)
        for idx, ((_, get_vt, width, sq), head) in enumerate(items):
            stream = sq * H_A + head
            m_tile = pending.pop(0)
            if idx + QK_LOOKAHEAD < len(ahead):
                pending.append(scores(*ahead[idx + QK_LOOKAHEAD], (idx + QK_LOOKAHEAD) % n_slots))
            m_old = m_s[stream:stream + 1, :]
            m_new = jnp.maximum(m_old, m_tile)
            alpha = jnp.exp2(m_old - m_new)
            p = jnp.exp2((s_scr[idx % n_slots, 0:width, :] - m_new).astype(BF16))
            ones = jnp.ones((ONES_ROWS, width), BF16)
            lhs = jnp.concatenate([get_vt(head), ones], axis=0)
            acc_s[stream] = alpha * acc_s[stream] + _bdot(lhs, p)
            m_s[stream:stream + 1, :] = m_new
        return tuple(pending)

    def main_tile(t, sq=0):
        start = (sq * n_tiles + t) * tk if isinstance(t, int) else pl.multiple_of(t * tk, tk)
        rows = pl.ds(start, tk)
        return (lambda h: k_ref[rows, h * HEAD_PAD:(h + 1) * HEAD_PAD],
                lambda h: vt_ref[sq * n_tiles + t, h * V_DIM:(h + 1) * V_DIM, :], tk, sq)

    def body(i, pending):
        first = i * tiles_per_step
        return update([main_tile(first + u) for u in range(tiles_per_step)], pending,
                      main_tile(first + tiles_per_step))

    tail = []
    if has_cache:
        tail = [(lambda h: kc_ref[:, h * HEAD_PAD:(h + 1) * HEAD_PAD],
                 lambda h: vct_ref[0, h * V_DIM:(h + 1) * V_DIM, :], kc_ref.shape[0], 0)]
    n_loop = (n_tiles - 1) // tiles_per_step
    tail = [main_tile(t, sq) for sq in range(n_seq) for t in range(n_loop * tiles_per_step, n_tiles)] + tail
    pending = tuple(scores(tail[0] if not n_loop else main_tile(0), h, h) for h in range(QK_LOOKAHEAD))
    if n_loop:
        assert (tiles_per_step * H_A) % n_slots == 0
        pending = lax.fori_loop(0, n_loop, body, pending)
    update(tail, pending, None)
    for stream in range(n_seq * H_A):
        acc = acc_s[stream]
        ot_s[stream * V_DIM:(stream + 1) * V_DIM, :] = acc[0:V_DIM] * (1.0 / acc[V_DIM:V_DIM + 1])
    for sq in range(n_seq):
        o_ref[sq * tq:(sq + 1) * tq, :] = ot_s[sq * d_o:(sq + 1) * d_o, :].T.astype(BF16)


def _attention(q, k, vt, kc, vct, batch, seq, tq, tiles_per_step, n_seq=1):
    nq = seq // tq
    tk = vt.shape[2]
    n_tiles = seq // tk
    has_cache = kc is not None
    assert n_seq == 1 or (nq == 1 and n_tiles == 1 and not has_cache and batch % n_seq == 0)
    max_width = max(tk, kc.shape[0] // batch) if has_cache else tk
    in_specs = [pl.BlockSpec((n_seq * tq, H_A * HEAD_PAD), lambda b, i: (b * nq + i, 0)),
                pl.BlockSpec((n_seq * seq, H_A * HEAD_PAD), lambda b, i: (b, 0)),
                pl.BlockSpec((n_seq * n_tiles, H_A * V_DIM, tk), lambda b, i: (b, 0, 0))]
    args = [q, k, vt]
    if has_cache:
        past = kc.shape[0] // batch
        in_specs += [pl.BlockSpec((past, H_A * HEAD_PAD), lambda b, i: (b, 0)),
                     pl.BlockSpec((1, H_A * V_DIM, past), lambda b, i: (b, 0, 0))]
        args += [kc, vct]
    return pl.pallas_call(
        functools.partial(_attn_kernel, tiles_per_step=tiles_per_step, has_cache=has_cache, n_seq=n_seq),
        grid=(batch // n_seq, nq),
        in_specs=in_specs,
        out_specs=pl.BlockSpec((n_seq * tq, H_A * V_DIM), lambda b, i: (b * nq + i, 0)),
        out_shape=jax.ShapeDtypeStruct((batch * seq, H_A * V_DIM), BF16),
        scratch_shapes=[pltpu.VMEM((n_seq * H_A * HEAD_PAD, tq), BF16),
                        pltpu.VMEM((n_seq * H_A, V_DIM + ONES_ROWS, tq), F32),
                        pltpu.VMEM((n_seq * H_A, tq), F32),
                        pltpu.VMEM((n_seq * H_A * V_DIM, tq), F32),
                        pltpu.VMEM((QK_LOOKAHEAD + 1, max_width, tq), F32)],
        compiler_params=_cparams(("arbitrary", "arbitrary")),
        name="mla_attention",
    )(*args)


def _stage_c_kernel(x_ref, mod_ref, hf_ref, hb_ref, og_ref, attn_ref, ga_ref, gb_ref, hn_ref,
                    post1_ref, pre2_ref, post2_ref, wmo_ref, wao_ref, wout_ref, w1_ref, w2_ref, y_ref):
    mod = mod_ref[0]
    gate1, shift2, scale2, gate2 = mod[2:3], mod[3:4], mod[4:5], mod[5:6]
    hn = hn_ref[...]
    rows = x_ref.shape[0] // ROW_SPLITS
    groups = [slice(r * rows, (r + 1) * rows) for r in range(ROW_SPLITS)]
    n_ff = D_FF // D_MODEL

    def mixer_in(rs):
        hm = hf_ref[rs, :] + hb_ref[rs, :]
        heads = [_rms(hm[:, hd * DH_M:(hd + 1) * DH_M], hn[:, hd * DH_M:(hd + 1) * DH_M]) for hd in range(H_M)]
        return (jnp.concatenate(heads, axis=1) * og_ref[rs, :].astype(F32)).astype(BF16)

    def branch_out(hm, rs):
        return _bdot(hm, wmo_ref[...]), _bdot(attn_ref[rs, :], wao_ref[...])

    def merge(y_m, y_a, rs):
        return (ga_ref[rs, :].astype(F32) * y_m + gb_ref[rs, :].astype(F32) * y_a).astype(BF16)

    def mid(mix, rs):
        x1 = x_ref[rs, :] + gate1 * _rms(mix, post1_ref[...])
        return x1, (_rms(x1, pre2_ref[...]) * (1.0 + scale2) + shift2).astype(BF16)

    def mlp(h2):
        up = lambda c: _bdot(h2, w1_ref[:, c * D_MODEL:(c + 1) * D_MODEL])
        ff = jnp.zeros((rows, D_MODEL), F32)
        nxt = up(0)
        for c in range(n_ff):
            a = jnp.maximum(nxt, 0.0)
            if c + 1 < n_ff:
                nxt = up(c + 1)
            ff = ff + _bdot((a * a).astype(BF16), w2_ref[c * D_MODEL:(c + 1) * D_MODEL, :])
        return ff

    hms = [mixer_in(rs) for rs in groups]
    ys = [branch_out(hm, rs) for hm, rs in zip(hms, groups)]
    mixes = [_bdot(merge(*y, rs), wout_ref[...]) for y, rs in zip(ys, groups)]
    mids = [mid(mix, rs) for mix, rs in zip(mixes, groups)]
    ffs = [mlp(h2) for _, h2 in mids]
    for (x1, _), ff, rs in zip(mids, ffs, groups):
        y_ref[rs, :] = x1 + gate2 * _rms(ff, post2_ref[...])


def _stage_c(x, mods, mod_index, hf, hb, og, attn, ga, gb, hn, post1, pre2, post2, wmo, wao, wout, w1, w2, tm):
    n = x.shape[0]
    tile = lambda w: pl.BlockSpec((tm, w), lambda i: (i, 0))
    return pl.pallas_call(
        _stage_c_kernel,
        grid=(n // tm,),
        in_specs=[tile(D_MODEL),
                  pl.BlockSpec((1, N_MOD, D_MODEL), lambda i: (mod_index(i), 0, 0)),
                  tile(MLSTM_W), tile(MLSTM_W), tile(MLSTM_W), tile(H_A * V_DIM),
                  tile(D_MODEL), tile(D_MODEL),
                  _const_spec((1, MLSTM_W)), _const_spec((1, D_MODEL)), _const_spec((1, D_MODEL)),
                  _const_spec((1, D_MODEL)),
                  _const_spec((MLSTM_W, D_MODEL)), _const_spec((H_A * V_DIM, D_MODEL)),
                  _const_spec((D_MODEL, D_MODEL)), _const_spec((D_MODEL, D_FF)), _const_spec((D_FF, D_MODEL))],
        out_specs=tile(D_MODEL),
        out_shape=jax.ShapeDtypeStruct((n, D_MODEL), F32),
        compiler_params=_cparams(("arbitrary",)),
        name="stage_c",
    )(x, mods, hf, hb, og, attn, ga, gb, hn, post1, pre2, post2, wmo, wao, wout, w1, w2)


def _rope_tables(n_tokens):
    pos = np.arange(n_tokens)
    row = (pos // GRID_W).astype(np.float32)
    col = (pos % GRID_W).astype(np.float32)
    inv = (ROPE_BASE ** (-np.arange(0, AX_DIM, 2, dtype=np.float32) / AX_DIM)).astype(np.float32)
    ang = np.concatenate([row[:, None] * inv, col[:, None] * inv], axis=-1)
    cos = np.cos(ang.astype(np.float64)).astype(np.float32)
    sin = np.sin(ang.astype(np.float64)).astype(np.float32)
    ones = np.ones((n_tokens, NOPE), np.float32)
    zeros = np.zeros((n_tokens, NOPE), np.float32)
    pad = np.zeros((n_tokens, HEAD_PAD - NOPE - ROPE_DIM), np.float32)
    ta = np.concatenate([ones, cos, cos, pad], axis=-1)
    tb = np.concatenate([zeros, -sin, sin, pad], axis=-1)
    return jnp.asarray(ta), jnp.asarray(tb)


def _plain_tables(n_tokens):
    ones = jnp.ones((n_tokens, NOPE + ROPE_DIM), F32)
    pad = jnp.zeros((n_tokens, HEAD_PAD - NOPE - ROPE_DIM), F32)
    return jnp.concatenate([ones, pad], axis=-1), jnp.zeros((n_tokens, HEAD_PAD), F32)


def kernel(x_prompt, x_sample, cache_mla_ckv, cache_mla_krope, state_mlstm_C, state_mlstm_n, state_mlstm_m,
           c, c_ctx, w_ada, b_ada, norm_pre1, norm_post1, norm_pre2, norm_post2, w_in, mlstm_gate_b,
           mla_q_norm, mla_kv_norm, w_uq, w_ukv, w_mla_o, mlstm_head_norm, w_mlstm_o, w_out, w_mlp1, w_mlp2):
    bp, sp, _ = x_prompt.shape
    bs, ss, _ = x_sample.shape
    depth = w_in.shape[0]
    past = cache_mla_ckv.shape[2]
    nj = 2 * H_M
    even = np.arange(0, ROPE_DIM, 2)
    odd = np.arange(1, ROPE_DIM, 2)
    perm = np.concatenate([even, odd])
    perm_sw = np.concatenate([odd, even])

    xp = x_prompt.reshape(bp * sp, D_MODEL)
    xs = x_sample.reshape(bs * ss, D_MODEL)
    cc = jnp.zeros((8, D_MODEL), F32).at[:bs].set(c).at[bs].set(c_ctx)
    ta_lat, tb_lat = _rope_tables(ss)
    tm_ctx, tm_lat = 512, 512
    assert past == tm_lat
    ta_ctx, tb_ctx = _plain_tables(tm_ctx)

    new_ckv, new_krope, new_c, new_n, new_m = [], [], [], [], []
    for l in range(depth):
        cols = np.cumsum((W_BIG_COLS, N_GATES, Q_LORA, KV_LORA, ROPE_DIM))
        w_big, w_g, w_cq, w_ckv, w_kr, w_merge = jnp.split(w_in[l].astype(BF16), cols.tolist(), axis=1)
        w_g = w_g.reshape(D_MODEL, 2, 2, H_M).transpose(0, 2, 1, 3).reshape(D_MODEL, N_GATES)
        w_misc = jnp.concatenate([w_kr, w_g, jnp.zeros((D_MODEL, MISC_W - 3 * ROPE_DIM - N_GATES), BF16),
                                  w_kr[:, perm], w_kr[:, perm_sw]], axis=1)
        wa = (w_big, w_merge, jnp.concatenate([w_cq, w_ckv, w_misc], axis=1))
        gbias = jnp.zeros((1, MISC_W), F32).at[0, GATE_LANE0:GATE_LANE0 + N_GATES].set(
            mlstm_gate_b[l].transpose(1, 0, 2).reshape(N_GATES))
        uq = w_uq[l]
        wq = jnp.concatenate([uq[..., :NOPE], uq[..., NOPE:][..., perm], uq[..., NOPE:][..., perm_sw]],
                             axis=-1).reshape(Q_LORA, H_A * HEAD_PAD).astype(BF16)
        ukv = w_ukv[l]
        wk = jnp.concatenate([ukv[..., :NOPE], jnp.zeros((KV_LORA, H_A, HEAD_PAD - NOPE), F32)],
                             axis=-1).reshape(KV_LORA, H_A * HEAD_PAD).astype(BF16)
        wv = ukv[..., NOPE:].reshape(KV_LORA, H_A * V_DIM).T.astype(BF16)
        wmo = w_mlstm_o[l].astype(BF16)
        wao = w_mla_o[l].astype(BF16)
        wout = w_out[l].astype(BF16)
        w1 = w_mlp1[l].astype(BF16)
        w2 = w_mlp2[l].astype(BF16)
        pre1, post1 = norm_pre1[l][None], norm_post1[l][None]
        pre2, post2 = norm_pre2[l][None], norm_post2[l][None]
        qn, kvn = mla_q_norm[l][None], mla_kv_norm[l][None]
        hn = mlstm_head_norm[l].reshape(1, MLSTM_W)

        mods = _modulation(cc, w_ada[l], b_ada[l][None]).reshape(8, N_MOD, D_MODEL)

        ctx_mod = lambda i: bs
        a = _stage_a(xp, mods, ctx_mod, pre1, wa, gbias, qn, kvn, wq, wk, wv, ta_ctx, tb_ctx,
                     lambda i: 0, tm_ctx, sp)
        mq, mk, mv, og, ga, gb, gates, q, k, v, ckv_n, kro = a
        hf, hb, c_fin, n_fin, m_fin = _mlstm(mq, mk, mv, *_gate_rows(gates, GATE_CHUNKS_PER_STEP),
                                             None, None, bp, sp)
        attn = _attention(q, k, v, None, None, bp, sp, sp, 1, n_seq=ATTN_SEQS_PER_STEP)
        xp = _stage_c(xp, mods, ctx_mod, hf, hb, og, attn, ga, gb, hn, post1, pre2, post2,
                      wmo, wao, wout, w1, w2, tm_ctx)
        new_ckv.append(ckv_n.reshape(bp, sp, KV_LORA))
        new_krope.append(kro.reshape(bp, sp, ROPE_DIM))
        new_c.append(c_fin.reshape(bp, 2, H_M, DH_M, DH_M))
        new_n.append(n_fin.reshape(bp, 2, H_M, DH_M))
        new_m.append(m_fin[:, :, 0].reshape(bp, 2, H_M))

        tiles_per_seq = ss // tm_lat
        lat_mod = lambda i: i // tiles_per_seq
        a = _stage_a(xs, mods, lat_mod, pre1, wa, gbias, qn, kvn, wq, wk, wv, ta_lat, tb_lat,
                     lambda i: i % tiles_per_seq, tm_lat, ATTN_KEY_TILE)
        mq, mk, mv, og, ga, gb, gates, q, k, v, _, _ = a
        kr_cache = jnp.zeros((bs * past, LANE), F32).at[:, NOPE:NOPE + ROPE_DIM].set(
            cache_mla_krope[:, l].reshape(bs * past, ROPE_DIM)[:, perm])
        kc, vc = _kv_cache(cache_mla_ckv[:, l].reshape(bs * past, KV_LORA), kr_cache, wk, wv, past)
        n0 = state_mlstm_n[:, l].reshape(bs, nj, DH_M, 1)
        c0 = jnp.concatenate([state_mlstm_C[:, l].reshape(bs, nj, DH_M, DH_M),
                              jnp.broadcast_to(n0, (bs, nj, DH_M, DH_M))], axis=-1)
        m0 = jnp.broadcast_to(state_mlstm_m[:, l].reshape(bs, nj, 1), (bs, nj, LANE))
        hf, hb, _, _, _ = _mlstm(mq, mk, mv, *_gate_rows(gates, GATE_CHUNKS_PER_STEP), c0, m0, bs, ss)
        attn = _attention(q, k, v, kc, vc, bs, ss, ATTN_QUERY_TILE, KEY_TILES_PER_STEP)
        xs = _stage_c(xs, mods, lat_mod, hf, hb, og, attn, ga, gb, hn, post1, pre2, post2,
                      wmo, wao, wout, w1, w2, tm_lat)

    return (xp.reshape(bp, sp, D_MODEL), xs.reshape(bs, ss, D_MODEL),
            jnp.stack(new_ckv, axis=1), jnp.stack(new_krope, axis=1), jnp.stack(new_c, axis=1),
            jnp.stack(new_n, axis=1), jnp.stack(new_m, axis=1))
```

```python
import functools

import numpy as np
import jax
import jax.numpy as jnp
from jax import lax
from jax.experimental import pallas as pl
from jax.experimental.pallas import tpu as pltpu

F32 = jnp.float32
BF16 = jnp.bfloat16

D_MODEL = 1024
H_M = 4
DH_M = 128
MLSTM_W = H_M * DH_M
CHUNK = 128
H_A = 8
NOPE = 64
ROPE_DIM = 32
V_DIM = 64
Q_LORA = 384
KV_LORA = 256
AX_DIM = ROPE_DIM // 2
ROPE_BASE = 10000.0
GRID_W = 64
D_FF = 4 * D_MODEL
N_MOD = 6
EPS = 1e-6
N_GATES = 4 * H_M
M_INIT = -1e30

LANE = 128
HEAD_PAD = LANE
MISC_W = LANE
GATE_LANE0 = ROPE_DIM
OFF_MQ, OFF_MK, OFF_MV, OFF_MO, W_BIG_COLS = 0, MLSTM_W, 2 * MLSTM_W, 3 * MLSTM_W, 4 * MLSTM_W
OFF_CQ, OFF_CKV, OFF_MISC = 0, Q_LORA, Q_LORA + KV_LORA
W_SMALL_COLS = OFF_MISC + MISC_W

VMEM_LIMIT = 60 * 1024 * 1024
QK_SCALE = float((NOPE + ROPE_DIM) ** -0.5 * np.log2(np.e))
QK_LOOKAHEAD = 3
ATTN_KEY_TILE = 512
ATTN_SEQS_PER_STEP = 4
ATTN_QUERY_TILES_PER_STEP = 2
ATTN_QUERY_TILE = 256
ROW_SPLITS = 2
KEY_TILES_PER_STEP = 3
ONES_ROWS = 16


def _cparams(sem):
    return pltpu.CompilerParams(dimension_semantics=sem, vmem_limit_bytes=VMEM_LIMIT)


def _const_spec(shape):
    nd = len(shape)
    return pl.BlockSpec(shape, lambda *_: (0,) * nd, pipeline_mode=pl.Buffered(1))


def _rms(x, w):
    return x * lax.rsqrt(jnp.mean(x * x, axis=-1, keepdims=True) + EPS) * w


def _sigmoid(x):
    return 1.0 / (1.0 + jnp.exp(-x))


def _log_sigmoid(x):
    return jnp.minimum(x, 0.0) - jnp.log(1.0 + jnp.exp(-jnp.abs(x)))


def _bdot(a, b):
    return jnp.dot(a, b, preferred_element_type=F32)


def _mod_kernel(c_ref, w_ref, b_ref, o_ref):
    c = c_ref[...]
    s = c * _sigmoid(c)
    o_ref[...] = jnp.dot(s, w_ref[...], preferred_element_type=F32,
                         precision=lax.Precision.HIGHEST) + b_ref[...]


def _modulation(cc, w_ada, b_ada):
    n_out = w_ada.shape[1]
    tn = 1536
    return pl.pallas_call(
        _mod_kernel,
        grid=(n_out // tn,),
        in_specs=[pl.BlockSpec((8, D_MODEL), lambda j: (0, 0)),
                  pl.BlockSpec((D_MODEL, tn), lambda j: (0, j)),
                  pl.BlockSpec((1, tn), lambda j: (0, j))],
        out_specs=pl.BlockSpec((8, tn), lambda j: (0, j)),
        out_shape=jax.ShapeDtypeStruct((8, n_out), F32),
        compiler_params=_cparams(("arbitrary",)),
        name="modulation",
    )(cc, w_ada, b_ada)


def _stage_a_kernel(x_ref, mod_ref, pre1_ref, wbig_ref, wmerge_ref, wsmall_ref, gbias_ref, qn_ref, kvn_ref,
                    wq_ref, wk_ref, wv_ref,
                    ta_ref, tb_ref,
                    mq_ref, mk_ref, mv_ref, og_ref, ga_ref, gb_ref, gates_ref, q_ref, k_ref, v_ref,
                    ckv_ref, kro_ref):
    mod = mod_ref[0]
    shift1, scale1 = mod[0:1], mod[1:2]
    tm = x_ref.shape[0]
    rows = tm // ROW_SPLITS
    tkv = v_ref.shape[2]

    def normed(rs):
        return (_rms(x_ref[rs, :], pre1_ref[...]) * (1.0 + scale1) + shift1).astype(BF16)

    def small_proj(hb):
        return tuple(_bdot(hb, wsmall_ref[:, lo:hi])
                     for lo, hi in ((OFF_CQ, OFF_CKV), (OFF_CKV, OFF_MISC), (OFF_MISC, W_SMALL_COLS)))

    def big_proj(hb, rs):
        proj = lambda lo, hi: _bdot(hb, wbig_ref[:, lo:hi])
        mq_ref[rs, :] = (proj(OFF_MQ, OFF_MK) * DH_M ** -0.5).astype(BF16)
        mk_ref[:, rs] = proj(OFF_MK, OFF_MV).T.astype(BF16)
        mv_ref[rs, :] = proj(OFF_MV, OFF_MO).astype(BF16)
        og_ref[rs, :] = _sigmoid(proj(OFF_MO, W_BIG_COLS)).astype(BF16)
        ga_ref[rs, :] = _sigmoid(_bdot(hb, wmerge_ref[:, :D_MODEL])).astype(BF16)
        gb_ref[rs, :] = _sigmoid(_bdot(hb, wmerge_ref[:, D_MODEL:])).astype(BF16)

    def mla_build(cq, ckv, misc, rs):
        lane = lax.broadcasted_iota(jnp.int32, misc.shape, 1)
        g = misc + gbias_ref[...]
        is_f = (lane >= GATE_LANE0 + 2 * H_M) & (lane < GATE_LANE0 + N_GATES)
        gates_ref[rs, :] = jnp.where(is_f, _log_sigmoid(g), g)
        kro_ref[rs, :] = misc[:, 0:ROPE_DIM]

        ckv_n = _rms(ckv, kvn_ref[...])
        ckv_ref[rs, :] = ckv_n
        ckv_b = ckv_n.astype(BF16)

        ta = ta_ref[rs, :]
        tb = tb_ref[rs, :]
        qf = _bdot(_rms(cq, qn_ref[...]).astype(BF16), wq_ref[...])
        ta8 = jnp.concatenate([ta] * H_A, axis=1)
        tb8 = jnp.concatenate([tb] * H_A, axis=1)
        q = qf * ta8 + pltpu.roll(qf, H_A * HEAD_PAD - ROPE_DIM, 1) * tb8
        q_ref[rs, :] = (q * QK_SCALE).astype(BF16)

        ta_k = jnp.where(lane < NOPE, 0.0, ta)
        kr = misc * ta_k + pltpu.roll(misc, MISC_W - ROPE_DIM, 1) * tb
        kk = _bdot(ckv_b, wk_ref[...]) + jnp.concatenate([kr] * H_A, axis=1)
        k_ref[rs, :] = kk.astype(BF16)
        vt = _bdot(wv_ref[...], ckv_n.T.astype(BF16)).astype(BF16)
        step = min(rows, tkv)
        for off in range(0, rows, step):
            lo = rs.start + off
            v_ref[lo // tkv, :, lo % tkv:lo % tkv + step] = vt[:, off:off + step]

    groups = [slice(r * rows, (r + 1) * rows) for r in range(ROW_SPLITS)]
    hbs = [normed(rs) for rs in groups]
    for hb, rs in zip(hbs, groups):
        small = small_proj(hb)
        big_proj(hb, rs)
        mla_build(*small, rs)


def _stage_a(x, mods, mod_index, pre1, wa, gbias, qn, kvn, wq, wk, wv, ta, tb, table_index, tm, tkv):
    n = x.shape[0]
    tile = lambda w: pl.BlockSpec((tm, w), lambda i: (i, 0))
    bf = lambda w: jax.ShapeDtypeStruct((n, w), BF16)
    f32 = lambda w: jax.ShapeDtypeStruct((n, w), F32)
    return pl.pallas_call(
        _stage_a_kernel,
        grid=(n // tm,),
        in_specs=[tile(D_MODEL),
                  pl.BlockSpec((1, N_MOD, D_MODEL), lambda i: (mod_index(i), 0, 0)),
                  _const_spec((1, D_MODEL)),
                  _const_spec((D_MODEL, W_BIG_COLS)),
                  _const_spec((D_MODEL, 2 * D_MODEL)),
                  _const_spec((D_MODEL, W_SMALL_COLS)),
                  _const_spec((1, MISC_W)),
                  _const_spec((1, Q_LORA)),
                  _const_spec((1, KV_LORA)),
                  _const_spec((Q_LORA, H_A * HEAD_PAD)),
                  _const_spec((KV_LORA, H_A * HEAD_PAD)),
                  _const_spec((H_A * V_DIM, KV_LORA)),
                  pl.BlockSpec((tm, LANE), lambda i: (table_index(i), 0)),
                  pl.BlockSpec((tm, LANE), lambda i: (table_index(i), 0))],
        out_specs=[tile(MLSTM_W), pl.BlockSpec((MLSTM_W, tm), lambda i: (0, i)), tile(MLSTM_W), tile(MLSTM_W),
                   tile(D_MODEL), tile(D_MODEL), tile(MISC_W),
                   tile(H_A * HEAD_PAD), tile(H_A * HEAD_PAD),
                   pl.BlockSpec((tm // tkv, H_A * V_DIM, tkv), lambda i: (i, 0, 0)),
                   tile(KV_LORA), tile(ROPE_DIM)],
        out_shape=[bf(MLSTM_W), jax.ShapeDtypeStruct((MLSTM_W, n), BF16), bf(MLSTM_W), bf(MLSTM_W),
                   bf(D_MODEL), bf(D_MODEL), f32(MISC_W),
                   bf(H_A * HEAD_PAD), bf(H_A * HEAD_PAD),
                   jax.ShapeDtypeStruct((n // tkv, H_A * V_DIM, tkv), BF16),
                   f32(KV_LORA), f32(ROPE_DIM)],
        compiler_params=_cparams(("arbitrary",)),
        name="stage_a",
    )(x, mods, pre1, *wa, gbias, qn, kvn, wq, wk, wv, ta, tb)


def _kv_cache_kernel(ckv_ref, kr_ref, wk_ref, wv_ref, k_ref, v_ref):
    ckv_b = ckv_ref[...].astype(BF16)
    kk = _bdot(ckv_b, wk_ref[...]) + jnp.concatenate([kr_ref[...]] * H_A, axis=1)
    k_ref[...] = kk.astype(BF16)
    v_ref[0] = _bdot(wv_ref[...], ckv_ref[...].T.astype(BF16)).astype(BF16)


def _kv_cache(ckv, kr, wk, wv, tm):
    n = ckv.shape[0]
    return pl.pallas_call(
        _kv_cache_kernel,
        grid=(n // tm,),
        in_specs=[pl.BlockSpec((tm, KV_LORA), lambda i: (i, 0)),
                  pl.BlockSpec((tm, LANE), lambda i: (i, 0)),
                  _const_spec((KV_LORA, H_A * HEAD_PAD)),
                  _const_spec((H_A * V_DIM, KV_LORA))],
        out_specs=[pl.BlockSpec((tm, H_A * HEAD_PAD), lambda i: (i, 0)),
                   pl.BlockSpec((1, H_A * V_DIM, tm), lambda i: (i, 0, 0))],
        out_shape=[jax.ShapeDtypeStruct((n, H_A * HEAD_PAD), BF16),
                   jax.ShapeDtypeStruct((n // tm, H_A * V_DIM, tm), BF16)],
        compiler_params=_cparams(("arbitrary",)),
        name="kv_cache",
    )(ckv, kr, wk, wv)


def _split3(x):
    x1 = x.astype(BF16).astype(F32)
    r = x - x1
    x2 = r.astype(BF16).astype(F32)
    x3 = (r - x2).astype(BF16).astype(F32)
    return x1, x2, x3


GATE_ROWS = 32
MLSTM_SEQS_PER_STEP = 2
MLSTM_CHUNKS_PER_STEP = 4
GATE_CHUNKS_PER_STEP = 16


def _gate_rows_kernel(g_ref, cols_ref, rows_ref):
    nj = 2 * H_M
    nch = g_ref.shape[0] // CHUNK
    n_rows = nch * nj
    row = lax.broadcasted_iota(jnp.int32, (CHUNK, CHUNK), 0)
    col = lax.broadcasted_iota(jnp.int32, (CHUNK, CHUNK), 1)
    lower = (col <= row).astype(F32)
    upper = (col >= row).astype(F32)
    hi = lax.Precision.HIGHEST
    rowi = lax.broadcasted_iota(jnp.int32, (n_rows, CHUNK), 0)
    lane = lax.broadcasted_iota(jnp.int32, (n_rows, CHUNK), 1)
    is_fwd = rowi % nj < H_M

    gi, gf = GATE_LANE0, GATE_LANE0 + nj
    i_rows, f_rows = [], []
    for c in range(nch):
        g_t = g_ref[c * CHUNK:(c + 1) * CHUNK, :].T
        i_rows.append(g_t[gi:gi + nj])
        f_rows.append(g_t[gf:gf + nj])
    i_all = jnp.concatenate(i_rows, axis=0)
    f_all = jnp.concatenate(f_rows, axis=0)
    b = jnp.where(is_fwd, jnp.dot(f_all, upper, preferred_element_type=F32, precision=hi),
                  jnp.dot(f_all, lower, preferred_element_type=F32, precision=hi))
    a = i_all - b
    cm = a
    shift = 1
    while shift < CHUNK:
        y_f = jnp.where(lane >= shift, pltpu.roll(cm, shift, 1), -jnp.inf)
        y_b = jnp.where(lane < CHUNK - shift, pltpu.roll(cm, CHUNK - shift, 1), -jnp.inf)
        cm = jnp.maximum(cm, jnp.where(is_fwd, y_f, y_b))
        shift *= 2
    a_max = jnp.broadcast_to(jnp.max(a, axis=1, keepdims=True), a.shape)
    f_sum = jnp.broadcast_to(jnp.sum(f_all, axis=1, keepdims=True), a.shape)
    zeros8 = jnp.zeros((nj, CHUNK), F32)
    for c in range(nch):
        sl = slice(c * nj, (c + 1) * nj)
        stack = jnp.concatenate([cm[sl], b[sl]] + [zeros8] * (CHUNK // nj - 2), axis=0)
        cols_ref[c * CHUNK:(c + 1) * CHUNK, :] = stack.T
        rows_ref[c] = jnp.concatenate([a[sl], a_max[sl], f_sum[sl], zeros8], axis=0)


def _gate_rows(gates, chunks_per_step):
    n = gates.shape[0]
    tm = chunks_per_step * CHUNK
    return pl.pallas_call(
        _gate_rows_kernel,
        grid=(n // tm,),
        in_specs=[pl.BlockSpec((tm, MISC_W), lambda i: (i, 0))],
        out_specs=[pl.BlockSpec((tm, LANE), lambda i: (i, 0)),
                   pl.BlockSpec((chunks_per_step, GATE_ROWS, CHUNK), lambda i: (i, 0, 0))],
        out_shape=[jax.ShapeDtypeStruct((n, LANE), F32),
                   jax.ShapeDtypeStruct((n // CHUNK, GATE_ROWS, CHUNK), F32)],
        compiler_params=_cparams(("arbitrary",)),
        name="gate_rows",
    )(gates)


def _mlstm_kernel(*refs, has_state, n_seq):
    if has_state:
        (qf_ref, ktf_ref, vf_ref, ptf_ref, rf_ref, qb_ref, ktb_ref, vb_ref, ptb_ref, rb_ref, c0_ref, m0_ref,
         hf_ref, hb_ref, cn_ref, nn_ref, mn_ref, c_s, m_s) = refs
    else:
        (qf_ref, ktf_ref, vf_ref, ptf_ref, rf_ref, qb_ref, ktb_ref, vb_ref, ptb_ref, rb_ref,
         hf_ref, hb_ref, cn_ref, nn_ref, mn_ref, c_s, m_s) = refs
    step = pl.program_id(1)

    nj = 2 * H_M

    @pl.when(step == 0)
    def _():
        if has_state:
            for sq in range(n_seq):
                c_s[sq * nj:(sq + 1) * nj] = c0_ref[sq]
                m_s[sq * nj:(sq + 1) * nj] = m0_ref[sq]
        else:
            c_s[...] = jnp.zeros(c_s.shape, F32)
            m_s[...] = jnp.full(m_s.shape, M_INIT, F32)

    row = lax.broadcasted_iota(jnp.int32, (CHUNK, CHUNK), 0)
    col = lax.broadcasted_iota(jnp.int32, (CHUNK, CHUNK), 1)
    is_fwd = lax.broadcasted_iota(jnp.int32, (nj, CHUNK), 0) < H_M
    ones_v = jnp.ones((CHUNK, DH_M), BF16)
    dirs = ((qf_ref, ktf_ref, vf_ref, ptf_ref, hf_ref), (qb_ref, ktb_ref, vb_ref, ptb_ref, hb_ref))
    chains = [(sq, d, hd) for sq in range(n_seq) for d in range(2) for hd in range(H_M)]
    n_sub = rf_ref.shape[0] // n_seq
    for sub in range(n_sub):
        sub_of = (sub, n_sub - 1 - sub)
        gate = []
        for sq in range(n_seq):
            rows_f, rows_b = rf_ref[sq * n_sub + sub_of[0]], rb_ref[sq * n_sub + sub_of[1]]
            pick = lambda g: jnp.where(is_fwd, rows_f[g * nj:(g + 1) * nj], rows_b[g * nj:(g + 1) * nj])
            a8, a_max8, f_sum8 = pick(0), pick(1), pick(2)
            m8 = m_s[sq * nj:(sq + 1) * nj]
            mx8 = jnp.maximum(m8, a_max8)
            gate.append((a8, m8, jnp.exp(a8 - mx8), jnp.exp(m8 - mx8)))
            m_s[sq * nj:(sq + 1) * nj] = f_sum8 + mx8
        tok = lambda sq, d: slice((sq * n_sub + sub_of[d]) * CHUNK, (sq * n_sub + sub_of[d] + 1) * CHUNK)

        first = []
        for sq, d, hd in chains:
            q_ref, kt_ref, v_ref, _, _ = dirs[d]
            j = d * H_M + hd
            sl = slice(hd * DH_M, (hd + 1) * DH_M)
            q = q_ref[tok(sq, d), sl]
            k_t = kt_ref[sl, tok(sq, d)]
            v_aug = jnp.concatenate([v_ref[tok(sq, d), sl], ones_v], axis=1)
            qk = _bdot(q, k_t)
            c_st = c_s[sq * nj + j]
            qc = _bdot(q, c_st.astype(BF16))
            upd = _bdot((k_t.astype(F32) * gate[sq][2][j:j + 1, :]).astype(BF16), v_aug)
            dec = gate[sq][3][j:j + 1, :]
            c_s[sq * nj + j] = jnp.concatenate([dec, dec], axis=1) * c_st + upd
            first.append((qk, qc, v_aug))
        for (sq, d, hd), (qk, qc, v_aug) in zip(chains, first):
            cols_ref, h_ref = dirs[d][3], dirs[d][4]
            j = d * H_M + hd
            sl = slice(hd * DH_M, (hd + 1) * DH_M)
            mask = (col <= row) if d == 0 else (col >= row)
            a8, m8 = gate[sq][0], gate[sq][1]
            m_row = m8[j:j + 1, :]
            u = jnp.maximum(m_row, cols_ref[tok(sq, d), j:j + 1])
            w = jnp.exp(jnp.where(mask, a8[j:j + 1, :] - u, -jnp.inf))
            s_inter = jnp.exp(m_row - u)
            intra = _bdot((qk * w).astype(BF16), v_aug)
            num = s_inter * qc[:, :DH_M] + intra[:, :DH_M]
            den = s_inter * qc[:, DH_M:] + intra[:, DH_M:]
            h_ref[tok(sq, d), sl] = num / jnp.maximum(
                jnp.abs(den), jnp.exp(-(cols_ref[tok(sq, d), nj + j:nj + j + 1] + u)))

    @pl.when(step == pl.num_programs(1) - 1)
    def _():
        for sq in range(n_seq):
            for j in range(nj):
                c_aug = c_s[sq * nj + j]
                cn_ref[sq, j] = c_aug[:, :DH_M]
                nn_ref[sq, j:j + 1, :] = c_aug[:, DH_M:].T[0:1, :]
            mn_ref[sq] = m_s[sq * nj:(sq + 1) * nj]


def _mlstm(mq, mkt, mv, gate_pt, gate_rows, c0, m0, batch, seq):
    has_state = c0 is not None
    n_sub = min(MLSTM_CHUNKS_PER_STEP, seq // CHUNK)
    nc = seq // (n_sub * CHUNK)
    n_seq = MLSTM_SEQS_PER_STEP if nc == 1 and batch % MLSTM_SEQS_PER_STEP == 0 else 1
    tm = n_seq * n_sub * CHUNK
    n = batch * seq
    nj = 2 * H_M
    fwd = lambda w: pl.BlockSpec((tm, w), lambda b, c: (b * nc + c, 0))
    bwd = lambda w: pl.BlockSpec((tm, w), lambda b, c: (b * nc + nc - 1 - c, 0))
    fwd_t = pl.BlockSpec((MLSTM_W, tm), lambda b, c: (0, b * nc + c))
    bwd_t = pl.BlockSpec((MLSTM_W, tm), lambda b, c: (0, b * nc + nc - 1 - c))
    st_m = pl.BlockSpec((n_seq, nj, LANE), lambda b, c: (b, 0, 0))
    st_c = pl.BlockSpec((n_seq, nj, DH_M, 2 * DH_M), lambda b, c: (b, 0, 0, 0))
    fwd_r = pl.BlockSpec((n_seq * n_sub, GATE_ROWS, CHUNK), lambda b, c: (b * nc + c, 0, 0))
    bwd_r = pl.BlockSpec((n_seq * n_sub, GATE_ROWS, CHUNK), lambda b, c: (b * nc + nc - 1 - c, 0, 0))
    in_specs = [fwd(MLSTM_W), fwd_t, fwd(MLSTM_W), fwd(LANE), fwd_r,
                bwd(MLSTM_W), bwd_t, bwd(MLSTM_W), bwd(LANE), bwd_r]
    args = [mq, mkt, mv, gate_pt, gate_rows, mq, mkt, mv, gate_pt, gate_rows]
    if has_state:
        in_specs += [st_c, st_m]
        args += [c0, m0]
    return pl.pallas_call(
        functools.partial(_mlstm_kernel, has_state=has_state, n_seq=n_seq),
        grid=(batch // n_seq, nc),
        in_specs=in_specs,
        out_specs=[fwd(MLSTM_W), bwd(MLSTM_W),
                   pl.BlockSpec((n_seq, nj, DH_M, DH_M), lambda b, c: (b, 0, 0, 0)), st_m, st_m],
        out_shape=[jax.ShapeDtypeStruct((n, MLSTM_W), F32), jax.ShapeDtypeStruct((n, MLSTM_W), F32),
                   jax.ShapeDtypeStruct((batch, nj, DH_M, DH_M), F32),
                   jax.ShapeDtypeStruct((batch, nj, DH_M), F32),
                   jax.ShapeDtypeStruct((batch, nj, LANE), F32)],
        scratch_shapes=[pltpu.VMEM((n_seq * nj, DH_M, 2 * DH_M), F32), pltpu.VMEM((n_seq * nj, LANE), F32)],
        compiler_params=_cparams(("arbitrary", "arbitrary")),
        name="mlstm",
    )(*args)


def _attn_kernel(*refs, tiles_per_step, has_cache, n_seq, shared_keys):
    if has_cache:
        q_ref, k_ref, vt_ref, kc_ref, vct_ref, o_ref, qt_s, acc_s, m_s, ot_s, s_scr = refs
    else:
        q_ref, k_ref, vt_ref, o_ref, qt_s, acc_s, m_s, ot_s, s_scr = refs
    tk = vt_ref.shape[2]
    n_tiles = vt_ref.shape[0] // (1 if shared_keys else n_seq)
    tq = q_ref.shape[0] // n_seq
    d_q, d_o = H_A * HEAD_PAD, H_A * V_DIM
    for sq in range(n_seq):
        qt_s[sq * d_q:(sq + 1) * d_q, :] = q_ref[sq * tq:(sq + 1) * tq, :].astype(F32).T.astype(BF16)
    acc_s[...] = jnp.zeros(acc_s.shape, F32)
    m_s[...] = jnp.full(m_s.shape, -jnp.inf, F32)

    n_slots = s_scr.shape[0]

    def scores(tile, head, slot):
        get_k, _, width, sq = tile
        stream = sq * H_A + head
        s = _bdot(get_k(head), qt_s[stream * HEAD_PAD:(stream + 1) * HEAD_PAD, :])
        s_scr[slot, 0:width, :] = s
        return jnp.max(s, axis=0, keepdims=True)

    def update(tiles, pending, next_tile):
        items = [(tile, head) for tile in tiles for head in range(H_A)]
        ahead = items + ([(next_tile, h) for h in range(QK_LOOKAHEAD)] if next_tile is not None else [])
        pending = list(pending)
        for idx, ((_, get_vt, width, sq), head) in enumerate(items):
            stream = sq * H_A + head
            m_tile = pending.pop(0)
            if idx + QK_LOOKAHEAD < len(ahead):
                pending.append(scores(*ahead[idx + QK_LOOKAHEAD], (idx + QK_LOOKAHEAD) % n_slots))
            m_old = m_s[stream:stream + 1, :]
            m_new = jnp.maximum(m_old, m_tile)
            alpha = jnp.exp2(m_old - m_new)
            p = jnp.exp2((s_scr[idx % n_slots, 0:width, :] - m_new).astype(BF16))
            ones = jnp.ones((ONES_ROWS, width), BF16)
            lhs = jnp.concatenate([get_vt(head), ones], axis=0)
            acc_s[stream] = alpha * acc_s[stream] + _bdot(lhs, p)
            m_s[stream:stream + 1, :] = m_new
        return tuple(pending)

    def main_tile(t, sq=0):
        base = 0 if shared_keys else sq * n_tiles
        start = (base + t) * tk if isinstance(t, int) else pl.multiple_of(t * tk, tk)
        rows = pl.ds(start, tk)
        return (lambda h: k_ref[rows, h * HEAD_PAD:(h + 1) * HEAD_PAD],
                lambda h: vt_ref[base + t, h * V_DIM:(h + 1) * V_DIM, :], tk, sq)

    def body(sq, i, pending):
        first = i * tiles_per_step
        return update([main_tile(first + u, sq) for u in range(tiles_per_step)], pending,
                      main_tile(first + tiles_per_step, sq))

    n_loop = (n_tiles - 1) // tiles_per_step

    def tail_tiles(sq):
        tiles = [main_tile(t, sq) for t in range(n_loop * tiles_per_step, n_tiles)]
        if has_cache:
            tiles.append((lambda h: kc_ref[:, h * HEAD_PAD:(h + 1) * HEAD_PAD],
                          lambda h: vct_ref[0, h * V_DIM:(h + 1) * V_DIM, :], kc_ref.shape[0], sq))
        return tiles

    pending = tuple(scores(main_tile(0, 0), h, h) for h in range(QK_LOOKAHEAD))
    if n_loop:
        assert (tiles_per_step * H_A) % n_slots == 0 and (len(tail_tiles(0)) * H_A) % n_slots == 0
        for sq in range(n_seq):
            pending = lax.fori_loop(0, n_loop, functools.partial(body, sq), pending)
            pending = update(tail_tiles(sq), pending, main_tile(0, sq + 1) if sq + 1 < n_seq else None)
    else:
        update([tile for sq in range(n_seq) for tile in tail_tiles(sq)], pending, None)
    for stream in range(n_seq * H_A):
        acc = acc_s[stream]
        ot_s[stream * V_DIM:(stream + 1) * V_DIM, :] = acc[0:V_DIM] * (1.0 / acc[V_DIM:V_DIM + 1])
    for sq in range(n_seq):
        o_ref[sq * tq:(sq + 1) * tq, :] = ot_s[sq * d_o:(sq + 1) * d_o, :].T.astype(BF16)


def _attention(q, k, vt, kc, vct, batch, seq, tq, tiles_per_step, n_seq=1):
    nq = seq // tq
    tk = vt.shape[2]
    n_tiles = seq // tk
    has_cache = kc is not None
    shared_keys = n_seq > 1 and nq > 1
    if shared_keys:
        assert nq % n_seq == 0
        nq, key_seqs, grid = nq // n_seq, 1, (batch, nq // n_seq)
    else:
        assert n_seq == 1 or (nq == 1 and n_tiles == 1 and not has_cache and batch % n_seq == 0)
        key_seqs, grid = n_seq, (batch // n_seq, nq)
    max_width = max(tk, kc.shape[0] // batch) if has_cache else tk
    in_specs = [pl.BlockSpec((n_seq * tq, H_A * HEAD_PAD), lambda b, i: (b * nq + i, 0)),
                pl.BlockSpec((key_seqs * seq, H_A * HEAD_PAD), lambda b, i: (b, 0)),
                pl.BlockSpec((key_seqs * n_tiles, H_A * V_DIM, tk), lambda b, i: (b, 0, 0))]
    args = [q, k, vt]
    if has_cache:
        past = kc.shape[0] // batch
        in_specs += [pl.BlockSpec((past, H_A * HEAD_PAD), lambda b, i: (b, 0)),
                     pl.BlockSpec((1, H_A * V_DIM, past), lambda b, i: (b, 0, 0))]
        args += [kc, vct]
    return pl.pallas_call(
        functools.partial(_attn_kernel, tiles_per_step=tiles_per_step, has_cache=has_cache, n_seq=n_seq,
                          shared_keys=shared_keys),
        grid=grid,
        in_specs=in_specs,
        out_specs=pl.BlockSpec((n_seq * tq, H_A * V_DIM), lambda b, i: (b * nq + i, 0)),
        out_shape=jax.ShapeDtypeStruct((batch * seq, H_A * V_DIM), BF16),
        scratch_shapes=[pltpu.VMEM((n_seq * H_A * HEAD_PAD, tq), BF16),
                        pltpu.VMEM((n_seq * H_A, V_DIM + ONES_ROWS, tq), F32),
                        pltpu.VMEM((n_seq * H_A, tq), F32),
                        pltpu.VMEM((n_seq * H_A * V_DIM, tq), F32),
                        pltpu.VMEM((QK_LOOKAHEAD + 1, max_width, tq), F32)],
        compiler_params=_cparams(("arbitrary", "arbitrary")),
        name="mla_attention",
    )(*args)


def _stage_c_kernel(x_ref, mod_ref, hf_ref, hb_ref, og_ref, attn_ref, ga_ref, gb_ref, hn_ref,
                    post1_ref, pre2_ref, post2_ref, wmo_ref, wao_ref, wout_ref, w1_ref, w2_ref, y_ref):
    mod = mod_ref[0]
    gate1, shift2, scale2, gate2 = mod[2:3], mod[3:4], mod[4:5], mod[5:6]
    hn = hn_ref[...]
    rows = x_ref.shape[0] // ROW_SPLITS
    groups = [slice(r * rows, (r + 1) * rows) for r in range(ROW_SPLITS)]
    n_ff = D_FF // D_MODEL

    def mixer_in(rs):
        hm = hf_ref[rs, :] + hb_ref[rs, :]
        heads = [_rms(hm[:, hd * DH_M:(hd + 1) * DH_M], hn[:, hd * DH_M:(hd + 1) * DH_M]) for hd in range(H_M)]
        return (jnp.concatenate(heads, axis=1) * og_ref[rs, :].astype(F32)).astype(BF16)

    def branch_out(hm, rs):
        return _bdot(hm, wmo_ref[...]), _bdot(attn_ref[rs, :], wao_ref[...])

    def merge(y_m, y_a, rs):
        return (ga_ref[rs, :].astype(F32) * y_m + gb_ref[rs, :].astype(F32) * y_a).astype(BF16)

    def mid(mix, rs):
        x1 = x_ref[rs, :] + gate1 * _rms(mix, post1_ref[...])
        return x1, (_rms(x1, pre2_ref[...]) * (1.0 + scale2) + shift2).astype(BF16)

    def mlp(h2):
        up = lambda c: _bdot(h2, w1_ref[:, c * D_MODEL:(c + 1) * D_MODEL])
        ff = jnp.zeros((rows, D_MODEL), F32)
        nxt = up(0)
        for c in range(n_ff):
            a = jnp.maximum(nxt, 0.0)
            if c + 1 < n_ff:
                nxt = up(c + 1)
            ff = ff + _bdot((a * a).astype(BF16), w2_ref[c * D_MODEL:(c + 1) * D_MODEL, :])
        return ff

    hms = [mixer_in(rs) for rs in groups]
    ys = [branch_out(hm, rs) for hm, rs in zip(hms, groups)]
    mixes = [_bdot(merge(*y, rs), wout_ref[...]) for y, rs in zip(ys, groups)]
    mids = [mid(mix, rs) for mix, rs in zip(mixes, groups)]
    ffs = [mlp(h2) for _, h2 in mids]
    for (x1, _), ff, rs in zip(mids, ffs, groups):
        y_ref[rs, :] = x1 + gate2 * _rms(ff, post2_ref[...])


def _stage_c(x, mods, mod_index, hf, hb, og, attn, ga, gb, hn, post1, pre2, post2, wmo, wao, wout, w1, w2, tm):
    n = x.shape[0]
    tile = lambda w: pl.BlockSpec((tm, w), lambda i: (i, 0))
    return pl.pallas_call(
        _stage_c_kernel,
        grid=(n // tm,),
        in_specs=[tile(D_MODEL),
                  pl.BlockSpec((1, N_MOD, D_MODEL), lambda i: (mod_index(i), 0, 0)),
                  tile(MLSTM_W), tile(MLSTM_W), tile(MLSTM_W), tile(H_A * V_DIM),
                  tile(D_MODEL), tile(D_MODEL),
                  _const_spec((1, MLSTM_W)), _const_spec((1, D_MODEL)), _const_spec((1, D_MODEL)),
                  _const_spec((1, D_MODEL)),
                  _const_spec((MLSTM_W, D_MODEL)), _const_spec((H_A * V_DIM, D_MODEL)),
                  _const_spec((D_MODEL, D_MODEL)), _const_spec((D_MODEL, D_FF)), _const_spec((D_FF, D_MODEL))],
        out_specs=tile(D_MODEL),
        out_shape=jax.ShapeDtypeStruct((n, D_MODEL), F32),
        compiler_params=_cparams(("arbitrary",)),
        name="stage_c",
    )(x, mods, hf, hb, og, attn, ga, gb, hn, post1, pre2, post2, wmo, wao, wout, w1, w2)


def _rope_tables(n_tokens):
    pos = np.arange(n_tokens)
    row = (pos // GRID_W).astype(np.float32)
    col = (pos % GRID_W).astype(np.float32)
    inv = (ROPE_BASE ** (-np.arange(0, AX_DIM, 2, dtype=np.float32) / AX_DIM)).astype(np.float32)
    ang = np.concatenate([row[:, None] * inv, col[:, None] * inv], axis=-1)
    cos = np.cos(ang.astype(np.float64)).astype(np.float32)
    sin = np.sin(ang.astype(np.float64)).astype(np.float32)
    ones = np.ones((n_tokens, NOPE), np.float32)
    zeros = np.zeros((n_tokens, NOPE), np.float32)
    pad = np.zeros((n_tokens, HEAD_PAD - NOPE - ROPE_DIM), np.float32)
    ta = np.concatenate([ones, cos, cos, pad], axis=-1)
    tb = np.concatenate([zeros, -sin, sin, pad], axis=-1)
    return jnp.asarray(ta), jnp.asarray(tb)


def _plain_tables(n_tokens):
    ones = jnp.ones((n_tokens, NOPE + ROPE_DIM), F32)
    pad = jnp.zeros((n_tokens, HEAD_PAD - NOPE - ROPE_DIM), F32)
    return jnp.concatenate([ones, pad], axis=-1), jnp.zeros((n_tokens, HEAD_PAD), F32)


def kernel(x_prompt, x_sample, cache_mla_ckv, cache_mla_krope, state_mlstm_C, state_mlstm_n, state_mlstm_m,
           c, c_ctx, w_ada, b_ada, norm_pre1, norm_post1, norm_pre2, norm_post2, w_in, mlstm_gate_b,
           mla_q_norm, mla_kv_norm, w_uq, w_ukv, w_mla_o, mlstm_head_norm, w_mlstm_o, w_out, w_mlp1, w_mlp2):
    bp, sp, _ = x_prompt.shape
    bs, ss, _ = x_sample.shape
    depth = w_in.shape[0]
    past = cache_mla_ckv.shape[2]
    nj = 2 * H_M
    even = np.arange(0, ROPE_DIM, 2)
    odd = np.arange(1, ROPE_DIM, 2)
    perm = np.concatenate([even, odd])
    perm_sw = np.concatenate([odd, even])

    xp = x_prompt.reshape(bp * sp, D_MODEL)
    xs = x_sample.reshape(bs * ss, D_MODEL)
    cc = jnp.zeros((8, D_MODEL), F32).at[:bs].set(c).at[bs].set(c_ctx)
    ta_lat, tb_lat = _rope_tables(ss)
    tm_ctx, tm_lat = 512, 512
    assert past == tm_lat
    ta_ctx, tb_ctx = _plain_tables(tm_ctx)

    new_ckv, new_krope, new_c, new_n, new_m = [], [], [], [], []
    for l in range(depth):
        cols = np.cumsum((W_BIG_COLS, N_GATES, Q_LORA, KV_LORA, ROPE_DIM))
        w_big, w_g, w_cq, w_ckv, w_kr, w_merge = jnp.split(w_in[l].astype(BF16), cols.tolist(), axis=1)
        w_g = w_g.reshape(D_MODEL, 2, 2, H_M).transpose(0, 2, 1, 3).reshape(D_MODEL, N_GATES)
        w_misc = jnp.concatenate([w_kr, w_g, jnp.zeros((D_MODEL, MISC_W - 3 * ROPE_DIM - N_GATES), BF16),
                                  w_kr[:, perm], w_kr[:, perm_sw]], axis=1)
        wa = (w_big, w_merge, jnp.concatenate([w_cq, w_ckv, w_misc], axis=1))
        gbias = jnp.zeros((1, MISC_W), F32).at[0, GATE_LANE0:GATE_LANE0 + N_GATES].set(
            mlstm_gate_b[l].transpose(1, 0, 2).reshape(N_GATES))
        uq = w_uq[l]
        wq = jnp.concatenate([uq[..., :NOPE], uq[..., NOPE:][..., perm], uq[..., NOPE:][..., perm_sw]],
                             axis=-1).reshape(Q_LORA, H_A * HEAD_PAD).astype(BF16)
        ukv = w_ukv[l]
        wk = jnp.concatenate([ukv[..., :NOPE], jnp.zeros((KV_LORA, H_A, HEAD_PAD - NOPE), F32)],
                             axis=-1).reshape(KV_LORA, H_A * HEAD_PAD).astype(BF16)
        wv = ukv[..., NOPE:].reshape(KV_LORA, H_A * V_DIM).T.astype(BF16)
        wmo = w_mlstm_o[l].astype(BF16)
        wao = w_mla_o[l].astype(BF16)
        wout = w_out[l].astype(BF16)
        w1 = w_mlp1[l].astype(BF16)
        w2 = w_mlp2[l].astype(BF16)
        pre1, post1 = norm_pre1[l][None], norm_post1[l][None]
        pre2, post2 = norm_pre2[l][None], norm_post2[l][None]
        qn, kvn = mla_q_norm[l][None], mla_kv_norm[l][None]
        hn = mlstm_head_norm[l].reshape(1, MLSTM_W)

        mods = _modulation(cc, w_ada[l], b_ada[l][None]).reshape(8, N_MOD, D_MODEL)

        ctx_mod = lambda i: bs
        a = _stage_a(xp, mods, ctx_mod, pre1, wa, gbias, qn, kvn, wq, wk, wv, ta_ctx, tb_ctx,
                     lambda i: 0, tm_ctx, sp)
        mq, mk, mv, og, ga, gb, gates, q, k, v, ckv_n, kro = a
        hf, hb, c_fin, n_fin, m_fin = _mlstm(mq, mk, mv, *_gate_rows(gates, GATE_CHUNKS_PER_STEP),
                                             None, None, bp, sp)
        attn = _attention(q, k, v, None, None, bp, sp, sp, 1, n_seq=ATTN_SEQS_PER_STEP)
        xp = _stage_c(xp, mods, ctx_mod, hf, hb, og, attn, ga, gb, hn, post1, pre2, post2,
                      wmo, wao, wout, w1, w2, tm_ctx)
        new_ckv.append(ckv_n.reshape(bp, sp, KV_LORA))
        new_krope.append(kro.reshape(bp, sp, ROPE_DIM))
        new_c.append(c_fin.reshape(bp, 2, H_M, DH_M, DH_M))
        new_n.append(n_fin.reshape(bp, 2, H_M, DH_M))
        new_m.append(m_fin[:, :, 0].reshape(bp, 2, H_M))

        tiles_per_seq = ss // tm_lat
        lat_mod = lambda i: i // tiles_per_seq
        a = _stage_a(xs, mods, lat_mod, pre1, wa, gbias, qn, kvn, wq, wk, wv, ta_lat, tb_lat,
                     lambda i: i % tiles_per_seq, tm_lat, ATTN_KEY_TILE)
        mq, mk, mv, og, ga, gb, gates, q, k, v, _, _ = a
        kr_cache = jnp.zeros((bs * past, LANE), F32).at[:, NOPE:NOPE + ROPE_DIM].set(
            cache_mla_krope[:, l].reshape(bs * past, ROPE_DIM)[:, perm])
        kc, vc = _kv_cache(cache_mla_ckv[:, l].reshape(bs * past, KV_LORA), kr_cache, wk, wv, past)
        n0 = state_mlstm_n[:, l].reshape(bs, nj, DH_M, 1)
        c0 = jnp.concatenate([state_mlstm_C[:, l].reshape(bs, nj, DH_M, DH_M),
                              jnp.broadcast_to(n0, (bs, nj, DH_M, DH_M))], axis=-1)
        m0 = jnp.broadcast_to(state_mlstm_m[:, l].reshape(bs, nj, 1), (bs, nj, LANE))
        hf, hb, _, _, _ = _mlstm(mq, mk, mv, *_gate_rows(gates, GATE_CHUNKS_PER_STEP), c0, m0, bs, ss)
        attn = _attention(q, k, v, kc, vc, bs, ss, ATTN_QUERY_TILE, KEY_TILES_PER_STEP,
                          n_seq=ATTN_QUERY_TILES_PER_STEP)
        xs = _stage_c(xs, mods, lat_mod, hf, hb, og, attn, ga, gb, hn, post1, pre2, post2,
                      wmo, wao, wout, w1, w2, tm_lat)

    return (xp.reshape(bp, sp, D_MODEL), xs.reshape(bs, ss, D_MODEL),
            jnp.stack(new_ckv, axis=1), jnp.stack(new_krope, axis=1), jnp.stack(new_c, axis=1),
            jnp.stack(new_n, axis=1), jnp.stack(new_m, axis=1))
```

```python
import functools

import numpy as np
import jax
import jax.numpy as jnp
from jax import lax
from jax.experimental import pallas as pl
from jax.experimental.pallas import tpu as pltpu

F32 = jnp.float32
BF16 = jnp.bfloat16

D_MODEL = 1024
H_M = 4
DH_M = 128
MLSTM_W = H_M * DH_M
CHUNK = 128
H_A = 8
NOPE = 64
ROPE_DIM = 32
V_DIM = 64
Q_LORA = 384
KV_LORA = 256
AX_DIM = ROPE_DIM // 2
ROPE_BASE = 10000.0
GRID_W = 64
D_FF = 4 * D_MODEL
N_MOD = 6
EPS = 1e-6
N_GATES = 4 * H_M
M_INIT = -1e30

LANE = 128
HEAD_PAD = LANE
MISC_W = LANE
GATE_LANE0 = ROPE_DIM
OFF_MQ, OFF_MK, OFF_MV, OFF_MO, W_BIG_COLS = 0, MLSTM_W, 2 * MLSTM_W, 3 * MLSTM_W, 4 * MLSTM_W
OFF_CQ, OFF_CKV, OFF_MISC = 0, Q_LORA, Q_LORA + KV_LORA
W_SMALL_COLS = OFF_MISC + MISC_W

VMEM_LIMIT = 60 * 1024 * 1024
QK_SCALE = float((NOPE + ROPE_DIM) ** -0.5 * np.log2(np.e))
QK_LOOKAHEAD = 3
ATTN_KEY_TILE = 512
ATTN_SEQS_PER_STEP = 4
ATTN_QUERY_TILES_PER_STEP = 4
ATTN_QUERY_TILE = 256
ROW_SPLITS = 2
KEY_TILES_PER_STEP = 3
ONES_ROWS = 16


def _cparams(sem):
    return pltpu.CompilerParams(dimension_semantics=sem, vmem_limit_bytes=VMEM_LIMIT)


def _const_spec(shape):
    nd = len(shape)
    return pl.BlockSpec(shape, lambda *_: (0,) * nd, pipeline_mode=pl.Buffered(1))


def _rms(x, w):
    return x * lax.rsqrt(jnp.mean(x * x, axis=-1, keepdims=True) + EPS) * w


def _sigmoid(x):
    return 1.0 / (1.0 + jnp.exp(-x))


def _log_sigmoid(x):
    return jnp.minimum(x, 0.0) - jnp.log(1.0 + jnp.exp(-jnp.abs(x)))


def _bdot(a, b):
    return jnp.dot(a, b, preferred_element_type=F32)


def _mod_kernel(c_ref, w_ref, b_ref, o_ref):
    c = c_ref[...]
    s = c * _sigmoid(c)
    o_ref[...] = jnp.dot(s, w_ref[...], preferred_element_type=F32,
                         precision=lax.Precision.HIGHEST) + b_ref[...]


def _modulation(cc, w_ada, b_ada):
    n_out = w_ada.shape[1]
    tn = 1536
    return pl.pallas_call(
        _mod_kernel,
        grid=(n_out // tn,),
        in_specs=[pl.BlockSpec((8, D_MODEL), lambda j: (0, 0)),
                  pl.BlockSpec((D_MODEL, tn), lambda j: (0, j)),
                  pl.BlockSpec((1, tn), lambda j: (0, j))],
        out_specs=pl.BlockSpec((8, tn), lambda j: (0, j)),
        out_shape=jax.ShapeDtypeStruct((8, n_out), F32),
        compiler_params=_cparams(("arbitrary",)),
        name="modulation",
    )(cc, w_ada, b_ada)


def _stage_a_kernel(x_ref, mod_ref, pre1_ref, wbig_ref, wmerge_ref, wsmall_ref, gbias_ref, qn_ref, kvn_ref,
                    wq_ref, wk_ref, wv_ref,
                    ta_ref, tb_ref,
                    mq_ref, mk_ref, mv_ref, og_ref, ga_ref, gb_ref, gates_ref, q_ref, k_ref, v_ref,
                    ckv_ref, kro_ref):
    mod = mod_ref[0]
    shift1, scale1 = mod[0:1], mod[1:2]
    tm = x_ref.shape[0]
    rows = tm // ROW_SPLITS
    tkv = v_ref.shape[2]

    def normed(rs):
        return (_rms(x_ref[rs, :], pre1_ref[...]) * (1.0 + scale1) + shift1).astype(BF16)

    def small_proj(hb):
        return tuple(_bdot(hb, wsmall_ref[:, lo:hi])
                     for lo, hi in ((OFF_CQ, OFF_CKV), (OFF_CKV, OFF_MISC), (OFF_MISC, W_SMALL_COLS)))

    def big_proj(hb, rs):
        proj = lambda lo, hi: _bdot(hb, wbig_ref[:, lo:hi])
        mq_ref[rs, :] = (proj(OFF_MQ, OFF_MK) * DH_M ** -0.5).astype(BF16)
        mk_ref[:, rs] = proj(OFF_MK, OFF_MV).T.astype(BF16)
        mv_ref[rs, :] = proj(OFF_MV, OFF_MO).astype(BF16)
        og_ref[rs, :] = _sigmoid(proj(OFF_MO, W_BIG_COLS)).astype(BF16)
        ga_ref[rs, :] = _sigmoid(_bdot(hb, wmerge_ref[:, :D_MODEL])).astype(BF16)
        gb_ref[rs, :] = _sigmoid(_bdot(hb, wmerge_ref[:, D_MODEL:])).astype(BF16)

    def mla_build(cq, ckv, misc, rs):
        lane = lax.broadcasted_iota(jnp.int32, misc.shape, 1)
        g = misc + gbias_ref[...]
        is_f = (lane >= GATE_LANE0 + 2 * H_M) & (lane < GATE_LANE0 + N_GATES)
        gates_ref[rs, :] = jnp.where(is_f, _log_sigmoid(g), g)
        kro_ref[rs, :] = misc[:, 0:ROPE_DIM]

        ckv_n = _rms(ckv, kvn_ref[...])
        ckv_ref[rs, :] = ckv_n
        ckv_b = ckv_n.astype(BF16)

        ta = ta_ref[rs, :]
        tb = tb_ref[rs, :]
        qf = _bdot(_rms(cq, qn_ref[...]).astype(BF16), wq_ref[...])
        ta8 = jnp.concatenate([ta] * H_A, axis=1)
        tb8 = jnp.concatenate([tb] * H_A, axis=1)
        q = qf * ta8 + pltpu.roll(qf, H_A * HEAD_PAD - ROPE_DIM, 1) * tb8
        q_ref[rs, :] = (q * QK_SCALE).astype(BF16)

        ta_k = jnp.where(lane < NOPE, 0.0, ta)
        kr = misc * ta_k + pltpu.roll(misc, MISC_W - ROPE_DIM, 1) * tb
        kk = _bdot(ckv_b, wk_ref[...]) + jnp.concatenate([kr] * H_A, axis=1)
        k_ref[rs, :] = kk.astype(BF16)
        vt = _bdot(wv_ref[...], ckv_n.T.astype(BF16)).astype(BF16)
        step = min(rows, tkv)
        for off in range(0, rows, step):
            lo = rs.start + off
            v_ref[lo // tkv, :, lo % tkv:lo % tkv + step] = vt[:, off:off + step]

    groups = [slice(r * rows, (r + 1) * rows) for r in range(ROW_SPLITS)]
    hbs = [normed(rs) for rs in groups]
    for hb, rs in zip(hbs, groups):
        small = small_proj(hb)
        big_proj(hb, rs)
        mla_build(*small, rs)


def _stage_a(x, mods, mod_index, pre1, wa, gbias, qn, kvn, wq, wk, wv, ta, tb, table_index, tm, tkv):
    n = x.shape[0]
    tile = lambda w: pl.BlockSpec((tm, w), lambda i: (i, 0))
    bf = lambda w: jax.ShapeDtypeStruct((n, w), BF16)
    f32 = lambda w: jax.ShapeDtypeStruct((n, w), F32)
    return pl.pallas_call(
        _stage_a_kernel,
        grid=(n // tm,),
        in_specs=[tile(D_MODEL),
                  pl.BlockSpec((1, N_MOD, D_MODEL), lambda i: (mod_index(i), 0, 0)),
                  _const_spec((1, D_MODEL)),
                  _const_spec((D_MODEL, W_BIG_COLS)),
                  _const_spec((D_MODEL, 2 * D_MODEL)),
                  _const_spec((D_MODEL, W_SMALL_COLS)),
                  _const_spec((1, MISC_W)),
                  _const_spec((1, Q_LORA)),
                  _const_spec((1, KV_LORA)),
                  _const_spec((Q_LORA, H_A * HEAD_PAD)),
                  _const_spec((KV_LORA, H_A * HEAD_PAD)),
                  _const_spec((H_A * V_DIM, KV_LORA)),
                  pl.BlockSpec((tm, LANE), lambda i: (table_index(i), 0)),
                  pl.BlockSpec((tm, LANE), lambda i: (table_index(i), 0))],
        out_specs=[tile(MLSTM_W), pl.BlockSpec((MLSTM_W, tm), lambda i: (0, i)), tile(MLSTM_W), tile(MLSTM_W),
                   tile(D_MODEL), tile(D_MODEL), tile(MISC_W),
                   tile(H_A * HEAD_PAD), tile(H_A * HEAD_PAD),
                   pl.BlockSpec((tm // tkv, H_A * V_DIM, tkv), lambda i: (i, 0, 0)),
                   tile(KV_LORA), tile(ROPE_DIM)],
        out_shape=[bf(MLSTM_W), jax.ShapeDtypeStruct((MLSTM_W, n), BF16), bf(MLSTM_W), bf(MLSTM_W),
                   bf(D_MODEL), bf(D_MODEL), f32(MISC_W),
                   bf(H_A * HEAD_PAD), bf(H_A * HEAD_PAD),
                   jax.ShapeDtypeStruct((n // tkv, H_A * V_DIM, tkv), BF16),
                   f32(KV_LORA), f32(ROPE_DIM)],
        compiler_params=_cparams(("arbitrary",)),
        name="stage_a",
    )(x, mods, pre1, *wa, gbias, qn, kvn, wq, wk, wv, ta, tb)


def _kv_cache_kernel(ckv_ref, kr_ref, wk_ref, wv_ref, k_ref, v_ref):
    ckv_b = ckv_ref[...].astype(BF16)
    kk = _bdot(ckv_b, wk_ref[...]) + jnp.concatenate([kr_ref[...]] * H_A, axis=1)
    k_ref[...] = kk.astype(BF16)
    v_ref[0] = _bdot(wv_ref[...], ckv_ref[...].T.astype(BF16)).astype(BF16)


def _kv_cache(ckv, kr, wk, wv, tm):
    n = ckv.shape[0]
    return pl.pallas_call(
        _kv_cache_kernel,
        grid=(n // tm,),
        in_specs=[pl.BlockSpec((tm, KV_LORA), lambda i: (i, 0)),
                  pl.BlockSpec((tm, LANE), lambda i: (i, 0)),
                  _const_spec((KV_LORA, H_A * HEAD_PAD)),
                  _const_spec((H_A * V_DIM, KV_LORA))],
        out_specs=[pl.BlockSpec((tm, H_A * HEAD_PAD), lambda i: (i, 0)),
                   pl.BlockSpec((1, H_A * V_DIM, tm), lambda i: (i, 0, 0))],
        out_shape=[jax.ShapeDtypeStruct((n, H_A * HEAD_PAD), BF16),
                   jax.ShapeDtypeStruct((n // tm, H_A * V_DIM, tm), BF16)],
        compiler_params=_cparams(("arbitrary",)),
        name="kv_cache",
    )(ckv, kr, wk, wv)


def _split3(x):
    x1 = x.astype(BF16).astype(F32)
    r = x - x1
    x2 = r.astype(BF16).astype(F32)
    x3 = (r - x2).astype(BF16).astype(F32)
    return x1, x2, x3


GATE_ROWS = 32
MLSTM_SEQS_PER_STEP = 2
MLSTM_CHUNKS_PER_STEP = 4
GATE_CHUNKS_PER_STEP = 32


def _gate_rows_kernel(g_ref, cols_ref, rows_ref):
    nj = 2 * H_M
    nch = g_ref.shape[0] // CHUNK
    n_rows = nch * nj
    row = lax.broadcasted_iota(jnp.int32, (CHUNK, CHUNK), 0)
    col = lax.broadcasted_iota(jnp.int32, (CHUNK, CHUNK), 1)
    lower = (col <= row).astype(F32)
    upper = (col >= row).astype(F32)
    hi = lax.Precision.HIGHEST
    rowi = lax.broadcasted_iota(jnp.int32, (n_rows, CHUNK), 0)
    lane = lax.broadcasted_iota(jnp.int32, (n_rows, CHUNK), 1)
    is_fwd = rowi % nj < H_M

    gi, gf = GATE_LANE0, GATE_LANE0 + nj
    i_rows, f_rows = [], []
    for c in range(nch):
        g_t = g_ref[c * CHUNK:(c + 1) * CHUNK, :].T
        i_rows.append(g_t[gi:gi + nj])
        f_rows.append(g_t[gf:gf + nj])
    i_all = jnp.concatenate(i_rows, axis=0)
    f_all = jnp.concatenate(f_rows, axis=0)
    b = jnp.where(is_fwd, jnp.dot(f_all, upper, preferred_element_type=F32, precision=hi),
                  jnp.dot(f_all, lower, preferred_element_type=F32, precision=hi))
    a = i_all - b
    cm = a
    shift = 1
    while shift < CHUNK:
        y_f = jnp.where(lane >= shift, pltpu.roll(cm, shift, 1), -jnp.inf)
        y_b = jnp.where(lane < CHUNK - shift, pltpu.roll(cm, CHUNK - shift, 1), -jnp.inf)
        cm = jnp.maximum(cm, jnp.where(is_fwd, y_f, y_b))
        shift *= 2
    a_max = jnp.broadcast_to(jnp.max(a, axis=1, keepdims=True), a.shape)
    f_sum = jnp.broadcast_to(jnp.sum(f_all, axis=1, keepdims=True), a.shape)
    zeros8 = jnp.zeros((nj, CHUNK), F32)
    for c in range(nch):
        sl = slice(c * nj, (c + 1) * nj)
        stack = jnp.concatenate([cm[sl], b[sl]] + [zeros8] * (CHUNK // nj - 2), axis=0)
        cols_ref[c * CHUNK:(c + 1) * CHUNK, :] = stack.T
        rows_ref[c] = jnp.concatenate([a[sl], a_max[sl], f_sum[sl], zeros8], axis=0)


def _gate_rows(gates, chunks_per_step):
    n = gates.shape[0]
    tm = chunks_per_step * CHUNK
    return pl.pallas_call(
        _gate_rows_kernel,
        grid=(n // tm,),
        in_specs=[pl.BlockSpec((tm, MISC_W), lambda i: (i, 0))],
        out_specs=[pl.BlockSpec((tm, LANE), lambda i: (i, 0)),
                   pl.BlockSpec((chunks_per_step, GATE_ROWS, CHUNK), lambda i: (i, 0, 0))],
        out_shape=[jax.ShapeDtypeStruct((n, LANE), F32),
                   jax.ShapeDtypeStruct((n // CHUNK, GATE_ROWS, CHUNK), F32)],
        compiler_params=_cparams(("arbitrary",)),
        name="gate_rows",
    )(gates)


def _mlstm_kernel(*refs, has_state, n_seq):
    if has_state:
        (qf_ref, ktf_ref, vf_ref, ptf_ref, rf_ref, qb_ref, ktb_ref, vb_ref, ptb_ref, rb_ref, c0_ref, m0_ref,
         hf_ref, hb_ref, cn_ref, nn_ref, mn_ref, c_s, m_s) = refs
    else:
        (qf_ref, ktf_ref, vf_ref, ptf_ref, rf_ref, qb_ref, ktb_ref, vb_ref, ptb_ref, rb_ref,
         hf_ref, hb_ref, cn_ref, nn_ref, mn_ref, c_s, m_s) = refs
    step = pl.program_id(1)

    nj = 2 * H_M

    @pl.when(step == 0)
    def _():
        if has_state:
            for sq in range(n_seq):
                c_s[sq * nj:(sq + 1) * nj] = c0_ref[sq]
                m_s[sq * nj:(sq + 1) * nj] = m0_ref[sq]
        else:
            c_s[...] = jnp.zeros(c_s.shape, F32)
            m_s[...] = jnp.full(m_s.shape, M_INIT, F32)

    row = lax.broadcasted_iota(jnp.int32, (CHUNK, CHUNK), 0)
    col = lax.broadcasted_iota(jnp.int32, (CHUNK, CHUNK), 1)
    is_fwd = lax.broadcasted_iota(jnp.int32, (nj, CHUNK), 0) < H_M
    ones_v = jnp.ones((CHUNK, DH_M), BF16)
    dirs = ((qf_ref, ktf_ref, vf_ref, ptf_ref, hf_ref), (qb_ref, ktb_ref, vb_ref, ptb_ref, hb_ref))
    chains = [(sq, d, hd) for sq in range(n_seq) for d in range(2) for hd in range(H_M)]
    n_sub = rf_ref.shape[0] // n_seq
    for sub in range(n_sub):
        sub_of = (sub, n_sub - 1 - sub)
        gate = []
        for sq in range(n_seq):
            rows_f, rows_b = rf_ref[sq * n_sub + sub_of[0]], rb_ref[sq * n_sub + sub_of[1]]
            pick = lambda g: jnp.where(is_fwd, rows_f[g * nj:(g + 1) * nj], rows_b[g * nj:(g + 1) * nj])
            a8, a_max8, f_sum8 = pick(0), pick(1), pick(2)
            m8 = m_s[sq * nj:(sq + 1) * nj]
            mx8 = jnp.maximum(m8, a_max8)
            gate.append((a8, m8, jnp.exp(a8 - mx8), jnp.exp(m8 - mx8)))
            m_s[sq * nj:(sq + 1) * nj] = f_sum8 + mx8
        tok = lambda sq, d: slice((sq * n_sub + sub_of[d]) * CHUNK, (sq * n_sub + sub_of[d] + 1) * CHUNK)

        first = []
        for sq, d, hd in chains:
            q_ref, kt_ref, v_ref, _, _ = dirs[d]
            j = d * H_M + hd
            sl = slice(hd * DH_M, (hd + 1) * DH_M)
            q = q_ref[tok(sq, d), sl]
            k_t = kt_ref[sl, tok(sq, d)]
            v_aug = jnp.concatenate([v_ref[tok(sq, d), sl], ones_v], axis=1)
            qk = _bdot(q, k_t)
            c_st = c_s[sq * nj + j]
            qc = _bdot(q, c_st.astype(BF16))
            upd = _bdot((k_t.astype(F32) * gate[sq][2][j:j + 1, :]).astype(BF16), v_aug)
            dec = gate[sq][3][j:j + 1, :]
            c_s[sq * nj + j] = jnp.concatenate([dec, dec], axis=1) * c_st + upd
            first.append((qk, qc, v_aug))
        for (sq, d, hd), (qk, qc, v_aug) in zip(chains, first):
            cols_ref, h_ref = dirs[d][3], dirs[d][4]
            j = d * H_M + hd
            sl = slice(hd * DH_M, (hd + 1) * DH_M)
            mask = (col <= row) if d == 0 else (col >= row)
            a8, m8 = gate[sq][0], gate[sq][1]
            m_row = m8[j:j + 1, :]
            u = jnp.maximum(m_row, cols_ref[tok(sq, d), j:j + 1])
            w = jnp.exp(jnp.where(mask, a8[j:j + 1, :] - u, -jnp.inf))
            s_inter = jnp.exp(m_row - u)
            intra = _bdot((qk * w).astype(BF16), v_aug)
            num = s_inter * qc[:, :DH_M] + intra[:, :DH_M]
            den = s_inter * qc[:, DH_M:] + intra[:, DH_M:]
            h_ref[tok(sq, d), sl] = num / jnp.maximum(
                jnp.abs(den), jnp.exp(-(cols_ref[tok(sq, d), nj + j:nj + j + 1] + u)))

    @pl.when(step == pl.num_programs(1) - 1)
    def _():
        for sq in range(n_seq):
            for j in range(nj):
                c_aug = c_s[sq * nj + j]
                cn_ref[sq, j] = c_aug[:, :DH_M]
                nn_ref[sq, j:j + 1, :] = c_aug[:, DH_M:].T[0:1, :]
            mn_ref[sq] = m_s[sq * nj:(sq + 1) * nj]


def _mlstm(mq, mkt, mv, gate_pt, gate_rows, c0, m0, batch, seq):
    has_state = c0 is not None
    n_sub = min(MLSTM_CHUNKS_PER_STEP, seq // CHUNK)
    nc = seq // (n_sub * CHUNK)
    n_seq = MLSTM_SEQS_PER_STEP if nc == 1 and batch % MLSTM_SEQS_PER_STEP == 0 else 1
    tm = n_seq * n_sub * CHUNK
    n = batch * seq
    nj = 2 * H_M
    fwd = lambda w: pl.BlockSpec((tm, w), lambda b, c: (b * nc + c, 0))
    bwd = lambda w: pl.BlockSpec((tm, w), lambda b, c: (b * nc + nc - 1 - c, 0))
    fwd_t = pl.BlockSpec((MLSTM_W, tm), lambda b, c: (0, b * nc + c))
    bwd_t = pl.BlockSpec((MLSTM_W, tm), lambda b, c: (0, b * nc + nc - 1 - c))
    st_m = pl.BlockSpec((n_seq, nj, LANE), lambda b, c: (b, 0, 0))
    st_c = pl.BlockSpec((n_seq, nj, DH_M, 2 * DH_M), lambda b, c: (b, 0, 0, 0))
    fwd_r = pl.BlockSpec((n_seq * n_sub, GATE_ROWS, CHUNK), lambda b, c: (b * nc + c, 0, 0))
    bwd_r = pl.BlockSpec((n_seq * n_sub, GATE_ROWS, CHUNK), lambda b, c: (b * nc + nc - 1 - c, 0, 0))
    in_specs = [fwd(MLSTM_W), fwd_t, fwd(MLSTM_W), fwd(LANE), fwd_r,
                bwd(MLSTM_W), bwd_t, bwd(MLSTM_W), bwd(LANE), bwd_r]
    args = [mq, mkt, mv, gate_pt, gate_rows, mq, mkt, mv, gate_pt, gate_rows]
    if has_state:
        in_specs += [st_c, st_m]
        args += [c0, m0]
    return pl.pallas_call(
        functools.partial(_mlstm_kernel, has_state=has_state, n_seq=n_seq),
        grid=(batch // n_seq, nc),
        in_specs=in_specs,
        out_specs=[fwd(MLSTM_W), bwd(MLSTM_W),
                   pl.BlockSpec((n_seq, nj, DH_M, DH_M), lambda b, c: (b, 0, 0, 0)), st_m, st_m],
        out_shape=[jax.ShapeDtypeStruct((n, MLSTM_W), F32), jax.ShapeDtypeStruct((n, MLSTM_W), F32),
                   jax.ShapeDtypeStruct((batch, nj, DH_M, DH_M), F32),
                   jax.ShapeDtypeStruct((batch, nj, DH_M), F32),
                   jax.ShapeDtypeStruct((batch, nj, LANE), F32)],
        scratch_shapes=[pltpu.VMEM((n_seq * nj, DH_M, 2 * DH_M), F32), pltpu.VMEM((n_seq * nj, LANE), F32)],
        compiler_params=_cparams(("arbitrary", "arbitrary")),
        name="mlstm",
    )(*args)


def _attn_kernel(*refs, tiles_per_step, has_cache, n_seq, shared_keys):
    if has_cache:
        q_ref, k_ref, vt_ref, kc_ref, vct_ref, o_ref, qt_s, acc_s, m_s, ot_s, s_scr = refs
    else:
        q_ref, k_ref, vt_ref, o_ref, qt_s, acc_s, m_s, ot_s, s_scr = refs
    tk = vt_ref.shape[2]
    n_tiles = vt_ref.shape[0] // (1 if shared_keys else n_seq)
    tq = q_ref.shape[0] // n_seq
    d_q, d_o = H_A * HEAD_PAD, H_A * V_DIM
    for sq in range(n_seq):
        qt_s[sq * d_q:(sq + 1) * d_q, :] = q_ref[sq * tq:(sq + 1) * tq, :].astype(F32).T.astype(BF16)
    acc_s[...] = jnp.zeros(acc_s.shape, F32)
    m_s[...] = jnp.full(m_s.shape, -jnp.inf, F32)

    n_slots = s_scr.shape[0]

    def scores(tile, head, slot):
        get_k, _, width, sq = tile
        stream = sq * H_A + head
        s = _bdot(get_k(head), qt_s[stream * HEAD_PAD:(stream + 1) * HEAD_PAD, :])
        s_scr[slot, 0:width, :] = s
        return jnp.max(s, axis=0, keepdims=True)

    def update(tiles, pending, next_tile):
        items = [(tile, head) for tile in tiles for head in range(H_A)]
        ahead = items + ([(next_tile, h) for h in range(QK_LOOKAHEAD)] if next_tile is not None else [])
        pending = list(pending)
        for idx, ((_, get_vt, width, sq), head) in enumerate(items):
            stream = sq * H_A + head
            m_tile = pending.pop(0)
            if idx + QK_LOOKAHEAD < len(ahead):
                pending.append(scores(*ahead[idx + QK_LOOKAHEAD], (idx + QK_LOOKAHEAD) % n_slots))
            m_old = m_s[stream:stream + 1, :]
            m_new = jnp.maximum(m_old, m_tile)
            alpha = jnp.exp2(m_old - m_new)
            p = jnp.exp2((s_scr[idx % n_slots, 0:width, :] - m_new).astype(BF16))
            ones = jnp.ones((ONES_ROWS, width), BF16)
            lhs = jnp.concatenate([get_vt(head), ones], axis=0)
            acc_s[stream] = alpha * acc_s[stream] + _bdot(lhs, p)
            m_s[stream:stream + 1, :] = m_new
        return tuple(pending)

    def main_tile(t, sq=0):
        base = 0 if shared_keys else sq * n_tiles
        start = (base + t) * tk if isinstance(t, int) else pl.multiple_of(t * tk, tk)
        rows = pl.ds(start, tk)
        return (lambda h: k_ref[rows, h * HEAD_PAD:(h + 1) * HEAD_PAD],
                lambda h: vt_ref[base + t, h * V_DIM:(h + 1) * V_DIM, :], tk, sq)

    def body(sq, i, pending):
        first = i * tiles_per_step
        return update([main_tile(first + u, sq) for u in range(tiles_per_step)], pending,
                      main_tile(first + tiles_per_step, sq))

    n_loop = (n_tiles - 1) // tiles_per_step

    def tail_tiles(sq):
        tiles = [main_tile(t, sq) for t in range(n_loop * tiles_per_step, n_tiles)]
        if has_cache:
            tiles.append((lambda h: kc_ref[:, h * HEAD_PAD:(h + 1) * HEAD_PAD],
                          lambda h: vct_ref[0, h * V_DIM:(h + 1) * V_DIM, :], kc_ref.shape[0], sq))
        return tiles

    pending = tuple(scores(main_tile(0, 0), h, h) for h in range(QK_LOOKAHEAD))
    if n_loop:
        assert (tiles_per_step * H_A) % n_slots == 0 and (len(tail_tiles(0)) * H_A) % n_slots == 0
        for sq in range(n_seq):
            pending = lax.fori_loop(0, n_loop, functools.partial(body, sq), pending)
            pending = update(tail_tiles(sq), pending, main_tile(0, sq + 1) if sq + 1 < n_seq else None)
    else:
        update([tile for sq in range(n_seq) for tile in tail_tiles(sq)], pending, None)
    for stream in range(n_seq * H_A):
        acc = acc_s[stream]
        ot_s[stream * V_DIM:(stream + 1) * V_DIM, :] = acc[0:V_DIM] * (1.0 / acc[V_DIM:V_DIM + 1])
    for sq in range(n_seq):
        o_ref[sq * tq:(sq + 1) * tq, :] = ot_s[sq * d_o:(sq + 1) * d_o, :].T.astype(BF16)


def _attention(q, k, vt, kc, vct, batch, seq, tq, tiles_per_step, n_seq=1):
    nq = seq // tq
    tk = vt.shape[2]
    n_tiles = seq // tk
    has_cache = kc is not None
    shared_keys = n_seq > 1 and nq > 1
    if shared_keys:
        assert nq % n_seq == 0
        nq, key_seqs, grid = nq // n_seq, 1, (batch, nq // n_seq)
    else:
        assert n_seq == 1 or (nq == 1 and n_tiles == 1 and not has_cache and batch % n_seq == 0)
        key_seqs, grid = n_seq, (batch // n_seq, nq)
    max_width = max(tk, kc.shape[0] // batch) if has_cache else tk
    in_specs = [pl.BlockSpec((n_seq * tq, H_A * HEAD_PAD), lambda b, i: (b * nq + i, 0)),
                pl.BlockSpec((key_seqs * seq, H_A * HEAD_PAD), lambda b, i: (b, 0)),
                pl.BlockSpec((key_seqs * n_tiles, H_A * V_DIM, tk), lambda b, i: (b, 0, 0))]
    args = [q, k, vt]
    if has_cache:
        past = kc.shape[0] // batch
        in_specs += [pl.BlockSpec((past, H_A * HEAD_PAD), lambda b, i: (b, 0)),
                     pl.BlockSpec((1, H_A * V_DIM, past), lambda b, i: (b, 0, 0))]
        args += [kc, vct]
    return pl.pallas_call(
        functools.partial(_attn_kernel, tiles_per_step=tiles_per_step, has_cache=has_cache, n_seq=n_seq,
                          shared_keys=shared_keys),
        grid=grid,
        in_specs=in_specs,
        out_specs=pl.BlockSpec((n_seq * tq, H_A * V_DIM), lambda b, i: (b * nq + i, 0)),
        out_shape=jax.ShapeDtypeStruct((batch * seq, H_A * V_DIM), BF16),
        scratch_shapes=[pltpu.VMEM((n_seq * H_A * HEAD_PAD, tq), BF16),
                        pltpu.VMEM((n_seq * H_A, V_DIM + ONES_ROWS, tq), F32),
                        pltpu.VMEM((n_seq * H_A, tq), F32),
                        pltpu.VMEM((n_seq * H_A * V_DIM, tq), F32),
                        pltpu.VMEM((QK_LOOKAHEAD + 1, max_width, tq), F32)],
        compiler_params=_cparams(("arbitrary", "arbitrary")),
        name="mla_attention",
    )(*args)


def _stage_c_kernel(x_ref, mod_ref, hf_ref, hb_ref, og_ref, attn_ref, ga_ref, gb_ref, hn_ref,
                    post1_ref, pre2_ref, post2_ref, wmo_ref, wao_ref, wout_ref, w1_ref, w2_ref, y_ref):
    mod = mod_ref[0]
    gate1, shift2, scale2, gate2 = mod[2:3], mod[3:4], mod[4:5], mod[5:6]
    hn = hn_ref[...]
    rows = x_ref.shape[0] // ROW_SPLITS
    groups = [slice(r * rows, (r + 1) * rows) for r in range(ROW_SPLITS)]
    n_ff = D_FF // D_MODEL

    def mixer_in(rs):
        hm = hf_ref[rs, :] + hb_ref[rs, :]
        heads = [_rms(hm[:, hd * DH_M:(hd + 1) * DH_M], hn[:, hd * DH_M:(hd + 1) * DH_M]) for hd in range(H_M)]
        return (jnp.concatenate(heads, axis=1) * og_ref[rs, :].astype(F32)).astype(BF16)

    def branch_out(hm, rs):
        return _bdot(hm, wmo_ref[...]), _bdot(attn_ref[rs, :], wao_ref[...])

    def merge(y_m, y_a, rs):
        return (ga_ref[rs, :].astype(F32) * y_m + gb_ref[rs, :].astype(F32) * y_a).astype(BF16)

    def mid(mix, rs):
        x1 = x_ref[rs, :] + gate1 * _rms(mix, post1_ref[...])
        return x1, (_rms(x1, pre2_ref[...]) * (1.0 + scale2) + shift2).astype(BF16)

    def mlp(h2):
        up = lambda c: _bdot(h2, w1_ref[:, c * D_MODEL:(c + 1) * D_MODEL])
        ff = jnp.zeros((rows, D_MODEL), F32)
        nxt = up(0)
        for c in range(n_ff):
            a = jnp.maximum(nxt, 0.0)
            if c + 1 < n_ff:
                nxt = up(c + 1)
            ff = ff + _bdot((a * a).astype(BF16), w2_ref[c * D_MODEL:(c + 1) * D_MODEL, :])
        return ff

    hms = [mixer_in(rs) for rs in groups]
    ys = [branch_out(hm, rs) for hm, rs in zip(hms, groups)]
    mixes = [_bdot(merge(*y, rs), wout_ref[...]) for y, rs in zip(ys, groups)]
    mids = [mid(mix, rs) for mix, rs in zip(mixes, groups)]
    ffs = [mlp(h2) for _, h2 in mids]
    for (x1, _), ff, rs in zip(mids, ffs, groups):
        y_ref[rs, :] = x1 + gate2 * _rms(ff, post2_ref[...])


def _stage_c(x, mods, mod_index, hf, hb, og, attn, ga, gb, hn, post1, pre2, post2, wmo, wao, wout, w1, w2, tm):
    n = x.shape[0]
    tile = lambda w: pl.BlockSpec((tm, w), lambda i: (i, 0))
    return pl.pallas_call(
        _stage_c_kernel,
        grid=(n // tm,),
        in_specs=[tile(D_MODEL),
                  pl.BlockSpec((1, N_MOD, D_MODEL), lambda i: (mod_index(i), 0, 0)),
                  tile(MLSTM_W), tile(MLSTM_W), tile(MLSTM_W), tile(H_A * V_DIM),
                  tile(D_MODEL), tile(D_MODEL),
                  _const_spec((1, MLSTM_W)), _const_spec((1, D_MODEL)), _const_spec((1, D_MODEL)),
                  _const_spec((1, D_MODEL)),
                  _const_spec((MLSTM_W, D_MODEL)), _const_spec((H_A * V_DIM, D_MODEL)),
                  _const_spec((D_MODEL, D_MODEL)), _const_spec((D_MODEL, D_FF)), _const_spec((D_FF, D_MODEL))],
        out_specs=tile(D_MODEL),
        out_shape=jax.ShapeDtypeStruct((n, D_MODEL), F32),
        compiler_params=_cparams(("arbitrary",)),
        name="stage_c",
    )(x, mods, hf, hb, og, attn, ga, gb, hn, post1, pre2, post2, wmo, wao, wout, w1, w2)


def _rope_tables(n_tokens):
    pos = np.arange(n_tokens)
    row = (pos // GRID_W).astype(np.float32)
    col = (pos % GRID_W).astype(np.float32)
    inv = (ROPE_BASE ** (-np.arange(0, AX_DIM, 2, dtype=np.float32) / AX_DIM)).astype(np.float32)
    ang = np.concatenate([row[:, None] * inv, col[:, None] * inv], axis=-1)
    cos = np.cos(ang.astype(np.float64)).astype(np.float32)
    sin = np.sin(ang.astype(np.float64)).astype(np.float32)
    ones = np.ones((n_tokens, NOPE), np.float32)
    zeros = np.zeros((n_tokens, NOPE), np.float32)
    pad = np.zeros((n_tokens, HEAD_PAD - NOPE - ROPE_DIM), np.float32)
    ta = np.concatenate([ones, cos, cos, pad], axis=-1)
    tb = np.concatenate([zeros, -sin, sin, pad], axis=-1)
    return jnp.asarray(ta), jnp.asarray(tb)


def _plain_tables(n_tokens):
    ones = jnp.ones((n_tokens, NOPE + ROPE_DIM), F32)
    pad = jnp.zeros((n_tokens, HEAD_PAD - NOPE - ROPE_DIM), F32)
    return jnp.concatenate([ones, pad], axis=-1), jnp.zeros((n_tokens, HEAD_PAD), F32)


def kernel(x_prompt, x_sample, cache_mla_ckv, cache_mla_krope, state_mlstm_C, state_mlstm_n, state_mlstm_m,
           c, c_ctx, w_ada, b_ada, norm_pre1, norm_post1, norm_pre2, norm_post2, w_in, mlstm_gate_b,
           mla_q_norm, mla_kv_norm, w_uq, w_ukv, w_mla_o, mlstm_head_norm, w_mlstm_o, w_out, w_mlp1, w_mlp2):
    bp, sp, _ = x_prompt.shape
    bs, ss, _ = x_sample.shape
    depth = w_in.shape[0]
    past = cache_mla_ckv.shape[2]
    nj = 2 * H_M
    even = np.arange(0, ROPE_DIM, 2)
    odd = np.arange(1, ROPE_DIM, 2)
    perm = np.concatenate([even, odd])
    perm_sw = np.concatenate([odd, even])

    xp = x_prompt.reshape(bp * sp, D_MODEL)
    xs = x_sample.reshape(bs * ss, D_MODEL)
    cc = jnp.zeros((8, D_MODEL), F32).at[:bs].set(c).at[bs].set(c_ctx)
    ta_lat, tb_lat = _rope_tables(ss)
    tm_ctx, tm_lat = 512, 512
    assert past == tm_lat
    ta_ctx, tb_ctx = _plain_tables(tm_ctx)

    new_ckv, new_krope, new_c, new_n, new_m = [], [], [], [], []
    for l in range(depth):
        cols = np.cumsum((W_BIG_COLS, N_GATES, Q_LORA, KV_LORA, ROPE_DIM))
        w_big, w_g, w_cq, w_ckv, w_kr, w_merge = jnp.split(w_in[l].astype(BF16), cols.tolist(), axis=1)
        w_g = w_g.reshape(D_MODEL, 2, 2, H_M).transpose(0, 2, 1, 3).reshape(D_MODEL, N_GATES)
        w_misc = jnp.concatenate([w_kr, w_g, jnp.zeros((D_MODEL, MISC_W - 3 * ROPE_DIM - N_GATES), BF16),
                                  w_kr[:, perm], w_kr[:, perm_sw]], axis=1)
        wa = (w_big, w_merge, jnp.concatenate([w_cq, w_ckv, w_misc], axis=1))
        gbias = jnp.zeros((1, MISC_W), F32).at[0, GATE_LANE0:GATE_LANE0 + N_GATES].set(
            mlstm_gate_b[l].transpose(1, 0, 2).reshape(N_GATES))
        uq = w_uq[l]
        wq = jnp.concatenate([uq[..., :NOPE], uq[..., NOPE:][..., perm], uq[..., NOPE:][..., perm_sw]],
                             axis=-1).reshape(Q_LORA, H_A * HEAD_PAD).astype(BF16)
        ukv = w_ukv[l]
        wk = jnp.concatenate([ukv[..., :NOPE], jnp.zeros((KV_LORA, H_A, HEAD_PAD - NOPE), F32)],
                             axis=-1).reshape(KV_LORA, H_A * HEAD_PAD).astype(BF16)
        wv = ukv[..., NOPE:].reshape(KV_LORA, H_A * V_DIM).T.astype(BF16)
        wmo = w_mlstm_o[l].astype(BF16)
        wao = w_mla_o[l].astype(BF16)
        wout = w_out[l].astype(BF16)
        w1 = w_mlp1[l].astype(BF16)
        w2 = w_mlp2[l].astype(BF16)
        pre1, post1 = norm_pre1[l][None], norm_post1[l][None]
        pre2, post2 = norm_pre2[l][None], norm_post2[l][None]
        qn, kvn = mla_q_norm[l][None], mla_kv_norm[l][None]
        hn = mlstm_head_norm[l].reshape(1, MLSTM_W)

        mods = _modulation(cc, w_ada[l], b_ada[l][None]).reshape(8, N_MOD, D_MODEL)

        ctx_mod = lambda i: bs
        a = _stage_a(xp, mods, ctx_mod, pre1, wa, gbias, qn, kvn, wq, wk, wv, ta_ctx, tb_ctx,
                     lambda i: 0, tm_ctx, sp)
        mq, mk, mv, og, ga, gb, gates, q, k, v, ckv_n, kro = a
        hf, hb, c_fin, n_fin, m_fin = _mlstm(mq, mk, mv, *_gate_rows(gates, GATE_CHUNKS_PER_STEP),
                                             None, None, bp, sp)
        attn = _attention(q, k, v, None, None, bp, sp, sp, 1, n_seq=ATTN_SEQS_PER_STEP)
        xp = _stage_c(xp, mods, ctx_mod, hf, hb, og, attn, ga, gb, hn, post1, pre2, post2,
                      wmo, wao, wout, w1, w2, tm_ctx)
        new_ckv.append(ckv_n.reshape(bp, sp, KV_LORA))
        new_krope.append(kro.reshape(bp, sp, ROPE_DIM))
        new_c.append(c_fin.reshape(bp, 2, H_M, DH_M, DH_M))
        new_n.append(n_fin.reshape(bp, 2, H_M, DH_M))
        new_m.append(m_fin[:, :, 0].reshape(bp, 2, H_M))

        tiles_per_seq = ss // tm_lat
        lat_mod = lambda i: i // tiles_per_seq
        a = _stage_a(xs, mods, lat_mod, pre1, wa, gbias, qn, kvn, wq, wk, wv, ta_lat, tb_lat,
                     lambda i: i % tiles_per_seq, tm_lat, ATTN_KEY_TILE)
        mq, mk, mv, og, ga, gb, gates, q, k, v, _, _ = a
        kr_cache = jnp.zeros((bs * past, LANE), F32).at[:, NOPE:NOPE + ROPE_DIM].set(
            cache_mla_krope[:, l].reshape(bs * past, ROPE_DIM)[:, perm])
        kc, vc = _kv_cache(cache_mla_ckv[:, l].reshape(bs * past, KV_LORA), kr_cache, wk, wv, past)
        n0 = state_mlstm_n[:, l].reshape(bs, nj, DH_M, 1)
        c0 = jnp.concatenate([state_mlstm_C[:, l].reshape(bs, nj, DH_M, DH_M),
                              jnp.broadcast_to(n0, (bs, nj, DH_M, DH_M))], axis=-1)
        m0 = jnp.broadcast_to(state_mlstm_m[:, l].reshape(bs, nj, 1), (bs, nj, LANE))
        hf, hb, _, _, _ = _mlstm(mq, mk, mv, *_gate_rows(gates, GATE_CHUNKS_PER_STEP), c0, m0, bs, ss)
        attn = _attention(q, k, v, kc, vc, bs, ss, ATTN_QUERY_TILE, KEY_TILES_PER_STEP,
                          n_seq=ATTN_QUERY_TILES_PER_STEP)
        xs = _stage_c(xs, mods, lat_mod, hf, hb, og, attn, ga, gb, hn, post1, pre2, post2,
                      wmo, wao, wout, w1, w2, tm_lat)

    return (xp.reshape(bp, sp, D_MODEL), xs.reshape(bs, ss, D_MODEL),
            jnp.stack(new_ckv, axis=1), jnp.stack(new_krope, axis=1), jnp.stack(new_c, axis=1),
            jnp.stack(new_n, axis=1), jnp.stack(new_m, axis=1))
```

```python
import functools

import numpy as np
import jax
import jax.numpy as jnp
from jax import lax
from jax.experimental import pallas as pl
from jax.experimental.pallas import tpu as pltpu

F32 = jnp.float32
BF16 = jnp.bfloat16

D_MODEL = 1024
H_M = 4
DH_M = 128
MLSTM_W = H_M * DH_M
CHUNK = 128
H_A = 8
NOPE = 64
ROPE_DIM = 32
V_DIM = 64
Q_LORA = 384
KV_LORA = 256
AX_DIM = ROPE_DIM // 2
ROPE_BASE = 10000.0
GRID_W = 64
D_FF = 4 * D_MODEL
N_MOD = 6
EPS = 1e-6
N_GATES = 4 * H_M
M_INIT = -1e30

LANE = 128
HEAD_PAD = LANE
MISC_W = LANE
GATE_LANE0 = ROPE_DIM
OFF_MQ, OFF_MK, OFF_MV, OFF_MO, W_BIG_COLS = 0, MLSTM_W, 2 * MLSTM_W, 3 * MLSTM_W, 4 * MLSTM_W
OFF_CQ, OFF_CKV, OFF_MISC = 0, Q_LORA, Q_LORA + KV_LORA
W_SMALL_COLS = OFF_MISC + MISC_W

VMEM_LIMIT = 60 * 1024 * 1024
QK_SCALE = float((NOPE + ROPE_DIM) ** -0.5 * np.log2(np.e))
QK_LOOKAHEAD = 3
ATTN_KEY_TILE = 512
ATTN_SEQS_PER_STEP = 8
ATTN_QUERY_TILES_PER_STEP = 2
ATTN_QUERY_TILE = 256
ROW_GROUP = 256
TM_A, TM_C = 512, 512
KEY_TILES_PER_STEP = 3
ONES_ROWS = 16


def _cparams(sem):
    return pltpu.CompilerParams(dimension_semantics=sem, vmem_limit_bytes=VMEM_LIMIT)


def _const_spec(shape):
    nd = len(shape)
    return pl.BlockSpec(shape, lambda *_: (0,) * nd, pipeline_mode=pl.Buffered(1))


def _rms(x, w):
    return x * lax.rsqrt(jnp.mean(x * x, axis=-1, keepdims=True) + EPS) * w


def _sigmoid(x):
    return 1.0 / (1.0 + jnp.exp(-x))


def _log_sigmoid(x):
    return jnp.minimum(x, 0.0) - jnp.log(1.0 + jnp.exp(-jnp.abs(x)))


def _bdot(a, b):
    return jnp.dot(a, b, preferred_element_type=F32)


def _mod_kernel(c_ref, w_ref, b_ref, o_ref):
    c = c_ref[...]
    s = c * _sigmoid(c)
    o_ref[...] = jnp.dot(s, w_ref[...], preferred_element_type=F32,
                         precision=lax.Precision.HIGHEST) + b_ref[...]


def _modulation(cc, w_ada, b_ada):
    n_out = w_ada.shape[1]
    tn = 1536
    return pl.pallas_call(
        _mod_kernel,
        grid=(n_out // tn,),
        in_specs=[pl.BlockSpec((8, D_MODEL), lambda j: (0, 0)),
                  pl.BlockSpec((D_MODEL, tn), lambda j: (0, j)),
                  pl.BlockSpec((1, tn), lambda j: (0, j))],
        out_specs=pl.BlockSpec((8, tn), lambda j: (0, j)),
        out_shape=jax.ShapeDtypeStruct((8, n_out), F32),
        compiler_params=_cparams(("arbitrary",)),
        name="modulation",
    )(cc, w_ada, b_ada)


def _stage_a_kernel(x_ref, mod_ref, pre1_ref, wbig_ref, wmerge_ref, wsmall_ref, gbias_ref, qn_ref, kvn_ref,
                    wq_ref, wk_ref, wv_ref,
                    ta_ref, tb_ref,
                    mq_ref, mk_ref, mv_ref, og_ref, ga_ref, gb_ref, gates_ref, q_ref, k_ref, v_ref,
                    ckv_ref, kro_ref):
    mod = mod_ref[0]
    shift1, scale1 = mod[0:1], mod[1:2]
    tm = x_ref.shape[0]
    rows = ROW_GROUP
    tkv = v_ref.shape[2]

    def normed(rs):
        return (_rms(x_ref[rs, :], pre1_ref[...]) * (1.0 + scale1) + shift1).astype(BF16)

    def small_proj(hb):
        return tuple(_bdot(hb, wsmall_ref[:, lo:hi])
                     for lo, hi in ((OFF_CQ, OFF_CKV), (OFF_CKV, OFF_MISC), (OFF_MISC, W_SMALL_COLS)))

    def big_proj(hb, rs):
        proj = lambda lo, hi: _bdot(hb, wbig_ref[:, lo:hi])
        mq_ref[rs, :] = (proj(OFF_MQ, OFF_MK) * DH_M ** -0.5).astype(BF16)
        mk_ref[:, rs] = proj(OFF_MK, OFF_MV).T.astype(BF16)
        mv_ref[rs, :] = proj(OFF_MV, OFF_MO).astype(BF16)
        og_ref[rs, :] = _sigmoid(proj(OFF_MO, W_BIG_COLS)).astype(BF16)
        ga_ref[rs, :] = _sigmoid(_bdot(hb, wmerge_ref[:, :D_MODEL])).astype(BF16)
        gb_ref[rs, :] = _sigmoid(_bdot(hb, wmerge_ref[:, D_MODEL:])).astype(BF16)

    def mla_build(cq, ckv, misc, rs):
        lane = lax.broadcasted_iota(jnp.int32, misc.shape, 1)
        g = misc + gbias_ref[...]
        is_f = (lane >= GATE_LANE0 + 2 * H_M) & (lane < GATE_LANE0 + N_GATES)
        gates_ref[rs, :] = jnp.where(is_f, _log_sigmoid(g), g)
        kro_ref[rs, :] = misc[:, 0:ROPE_DIM]

        ckv_n = _rms(ckv, kvn_ref[...])
        ckv_ref[rs, :] = ckv_n
        ckv_b = ckv_n.astype(BF16)

        ta = ta_ref[rs, :]
        tb = tb_ref[rs, :]
        qf = _bdot(_rms(cq, qn_ref[...]).astype(BF16), wq_ref[...])
        ta8 = jnp.concatenate([ta] * H_A, axis=1)
        tb8 = jnp.concatenate([tb] * H_A, axis=1)
        q = qf * ta8 + pltpu.roll(qf, H_A * HEAD_PAD - ROPE_DIM, 1) * tb8
        q_ref[rs, :] = (q * QK_SCALE).astype(BF16)

        ta_k = jnp.where(lane < NOPE, 0.0, ta)
        kr = misc * ta_k + pltpu.roll(misc, MISC_W - ROPE_DIM, 1) * tb
        kk = _bdot(ckv_b, wk_ref[...]) + jnp.concatenate([kr] * H_A, axis=1)
        k_ref[rs, :] = kk.astype(BF16)
        vt = _bdot(wv_ref[...], ckv_n.T.astype(BF16)).astype(BF16)
        step = min(rows, tkv)
        for off in range(0, rows, step):
            lo = rs.start + off
            v_ref[lo // tkv, :, lo % tkv:lo % tkv + step] = vt[:, off:off + step]

    groups = [slice(r * rows, (r + 1) * rows) for r in range(tm // rows)]
    hbs = [normed(rs) for rs in groups]
    for hb, rs in zip(hbs, groups):
        small = small_proj(hb)
        big_proj(hb, rs)
        mla_build(*small, rs)


def _stage_a(x, mods, mod_index, pre1, wa, gbias, qn, kvn, wq, wk, wv, ta, tb, table_index, tm, tkv):
    n = x.shape[0]
    tile = lambda w: pl.BlockSpec((tm, w), lambda i: (i, 0))
    bf = lambda w: jax.ShapeDtypeStruct((n, w), BF16)
    f32 = lambda w: jax.ShapeDtypeStruct((n, w), F32)
    return pl.pallas_call(
        _stage_a_kernel,
        grid=(n // tm,),
        in_specs=[tile(D_MODEL),
                  pl.BlockSpec((1, N_MOD, D_MODEL), lambda i: (mod_index(i), 0, 0)),
                  _const_spec((1, D_MODEL)),
                  _const_spec((D_MODEL, W_BIG_COLS)),
                  _const_spec((D_MODEL, 2 * D_MODEL)),
                  _const_spec((D_MODEL, W_SMALL_COLS)),
                  _const_spec((1, MISC_W)),
                  _const_spec((1, Q_LORA)),
                  _const_spec((1, KV_LORA)),
                  _const_spec((Q_LORA, H_A * HEAD_PAD)),
                  _const_spec((KV_LORA, H_A * HEAD_PAD)),
                  _const_spec((H_A * V_DIM, KV_LORA)),
                  pl.BlockSpec((tm, LANE), lambda i: (table_index(i), 0)),
                  pl.BlockSpec((tm, LANE), lambda i: (table_index(i), 0))],
        out_specs=[tile(MLSTM_W), pl.BlockSpec((MLSTM_W, tm), lambda i: (0, i)), tile(MLSTM_W), tile(MLSTM_W),
                   tile(D_MODEL), tile(D_MODEL), tile(MISC_W),
                   tile(H_A * HEAD_PAD), tile(H_A * HEAD_PAD),
                   pl.BlockSpec((tm // tkv, H_A * V_DIM, tkv), lambda i: (i, 0, 0)),
                   tile(KV_LORA), tile(ROPE_DIM)],
        out_shape=[bf(MLSTM_W), jax.ShapeDtypeStruct((MLSTM_W, n), BF16), bf(MLSTM_W), bf(MLSTM_W),
                   bf(D_MODEL), bf(D_MODEL), f32(MISC_W),
                   bf(H_A * HEAD_PAD), bf(H_A * HEAD_PAD),
                   jax.ShapeDtypeStruct((n // tkv, H_A * V_DIM, tkv), BF16),
                   f32(KV_LORA), f32(ROPE_DIM)],
        compiler_params=_cparams(("arbitrary",)),
        name="stage_a",
    )(x, mods, pre1, *wa, gbias, qn, kvn, wq, wk, wv, ta, tb)


def _kv_cache_kernel(ckv_ref, kr_ref, wk_ref, wv_ref, k_ref, v_ref):
    ckv_b = ckv_ref[...].astype(BF16)
    kk = _bdot(ckv_b, wk_ref[...]) + jnp.concatenate([kr_ref[...]] * H_A, axis=1)
    k_ref[...] = kk.astype(BF16)
    v_ref[0] = _bdot(wv_ref[...], ckv_ref[...].T.astype(BF16)).astype(BF16)


def _kv_cache(ckv, kr, wk, wv, tm):
    n = ckv.shape[0]
    return pl.pallas_call(
        _kv_cache_kernel,
        grid=(n // tm,),
        in_specs=[pl.BlockSpec((tm, KV_LORA), lambda i: (i, 0)),
                  pl.BlockSpec((tm, LANE), lambda i: (i, 0)),
                  _const_spec((KV_LORA, H_A * HEAD_PAD)),
                  _const_spec((H_A * V_DIM, KV_LORA))],
        out_specs=[pl.BlockSpec((tm, H_A * HEAD_PAD), lambda i: (i, 0)),
                   pl.BlockSpec((1, H_A * V_DIM, tm), lambda i: (i, 0, 0))],
        out_shape=[jax.ShapeDtypeStruct((n, H_A * HEAD_PAD), BF16),
                   jax.ShapeDtypeStruct((n // tm, H_A * V_DIM, tm), BF16)],
        compiler_params=_cparams(("arbitrary",)),
        name="kv_cache",
    )(ckv, kr, wk, wv)


def _split3(x):
    x1 = x.astype(BF16).astype(F32)
    r = x - x1
    x2 = r.astype(BF16).astype(F32)
    x3 = (r - x2).astype(BF16).astype(F32)
    return x1, x2, x3


GATE_ROWS = 32
MLSTM_SEQS_PER_STEP = 4
MLSTM_CHUNKS_PER_STEP = 4
GATE_CHUNKS_PER_STEP = 32


def _gate_rows_kernel(g_ref, cols_ref, rows_ref):
    nj = 2 * H_M
    nch = g_ref.shape[0] // CHUNK
    n_rows = nch * nj
    row = lax.broadcasted_iota(jnp.int32, (CHUNK, CHUNK), 0)
    col = lax.broadcasted_iota(jnp.int32, (CHUNK, CHUNK), 1)
    lower = (col <= row).astype(F32)
    upper = (col >= row).astype(F32)
    hi = lax.Precision.HIGHEST
    rowi = lax.broadcasted_iota(jnp.int32, (n_rows, CHUNK), 0)
    lane = lax.broadcasted_iota(jnp.int32, (n_rows, CHUNK), 1)
    is_fwd = rowi % nj < H_M

    gi, gf = GATE_LANE0, GATE_LANE0 + nj
    i_rows, f_rows = [], []
    for c in range(nch):
        g_t = g_ref[c * CHUNK:(c + 1) * CHUNK, :].T
        i_rows.append(g_t[gi:gi + nj])
        f_rows.append(g_t[gf:gf + nj])
    i_all = jnp.concatenate(i_rows, axis=0)
    f_all = jnp.concatenate(f_rows, axis=0)
    b = jnp.where(is_fwd, jnp.dot(f_all, upper, preferred_element_type=F32, precision=hi),
                  jnp.dot(f_all, lower, preferred_element_type=F32, precision=hi))
    a = i_all - b
    cm = a
    shift = 1
    while shift < CHUNK:
        y_f = jnp.where(lane >= shift, pltpu.roll(cm, shift, 1), -jnp.inf)
        y_b = jnp.where(lane < CHUNK - shift, pltpu.roll(cm, CHUNK - shift, 1), -jnp.inf)
        cm = jnp.maximum(cm, jnp.where(is_fwd, y_f, y_b))
        shift *= 2
    a_max = jnp.broadcast_to(jnp.max(a, axis=1, keepdims=True), a.shape)
    f_sum = jnp.broadcast_to(jnp.sum(f_all, axis=1, keepdims=True), a.shape)
    zeros8 = jnp.zeros((nj, CHUNK), F32)
    for c in range(nch):
        sl = slice(c * nj, (c + 1) * nj)
        stack = jnp.concatenate([cm[sl], b[sl]] + [zeros8] * (CHUNK // nj - 2), axis=0)
        cols_ref[c * CHUNK:(c + 1) * CHUNK, :] = stack.T
        rows_ref[c] = jnp.concatenate([a[sl], a_max[sl], f_sum[sl], zeros8], axis=0)


def _gate_rows(gates, chunks_per_step):
    n = gates.shape[0]
    tm = chunks_per_step * CHUNK
    return pl.pallas_call(
        _gate_rows_kernel,
        grid=(n // tm,),
        in_specs=[pl.BlockSpec((tm, MISC_W), lambda i: (i, 0))],
        out_specs=[pl.BlockSpec((tm, LANE), lambda i: (i, 0)),
                   pl.BlockSpec((chunks_per_step, GATE_ROWS, CHUNK), lambda i: (i, 0, 0))],
        out_shape=[jax.ShapeDtypeStruct((n, LANE), F32),
                   jax.ShapeDtypeStruct((n // CHUNK, GATE_ROWS, CHUNK), F32)],
        compiler_params=_cparams(("arbitrary",)),
        name="gate_rows",
    )(gates)


def _mlstm_kernel(*refs, has_state, n_seq):
    if has_state:
        (qf_ref, ktf_ref, vf_ref, ptf_ref, rf_ref, qb_ref, ktb_ref, vb_ref, ptb_ref, rb_ref, c0_ref, m0_ref,
         hf_ref, hb_ref, cn_ref, nn_ref, mn_ref, c_s, m_s) = refs
    else:
        (qf_ref, ktf_ref, vf_ref, ptf_ref, rf_ref, qb_ref, ktb_ref, vb_ref, ptb_ref, rb_ref,
         hf_ref, hb_ref, cn_ref, nn_ref, mn_ref, c_s, m_s) = refs
    step = pl.program_id(1)

    nj = 2 * H_M

    @pl.when(step == 0)
    def _():
        if has_state:
            for sq in range(n_seq):
                c_s[sq * nj:(sq + 1) * nj] = c0_ref[sq]
                m_s[sq * nj:(sq + 1) * nj] = m0_ref[sq]
        else:
            c_s[...] = jnp.zeros(c_s.shape, F32)
            m_s[...] = jnp.full(m_s.shape, M_INIT, F32)

    row = lax.broadcasted_iota(jnp.int32, (CHUNK, CHUNK), 0)
    col = lax.broadcasted_iota(jnp.int32, (CHUNK, CHUNK), 1)
    is_fwd = lax.broadcasted_iota(jnp.int32, (nj, CHUNK), 0) < H_M
    ones_v = jnp.ones((CHUNK, DH_M), BF16)
    dirs = ((qf_ref, ktf_ref, vf_ref, ptf_ref, hf_ref), (qb_ref, ktb_ref, vb_ref, ptb_ref, hb_ref))
    chains = [(sq, d, hd) for sq in range(n_seq) for d in range(2) for hd in range(H_M)]
    n_sub = rf_ref.shape[0] // n_seq
    for sub in range(n_sub):
        sub_of = (sub, n_sub - 1 - sub)
        gate = []
        for sq in range(n_seq):
            rows_f, rows_b = rf_ref[sq * n_sub + sub_of[0]], rb_ref[sq * n_sub + sub_of[1]]
            pick = lambda g: jnp.where(is_fwd, rows_f[g * nj:(g + 1) * nj], rows_b[g * nj:(g + 1) * nj])
            a8, a_max8, f_sum8 = pick(0), pick(1), pick(2)
            m8 = m_s[sq * nj:(sq + 1) * nj]
            mx8 = jnp.maximum(m8, a_max8)
            gate.append((a8, m8, jnp.exp(a8 - mx8), jnp.exp(m8 - mx8)))
            m_s[sq * nj:(sq + 1) * nj] = f_sum8 + mx8
        tok = lambda sq, d: slice((sq * n_sub + sub_of[d]) * CHUNK, (sq * n_sub + sub_of[d] + 1) * CHUNK)

        first = []
        for sq, d, hd in chains:
            q_ref, kt_ref, v_ref, _, _ = dirs[d]
            j = d * H_M + hd
            sl = slice(hd * DH_M, (hd + 1) * DH_M)
            q = q_ref[tok(sq, d), sl]
            k_t = kt_ref[sl, tok(sq, d)]
            v_aug = jnp.concatenate([v_ref[tok(sq, d), sl], ones_v], axis=1)
            qk = _bdot(q, k_t)
            c_st = c_s[sq * nj + j]
            qc = _bdot(q, c_st.astype(BF16))
            upd = _bdot((k_t.astype(F32) * gate[sq][2][j:j + 1, :]).astype(BF16), v_aug)
            dec = gate[sq][3][j:j + 1, :]
            c_s[sq * nj + j] = jnp.concatenate([dec, dec], axis=1) * c_st + upd
            first.append((qk, qc, v_aug))
        for (sq, d, hd), (qk, qc, v_aug) in zip(chains, first):
            cols_ref, h_ref = dirs[d][3], dirs[d][4]
            j = d * H_M + hd
            sl = slice(hd * DH_M, (hd + 1) * DH_M)
            mask = (col <= row) if d == 0 else (col >= row)
            a8, m8 = gate[sq][0], gate[sq][1]
            m_row = m8[j:j + 1, :]
            u = jnp.maximum(m_row, cols_ref[tok(sq, d), j:j + 1])
            w = jnp.exp(jnp.where(mask, a8[j:j + 1, :] - u, -jnp.inf))
            s_inter = jnp.exp(m_row - u)
            intra = _bdot((qk * w).astype(BF16), v_aug)
            num = s_inter * qc[:, :DH_M] + intra[:, :DH_M]
            den = s_inter * qc[:, DH_M:] + intra[:, DH_M:]
            h_ref[tok(sq, d), sl] = num / jnp.maximum(
                jnp.abs(den), jnp.exp(-(cols_ref[tok(sq, d), nj + j:nj + j + 1] + u)))

    @pl.when(step == pl.num_programs(1) - 1)
    def _():
        for sq in range(n_seq):
            for j in range(nj):
                c_aug = c_s[sq * nj + j]
                cn_ref[sq, j] = c_aug[:, :DH_M]
                nn_ref[sq, j:j + 1, :] = c_aug[:, DH_M:].T[0:1, :]
            mn_ref[sq] = m_s[sq * nj:(sq + 1) * nj]


def _mlstm(mq, mkt, mv, gate_pt, gate_rows, c0, m0, batch, seq):
    has_state = c0 is not None
    n_sub = min(MLSTM_CHUNKS_PER_STEP, seq // CHUNK)
    nc = seq // (n_sub * CHUNK)
    n_seq = MLSTM_SEQS_PER_STEP if nc == 1 and batch % MLSTM_SEQS_PER_STEP == 0 else 1
    tm = n_seq * n_sub * CHUNK
    n = batch * seq
    nj = 2 * H_M
    fwd = lambda w: pl.BlockSpec((tm, w), lambda b, c: (b * nc + c, 0))
    bwd = lambda w: pl.BlockSpec((tm, w), lambda b, c: (b * nc + nc - 1 - c, 0))
    fwd_t = pl.BlockSpec((MLSTM_W, tm), lambda b, c: (0, b * nc + c))
    bwd_t = pl.BlockSpec((MLSTM_W, tm), lambda b, c: (0, b * nc + nc - 1 - c))
    st_m = pl.BlockSpec((n_seq, nj, LANE), lambda b, c: (b, 0, 0))
    st_c = pl.BlockSpec((n_seq, nj, DH_M, 2 * DH_M), lambda b, c: (b, 0, 0, 0))
    fwd_r = pl.BlockSpec((n_seq * n_sub, GATE_ROWS, CHUNK), lambda b, c: (b * nc + c, 0, 0))
    bwd_r = pl.BlockSpec((n_seq * n_sub, GATE_ROWS, CHUNK), lambda b, c: (b * nc + nc - 1 - c, 0, 0))
    in_specs = [fwd(MLSTM_W), fwd_t, fwd(MLSTM_W), fwd(LANE), fwd_r,
                bwd(MLSTM_W), bwd_t, bwd(MLSTM_W), bwd(LANE), bwd_r]
    args = [mq, mkt, mv, gate_pt, gate_rows, mq, mkt, mv, gate_pt, gate_rows]
    if has_state:
        in_specs += [st_c, st_m]
        args += [c0, m0]
    return pl.pallas_call(
        functools.partial(_mlstm_kernel, has_state=has_state, n_seq=n_seq),
        grid=(batch // n_seq, nc),
        in_specs=in_specs,
        out_specs=[fwd(MLSTM_W), bwd(MLSTM_W),
                   pl.BlockSpec((n_seq, nj, DH_M, DH_M), lambda b, c: (b, 0, 0, 0)), st_m, st_m],
        out_shape=[jax.ShapeDtypeStruct((n, MLSTM_W), F32), jax.ShapeDtypeStruct((n, MLSTM_W), F32),
                   jax.ShapeDtypeStruct((batch, nj, DH_M, DH_M), F32),
                   jax.ShapeDtypeStruct((batch, nj, DH_M), F32),
                   jax.ShapeDtypeStruct((batch, nj, LANE), F32)],
        scratch_shapes=[pltpu.VMEM((n_seq * nj, DH_M, 2 * DH_M), F32), pltpu.VMEM((n_seq * nj, LANE), F32)],
        compiler_params=_cparams(("arbitrary", "arbitrary")),
        name="mlstm",
    )(*args)


def _attn_kernel(*refs, tiles_per_step, has_cache, n_seq, shared_keys):
    if has_cache:
        q_ref, k_ref, vt_ref, kc_ref, vct_ref, o_ref, qt_s, acc_s, m_s, ot_s, s_scr = refs
    else:
        q_ref, k_ref, vt_ref, o_ref, qt_s, acc_s, m_s, ot_s, s_scr = refs
    tk = vt_ref.shape[2]
    n_tiles = vt_ref.shape[0] // (1 if shared_keys else n_seq)
    tq = q_ref.shape[0] // n_seq
    d_q, d_o = H_A * HEAD_PAD, H_A * V_DIM
    for sq in range(n_seq):
        qt_s[sq * d_q:(sq + 1) * d_q, :] = q_ref[sq * tq:(sq + 1) * tq, :].astype(F32).T.astype(BF16)
    acc_s[...] = jnp.zeros(acc_s.shape, F32)
    m_s[...] = jnp.full(m_s.shape, -jnp.inf, F32)

    n_slots = s_scr.shape[0]

    def scores(tile, head, slot):
        get_k, _, width, sq = tile
        stream = sq * H_A + head
        s = _bdot(get_k(head), qt_s[stream * HEAD_PAD:(stream + 1) * HEAD_PAD, :])
        s_scr[slot, 0:width, :] = s
        return jnp.max(s, axis=0, keepdims=True)

    def update(tiles, pending, next_tile):
        items = [(tile, head) for tile in tiles for head in range(H_A)]
        ahead = items + ([(next_tile, h) for h in range(QK_LOOKAHEAD)] if next_tile is not None else [])
        pending = list(pending)
        for idx, ((_, get_vt, width, sq), head) in enumerate(items):
            stream = sq * H_A + head
            m_tile = pending.pop(0)
            if idx + QK_LOOKAHEAD < len(ahead):
                pending.append(scores(*ahead[idx + QK_LOOKAHEAD], (idx + QK_LOOKAHEAD) % n_slots))
            m_old = m_s[stream:stream + 1, :]
            m_new = jnp.maximum(m_old, m_tile)
            alpha = jnp.exp2(m_old - m_new)
            p = jnp.exp2((s_scr[idx % n_slots, 0:width, :] - m_new).astype(BF16))
            ones = jnp.ones((ONES_ROWS, width), BF16)
            lhs = jnp.concatenate([get_vt(head), ones], axis=0)
            acc_s[stream] = alpha * acc_s[stream] + _bdot(lhs, p)
            m_s[stream:stream + 1, :] = m_new
        return tuple(pending)

    def main_tile(t, sq=0):
        base = 0 if shared_keys else sq * n_tiles
        start = (base + t) * tk if isinstance(t, int) else pl.multiple_of(t * tk, tk)
        rows = pl.ds(start, tk)
        return (lambda h: k_ref[rows, h * HEAD_PAD:(h + 1) * HEAD_PAD],
                lambda h: vt_ref[base + t, h * V_DIM:(h + 1) * V_DIM, :], tk, sq)

    def body(sq, i, pending):
        first = i * tiles_per_step
        return update([main_tile(first + u, sq) for u in range(tiles_per_step)], pending,
                      main_tile(first + tiles_per_step, sq))

    n_loop = (n_tiles - 1) // tiles_per_step

    def tail_tiles(sq):
        tiles = [main_tile(t, sq) for t in range(n_loop * tiles_per_step, n_tiles)]
        if has_cache:
            tiles.append((lambda h: kc_ref[:, h * HEAD_PAD:(h + 1) * HEAD_PAD],
                          lambda h: vct_ref[0, h * V_DIM:(h + 1) * V_DIM, :], kc_ref.shape[0], sq))
        return tiles

    pending = tuple(scores(main_tile(0, 0), h, h) for h in range(QK_LOOKAHEAD))
    if n_loop:
        assert (tiles_per_step * H_A) % n_slots == 0 and (len(tail_tiles(0)) * H_A) % n_slots == 0
        for sq in range(n_seq):
            pending = lax.fori_loop(0, n_loop, functools.partial(body, sq), pending)
            pending = update(tail_tiles(sq), pending, main_tile(0, sq + 1) if sq + 1 < n_seq else None)
    else:
        update([tile for sq in range(n_seq) for tile in tail_tiles(sq)], pending, None)
    for stream in range(n_seq * H_A):
        acc = acc_s[stream]
        ot_s[stream * V_DIM:(stream + 1) * V_DIM, :] = acc[0:V_DIM] * (1.0 / acc[V_DIM:V_DIM + 1])
    for sq in range(n_seq):
        o_ref[sq * tq:(sq + 1) * tq, :] = ot_s[sq * d_o:(sq + 1) * d_o, :].T.astype(BF16)


def _attention(q, k, vt, kc, vct, batch, seq, tq, tiles_per_step, n_seq=1):
    nq = seq // tq
    tk = vt.shape[2]
    n_tiles = seq // tk
    has_cache = kc is not None
    shared_keys = n_seq > 1 and nq > 1
    if shared_keys:
        assert nq % n_seq == 0
        nq, key_seqs, grid = nq // n_seq, 1, (batch, nq // n_seq)
    else:
        assert n_seq == 1 or (nq == 1 and n_tiles == 1 and not has_cache and batch % n_seq == 0)
        key_seqs, grid = n_seq, (batch // n_seq, nq)
    max_width = max(tk, kc.shape[0] // batch) if has_cache else tk
    in_specs = [pl.BlockSpec((n_seq * tq, H_A * HEAD_PAD), lambda b, i: (b * nq + i, 0)),
                pl.BlockSpec((key_seqs * seq, H_A * HEAD_PAD), lambda b, i: (b, 0)),
                pl.BlockSpec((key_seqs * n_tiles, H_A * V_DIM, tk), lambda b, i: (b, 0, 0))]
    args = [q, k, vt]
    if has_cache:
        past = kc.shape[0] // batch
        in_specs += [pl.BlockSpec((past, H_A * HEAD_PAD), lambda b, i: (b, 0)),
                     pl.BlockSpec((1, H_A * V_DIM, past), lambda b, i: (b, 0, 0))]
        args += [kc, vct]
    return pl.pallas_call(
        functools.partial(_attn_kernel, tiles_per_step=tiles_per_step, has_cache=has_cache, n_seq=n_seq,
                          shared_keys=shared_keys),
        grid=grid,
        in_specs=in_specs,
        out_specs=pl.BlockSpec((n_seq * tq, H_A * V_DIM), lambda b, i: (b * nq + i, 0)),
        out_shape=jax.ShapeDtypeStruct((batch * seq, H_A * V_DIM), BF16),
        scratch_shapes=[pltpu.VMEM((n_seq * H_A * HEAD_PAD, tq), BF16),
                        pltpu.VMEM((n_seq * H_A, V_DIM + ONES_ROWS, tq), F32),
                        pltpu.VMEM((n_seq * H_A, tq), F32),
                        pltpu.VMEM((n_seq * H_A * V_DIM, tq), F32),
                        pltpu.VMEM((QK_LOOKAHEAD + 1, max_width, tq), F32)],
        compiler_params=_cparams(("arbitrary", "arbitrary")),
        name="mla_attention",
    )(*args)


def _stage_c_kernel(x_ref, mod_ref, hf_ref, hb_ref, og_ref, attn_ref, ga_ref, gb_ref, hn_ref,
                    post1_ref, pre2_ref, post2_ref, wmo_ref, wao_ref, wout_ref, w1_ref, w2_ref, y_ref):
    mod = mod_ref[0]
    gate1, shift2, scale2, gate2 = mod[2:3], mod[3:4], mod[4:5], mod[5:6]
    hn = hn_ref[...]
    rows = ROW_GROUP
    groups = [slice(r * rows, (r + 1) * rows) for r in range(x_ref.shape[0] // rows)]
    n_ff = D_FF // D_MODEL

    def mixer_in(rs):
        hm = hf_ref[rs, :] + hb_ref[rs, :]
        heads = [_rms(hm[:, hd * DH_M:(hd + 1) * DH_M], hn[:, hd * DH_M:(hd + 1) * DH_M]) for hd in range(H_M)]
        return (jnp.concatenate(heads, axis=1) * og_ref[rs, :].astype(F32)).astype(BF16)

    def branch_out(hm, rs):
        return _bdot(hm, wmo_ref[...]), _bdot(attn_ref[rs, :], wao_ref[...])

    def merge(y_m, y_a, rs):
        return (ga_ref[rs, :].astype(F32) * y_m + gb_ref[rs, :].astype(F32) * y_a).astype(BF16)

    def mid(mix, rs):
        x1 = x_ref[rs, :] + gate1 * _rms(mix, post1_ref[...])
        return x1, (_rms(x1, pre2_ref[...]) * (1.0 + scale2) + shift2).astype(BF16)

    def mlp(h2):
        up = lambda c: _bdot(h2, w1_ref[:, c * D_MODEL:(c + 1) * D_MODEL])
        ff = jnp.zeros((rows, D_MODEL), F32)
        nxt = up(0)
        for c in range(n_ff):
            a = jnp.maximum(nxt, 0.0)
            if c + 1 < n_ff:
                nxt = up(c + 1)
            ff = ff + _bdot((a * a).astype(BF16), w2_ref[c * D_MODEL:(c + 1) * D_MODEL, :])
        return ff

    hms = [mixer_in(rs) for rs in groups]
    ys = [branch_out(hm, rs) for hm, rs in zip(hms, groups)]
    mixes = [_bdot(merge(*y, rs), wout_ref[...]) for y, rs in zip(ys, groups)]
    mids = [mid(mix, rs) for mix, rs in zip(mixes, groups)]
    ffs = [mlp(h2) for _, h2 in mids]
    for (x1, _), ff, rs in zip(mids, ffs, groups):
        y_ref[rs, :] = x1 + gate2 * _rms(ff, post2_ref[...])


def _stage_c(x, mods, mod_index, hf, hb, og, attn, ga, gb, hn, post1, pre2, post2, wmo, wao, wout, w1, w2, tm):
    n = x.shape[0]
    tile = lambda w: pl.BlockSpec((tm, w), lambda i: (i, 0))
    return pl.pallas_call(
        _stage_c_kernel,
        grid=(n // tm,),
        in_specs=[tile(D_MODEL),
                  pl.BlockSpec((1, N_MOD, D_MODEL), lambda i: (mod_index(i), 0, 0)),
                  tile(MLSTM_W), tile(MLSTM_W), tile(MLSTM_W), tile(H_A * V_DIM),
                  tile(D_MODEL), tile(D_MODEL),
                  _const_spec((1, MLSTM_W)), _const_spec((1, D_MODEL)), _const_spec((1, D_MODEL)),
                  _const_spec((1, D_MODEL)),
                  _const_spec((MLSTM_W, D_MODEL)), _const_spec((H_A * V_DIM, D_MODEL)),
                  _const_spec((D_MODEL, D_MODEL)), _const_spec((D_MODEL, D_FF)), _const_spec((D_FF, D_MODEL))],
        out_specs=tile(D_MODEL),
        out_shape=jax.ShapeDtypeStruct((n, D_MODEL), F32),
        compiler_params=_cparams(("arbitrary",)),
        name="stage_c",
    )(x, mods, hf, hb, og, attn, ga, gb, hn, post1, pre2, post2, wmo, wao, wout, w1, w2)


def _rope_tables(n_tokens):
    pos = np.arange(n_tokens)
    row = (pos // GRID_W).astype(np.float32)
    col = (pos % GRID_W).astype(np.float32)
    inv = (ROPE_BASE ** (-np.arange(0, AX_DIM, 2, dtype=np.float32) / AX_DIM)).astype(np.float32)
    ang = np.concatenate([row[:, None] * inv, col[:, None] * inv], axis=-1)
    cos = np.cos(ang.astype(np.float64)).astype(np.float32)
    sin = np.sin(ang.astype(np.float64)).astype(np.float32)
    ones = np.ones((n_tokens, NOPE), np.float32)
    zeros = np.zeros((n_tokens, NOPE), np.float32)
    pad = np.zeros((n_tokens, HEAD_PAD - NOPE - ROPE_DIM), np.float32)
    ta = np.concatenate([ones, cos, cos, pad], axis=-1)
    tb = np.concatenate([zeros, -sin, sin, pad], axis=-1)
    return jnp.asarray(ta), jnp.asarray(tb)


def _plain_tables(n_tokens):
    ones = jnp.ones((n_tokens, NOPE + ROPE_DIM), F32)
    pad = jnp.zeros((n_tokens, HEAD_PAD - NOPE - ROPE_DIM), F32)
    return jnp.concatenate([ones, pad], axis=-1), jnp.zeros((n_tokens, HEAD_PAD), F32)


def kernel(x_prompt, x_sample, cache_mla_ckv, cache_mla_krope, state_mlstm_C, state_mlstm_n, state_mlstm_m,
           c, c_ctx, w_ada, b_ada, norm_pre1, norm_post1, norm_pre2, norm_post2, w_in, mlstm_gate_b,
           mla_q_norm, mla_kv_norm, w_uq, w_ukv, w_mla_o, mlstm_head_norm, w_mlstm_o, w_out, w_mlp1, w_mlp2):
    bp, sp, _ = x_prompt.shape
    bs, ss, _ = x_sample.shape
    depth = w_in.shape[0]
    past = cache_mla_ckv.shape[2]
    nj = 2 * H_M
    even = np.arange(0, ROPE_DIM, 2)
    odd = np.arange(1, ROPE_DIM, 2)
    perm = np.concatenate([even, odd])
    perm_sw = np.concatenate([odd, even])

    xp = x_prompt.reshape(bp * sp, D_MODEL)
    xs = x_sample.reshape(bs * ss, D_MODEL)
    cc = jnp.zeros((8, D_MODEL), F32).at[:bs].set(c).at[bs].set(c_ctx)
    ta_lat, tb_lat = _rope_tables(ss)
    assert past == ATTN_KEY_TILE
    ta_ctx, tb_ctx = _plain_tables(TM_A)

    new_ckv, new_krope, new_c, new_n, new_m = [], [], [], [], []
    for l in range(depth):
        cols = np.cumsum((W_BIG_COLS, N_GATES, Q_LORA, KV_LORA, ROPE_DIM))
        w_big, w_g, w_cq, w_ckv, w_kr, w_merge = jnp.split(w_in[l].astype(BF16), cols.tolist(), axis=1)
        w_g = w_g.reshape(D_MODEL, 2, 2, H_M).transpose(0, 2, 1, 3).reshape(D_MODEL, N_GATES)
        w_misc = jnp.concatenate([w_kr, w_g, jnp.zeros((D_MODEL, MISC_W - 3 * ROPE_DIM - N_GATES), BF16),
                                  w_kr[:, perm], w_kr[:, perm_sw]], axis=1)
        wa = (w_big, w_merge, jnp.concatenate([w_cq, w_ckv, w_misc], axis=1))
        gbias = jnp.zeros((1, MISC_W), F32).at[0, GATE_LANE0:GATE_LANE0 + N_GATES].set(
            mlstm_gate_b[l].transpose(1, 0, 2).reshape(N_GATES))
        uq = w_uq[l]
        wq = jnp.concatenate([uq[..., :NOPE], uq[..., NOPE:][..., perm], uq[..., NOPE:][..., perm_sw]],
                             axis=-1).reshape(Q_LORA, H_A * HEAD_PAD).astype(BF16)
        ukv = w_ukv[l]
        wk = jnp.concatenate([ukv[..., :NOPE], jnp.zeros((KV_LORA, H_A, HEAD_PAD - NOPE), F32)],
                             axis=-1).reshape(KV_LORA, H_A * HEAD_PAD).astype(BF16)
        wv = ukv[..., NOPE:].reshape(KV_LORA, H_A * V_DIM).T.astype(BF16)
        wmo = w_mlstm_o[l].astype(BF16)
        wao = w_mla_o[l].astype(BF16)
        wout = w_out[l].astype(BF16)
        w1 = w_mlp1[l].astype(BF16)
        w2 = w_mlp2[l].astype(BF16)
        pre1, post1 = norm_pre1[l][None], norm_post1[l][None]
        pre2, post2 = norm_pre2[l][None], norm_post2[l][None]
        qn, kvn = mla_q_norm[l][None], mla_kv_norm[l][None]
        hn = mlstm_head_norm[l].reshape(1, MLSTM_W)

        mods = _modulation(cc, w_ada[l], b_ada[l][None]).reshape(8, N_MOD, D_MODEL)

        ctx_mod = lambda i: bs
        a = _stage_a(xp, mods, ctx_mod, pre1, wa, gbias, qn, kvn, wq, wk, wv, ta_ctx, tb_ctx,
                     lambda i: 0, TM_A, sp)
        mq, mk, mv, og, ga, gb, gates, q, k, v, ckv_n, kro = a
        hf, hb, c_fin, n_fin, m_fin = _mlstm(mq, mk, mv, *_gate_rows(gates, GATE_CHUNKS_PER_STEP),
                                             None, None, bp, sp)
        attn = _attention(q, k, v, None, None, bp, sp, sp, 1, n_seq=ATTN_SEQS_PER_STEP)
        xp = _stage_c(xp, mods, ctx_mod, hf, hb, og, attn, ga, gb, hn, post1, pre2, post2,
                      wmo, wao, wout, w1, w2, TM_C)
        new_ckv.append(ckv_n.reshape(bp, sp, KV_LORA))
        new_krope.append(kro.reshape(bp, sp, ROPE_DIM))
        new_c.append(c_fin.reshape(bp, 2, H_M, DH_M, DH_M))
        new_n.append(n_fin.reshape(bp, 2, H_M, DH_M))
        new_m.append(m_fin[:, :, 0].reshape(bp, 2, H_M))

        tiles_a, tiles_c = ss // TM_A, ss // TM_C
        a = _stage_a(xs, mods, lambda i: i // tiles_a, pre1, wa, gbias, qn, kvn, wq, wk, wv, ta_lat, tb_lat,
                     lambda i: i % tiles_a, TM_A, ATTN_KEY_TILE)
        mq, mk, mv, og, ga, gb, gates, q, k, v, _, _ = a
        kr_cache = jnp.zeros((bs * past, LANE), F32).at[:, NOPE:NOPE + ROPE_DIM].set(
            cache_mla_krope[:, l].reshape(bs * past, ROPE_DIM)[:, perm])
        kc, vc = _kv_cache(cache_mla_ckv[:, l].reshape(bs * past, KV_LORA), kr_cache, wk, wv, past)
        n0 = state_mlstm_n[:, l].reshape(bs, nj, DH_M, 1)
        c0 = jnp.concatenate([state_mlstm_C[:, l].reshape(bs, nj, DH_M, DH_M),
                              jnp.broadcast_to(n0, (bs, nj, DH_M, DH_M))], axis=-1)
        m0 = jnp.broadcast_to(state_mlstm_m[:, l].reshape(bs, nj, 1), (bs, nj, LANE))
        hf, hb, _, _, _ = _mlstm(mq, mk, mv, *_gate_rows(gates, GATE_CHUNKS_PER_STEP), c0, m0, bs, ss)
        attn = _attention(q, k, v, kc, vc, bs, ss, ATTN_QUERY_TILE, KEY_TILES_PER_STEP,
                          n_seq=ATTN_QUERY_TILES_PER_STEP)
        xs = _stage_c(xs, mods, lambda i: i // tiles_c, hf, hb, og, attn, ga, gb, hn, post1, pre2, post2,
                      wmo, wao, wout, w1, w2, TM_C)

    return (xp.reshape(bp, sp, D_MODEL), xs.reshape(bs, ss, D_MODEL),
            jnp.stack(new_ckv, axis=1), jnp.stack(new_krope, axis=1), jnp.stack(new_c, axis=1),
            jnp.stack(new_n, axis=1), jnp.stack(new_m, axis=1))
```

```python
import functools

import numpy as np
import jax
import jax.numpy as jnp
from jax import lax
from jax.experimental import pallas as pl
from jax.experimental.pallas import tpu as pltpu

F32 = jnp.float32
BF16 = jnp.bfloat16

D_MODEL = 1024
H_M = 4
DH_M = 128
MLSTM_W = H_M * DH_M
CHUNK = 128
H_A = 8
NOPE = 64
ROPE_DIM = 32
V_DIM = 64
Q_LORA = 384
KV_LORA = 256
AX_DIM = ROPE_DIM // 2
ROPE_BASE = 10000.0
GRID_W = 64
D_FF = 4 * D_MODEL
N_MOD = 6
EPS = 1e-6
N_GATES = 4 * H_M
M_INIT = -1e30

LANE = 128
HEAD_PAD = LANE
MISC_W = LANE
GATE_LANE0 = ROPE_DIM
OFF_MQ, OFF_MK, OFF_MV, OFF_MO, W_BIG_COLS = 0, MLSTM_W, 2 * MLSTM_W, 3 * MLSTM_W, 4 * MLSTM_W
OFF_CQ, OFF_CKV, OFF_MISC = 0, Q_LORA, Q_LORA + KV_LORA
W_SMALL_COLS = OFF_MISC + MISC_W

VMEM_LIMIT = 60 * 1024 * 1024
QK_SCALE = float((NOPE + ROPE_DIM) ** -0.5 * np.log2(np.e))
QK_LOOKAHEAD = 3
ATTN_KEY_TILE = 512
ATTN_SEQS_PER_STEP = 8
ATTN_QUERY_TILES_PER_STEP = 2
ATTN_QUERY_TILE = 256
ROW_GROUP = 256
TM_A, TM_C = 512, 512
KEY_TILES_PER_STEP = 3
ONES_ROWS = 16


def _cparams(sem):
    return pltpu.CompilerParams(dimension_semantics=sem, vmem_limit_bytes=VMEM_LIMIT)


def _const_spec(shape):
    nd = len(shape)
    return pl.BlockSpec(shape, lambda *_: (0,) * nd, pipeline_mode=pl.Buffered(1))


def _rms(x, w):
    return x * lax.rsqrt(jnp.mean(x * x, axis=-1, keepdims=True) + EPS) * w


def _sigmoid(x):
    return 1.0 / (1.0 + jnp.exp(-x))


def _log_sigmoid(x):
    return jnp.minimum(x, 0.0) - jnp.log(1.0 + jnp.exp(-jnp.abs(x)))


def _bdot(a, b):
    return jnp.dot(a, b, preferred_element_type=F32)


def _mod_kernel(c_ref, w_ref, b_ref, o_ref):
    c = c_ref[...]
    s = c * _sigmoid(c)
    o_ref[...] = jnp.dot(s, w_ref[...], preferred_element_type=F32,
                         precision=lax.Precision.HIGHEST) + b_ref[...]


def _modulation(cc, w_ada, b_ada):
    n_out = w_ada.shape[1]
    tn = 1536
    return pl.pallas_call(
        _mod_kernel,
        grid=(n_out // tn,),
        in_specs=[pl.BlockSpec((8, D_MODEL), lambda j: (0, 0)),
                  pl.BlockSpec((D_MODEL, tn), lambda j: (0, j)),
                  pl.BlockSpec((1, tn), lambda j: (0, j))],
        out_specs=pl.BlockSpec((8, tn), lambda j: (0, j)),
        out_shape=jax.ShapeDtypeStruct((8, n_out), F32),
        compiler_params=_cparams(("arbitrary",)),
        name="modulation",
    )(cc, w_ada, b_ada)


def _stage_a_kernel(x_ref, mod_ref, pre1_ref, wbig_ref, wmerge_ref, wsmall_ref, gbias_ref, qn_ref, kvn_ref,
                    wq_ref, wk_ref, wv_ref,
                    ta_ref, tb_ref,
                    mq_ref, mk_ref, mv_ref, og_ref, ga_ref, gb_ref, gates_ref, q_ref, k_ref, v_ref,
                    ckv_ref, kro_ref):
    mod = mod_ref[0]
    shift1, scale1 = mod[0:1], mod[1:2]
    tm = x_ref.shape[0]
    rows = ROW_GROUP
    tkv = v_ref.shape[2]

    def normed(rs):
        return (_rms(x_ref[rs, :], pre1_ref[...]) * (1.0 + scale1) + shift1).astype(BF16)

    def small_proj(hb):
        return tuple(_bdot(hb, wsmall_ref[:, lo:hi])
                     for lo, hi in ((OFF_CQ, OFF_CKV), (OFF_CKV, OFF_MISC), (OFF_MISC, W_SMALL_COLS)))

    def big_proj(hb, rs):
        proj = lambda lo, hi: _bdot(hb, wbig_ref[:, lo:hi])
        mq_ref[rs, :] = (proj(OFF_MQ, OFF_MK) * DH_M ** -0.5).astype(BF16)
        mk_ref[:, rs] = proj(OFF_MK, OFF_MV).T.astype(BF16)
        mv_ref[rs, :] = proj(OFF_MV, OFF_MO).astype(BF16)
        og_ref[rs, :] = _sigmoid(proj(OFF_MO, W_BIG_COLS)).astype(BF16)
        ga_ref[rs, :] = _sigmoid(_bdot(hb, wmerge_ref[:, :D_MODEL])).astype(BF16)
        gb_ref[rs, :] = _sigmoid(_bdot(hb, wmerge_ref[:, D_MODEL:])).astype(BF16)

    def mla_build(cq, ckv, misc, rs):
        lane = lax.broadcasted_iota(jnp.int32, misc.shape, 1)
        g = misc + gbias_ref[...]
        is_f = (lane >= GATE_LANE0 + 2 * H_M) & (lane < GATE_LANE0 + N_GATES)
        gates_ref[rs, :] = jnp.where(is_f, _log_sigmoid(g), g)
        kro_ref[rs, :] = misc[:, 0:ROPE_DIM]

        ckv_n = _rms(ckv, kvn_ref[...])
        ckv_ref[rs, :] = ckv_n
        ckv_b = ckv_n.astype(BF16)

        ta = ta_ref[rs, :]
        tb = tb_ref[rs, :]
        qf = _bdot(_rms(cq, qn_ref[...]).astype(BF16), wq_ref[...])
        ta8 = jnp.concatenate([ta] * H_A, axis=1)
        tb8 = jnp.concatenate([tb] * H_A, axis=1)
        q = qf * ta8 + pltpu.roll(qf, H_A * HEAD_PAD - ROPE_DIM, 1) * tb8
        q_ref[rs, :] = (q * QK_SCALE).astype(BF16)

        ta_k = jnp.where(lane < NOPE, 0.0, ta)
        kr = misc * ta_k + pltpu.roll(misc, MISC_W - ROPE_DIM, 1) * tb
        kk = _bdot(ckv_b, wk_ref[...]) + jnp.concatenate([kr] * H_A, axis=1)
        k_ref[rs, :] = kk.astype(BF16)
        vt = _bdot(wv_ref[...], ckv_n.T.astype(BF16)).astype(BF16)
        step = min(rows, tkv)
        for off in range(0, rows, step):
            lo = rs.start + off
            v_ref[lo // tkv, :, lo % tkv:lo % tkv + step] = vt[:, off:off + step]

    groups = [slice(r * rows, (r + 1) * rows) for r in range(tm // rows)]
    hbs = [normed(rs) for rs in groups]
    for hb, rs in zip(hbs, groups):
        small = small_proj(hb)
        big_proj(hb, rs)
        mla_build(*small, rs)


def _stage_a(x, mods, mod_index, pre1, wa, gbias, qn, kvn, wq, wk, wv, ta, tb, table_index, tm, tkv):
    n = x.shape[0]
    tile = lambda w: pl.BlockSpec((tm, w), lambda i: (i, 0))
    bf = lambda w: jax.ShapeDtypeStruct((n, w), BF16)
    f32 = lambda w: jax.ShapeDtypeStruct((n, w), F32)
    return pl.pallas_call(
        _stage_a_kernel,
        grid=(n // tm,),
        in_specs=[tile(D_MODEL),
                  pl.BlockSpec((1, N_MOD, D_MODEL), lambda i: (mod_index(i), 0, 0)),
                  _const_spec((1, D_MODEL)),
                  _const_spec((D_MODEL, W_BIG_COLS)),
                  _const_spec((D_MODEL, 2 * D_MODEL)),
                  _const_spec((D_MODEL, W_SMALL_COLS)),
                  _const_spec((1, MISC_W)),
                  _const_spec((1, Q_LORA)),
                  _const_spec((1, KV_LORA)),
                  _const_spec((Q_LORA, H_A * HEAD_PAD)),
                  _const_spec((KV_LORA, H_A * HEAD_PAD)),
                  _const_spec((H_A * V_DIM, KV_LORA)),
                  pl.BlockSpec((tm, LANE), lambda i: (table_index(i), 0)),
                  pl.BlockSpec((tm, LANE), lambda i: (table_index(i), 0))],
        out_specs=[tile(MLSTM_W), pl.BlockSpec((MLSTM_W, tm), lambda i: (0, i)), tile(MLSTM_W), tile(MLSTM_W),
                   tile(D_MODEL), tile(D_MODEL), tile(MISC_W),
                   tile(H_A * HEAD_PAD), tile(H_A * HEAD_PAD),
                   pl.BlockSpec((tm // tkv, H_A * V_DIM, tkv), lambda i: (i, 0, 0)),
                   tile(KV_LORA), tile(ROPE_DIM)],
        out_shape=[bf(MLSTM_W), jax.ShapeDtypeStruct((MLSTM_W, n), BF16), bf(MLSTM_W), bf(MLSTM_W),
                   bf(D_MODEL), bf(D_MODEL), f32(MISC_W),
                   bf(H_A * HEAD_PAD), bf(H_A * HEAD_PAD),
                   jax.ShapeDtypeStruct((n // tkv, H_A * V_DIM, tkv), BF16),
                   f32(KV_LORA), f32(ROPE_DIM)],
        compiler_params=_cparams(("arbitrary",)),
        name="stage_a",
    )(x, mods, pre1, *wa, gbias, qn, kvn, wq, wk, wv, ta, tb)


def _kv_cache_kernel(ckv_ref, kr_ref, wk_ref, wv_ref, k_ref, v_ref):
    ckv_b = ckv_ref[...].astype(BF16)
    kk = _bdot(ckv_b, wk_ref[...]) + jnp.concatenate([kr_ref[...]] * H_A, axis=1)
    k_ref[...] = kk.astype(BF16)
    v_ref[0] = _bdot(wv_ref[...], ckv_ref[...].T.astype(BF16)).astype(BF16)


def _kv_cache(ckv, kr, wk, wv, tm):
    n = ckv.shape[0]
    return pl.pallas_call(
        _kv_cache_kernel,
        grid=(n // tm,),
        in_specs=[pl.BlockSpec((tm, KV_LORA), lambda i: (i, 0)),
                  pl.BlockSpec((tm, LANE), lambda i: (i, 0)),
                  _const_spec((KV_LORA, H_A * HEAD_PAD)),
                  _const_spec((H_A * V_DIM, KV_LORA))],
        out_specs=[pl.BlockSpec((tm, H_A * HEAD_PAD), lambda i: (i, 0)),
                   pl.BlockSpec((1, H_A * V_DIM, tm), lambda i: (i, 0, 0))],
        out_shape=[jax.ShapeDtypeStruct((n, H_A * HEAD_PAD), BF16),
                   jax.ShapeDtypeStruct((n // tm, H_A * V_DIM, tm), BF16)],
        compiler_params=_cparams(("arbitrary",)),
        name="kv_cache",
    )(ckv, kr, wk, wv)


GATE_ROWS = 32
MLSTM_SEQS_PER_STEP = 4
MLSTM_CHUNKS_PER_STEP = 4
GATE_CHUNKS_PER_STEP = 32


def _gate_rows_kernel(g_ref, cols_ref, rows_ref):
    nj = 2 * H_M
    nch = g_ref.shape[0] // CHUNK
    n_rows = nch * nj
    row = lax.broadcasted_iota(jnp.int32, (CHUNK, CHUNK), 0)
    col = lax.broadcasted_iota(jnp.int32, (CHUNK, CHUNK), 1)
    lower = (col <= row).astype(F32)
    upper = (col >= row).astype(F32)
    hi = lax.Precision.HIGHEST
    rowi = lax.broadcasted_iota(jnp.int32, (n_rows, CHUNK), 0)
    lane = lax.broadcasted_iota(jnp.int32, (n_rows, CHUNK), 1)
    is_fwd = rowi % nj < H_M

    gi, gf = GATE_LANE0, GATE_LANE0 + nj
    i_rows, f_rows = [], []
    for c in range(nch):
        g_t = g_ref[c * CHUNK:(c + 1) * CHUNK, :].T
        i_rows.append(g_t[gi:gi + nj])
        f_rows.append(g_t[gf:gf + nj])
    i_all = jnp.concatenate(i_rows, axis=0)
    f_all = jnp.concatenate(f_rows, axis=0)
    b = jnp.where(is_fwd, jnp.dot(f_all, upper, preferred_element_type=F32, precision=hi),
                  jnp.dot(f_all, lower, preferred_element_type=F32, precision=hi))
    a = i_all - b
    cm = a
    shift = 1
    while shift < CHUNK:
        y_f = jnp.where(lane >= shift, pltpu.roll(cm, shift, 1), -jnp.inf)
        y_b = jnp.where(lane < CHUNK - shift, pltpu.roll(cm, CHUNK - shift, 1), -jnp.inf)
        cm = jnp.maximum(cm, jnp.where(is_fwd, y_f, y_b))
        shift *= 2
    a_max = jnp.broadcast_to(jnp.max(a, axis=1, keepdims=True), a.shape)
    f_sum = jnp.broadcast_to(jnp.sum(f_all, axis=1, keepdims=True), a.shape)
    zeros8 = jnp.zeros((nj, CHUNK), F32)
    for c in range(nch):
        sl = slice(c * nj, (c + 1) * nj)
        stack = jnp.concatenate([cm[sl], b[sl]] + [zeros8] * (CHUNK // nj - 2), axis=0)
        cols_ref[c * CHUNK:(c + 1) * CHUNK, :] = stack.T
        rows_ref[c] = jnp.concatenate([a[sl], a_max[sl], f_sum[sl], zeros8], axis=0)


def _gate_rows(gates, chunks_per_step):
    n = gates.shape[0]
    tm = chunks_per_step * CHUNK
    return pl.pallas_call(
        _gate_rows_kernel,
        grid=(n // tm,),
        in_specs=[pl.BlockSpec((tm, MISC_W), lambda i: (i, 0))],
        out_specs=[pl.BlockSpec((tm, LANE), lambda i: (i, 0)),
                   pl.BlockSpec((chunks_per_step, GATE_ROWS, CHUNK), lambda i: (i, 0, 0))],
        out_shape=[jax.ShapeDtypeStruct((n, LANE), F32),
                   jax.ShapeDtypeStruct((n // CHUNK, GATE_ROWS, CHUNK), F32)],
        compiler_params=_cparams(("arbitrary",)),
        name="gate_rows",
    )(gates)


def _mlstm_kernel(*refs, has_state, n_seq):
    if has_state:
        (qf_ref, ktf_ref, vf_ref, colsf_ref, rf_ref, qb_ref, ktb_ref, vb_ref, colsb_ref, rb_ref, c0_ref, m0_ref,
         hf_ref, hb_ref, cn_ref, nn_ref, mn_ref, c_s, m_s) = refs
    else:
        (qf_ref, ktf_ref, vf_ref, colsf_ref, rf_ref, qb_ref, ktb_ref, vb_ref, colsb_ref, rb_ref,
         hf_ref, hb_ref, cn_ref, nn_ref, mn_ref, c_s, m_s) = refs
    step = pl.program_id(1)

    nj = 2 * H_M

    @pl.when(step == 0)
    def _():
        if has_state:
            for sq in range(n_seq):
                c_s[sq * nj:(sq + 1) * nj] = c0_ref[sq]
                m_s[sq * nj:(sq + 1) * nj] = m0_ref[sq]
        else:
            c_s[...] = jnp.zeros(c_s.shape, F32)
            m_s[...] = jnp.full(m_s.shape, M_INIT, F32)

    row = lax.broadcasted_iota(jnp.int32, (CHUNK, CHUNK), 0)
    col = lax.broadcasted_iota(jnp.int32, (CHUNK, CHUNK), 1)
    is_fwd = lax.broadcasted_iota(jnp.int32, (nj, CHUNK), 0) < H_M
    ones_v = jnp.ones((CHUNK, DH_M), BF16)
    dirs = ((qf_ref, ktf_ref, vf_ref, colsf_ref, hf_ref), (qb_ref, ktb_ref, vb_ref, colsb_ref, hb_ref))
    chains = [(sq, d, hd) for sq in range(n_seq) for d in range(2) for hd in range(H_M)]
    n_sub = rf_ref.shape[0] // n_seq
    for sub in range(n_sub):
        sub_of = (sub, n_sub - 1 - sub)
        gate = []
        for sq in range(n_seq):
            rows_f, rows_b = rf_ref[sq * n_sub + sub_of[0]], rb_ref[sq * n_sub + sub_of[1]]
            pick = lambda g: jnp.where(is_fwd, rows_f[g * nj:(g + 1) * nj], rows_b[g * nj:(g + 1) * nj])
            a8, a_max8, f_sum8 = pick(0), pick(1), pick(2)
            m8 = m_s[sq * nj:(sq + 1) * nj]
            mx8 = jnp.maximum(m8, a_max8)
            gate.append((a8, m8, jnp.exp(a8 - mx8), jnp.exp(m8 - mx8)))
            m_s[sq * nj:(sq + 1) * nj] = f_sum8 + mx8
        tok = lambda sq, d: slice((sq * n_sub + sub_of[d]) * CHUNK, (sq * n_sub + sub_of[d] + 1) * CHUNK)

        first = []
        for sq, d, hd in chains:
            q_ref, kt_ref, v_ref, _, _ = dirs[d]
            j = d * H_M + hd
            sl = slice(hd * DH_M, (hd + 1) * DH_M)
            q = q_ref[tok(sq, d), sl]
            k_t = kt_ref[sl, tok(sq, d)]
            v_aug = jnp.concatenate([v_ref[tok(sq, d), sl], ones_v], axis=1)
            qk = _bdot(q, k_t)
            c_st = c_s[sq * nj + j]
            qc = _bdot(q, c_st.astype(BF16))
            upd = _bdot((k_t.astype(F32) * gate[sq][2][j:j + 1, :]).astype(BF16), v_aug)
            dec = gate[sq][3][j:j + 1, :]
            c_s[sq * nj + j] = jnp.concatenate([dec, dec], axis=1) * c_st + upd
            first.append((qk, qc, v_aug))
        for (sq, d, hd), (qk, qc, v_aug) in zip(chains, first):
            cols_ref, h_ref = dirs[d][3], dirs[d][4]
            j = d * H_M + hd
            sl = slice(hd * DH_M, (hd + 1) * DH_M)
            mask = (col <= row) if d == 0 else (col >= row)
            a8, m8 = gate[sq][0], gate[sq][1]
            m_row = m8[j:j + 1, :]
            u = jnp.maximum(m_row, cols_ref[tok(sq, d), j:j + 1])
            w = jnp.exp(jnp.where(mask, a8[j:j + 1, :] - u, -jnp.inf))
            s_inter = jnp.exp(m_row - u)
            intra = _bdot((qk * w).astype(BF16), v_aug)
            num = s_inter * qc[:, :DH_M] + intra[:, :DH_M]
            den = s_inter * qc[:, DH_M:] + intra[:, DH_M:]
            h_ref[tok(sq, d), sl] = num / jnp.maximum(
                jnp.abs(den), jnp.exp(-(cols_ref[tok(sq, d), nj + j:nj + j + 1] + u)))

    @pl.when(step == pl.num_programs(1) - 1)
    def _():
        for sq in range(n_seq):
            for j in range(nj):
                c_aug = c_s[sq * nj + j]
                cn_ref[sq, j] = c_aug[:, :DH_M]
                nn_ref[sq, j:j + 1, :] = c_aug[:, DH_M:].T[0:1, :]
            mn_ref[sq] = m_s[sq * nj:(sq + 1) * nj]


def _mlstm(mq, mkt, mv, gate_cols, gate_rows, c0, m0, batch, seq):
    has_state = c0 is not None
    n_sub = min(MLSTM_CHUNKS_PER_STEP, seq // CHUNK)
    nc = seq // (n_sub * CHUNK)
    n_seq = MLSTM_SEQS_PER_STEP if nc == 1 and batch % MLSTM_SEQS_PER_STEP == 0 else 1
    tm = n_seq * n_sub * CHUNK
    n = batch * seq
    nj = 2 * H_M
    fwd = lambda w: pl.BlockSpec((tm, w), lambda b, c: (b * nc + c, 0))
    bwd = lambda w: pl.BlockSpec((tm, w), lambda b, c: (b * nc + nc - 1 - c, 0))
    fwd_t = pl.BlockSpec((MLSTM_W, tm), lambda b, c: (0, b * nc + c))
    bwd_t = pl.BlockSpec((MLSTM_W, tm), lambda b, c: (0, b * nc + nc - 1 - c))
    st_m = pl.BlockSpec((n_seq, nj, LANE), lambda b, c: (b, 0, 0))
    st_c = pl.BlockSpec((n_seq, nj, DH_M, 2 * DH_M), lambda b, c: (b, 0, 0, 0))
    fwd_r = pl.BlockSpec((n_seq * n_sub, GATE_ROWS, CHUNK), lambda b, c: (b * nc + c, 0, 0))
    bwd_r = pl.BlockSpec((n_seq * n_sub, GATE_ROWS, CHUNK), lambda b, c: (b * nc + nc - 1 - c, 0, 0))
    in_specs = [fwd(MLSTM_W), fwd_t, fwd(MLSTM_W), fwd(LANE), fwd_r,
                bwd(MLSTM_W), bwd_t, bwd(MLSTM_W), bwd(LANE), bwd_r]
    args = [mq, mkt, mv, gate_cols, gate_rows, mq, mkt, mv, gate_cols, gate_rows]
    if has_state:
        in_specs += [st_c, st_m]
        args += [c0, m0]
    return pl.pallas_call(
        functools.partial(_mlstm_kernel, has_state=has_state, n_seq=n_seq),
        grid=(batch // n_seq, nc),
        in_specs=in_specs,
        out_specs=[fwd(MLSTM_W), bwd(MLSTM_W),
                   pl.BlockSpec((n_seq, nj, DH_M, DH_M), lambda b, c: (b, 0, 0, 0)), st_m, st_m],
        out_shape=[jax.ShapeDtypeStruct((n, MLSTM_W), F32), jax.ShapeDtypeStruct((n, MLSTM_W), F32),
                   jax.ShapeDtypeStruct((batch, nj, DH_M, DH_M), F32),
                   jax.ShapeDtypeStruct((batch, nj, DH_M), F32),
                   jax.ShapeDtypeStruct((batch, nj, LANE), F32)],
        scratch_shapes=[pltpu.VMEM((n_seq * nj, DH_M, 2 * DH_M), F32), pltpu.VMEM((n_seq * nj, LANE), F32)],
        compiler_params=_cparams(("arbitrary", "arbitrary")),
        name="mlstm",
    )(*args)


def _attn_kernel(*refs, tiles_per_step, has_cache, n_seq, shared_keys):
    if has_cache:
        q_ref, k_ref, vt_ref, kc_ref, vct_ref, o_ref, qt_s, acc_s, m_s, ot_s, s_scr = refs
    else:
        q_ref, k_ref, vt_ref, o_ref, qt_s, acc_s, m_s, ot_s, s_scr = refs
    tk = vt_ref.shape[2]
    n_tiles = vt_ref.shape[0] // (1 if shared_keys else n_seq)
    tq = q_ref.shape[0] // n_seq
    d_q, d_o = H_A * HEAD_PAD, H_A * V_DIM
    for sq in range(n_seq):
        qt_s[sq * d_q:(sq + 1) * d_q, :] = q_ref[sq * tq:(sq + 1) * tq, :].astype(F32).T.astype(BF16)
    acc_s[...] = jnp.zeros(acc_s.shape, F32)
    m_s[...] = jnp.full(m_s.shape, -jnp.inf, F32)

    n_slots = s_scr.shape[0]

    def scores(tile, head, slot):
        get_k, _, width, sq = tile
        stream = sq * H_A + head
        s = _bdot(get_k(head), qt_s[stream * HEAD_PAD:(stream + 1) * HEAD_PAD, :])
        s_scr[slot, 0:width, :] = s
        return jnp.max(s, axis=0, keepdims=True)

    def update(tiles, pending, next_tile):
        items = [(tile, head) for tile in tiles for head in range(H_A)]
        ahead = items + ([(next_tile, h) for h in range(QK_LOOKAHEAD)] if next_tile is not None else [])
        pending = list(pending)
        for idx, ((_, get_vt, width, sq), head) in enumerate(items):
            stream = sq * H_A + head
            m_tile = pending.pop(0)
            if idx + QK_LOOKAHEAD < len(ahead):
                pending.append(scores(*ahead[idx + QK_LOOKAHEAD], (idx + QK_LOOKAHEAD) % n_slots))
            m_old = m_s[stream:stream + 1, :]
            m_new = jnp.maximum(m_old, m_tile)
            alpha = jnp.exp2(m_old - m_new)
            p = jnp.exp2((s_scr[idx % n_slots, 0:width, :] - m_new).astype(BF16))
            ones = jnp.ones((ONES_ROWS, width), BF16)
            lhs = jnp.concatenate([get_vt(head), ones], axis=0)
            acc_s[stream] = alpha * acc_s[stream] + _bdot(lhs, p)
            m_s[stream:stream + 1, :] = m_new
        return tuple(pending)

    def main_tile(t, sq=0):
        base = 0 if shared_keys else sq * n_tiles
        start = (base + t) * tk if isinstance(t, int) else pl.multiple_of(t * tk, tk)
        rows = pl.ds(start, tk)
        return (lambda h: k_ref[rows, h * HEAD_PAD:(h + 1) * HEAD_PAD],
                lambda h: vt_ref[base + t, h * V_DIM:(h + 1) * V_DIM, :], tk, sq)

    def body(sq, i, pending):
        first = i * tiles_per_step
        return update([main_tile(first + u, sq) for u in range(tiles_per_step)], pending,
                      main_tile(first + tiles_per_step, sq))

    n_loop = (n_tiles - 1) // tiles_per_step

    def tail_tiles(sq):
        tiles = [main_tile(t, sq) for t in range(n_loop * tiles_per_step, n_tiles)]
        if has_cache:
            tiles.append((lambda h: kc_ref[:, h * HEAD_PAD:(h + 1) * HEAD_PAD],
                          lambda h: vct_ref[0, h * V_DIM:(h + 1) * V_DIM, :], kc_ref.shape[0], sq))
        return tiles

    pending = tuple(scores(main_tile(0, 0), h, h) for h in range(QK_LOOKAHEAD))
    if n_loop:
        assert (tiles_per_step * H_A) % n_slots == 0 and (len(tail_tiles(0)) * H_A) % n_slots == 0
        for sq in range(n_seq):
            pending = lax.fori_loop(0, n_loop, functools.partial(body, sq), pending)
            pending = update(tail_tiles(sq), pending, main_tile(0, sq + 1) if sq + 1 < n_seq else None)
    else:
        update([tile for sq in range(n_seq) for tile in tail_tiles(sq)], pending, None)
    for stream in range(n_seq * H_A):
        acc = acc_s[stream]
        ot_s[stream * V_DIM:(stream + 1) * V_DIM, :] = acc[0:V_DIM] * (1.0 / acc[V_DIM:V_DIM + 1])
    for sq in range(n_seq):
        o_ref[sq * tq:(sq + 1) * tq, :] = ot_s[sq * d_o:(sq + 1) * d_o, :].T.astype(BF16)


def _attention(q, k, vt, kc, vct, batch, seq, tq, tiles_per_step, n_seq=1):
    nq = seq // tq
    tk = vt.shape[2]
    n_tiles = seq // tk
    has_cache = kc is not None
    shared_keys = n_seq > 1 and nq > 1
    if shared_keys:
        assert nq % n_seq == 0
        nq, key_seqs, grid = nq // n_seq, 1, (batch, nq // n_seq)
    else:
        assert n_seq == 1 or (nq == 1 and n_tiles == 1 and not has_cache and batch % n_seq == 0)
        key_seqs, grid = n_seq, (batch // n_seq, nq)
    max_width = max(tk, kc.shape[0] // batch) if has_cache else tk
    in_specs = [pl.BlockSpec((n_seq * tq, H_A * HEAD_PAD), lambda b, i: (b * nq + i, 0)),
                pl.BlockSpec((key_seqs * seq, H_A * HEAD_PAD), lambda b, i: (b, 0)),
                pl.BlockSpec((key_seqs * n_tiles, H_A * V_DIM, tk), lambda b, i: (b, 0, 0))]
    args = [q, k, vt]
    if has_cache:
        past = kc.shape[0] // batch
        in_specs += [pl.BlockSpec((past, H_A * HEAD_PAD), lambda b, i: (b, 0)),
                     pl.BlockSpec((1, H_A * V_DIM, past), lambda b, i: (b, 0, 0))]
        args += [kc, vct]
    return pl.pallas_call(
        functools.partial(_attn_kernel, tiles_per_step=tiles_per_step, has_cache=has_cache, n_seq=n_seq,
                          shared_keys=shared_keys),
        grid=grid,
        in_specs=in_specs,
        out_specs=pl.BlockSpec((n_seq * tq, H_A * V_DIM), lambda b, i: (b * nq + i, 0)),
        out_shape=jax.ShapeDtypeStruct((batch * seq, H_A * V_DIM), BF16),
        scratch_shapes=[pltpu.VMEM((n_seq * H_A * HEAD_PAD, tq), BF16),
                        pltpu.VMEM((n_seq * H_A, V_DIM + ONES_ROWS, tq), F32),
                        pltpu.VMEM((n_seq * H_A, tq), F32),
                        pltpu.VMEM((n_seq * H_A * V_DIM, tq), F32),
                        pltpu.VMEM((QK_LOOKAHEAD + 1, max_width, tq), F32)],
        compiler_params=_cparams(("arbitrary", "arbitrary")),
        name="mla_attention",
    )(*args)


def _stage_c_kernel(x_ref, mod_ref, hf_ref, hb_ref, og_ref, attn_ref, ga_ref, gb_ref, hn_ref,
                    post1_ref, pre2_ref, post2_ref, wmo_ref, wao_ref, wout_ref, w1_ref, w2_ref, y_ref):
    mod = mod_ref[0]
    gate1, shift2, scale2, gate2 = mod[2:3], mod[3:4], mod[4:5], mod[5:6]
    hn = hn_ref[...]
    rows = ROW_GROUP
    groups = [slice(r * rows, (r + 1) * rows) for r in range(x_ref.shape[0] // rows)]
    n_ff = D_FF // D_MODEL

    def mixer_in(rs):
        hm = hf_ref[rs, :] + hb_ref[rs, :]
        heads = [_rms(hm[:, hd * DH_M:(hd + 1) * DH_M], hn[:, hd * DH_M:(hd + 1) * DH_M]) for hd in range(H_M)]
        return (jnp.concatenate(heads, axis=1) * og_ref[rs, :].astype(F32)).astype(BF16)

    def branch_out(hm, rs):
        return _bdot(hm, wmo_ref[...]), _bdot(attn_ref[rs, :], wao_ref[...])

    def merge(y_m, y_a, rs):
        return (ga_ref[rs, :].astype(F32) * y_m + gb_ref[rs, :].astype(F32) * y_a).astype(BF16)

    def mid(mix, rs):
        x1 = x_ref[rs, :] + gate1 * _rms(mix, post1_ref[...])
        return x1, (_rms(x1, pre2_ref[...]) * (1.0 + scale2) + shift2).astype(BF16)

    def mlp(h2):
        up = lambda c: _bdot(h2, w1_ref[:, c * D_MODEL:(c + 1) * D_MODEL])
        ff = jnp.zeros((rows, D_MODEL), F32)
        nxt = up(0)
        for c in range(n_ff):
            a = jnp.maximum(nxt, 0.0)
            if c + 1 < n_ff:
                nxt = up(c + 1)
            ff = ff + _bdot((a * a).astype(BF16), w2_ref[c * D_MODEL:(c + 1) * D_MODEL, :])
        return ff

    hms = [mixer_in(rs) for rs in groups]
    ys = [branch_out(hm, rs) for hm, rs in zip(hms, groups)]
    mixes = [_bdot(merge(*y, rs), wout_ref[...]) for y, rs in zip(ys, groups)]
    mids = [mid(mix, rs) for mix, rs in zip(mixes, groups)]
    ffs = [mlp(h2) for _, h2 in mids]
    for (x1, _), ff, rs in zip(mids, ffs, groups):
        y_ref[rs, :] = x1 + gate2 * _rms(ff, post2_ref[...])


def _stage_c(x, mods, mod_index, hf, hb, og, attn, ga, gb, hn, post1, pre2, post2, wmo, wao, wout, w1, w2, tm):
    n = x.shape[0]
    tile = lambda w: pl.BlockSpec((tm, w), lambda i: (i, 0))
    return pl.pallas_call(
        _stage_c_kernel,
        grid=(n // tm,),
        in_specs=[tile(D_MODEL),
                  pl.BlockSpec((1, N_MOD, D_MODEL), lambda i: (mod_index(i), 0, 0)),
                  tile(MLSTM_W), tile(MLSTM_W), tile(MLSTM_W), tile(H_A * V_DIM),
                  tile(D_MODEL), tile(D_MODEL),
                  _const_spec((1, MLSTM_W)), _const_spec((1, D_MODEL)), _const_spec((1, D_MODEL)),
                  _const_spec((1, D_MODEL)),
                  _const_spec((MLSTM_W, D_MODEL)), _const_spec((H_A * V_DIM, D_MODEL)),
                  _const_spec((D_MODEL, D_MODEL)), _const_spec((D_MODEL, D_FF)), _const_spec((D_FF, D_MODEL))],
        out_specs=tile(D_MODEL),
        out_shape=jax.ShapeDtypeStruct((n, D_MODEL), F32),
        compiler_params=_cparams(("arbitrary",)),
        name="stage_c",
    )(x, mods, hf, hb, og, attn, ga, gb, hn, post1, pre2, post2, wmo, wao, wout, w1, w2)


def _rope_tables(n_tokens):
    pos = np.arange(n_tokens)
    row = (pos // GRID_W).astype(np.float32)
    col = (pos % GRID_W).astype(np.float32)
    inv = (ROPE_BASE ** (-np.arange(0, AX_DIM, 2, dtype=np.float32) / AX_DIM)).astype(np.float32)
    ang = np.concatenate([row[:, None] * inv, col[:, None] * inv], axis=-1)
    cos = np.cos(ang.astype(np.float64)).astype(np.float32)
    sin = np.sin(ang.astype(np.float64)).astype(np.float32)
    ones = np.ones((n_tokens, NOPE), np.float32)
    zeros = np.zeros((n_tokens, NOPE), np.float32)
    pad = np.zeros((n_tokens, HEAD_PAD - NOPE - ROPE_DIM), np.float32)
    ta = np.concatenate([ones, cos, cos, pad], axis=-1)
    tb = np.concatenate([zeros, -sin, sin, pad], axis=-1)
    return jnp.asarray(ta), jnp.asarray(tb)


def _plain_tables(n_tokens):
    ones = jnp.ones((n_tokens, NOPE + ROPE_DIM), F32)
    pad = jnp.zeros((n_tokens, HEAD_PAD - NOPE - ROPE_DIM), F32)
    return jnp.concatenate([ones, pad], axis=-1), jnp.zeros((n_tokens, HEAD_PAD), F32)


def kernel(x_prompt, x_sample, cache_mla_ckv, cache_mla_krope, state_mlstm_C, state_mlstm_n, state_mlstm_m,
           c, c_ctx, w_ada, b_ada, norm_pre1, norm_post1, norm_pre2, norm_post2, w_in, mlstm_gate_b,
           mla_q_norm, mla_kv_norm, w_uq, w_ukv, w_mla_o, mlstm_head_norm, w_mlstm_o, w_out, w_mlp1, w_mlp2):
    bp, sp, _ = x_prompt.shape
    bs, ss, _ = x_sample.shape
    depth = w_in.shape[0]
    past = cache_mla_ckv.shape[2]
    nj = 2 * H_M
    even = np.arange(0, ROPE_DIM, 2)
    odd = np.arange(1, ROPE_DIM, 2)
    perm = np.concatenate([even, odd])
    perm_sw = np.concatenate([odd, even])

    xp = x_prompt.reshape(bp * sp, D_MODEL)
    xs = x_sample.reshape(bs * ss, D_MODEL)
    cc = jnp.zeros((8, D_MODEL), F32).at[:bs].set(c).at[bs].set(c_ctx)
    ta_lat, tb_lat = _rope_tables(ss)
    assert past == ATTN_KEY_TILE
    ta_ctx, tb_ctx = _plain_tables(TM_A)

    new_ckv, new_krope, new_c, new_n, new_m = [], [], [], [], []
    for l in range(depth):
        cols = np.cumsum((W_BIG_COLS, N_GATES, Q_LORA, KV_LORA, ROPE_DIM))
        w_big, w_g, w_cq, w_ckv, w_kr, w_merge = jnp.split(w_in[l].astype(BF16), cols.tolist(), axis=1)
        w_g = w_g.reshape(D_MODEL, 2, 2, H_M).transpose(0, 2, 1, 3).reshape(D_MODEL, N_GATES)
        w_misc = jnp.concatenate([w_kr, w_g, jnp.zeros((D_MODEL, MISC_W - 3 * ROPE_DIM - N_GATES), BF16),
                                  w_kr[:, perm], w_kr[:, perm_sw]], axis=1)
        wa = (w_big, w_merge, jnp.concatenate([w_cq, w_ckv, w_misc], axis=1))
        gbias = jnp.zeros((1, MISC_W), F32).at[0, GATE_LANE0:GATE_LANE0 + N_GATES].set(
            mlstm_gate_b[l].transpose(1, 0, 2).reshape(N_GATES))
        uq = w_uq[l]
        wq = jnp.concatenate([uq[..., :NOPE], uq[..., NOPE:][..., perm], uq[..., NOPE:][..., perm_sw]],
                             axis=-1).reshape(Q_LORA, H_A * HEAD_PAD).astype(BF16)
        ukv = w_ukv[l]
        wk = jnp.concatenate([ukv[..., :NOPE], jnp.zeros((KV_LORA, H_A, HEAD_PAD - NOPE), F32)],
                             axis=-1).reshape(KV_LORA, H_A * HEAD_PAD).astype(BF16)
        wv = ukv[..., NOPE:].reshape(KV_LORA, H_A * V_DIM).T.astype(BF16)
        wmo = w_mlstm_o[l].astype(BF16)
        wao = w_mla_o[l].astype(BF16)
        wout = w_out[l].astype(BF16)
        w1 = w_mlp1[l].astype(BF16)
        w2 = w_mlp2[l].astype(BF16)
        pre1, post1 = norm_pre1[l][None], norm_post1[l][None]
        pre2, post2 = norm_pre2[l][None], norm_post2[l][None]
        qn, kvn = mla_q_norm[l][None], mla_kv_norm[l][None]
        hn = mlstm_head_norm[l].reshape(1, MLSTM_W)

        mods = _modulation(cc, w_ada[l], b_ada[l][None]).reshape(8, N_MOD, D_MODEL)

        ctx_mod = lambda i: bs
        a = _stage_a(xp, mods, ctx_mod, pre1, wa, gbias, qn, kvn, wq, wk, wv, ta_ctx, tb_ctx,
                     lambda i: 0, TM_A, sp)
        mq, mk, mv, og, ga, gb, gates, q, k, v, ckv_n, kro = a
        hf, hb, c_fin, n_fin, m_fin = _mlstm(mq, mk, mv, *_gate_rows(gates, GATE_CHUNKS_PER_STEP),
                                             None, None, bp, sp)
        attn = _attention(q, k, v, None, None, bp, sp, sp, 1, n_seq=ATTN_SEQS_PER_STEP)
        xp = _stage_c(xp, mods, ctx_mod, hf, hb, og, attn, ga, gb, hn, post1, pre2, post2,
                      wmo, wao, wout, w1, w2, TM_C)
        new_ckv.append(ckv_n.reshape(bp, sp, KV_LORA))
        new_krope.append(kro.reshape(bp, sp, ROPE_DIM))
        new_c.append(c_fin.reshape(bp, 2, H_M, DH_M, DH_M))
        new_n.append(n_fin.reshape(bp, 2, H_M, DH_M))
        new_m.append(m_fin[:, :, 0].reshape(bp, 2, H_M))

        tiles_a, tiles_c = ss // TM_A, ss // TM_C
        a = _stage_a(xs, mods, lambda i: i // tiles_a, pre1, wa, gbias, qn, kvn, wq, wk, wv, ta_lat, tb_lat,
                     lambda i: i % tiles_a, TM_A, ATTN_KEY_TILE)
        mq, mk, mv, og, ga, gb, gates, q, k, v, _, _ = a
        kr_cache = jnp.zeros((bs * past, LANE), F32).at[:, NOPE:NOPE + ROPE_DIM].set(
            cache_mla_krope[:, l].reshape(bs * past, ROPE_DIM)[:, perm])
        kc, vc = _kv_cache(cache_mla_ckv[:, l].reshape(bs * past, KV_LORA), kr_cache, wk, wv, past)
        n0 = state_mlstm_n[:, l].reshape(bs, nj, DH_M, 1)
        c0 = jnp.concatenate([state_mlstm_C[:, l].reshape(bs, nj, DH_M, DH_M),
                              jnp.broadcast_to(n0, (bs, nj, DH_M, DH_M))], axis=-1)
        m0 = jnp.broadcast_to(state_mlstm_m[:, l].reshape(bs, nj, 1), (bs, nj, LANE))
        hf, hb, _, _, _ = _mlstm(mq, mk, mv, *_gate_rows(gates, GATE_CHUNKS_PER_STEP), c0, m0, bs, ss)
        attn = _attention(q, k, v, kc, vc, bs, ss, ATTN_QUERY_TILE, KEY_TILES_PER_STEP,
                          n_seq=ATTN_QUERY_TILES_PER_STEP)
        xs = _stage_c(xs, mods, lambda i: i // tiles_c, hf, hb, og, attn, ga, gb, hn, post1, pre2, post2,
                      wmo, wao, wout, w1, w2, TM_C)

    return (xp.reshape(bp, sp, D_MODEL), xs.reshape(bs, ss, D_MODEL),
            jnp.stack(new_ckv, axis=1), jnp.stack(new_krope, axis=1), jnp.stack(new_c, axis=1),
            jnp.stack(new_n, axis=1), jnp.stack(new_m, axis=1))
```

```python
import functools

import numpy as np
import jax
import jax.numpy as jnp
from jax import lax
from jax.experimental import pallas as pl
from jax.experimental.pallas import tpu as pltpu

F32 = jnp.float32
BF16 = jnp.bfloat16

D_MODEL = 1024
H_M = 4
DH_M = 128
MLSTM_W = H_M * DH_M
CHUNK = 128
H_A = 8
NOPE = 64
ROPE_DIM = 32
V_DIM = 64
Q_LORA = 384
KV_LORA = 256
AX_DIM = ROPE_DIM // 2
ROPE_BASE = 10000.0
GRID_W = 64
D_FF = 4 * D_MODEL
N_MOD = 6
EPS = 1e-6
N_GATES = 4 * H_M
M_INIT = -1e30

LANE = 128
HEAD_PAD = LANE
MISC_W = LANE
GATE_LANE0 = ROPE_DIM
OFF_MQ, OFF_MK, OFF_MV, OFF_MO, W_BIG_COLS = 0, MLSTM_W, 2 * MLSTM_W, 3 * MLSTM_W, 4 * MLSTM_W
OFF_CQ, OFF_CKV, OFF_MISC = 0, Q_LORA, Q_LORA + KV_LORA
W_SMALL_COLS = OFF_MISC + MISC_W

VMEM_LIMIT = 60 * 1024 * 1024
QK_SCALE = float((NOPE + ROPE_DIM) ** -0.5 * np.log2(np.e))
QK_LOOKAHEAD = 3
ATTN_KEY_TILE = 512
ATTN_SEQS_PER_STEP = 8
ATTN_QUERY_TILES_PER_STEP = 2
ATTN_QUERY_TILE = 256
ROW_GROUP = 256
TM_A, TM_C = 512, 512
KEY_TILES_PER_STEP = 3
ONES_ROWS = 16


def _cparams(sem):
    return pltpu.CompilerParams(dimension_semantics=sem, vmem_limit_bytes=VMEM_LIMIT)


def _const_spec(shape):
    nd = len(shape)
    return pl.BlockSpec(shape, lambda *_: (0,) * nd, pipeline_mode=pl.Buffered(1))


def _rms(x, w):
    return x * lax.rsqrt(jnp.mean(x * x, axis=-1, keepdims=True) + EPS) * w


def _sigmoid(x):
    return 1.0 / (1.0 + jnp.exp(-x))


def _log_sigmoid(x):
    return jnp.minimum(x, 0.0) - jnp.log(1.0 + jnp.exp(-jnp.abs(x)))


def _bdot(a, b):
    return jnp.dot(a, b, preferred_element_type=F32)


def _mod_kernel(c_ref, w_ref, b_ref, o_ref):
    c = c_ref[...]
    s = c * _sigmoid(c)
    o_ref[...] = jnp.dot(s, w_ref[...], preferred_element_type=F32,
                         precision=lax.Precision.HIGHEST) + b_ref[...]


def _modulation(cc, w_ada, b_ada):
    n_out = w_ada.shape[1]
    tn = 1536
    return pl.pallas_call(
        _mod_kernel,
        grid=(n_out // tn,),
        in_specs=[pl.BlockSpec((8, D_MODEL), lambda j: (0, 0)),
                  pl.BlockSpec((D_MODEL, tn), lambda j: (0, j)),
                  pl.BlockSpec((1, tn), lambda j: (0, j))],
        out_specs=pl.BlockSpec((8, tn), lambda j: (0, j)),
        out_shape=jax.ShapeDtypeStruct((8, n_out), F32),
        compiler_params=_cparams(("arbitrary",)),
        name="modulation",
    )(cc, w_ada, b_ada)


def _stage_a_kernel(x_ref, mod_ref, pre1_ref, wbig_ref, wmerge_ref, wsmall_ref, gbias_ref, qn_ref, kvn_ref,
                    wq_ref, wk_ref, wv_ref,
                    ta_ref, tb_ref,
                    mq_ref, mk_ref, mv_ref, og_ref, ga_ref, gb_ref, gates_ref, q_ref, k_ref, v_ref,
                    ckv_ref, kro_ref):
    mod = mod_ref[0]
    shift1, scale1 = mod[0:1], mod[1:2]
    tm = x_ref.shape[0]
    rows = ROW_GROUP
    tkv = v_ref.shape[2]

    def normed(rs):
        return (_rms(x_ref[rs, :], pre1_ref[...]) * (1.0 + scale1) + shift1).astype(BF16)

    def small_proj(hb):
        return tuple(_bdot(hb, wsmall_ref[:, lo:hi])
                     for lo, hi in ((OFF_CQ, OFF_CKV), (OFF_CKV, OFF_MISC), (OFF_MISC, W_SMALL_COLS)))

    def big_proj(hb, rs):
        proj = lambda lo, hi: _bdot(hb, wbig_ref[:, lo:hi])
        mq_ref[rs, :] = (proj(OFF_MQ, OFF_MK) * DH_M ** -0.5).astype(BF16)
        mk_ref[:, rs] = proj(OFF_MK, OFF_MV).T.astype(BF16)
        mv_ref[rs, :] = proj(OFF_MV, OFF_MO).astype(BF16)
        og_ref[rs, :] = _sigmoid(proj(OFF_MO, W_BIG_COLS)).astype(BF16)
        ga_ref[rs, :] = _sigmoid(_bdot(hb, wmerge_ref[:, :D_MODEL])).astype(BF16)
        gb_ref[rs, :] = _sigmoid(_bdot(hb, wmerge_ref[:, D_MODEL:])).astype(BF16)

    def mla_build(cq, ckv, misc, rs):
        lane = lax.broadcasted_iota(jnp.int32, misc.shape, 1)
        g = misc + gbias_ref[...]
        is_f = (lane >= GATE_LANE0 + 2 * H_M) & (lane < GATE_LANE0 + N_GATES)
        gates_ref[rs, :] = jnp.where(is_f, _log_sigmoid(g), g)
        kro_ref[rs, :] = misc[:, 0:ROPE_DIM]

        ckv_n = _rms(ckv, kvn_ref[...])
        ckv_ref[rs, :] = ckv_n
        ckv_b = ckv_n.astype(BF16)

        ta = ta_ref[rs, :]
        tb = tb_ref[rs, :]
        qf = _bdot(_rms(cq, qn_ref[...]).astype(BF16), wq_ref[...])
        ta8 = jnp.concatenate([ta] * H_A, axis=1)
        tb8 = jnp.concatenate([tb] * H_A, axis=1)
        q = qf * ta8 + pltpu.roll(qf, H_A * HEAD_PAD - ROPE_DIM, 1) * tb8
        q_ref[rs, :] = (q * QK_SCALE).astype(BF16)

        ta_k = jnp.where(lane < NOPE, 0.0, ta)
        kr = misc * ta_k + pltpu.roll(misc, MISC_W - ROPE_DIM, 1) * tb
        kk = _bdot(ckv_b, wk_ref[...]) + jnp.concatenate([kr] * H_A, axis=1)
        k_ref[rs, :] = kk.astype(BF16)
        vt = _bdot(wv_ref[...], ckv_n.T.astype(BF16)).astype(BF16)
        step = min(rows, tkv)
        for off in range(0, rows, step):
            lo = rs.start + off
            v_ref[lo // tkv, :, lo % tkv:lo % tkv + step] = vt[:, off:off + step]

    groups = [slice(r * rows, (r + 1) * rows) for r in range(tm // rows)]
    hbs = [normed(rs) for rs in groups]
    for hb, rs in zip(hbs, groups):
        small = small_proj(hb)
        big_proj(hb, rs)
        mla_build(*small, rs)


def _stage_a(x, mods, mod_index, pre1, wa, gbias, qn, kvn, wq, wk, wv, ta, tb, table_index, tm, tkv):
    n = x.shape[0]
    tile = lambda w: pl.BlockSpec((tm, w), lambda i: (i, 0))
    bf = lambda w: jax.ShapeDtypeStruct((n, w), BF16)
    f32 = lambda w: jax.ShapeDtypeStruct((n, w), F32)
    return pl.pallas_call(
        _stage_a_kernel,
        grid=(n // tm,),
        in_specs=[tile(D_MODEL),
                  pl.BlockSpec((1, N_MOD, D_MODEL), lambda i: (mod_index(i), 0, 0)),
                  _const_spec((1, D_MODEL)),
                  _const_spec((D_MODEL, W_BIG_COLS)),
                  _const_spec((D_MODEL, 2 * D_MODEL)),
                  _const_spec((D_MODEL, W_SMALL_COLS)),
                  _const_spec((1, MISC_W)),
                  _const_spec((1, Q_LORA)),
                  _const_spec((1, KV_LORA)),
                  _const_spec((Q_LORA, H_A * HEAD_PAD)),
                  _const_spec((KV_LORA, H_A * HEAD_PAD)),
                  _const_spec((H_A * V_DIM, KV_LORA)),
                  pl.BlockSpec((tm, LANE), lambda i: (table_index(i), 0)),
                  pl.BlockSpec((tm, LANE), lambda i: (table_index(i), 0))],
        out_specs=[tile(MLSTM_W), pl.BlockSpec((MLSTM_W, tm), lambda i: (0, i)), tile(MLSTM_W), tile(MLSTM_W),
                   tile(D_MODEL), tile(D_MODEL), tile(MISC_W),
                   tile(H_A * HEAD_PAD), tile(H_A * HEAD_PAD),
                   pl.BlockSpec((tm // tkv, H_A * V_DIM, tkv), lambda i: (i, 0, 0)),
                   tile(KV_LORA), tile(ROPE_DIM)],
        out_shape=[bf(MLSTM_W), jax.ShapeDtypeStruct((MLSTM_W, n), BF16), bf(MLSTM_W), bf(MLSTM_W),
                   bf(D_MODEL), bf(D_MODEL), f32(MISC_W),
                   bf(H_A * HEAD_PAD), bf(H_A * HEAD_PAD),
                   jax.ShapeDtypeStruct((n // tkv, H_A * V_DIM, tkv), BF16),
                   f32(KV_LORA), f32(ROPE_DIM)],
        compiler_params=_cparams(("arbitrary",)),
        name="stage_a",
    )(x, mods, pre1, *wa, gbias, qn, kvn, wq, wk, wv, ta, tb)


def _kv_cache_kernel(ckv_ref, kr_ref, wk_ref, wv_ref, k_ref, v_ref):
    ckv_b = ckv_ref[...].astype(BF16)
    kk = _bdot(ckv_b, wk_ref[...]) + jnp.concatenate([kr_ref[...]] * H_A, axis=1)
    k_ref[...] = kk.astype(BF16)
    v_ref[0] = _bdot(wv_ref[...], ckv_ref[...].T.astype(BF16)).astype(BF16)


def _kv_cache(ckv, kr, wk, wv, tm):
    n = ckv.shape[0]
    return pl.pallas_call(
        _kv_cache_kernel,
        grid=(n // tm,),
        in_specs=[pl.BlockSpec((tm, KV_LORA), lambda i: (i, 0)),
                  pl.BlockSpec((tm, LANE), lambda i: (i, 0)),
                  _const_spec((KV_LORA, H_A * HEAD_PAD)),
                  _const_spec((H_A * V_DIM, KV_LORA))],
        out_specs=[pl.BlockSpec((tm, H_A * HEAD_PAD), lambda i: (i, 0)),
                   pl.BlockSpec((1, H_A * V_DIM, tm), lambda i: (i, 0, 0))],
        out_shape=[jax.ShapeDtypeStruct((n, H_A * HEAD_PAD), BF16),
                   jax.ShapeDtypeStruct((n // tm, H_A * V_DIM, tm), BF16)],
        compiler_params=_cparams(("arbitrary",)),
        name="kv_cache",
    )(ckv, kr, wk, wv)


GATE_ROWS = 32
MLSTM_SEQS_PER_STEP = 4
MLSTM_CHUNKS_PER_STEP = 8
GATE_CHUNKS_PER_STEP = 32


def _gate_rows_kernel(g_ref, cols_ref, rows_ref):
    nj = 2 * H_M
    nch = g_ref.shape[0] // CHUNK
    n_rows = nch * nj
    row = lax.broadcasted_iota(jnp.int32, (CHUNK, CHUNK), 0)
    col = lax.broadcasted_iota(jnp.int32, (CHUNK, CHUNK), 1)
    lower = (col <= row).astype(F32)
    upper = (col >= row).astype(F32)
    hi = lax.Precision.HIGHEST
    rowi = lax.broadcasted_iota(jnp.int32, (n_rows, CHUNK), 0)
    lane = lax.broadcasted_iota(jnp.int32, (n_rows, CHUNK), 1)
    is_fwd = rowi % nj < H_M

    gi, gf = GATE_LANE0, GATE_LANE0 + nj
    i_rows, f_rows = [], []
    for c in range(nch):
        g_t = g_ref[c * CHUNK:(c + 1) * CHUNK, :].T
        i_rows.append(g_t[gi:gi + nj])
        f_rows.append(g_t[gf:gf + nj])
    i_all = jnp.concatenate(i_rows, axis=0)
    f_all = jnp.concatenate(f_rows, axis=0)
    b = jnp.where(is_fwd, jnp.dot(f_all, upper, preferred_element_type=F32, precision=hi),
                  jnp.dot(f_all, lower, preferred_element_type=F32, precision=hi))
    a = i_all - b
    cm = a
    shift = 1
    while shift < CHUNK:
        y_f = jnp.where(lane >= shift, pltpu.roll(cm, shift, 1), -jnp.inf)
        y_b = jnp.where(lane < CHUNK - shift, pltpu.roll(cm, CHUNK - shift, 1), -jnp.inf)
        cm = jnp.maximum(cm, jnp.where(is_fwd, y_f, y_b))
        shift *= 2
    a_max = jnp.broadcast_to(jnp.max(a, axis=1, keepdims=True), a.shape)
    f_sum = jnp.broadcast_to(jnp.sum(f_all, axis=1, keepdims=True), a.shape)
    zeros8 = jnp.zeros((nj, CHUNK), F32)
    for c in range(nch):
        sl = slice(c * nj, (c + 1) * nj)
        stack = jnp.concatenate([cm[sl], b[sl]] + [zeros8] * (CHUNK // nj - 2), axis=0)
        cols_ref[c * CHUNK:(c + 1) * CHUNK, :] = stack.T
        rows_ref[c] = jnp.concatenate([a[sl], a_max[sl], f_sum[sl], zeros8], axis=0)


def _gate_rows(gates, chunks_per_step):
    n = gates.shape[0]
    tm = chunks_per_step * CHUNK
    return pl.pallas_call(
        _gate_rows_kernel,
        grid=(n // tm,),
        in_specs=[pl.BlockSpec((tm, MISC_W), lambda i: (i, 0))],
        out_specs=[pl.BlockSpec((tm, LANE), lambda i: (i, 0)),
                   pl.BlockSpec((chunks_per_step, GATE_ROWS, CHUNK), lambda i: (i, 0, 0))],
        out_shape=[jax.ShapeDtypeStruct((n, LANE), F32),
                   jax.ShapeDtypeStruct((n // CHUNK, GATE_ROWS, CHUNK), F32)],
        compiler_params=_cparams(("arbitrary",)),
        name="gate_rows",
    )(gates)


def _mlstm_kernel(*refs, has_state, n_seq):
    if has_state:
        (qf_ref, ktf_ref, vf_ref, colsf_ref, rf_ref, qb_ref, ktb_ref, vb_ref, colsb_ref, rb_ref, c0_ref, m0_ref,
         hf_ref, hb_ref, cn_ref, nn_ref, mn_ref, c_s, m_s) = refs
    else:
        (qf_ref, ktf_ref, vf_ref, colsf_ref, rf_ref, qb_ref, ktb_ref, vb_ref, colsb_ref, rb_ref,
         hf_ref, hb_ref, cn_ref, nn_ref, mn_ref, c_s, m_s) = refs
    step = pl.program_id(1)

    nj = 2 * H_M

    @pl.when(step == 0)
    def _():
        if has_state:
            for sq in range(n_seq):
                c_s[sq * nj:(sq + 1) * nj] = c0_ref[sq]
                m_s[sq * nj:(sq + 1) * nj] = m0_ref[sq]
        else:
            c_s[...] = jnp.zeros(c_s.shape, F32)
            m_s[...] = jnp.full(m_s.shape, M_INIT, F32)

    row = lax.broadcasted_iota(jnp.int32, (CHUNK, CHUNK), 0)
    col = lax.broadcasted_iota(jnp.int32, (CHUNK, CHUNK), 1)
    is_fwd = lax.broadcasted_iota(jnp.int32, (nj, CHUNK), 0) < H_M
    ones_v = jnp.ones((CHUNK, DH_M), BF16)
    dirs = ((qf_ref, ktf_ref, vf_ref, colsf_ref, hf_ref), (qb_ref, ktb_ref, vb_ref, colsb_ref, hb_ref))
    chains = [(sq, d, hd) for sq in range(n_seq) for d in range(2) for hd in range(H_M)]
    n_sub = rf_ref.shape[0] // n_seq
    for sub in range(n_sub):
        sub_of = (sub, n_sub - 1 - sub)
        gate = []
        for sq in range(n_seq):
            rows_f, rows_b = rf_ref[sq * n_sub + sub_of[0]], rb_ref[sq * n_sub + sub_of[1]]
            pick = lambda g: jnp.where(is_fwd, rows_f[g * nj:(g + 1) * nj], rows_b[g * nj:(g + 1) * nj])
            a8, a_max8, f_sum8 = pick(0), pick(1), pick(2)
            m8 = m_s[sq * nj:(sq + 1) * nj]
            mx8 = jnp.maximum(m8, a_max8)
            gate.append((a8, m8, jnp.exp(a8 - mx8), jnp.exp(m8 - mx8)))
            m_s[sq * nj:(sq + 1) * nj] = f_sum8 + mx8
        tok = lambda sq, d: slice((sq * n_sub + sub_of[d]) * CHUNK, (sq * n_sub + sub_of[d] + 1) * CHUNK)

        first = []
        for sq, d, hd in chains:
            q_ref, kt_ref, v_ref, _, _ = dirs[d]
            j = d * H_M + hd
            sl = slice(hd * DH_M, (hd + 1) * DH_M)
            q = q_ref[tok(sq, d), sl]
            k_t = kt_ref[sl, tok(sq, d)]
            v_aug = jnp.concatenate([v_ref[tok(sq, d), sl], ones_v], axis=1)
            qk = _bdot(q, k_t)
            c_st = c_s[sq * nj + j]
            qc = _bdot(q, c_st.astype(BF16))
            upd = _bdot((k_t.astype(F32) * gate[sq][2][j:j + 1, :]).astype(BF16), v_aug)
            dec = gate[sq][3][j:j + 1, :]
            c_s[sq * nj + j] = jnp.concatenate([dec, dec], axis=1) * c_st + upd
            first.append((qk, qc, v_aug))
        for (sq, d, hd), (qk, qc, v_aug) in zip(chains, first):
            cols_ref, h_ref = dirs[d][3], dirs[d][4]
            j = d * H_M + hd
            sl = slice(hd * DH_M, (hd + 1) * DH_M)
            mask = (col <= row) if d == 0 else (col >= row)
            a8, m8 = gate[sq][0], gate[sq][1]
            m_row = m8[j:j + 1, :]
            u = jnp.maximum(m_row, cols_ref[tok(sq, d), j:j + 1])
            w = jnp.exp(jnp.where(mask, a8[j:j + 1, :] - u, -jnp.inf))
            s_inter = jnp.exp(m_row - u)
            intra = _bdot((qk * w).astype(BF16), v_aug)
            num = s_inter * qc[:, :DH_M] + intra[:, :DH_M]
            den = s_inter * qc[:, DH_M:] + intra[:, DH_M:]
            h_ref[tok(sq, d), sl] = num / jnp.maximum(
                jnp.abs(den), jnp.exp(-(cols_ref[tok(sq, d), nj + j:nj + j + 1] + u)))

    @pl.when(step == pl.num_programs(1) - 1)
    def _():
        for sq in range(n_seq):
            for j in range(nj):
                c_aug = c_s[sq * nj + j]
                cn_ref[sq, j] = c_aug[:, :DH_M]
                nn_ref[sq, j:j + 1, :] = c_aug[:, DH_M:].T[0:1, :]
            mn_ref[sq] = m_s[sq * nj:(sq + 1) * nj]


def _mlstm(mq, mkt, mv, gate_cols, gate_rows, c0, m0, batch, seq):
    has_state = c0 is not None
    n_sub = min(MLSTM_CHUNKS_PER_STEP, seq // CHUNK)
    nc = seq // (n_sub * CHUNK)
    n_seq = MLSTM_SEQS_PER_STEP if nc == 1 and batch % MLSTM_SEQS_PER_STEP == 0 else 1
    tm = n_seq * n_sub * CHUNK
    n = batch * seq
    nj = 2 * H_M
    fwd = lambda w: pl.BlockSpec((tm, w), lambda b, c: (b * nc + c, 0))
    bwd = lambda w: pl.BlockSpec((tm, w), lambda b, c: (b * nc + nc - 1 - c, 0))
    fwd_t = pl.BlockSpec((MLSTM_W, tm), lambda b, c: (0, b * nc + c))
    bwd_t = pl.BlockSpec((MLSTM_W, tm), lambda b, c: (0, b * nc + nc - 1 - c))
    st_m = pl.BlockSpec((n_seq, nj, LANE), lambda b, c: (b, 0, 0))
    st_c = pl.BlockSpec((n_seq, nj, DH_M, 2 * DH_M), lambda b, c: (b, 0, 0, 0))
    fwd_r = pl.BlockSpec((n_seq * n_sub, GATE_ROWS, CHUNK), lambda b, c: (b * nc + c, 0, 0))
    bwd_r = pl.BlockSpec((n_seq * n_sub, GATE_ROWS, CHUNK), lambda b, c: (b * nc + nc - 1 - c, 0, 0))
    in_specs = [fwd(MLSTM_W), fwd_t, fwd(MLSTM_W), fwd(LANE), fwd_r,
                bwd(MLSTM_W), bwd_t, bwd(MLSTM_W), bwd(LANE), bwd_r]
    args = [mq, mkt, mv, gate_cols, gate_rows, mq, mkt, mv, gate_cols, gate_rows]
    if has_state:
        in_specs += [st_c, st_m]
        args += [c0, m0]
    return pl.pallas_call(
        functools.partial(_mlstm_kernel, has_state=has_state, n_seq=n_seq),
        grid=(batch // n_seq, nc),
        in_specs=in_specs,
        out_specs=[fwd(MLSTM_W), bwd(MLSTM_W),
                   pl.BlockSpec((n_seq, nj, DH_M, DH_M), lambda b, c: (b, 0, 0, 0)), st_m, st_m],
        out_shape=[jax.ShapeDtypeStruct((n, MLSTM_W), F32), jax.ShapeDtypeStruct((n, MLSTM_W), F32),
                   jax.ShapeDtypeStruct((batch, nj, DH_M, DH_M), F32),
                   jax.ShapeDtypeStruct((batch, nj, DH_M), F32),
                   jax.ShapeDtypeStruct((batch, nj, LANE), F32)],
        scratch_shapes=[pltpu.VMEM((n_seq * nj, DH_M, 2 * DH_M), F32), pltpu.VMEM((n_seq * nj, LANE), F32)],
        compiler_params=_cparams(("arbitrary", "arbitrary")),
        name="mlstm",
    )(*args)


def _attn_kernel(*refs, tiles_per_step, has_cache, n_seq, shared_keys):
    if has_cache:
        q_ref, k_ref, vt_ref, kc_ref, vct_ref, o_ref, qt_s, acc_s, m_s, ot_s, s_scr = refs
    else:
        q_ref, k_ref, vt_ref, o_ref, qt_s, acc_s, m_s, ot_s, s_scr = refs
    tk = vt_ref.shape[2]
    n_tiles = vt_ref.shape[0] // (1 if shared_keys else n_seq)
    tq = q_ref.shape[0] // n_seq
    d_q, d_o = H_A * HEAD_PAD, H_A * V_DIM
    for sq in range(n_seq):
        qt_s[sq * d_q:(sq + 1) * d_q, :] = q_ref[sq * tq:(sq + 1) * tq, :].astype(F32).T.astype(BF16)
    acc_s[...] = jnp.zeros(acc_s.shape, F32)
    m_s[...] = jnp.full(m_s.shape, -jnp.inf, F32)

    n_slots = s_scr.shape[0]

    def scores(tile, head, slot):
        get_k, _, width, sq = tile
        stream = sq * H_A + head
        s = _bdot(get_k(head), qt_s[stream * HEAD_PAD:(stream + 1) * HEAD_PAD, :])
        s_scr[slot, 0:width, :] = s
        return jnp.max(s, axis=0, keepdims=True)

    def update(tiles, pending, next_tile):
        items = [(tile, head) for tile in tiles for head in range(H_A)]
        ahead = items + ([(next_tile, h) for h in range(QK_LOOKAHEAD)] if next_tile is not None else [])
        pending = list(pending)
        for idx, ((_, get_vt, width, sq), head) in enumerate(items):
            stream = sq * H_A + head
            m_tile = pending.pop(0)
            if idx + QK_LOOKAHEAD < len(ahead):
                pending.append(scores(*ahead[idx + QK_LOOKAHEAD], (idx + QK_LOOKAHEAD) % n_slots))
            m_old = m_s[stream:stream + 1, :]
            m_new = jnp.maximum(m_old, m_tile)
            alpha = jnp.exp2(m_old - m_new)
            p = jnp.exp2((s_scr[idx % n_slots, 0:width, :] - m_new).astype(BF16))
            ones = jnp.ones((ONES_ROWS, width), BF16)
            lhs = jnp.concatenate([get_vt(head), ones], axis=0)
            acc_s[stream] = alpha * acc_s[stream] + _bdot(lhs, p)
            m_s[stream:stream + 1, :] = m_new
        return tuple(pending)

    def main_tile(t, sq=0):
        base = 0 if shared_keys else sq * n_tiles
        start = (base + t) * tk if isinstance(t, int) else pl.multiple_of(t * tk, tk)
        rows = pl.ds(start, tk)
        return (lambda h: k_ref[rows, h * HEAD_PAD:(h + 1) * HEAD_PAD],
                lambda h: vt_ref[base + t, h * V_DIM:(h + 1) * V_DIM, :], tk, sq)

    def body(sq, i, pending):
        first = i * tiles_per_step
        return update([main_tile(first + u, sq) for u in range(tiles_per_step)], pending,
                      main_tile(first + tiles_per_step, sq))

    n_loop = (n_tiles - 1) // tiles_per_step

    def tail_tiles(sq):
        tiles = [main_tile(t, sq) for t in range(n_loop * tiles_per_step, n_tiles)]
        if has_cache:
            tiles.append((lambda h: kc_ref[:, h * HEAD_PAD:(h + 1) * HEAD_PAD],
                          lambda h: vct_ref[0, h * V_DIM:(h + 1) * V_DIM, :], kc_ref.shape[0], sq))
        return tiles

    pending = tuple(scores(main_tile(0, 0), h, h) for h in range(QK_LOOKAHEAD))
    if n_loop:
        assert (tiles_per_step * H_A) % n_slots == 0 and (len(tail_tiles(0)) * H_A) % n_slots == 0
        for sq in range(n_seq):
            pending = lax.fori_loop(0, n_loop, functools.partial(body, sq), pending)
            pending = update(tail_tiles(sq), pending, main_tile(0, sq + 1) if sq + 1 < n_seq else None)
    else:
        update([tile for sq in range(n_seq) for tile in tail_tiles(sq)], pending, None)
    for stream in range(n_seq * H_A):
        acc = acc_s[stream]
        ot_s[stream * V_DIM:(stream + 1) * V_DIM, :] = acc[0:V_DIM] * (1.0 / acc[V_DIM:V_DIM + 1])
    for sq in range(n_seq):
        o_ref[sq * tq:(sq + 1) * tq, :] = ot_s[sq * d_o:(sq + 1) * d_o, :].T.astype(BF16)


def _attention(q, k, vt, kc, vct, batch, seq, tq, tiles_per_step, n_seq=1):
    nq = seq // tq
    tk = vt.shape[2]
    n_tiles = seq // tk
    has_cache = kc is not None
    shared_keys = n_seq > 1 and nq > 1
    if shared_keys:
        assert nq % n_seq == 0
        nq, key_seqs, grid = nq // n_seq, 1, (batch, nq // n_seq)
    else:
        assert n_seq == 1 or (nq == 1 and n_tiles == 1 and not has_cache and batch % n_seq == 0)
        key_seqs, grid = n_seq, (batch // n_seq, nq)
    max_width = max(tk, kc.shape[0] // batch) if has_cache else tk
    in_specs = [pl.BlockSpec((n_seq * tq, H_A * HEAD_PAD), lambda b, i: (b * nq + i, 0)),
                pl.BlockSpec((key_seqs * seq, H_A * HEAD_PAD), lambda b, i: (b, 0)),
                pl.BlockSpec((key_seqs * n_tiles, H_A * V_DIM, tk), lambda b, i: (b, 0, 0))]
    args = [q, k, vt]
    if has_cache:
        past = kc.shape[0] // batch
        in_specs += [pl.BlockSpec((past, H_A * HEAD_PAD), lambda b, i: (b, 0)),
                     pl.BlockSpec((1, H_A * V_DIM, past), lambda b, i: (b, 0, 0))]
        args += [kc, vct]
    return pl.pallas_call(
        functools.partial(_attn_kernel, tiles_per_step=tiles_per_step, has_cache=has_cache, n_seq=n_seq,
                          shared_keys=shared_keys),
        grid=grid,
        in_specs=in_specs,
        out_specs=pl.BlockSpec((n_seq * tq, H_A * V_DIM), lambda b, i: (b * nq + i, 0)),
        out_shape=jax.ShapeDtypeStruct((batch * seq, H_A * V_DIM), BF16),
        scratch_shapes=[pltpu.VMEM((n_seq * H_A * HEAD_PAD, tq), BF16),
                        pltpu.VMEM((n_seq * H_A, V_DIM + ONES_ROWS, tq), F32),
                        pltpu.VMEM((n_seq * H_A, tq), F32),
                        pltpu.VMEM((n_seq * H_A * V_DIM, tq), F32),
                        pltpu.VMEM((QK_LOOKAHEAD + 1, max_width, tq), F32)],
        compiler_params=_cparams(("arbitrary", "arbitrary")),
        name="mla_attention",
    )(*args)


def _stage_c_kernel(x_ref, mod_ref, hf_ref, hb_ref, og_ref, attn_ref, ga_ref, gb_ref, hn_ref,
                    post1_ref, pre2_ref, post2_ref, wmo_ref, wao_ref, wout_ref, w1_ref, w2_ref, y_ref):
    mod = mod_ref[0]
    gate1, shift2, scale2, gate2 = mod[2:3], mod[3:4], mod[4:5], mod[5:6]
    hn = hn_ref[...]
    rows = ROW_GROUP
    groups = [slice(r * rows, (r + 1) * rows) for r in range(x_ref.shape[0] // rows)]
    n_ff = D_FF // D_MODEL

    def mixer_in(rs):
        hm = hf_ref[rs, :] + hb_ref[rs, :]
        heads = [_rms(hm[:, hd * DH_M:(hd + 1) * DH_M], hn[:, hd * DH_M:(hd + 1) * DH_M]) for hd in range(H_M)]
        return (jnp.concatenate(heads, axis=1) * og_ref[rs, :].astype(F32)).astype(BF16)

    def branch_out(hm, rs):
        return _bdot(hm, wmo_ref[...]), _bdot(attn_ref[rs, :], wao_ref[...])

    def merge(y_m, y_a, rs):
        return (ga_ref[rs, :].astype(F32) * y_m + gb_ref[rs, :].astype(F32) * y_a).astype(BF16)

    def mid(mix, rs):
        x1 = x_ref[rs, :] + gate1 * _rms(mix, post1_ref[...])
        return x1, (_rms(x1, pre2_ref[...]) * (1.0 + scale2) + shift2).astype(BF16)

    def mlp(h2):
        up = lambda c: _bdot(h2, w1_ref[:, c * D_MODEL:(c + 1) * D_MODEL])
        ff = jnp.zeros((rows, D_MODEL), F32)
        nxt = up(0)
        for c in range(n_ff):
            a = jnp.maximum(nxt, 0.0)
            if c + 1 < n_ff:
                nxt = up(c + 1)
            ff = ff + _bdot((a * a).astype(BF16), w2_ref[c * D_MODEL:(c + 1) * D_MODEL, :])
        return ff

    hms = [mixer_in(rs) for rs in groups]
    ys = [branch_out(hm, rs) for hm, rs in zip(hms, groups)]
    mixes = [_bdot(merge(*y, rs), wout_ref[...]) for y, rs in zip(ys, groups)]
    mids = [mid(mix, rs) for mix, rs in zip(mixes, groups)]
    ffs = [mlp(h2) for _, h2 in mids]
    for (x1, _), ff, rs in zip(mids, ffs, groups):
        y_ref[rs, :] = x1 + gate2 * _rms(ff, post2_ref[...])


def _stage_c(x, mods, mod_index, hf, hb, og, attn, ga, gb, hn, post1, pre2, post2, wmo, wao, wout, w1, w2, tm):
    n = x.shape[0]
    tile = lambda w: pl.BlockSpec((tm, w), lambda i: (i, 0))
    return pl.pallas_call(
        _stage_c_kernel,
        grid=(n // tm,),
        in_specs=[tile(D_MODEL),
                  pl.BlockSpec((1, N_MOD, D_MODEL), lambda i: (mod_index(i), 0, 0)),
                  tile(MLSTM_W), tile(MLSTM_W), tile(MLSTM_W), tile(H_A * V_DIM),
                  tile(D_MODEL), tile(D_MODEL),
                  _const_spec((1, MLSTM_W)), _const_spec((1, D_MODEL)), _const_spec((1, D_MODEL)),
                  _const_spec((1, D_MODEL)),
                  _const_spec((MLSTM_W, D_MODEL)), _const_spec((H_A * V_DIM, D_MODEL)),
                  _const_spec((D_MODEL, D_MODEL)), _const_spec((D_MODEL, D_FF)), _const_spec((D_FF, D_MODEL))],
        out_specs=tile(D_MODEL),
        out_shape=jax.ShapeDtypeStruct((n, D_MODEL), F32),
        compiler_params=_cparams(("arbitrary",)),
        name="stage_c",
    )(x, mods, hf, hb, og, attn, ga, gb, hn, post1, pre2, post2, wmo, wao, wout, w1, w2)


def _rope_tables(n_tokens):
    pos = np.arange(n_tokens)
    row = (pos // GRID_W).astype(np.float32)
    col = (pos % GRID_W).astype(np.float32)
    inv = (ROPE_BASE ** (-np.arange(0, AX_DIM, 2, dtype=np.float32) / AX_DIM)).astype(np.float32)
    ang = np.concatenate([row[:, None] * inv, col[:, None] * inv], axis=-1)
    cos = np.cos(ang.astype(np.float64)).astype(np.float32)
    sin = np.sin(ang.astype(np.float64)).astype(np.float32)
    ones = np.ones((n_tokens, NOPE), np.float32)
    zeros = np.zeros((n_tokens, NOPE), np.float32)
    pad = np.zeros((n_tokens, HEAD_PAD - NOPE - ROPE_DIM), np.float32)
    ta = np.concatenate([ones, cos, cos, pad], axis=-1)
    tb = np.concatenate([zeros, -sin, sin, pad], axis=-1)
    return jnp.asarray(ta), jnp.asarray(tb)


def _plain_tables(n_tokens):
    ones = jnp.ones((n_tokens, NOPE + ROPE_DIM), F32)
    pad = jnp.zeros((n_tokens, HEAD_PAD - NOPE - ROPE_DIM), F32)
    return jnp.concatenate([ones, pad], axis=-1), jnp.zeros((n_tokens, HEAD_PAD), F32)


def kernel(x_prompt, x_sample, cache_mla_ckv, cache_mla_krope, state_mlstm_C, state_mlstm_n, state_mlstm_m,
           c, c_ctx, w_ada, b_ada, norm_pre1, norm_post1, norm_pre2, norm_post2, w_in, mlstm_gate_b,
           mla_q_norm, mla_kv_norm, w_uq, w_ukv, w_mla_o, mlstm_head_norm, w_mlstm_o, w_out, w_mlp1, w_mlp2):
    bp, sp, _ = x_prompt.shape
    bs, ss, _ = x_sample.shape
    depth = w_in.shape[0]
    past = cache_mla_ckv.shape[2]
    nj = 2 * H_M
    even = np.arange(0, ROPE_DIM, 2)
    odd = np.arange(1, ROPE_DIM, 2)
    perm = np.concatenate([even, odd])
    perm_sw = np.concatenate([odd, even])

    xp = x_prompt.reshape(bp * sp, D_MODEL)
    xs = x_sample.reshape(bs * ss, D_MODEL)
    cc = jnp.zeros((8, D_MODEL), F32).at[:bs].set(c).at[bs].set(c_ctx)
    ta_lat, tb_lat = _rope_tables(ss)
    assert past == ATTN_KEY_TILE
    ta_ctx, tb_ctx = _plain_tables(TM_A)

    new_ckv, new_krope, new_c, new_n, new_m = [], [], [], [], []
    for l in range(depth):
        cols = np.cumsum((W_BIG_COLS, N_GATES, Q_LORA, KV_LORA, ROPE_DIM))
        w_big, w_g, w_cq, w_ckv, w_kr, w_merge = jnp.split(w_in[l].astype(BF16), cols.tolist(), axis=1)
        w_g = w_g.reshape(D_MODEL, 2, 2, H_M).transpose(0, 2, 1, 3).reshape(D_MODEL, N_GATES)
        w_misc = jnp.concatenate([w_kr, w_g, jnp.zeros((D_MODEL, MISC_W - 3 * ROPE_DIM - N_GATES), BF16),
                                  w_kr[:, perm], w_kr[:, perm_sw]], axis=1)
        wa = (w_big, w_merge, jnp.concatenate([w_cq, w_ckv, w_misc], axis=1))
        gbias = jnp.zeros((1, MISC_W), F32).at[0, GATE_LANE0:GATE_LANE0 + N_GATES].set(
            mlstm_gate_b[l].transpose(1, 0, 2).reshape(N_GATES))
        uq = w_uq[l]
        wq = jnp.concatenate([uq[..., :NOPE], uq[..., NOPE:][..., perm], uq[..., NOPE:][..., perm_sw]],
                             axis=-1).reshape(Q_LORA, H_A * HEAD_PAD).astype(BF16)
        ukv = w_ukv[l]
        wk = jnp.concatenate([ukv[..., :NOPE], jnp.zeros((KV_LORA, H_A, HEAD_PAD - NOPE), F32)],
                             axis=-1).reshape(KV_LORA, H_A * HEAD_PAD).astype(BF16)
        wv = ukv[..., NOPE:].reshape(KV_LORA, H_A * V_DIM).T.astype(BF16)
        wmo = w_mlstm_o[l].astype(BF16)
        wao = w_mla_o[l].astype(BF16)
        wout = w_out[l].astype(BF16)
        w1 = w_mlp1[l].astype(BF16)
        w2 = w_mlp2[l].astype(BF16)
        pre1, post1 = norm_pre1[l][None], norm_post1[l][None]
        pre2, post2 = norm_pre2[l][None], norm_post2[l][None]
        qn, kvn = mla_q_norm[l][None], mla_kv_norm[l][None]
        hn = mlstm_head_norm[l].reshape(1, MLSTM_W)

        mods = _modulation(cc, w_ada[l], b_ada[l][None]).reshape(8, N_MOD, D_MODEL)

        ctx_mod = lambda i: bs
        a = _stage_a(xp, mods, ctx_mod, pre1, wa, gbias, qn, kvn, wq, wk, wv, ta_ctx, tb_ctx,
                     lambda i: 0, TM_A, sp)
        mq, mk, mv, og, ga, gb, gates, q, k, v, ckv_n, kro = a
        hf, hb, c_fin, n_fin, m_fin = _mlstm(mq, mk, mv, *_gate_rows(gates, GATE_CHUNKS_PER_STEP),
                                             None, None, bp, sp)
        attn = _attention(q, k, v, None, None, bp, sp, sp, 1, n_seq=ATTN_SEQS_PER_STEP)
        xp = _stage_c(xp, mods, ctx_mod, hf, hb, og, attn, ga, gb, hn, post1, pre2, post2,
                      wmo, wao, wout, w1, w2, TM_C)
        new_ckv.append(ckv_n.reshape(bp, sp, KV_LORA))
        new_krope.append(kro.reshape(bp, sp, ROPE_DIM))
        new_c.append(c_fin.reshape(bp, 2, H_M, DH_M, DH_M))
        new_n.append(n_fin.reshape(bp, 2, H_M, DH_M))
        new_m.append(m_fin[:, :, 0].reshape(bp, 2, H_M))

        tiles_a, tiles_c = ss // TM_A, ss // TM_C
        a = _stage_a(xs, mods, lambda i: i // tiles_a, pre1, wa, gbias, qn, kvn, wq, wk, wv, ta_lat, tb_lat,
                     lambda i: i % tiles_a, TM_A, ATTN_KEY_TILE)
        mq, mk, mv, og, ga, gb, gates, q, k, v, _, _ = a
        kr_cache = jnp.zeros((bs * past, LANE), F32).at[:, NOPE:NOPE + ROPE_DIM].set(
            cache_mla_krope[:, l].reshape(bs * past, ROPE_DIM)[:, perm])
        kc, vc = _kv_cache(cache_mla_ckv[:, l].reshape(bs * past, KV_LORA), kr_cache, wk, wv, past)
        n0 = state_mlstm_n[:, l].reshape(bs, nj, DH_M, 1)
        c0 = jnp.concatenate([state_mlstm_C[:, l].reshape(bs, nj, DH_M, DH_M),
                              jnp.broadcast_to(n0, (bs, nj, DH_M, DH_M))], axis=-1)
        m0 = jnp.broadcast_to(state_mlstm_m[:, l].reshape(bs, nj, 1), (bs, nj, LANE))
        hf, hb, _, _, _ = _mlstm(mq, mk, mv, *_gate_rows(gates, GATE_CHUNKS_PER_STEP), c0, m0, bs, ss)
        attn = _attention(q, k, v, kc, vc, bs, ss, ATTN_QUERY_TILE, KEY_TILES_PER_STEP,
                          n_seq=ATTN_QUERY_TILES_PER_STEP)
        xs = _stage_c(xs, mods, lambda i: i // tiles_c, hf, hb, og, attn, ga, gb, hn, post1, pre2, post2,
                      wmo, wao, wout, w1, w2, TM_C)

    return (xp.reshape(bp, sp, D_MODEL), xs.reshape(bs, ss, D_MODEL),
            jnp.stack(new_ckv, axis=1), jnp.stack(new_krope, axis=1), jnp.stack(new_c, axis=1),
            jnp.stack(new_n, axis=1), jnp.stack(new_m, axis=1))
```

```python
import functools

import numpy as np
import jax
import jax.numpy as jnp
from jax import lax
from jax.experimental import pallas as pl
from jax.experimental.pallas import tpu as pltpu

F32 = jnp.float32
BF16 = jnp.bfloat16

D_MODEL = 1024
H_M = 4
DH_M = 128
MLSTM_W = H_M * DH_M
CHUNK = 128
H_A = 8
NOPE = 64
ROPE_DIM = 32
V_DIM = 64
Q_LORA = 384
KV_LORA = 256
AX_DIM = ROPE_DIM // 2
ROPE_BASE = 10000.0
GRID_W = 64
D_FF = 4 * D_MODEL
N_MOD = 6
EPS = 1e-6
N_GATES = 4 * H_M
M_INIT = -1e30

LANE = 128
HEAD_PAD = LANE
MISC_W = LANE
GATE_LANE0 = ROPE_DIM
OFF_MQ, OFF_MK, OFF_MV, OFF_MO, W_BIG_COLS = 0, MLSTM_W, 2 * MLSTM_W, 3 * MLSTM_W, 4 * MLSTM_W
OFF_CQ, OFF_CKV, OFF_MISC = 0, Q_LORA, Q_LORA + KV_LORA
W_SMALL_COLS = OFF_MISC + MISC_W

VMEM_LIMIT = 60 * 1024 * 1024
QK_SCALE = float((NOPE + ROPE_DIM) ** -0.5 * np.log2(np.e))
QK_LOOKAHEAD = 3
ATTN_KEY_TILE = 512
ATTN_SEQS_PER_STEP = 8
ATTN_QUERY_TILES_PER_STEP = 2
ATTN_QUERY_TILE = 256
ROW_GROUP = 256
TM_A, TM_C = 512, 512
KEY_TILES_PER_STEP = 3
ONES_ROWS = 16


def _cparams(sem):
    return pltpu.CompilerParams(dimension_semantics=sem, vmem_limit_bytes=VMEM_LIMIT)


def _const_spec(shape):
    nd = len(shape)
    return pl.BlockSpec(shape, lambda *_: (0,) * nd, pipeline_mode=pl.Buffered(1))


def _rms(x, w):
    return x * lax.rsqrt(jnp.mean(x * x, axis=-1, keepdims=True) + EPS) * w


def _sigmoid(x):
    return 1.0 / (1.0 + jnp.exp(-x))


def _log_sigmoid(x):
    return jnp.minimum(x, 0.0) - jnp.log(1.0 + jnp.exp(-jnp.abs(x)))


def _bdot(a, b):
    return jnp.dot(a, b, preferred_element_type=F32)


def _mod_kernel(c_ref, w_ref, b_ref, o_ref):
    c = c_ref[...]
    s = c * _sigmoid(c)
    s_hi = s.astype(BF16).astype(F32)
    w = w_ref[...]
    w_hi = w.astype(BF16)
    w_lo = (w - w_hi.astype(F32)).astype(BF16)
    rows = s.shape[0]
    stacked = _bdot(jnp.concatenate([s_hi, s - s_hi], axis=0).astype(BF16), w_hi)
    o_ref[...] = stacked[:rows] + stacked[rows:] + _bdot(s_hi.astype(BF16), w_lo) + b_ref[...]


def _modulation(cc, w_ada, b_ada):
    n_out = w_ada.shape[1]
    tn = 1536
    return pl.pallas_call(
        _mod_kernel,
        grid=(n_out // tn,),
        in_specs=[pl.BlockSpec((8, D_MODEL), lambda j: (0, 0)),
                  pl.BlockSpec((D_MODEL, tn), lambda j: (0, j)),
                  pl.BlockSpec((1, tn), lambda j: (0, j))],
        out_specs=pl.BlockSpec((8, tn), lambda j: (0, j)),
        out_shape=jax.ShapeDtypeStruct((8, n_out), F32),
        compiler_params=_cparams(("arbitrary",)),
        name="modulation",
    )(cc, w_ada, b_ada)


def _stage_a_kernel(x_ref, mod_ref, pre1_ref, wbig_ref, wmerge_ref, wsmall_ref, gbias_ref, qn_ref, kvn_ref,
                    wq_ref, wk_ref, wv_ref,
                    ta_ref, tb_ref,
                    mq_ref, mk_ref, mv_ref, og_ref, ga_ref, gb_ref, gates_ref, q_ref, k_ref, v_ref,
                    ckv_ref, kro_ref):
    mod = mod_ref[0]
    shift1, scale1 = mod[0:1], mod[1:2]
    tm = x_ref.shape[0]
    rows = ROW_GROUP
    tkv = v_ref.shape[2]

    def normed(rs):
        return (_rms(x_ref[rs, :], pre1_ref[...]) * (1.0 + scale1) + shift1).astype(BF16)

    def small_proj(hb):
        return tuple(_bdot(hb, wsmall_ref[:, lo:hi])
                     for lo, hi in ((OFF_CQ, OFF_CKV), (OFF_CKV, OFF_MISC), (OFF_MISC, W_SMALL_COLS)))

    def big_proj(hb, rs):
        proj = lambda lo, hi: _bdot(hb, wbig_ref[:, lo:hi])
        mq_ref[rs, :] = (proj(OFF_MQ, OFF_MK) * DH_M ** -0.5).astype(BF16)
        mk_ref[:, rs] = proj(OFF_MK, OFF_MV).T.astype(BF16)
        mv_ref[rs, :] = proj(OFF_MV, OFF_MO).astype(BF16)
        og_ref[rs, :] = _sigmoid(proj(OFF_MO, W_BIG_COLS)).astype(BF16)
        ga_ref[rs, :] = _sigmoid(_bdot(hb, wmerge_ref[:, :D_MODEL])).astype(BF16)
        gb_ref[rs, :] = _sigmoid(_bdot(hb, wmerge_ref[:, D_MODEL:])).astype(BF16)

    def mla_build(cq, ckv, misc, rs):
        lane = lax.broadcasted_iota(jnp.int32, misc.shape, 1)
        g = misc + gbias_ref[...]
        is_f = (lane >= GATE_LANE0 + 2 * H_M) & (lane < GATE_LANE0 + N_GATES)
        gates_ref[rs, :] = jnp.where(is_f, _log_sigmoid(g), g)
        kro_ref[rs, :] = misc[:, 0:ROPE_DIM]

        ckv_n = _rms(ckv, kvn_ref[...])
        ckv_ref[rs, :] = ckv_n
        ckv_b = ckv_n.astype(BF16)

        ta = ta_ref[rs, :]
        tb = tb_ref[rs, :]
        qf = _bdot(_rms(cq, qn_ref[...]).astype(BF16), wq_ref[...])
        ta8 = jnp.concatenate([ta] * H_A, axis=1)
        tb8 = jnp.concatenate([tb] * H_A, axis=1)
        q = qf * ta8 + pltpu.roll(qf, H_A * HEAD_PAD - ROPE_DIM, 1) * tb8
        q_ref[rs, :] = (q * QK_SCALE).astype(BF16)

        ta_k = jnp.where(lane < NOPE, 0.0, ta)
        kr = misc * ta_k + pltpu.roll(misc, MISC_W - ROPE_DIM, 1) * tb
        kk = _bdot(ckv_b, wk_ref[...]) + jnp.concatenate([kr] * H_A, axis=1)
        k_ref[rs, :] = kk.astype(BF16)
        vt = _bdot(wv_ref[...], ckv_n.T.astype(BF16)).astype(BF16)
        step = min(rows, tkv)
        for off in range(0, rows, step):
            lo = rs.start + off
            v_ref[lo // tkv, :, lo % tkv:lo % tkv + step] = vt[:, off:off + step]

    groups = [slice(r * rows, (r + 1) * rows) for r in range(tm // rows)]
    hbs = [normed(rs) for rs in groups]
    for hb, rs in zip(hbs, groups):
        small = small_proj(hb)
        big_proj(hb, rs)
        mla_build(*small, rs)


def _stage_a(x, mods, mod_index, pre1, wa, gbias, qn, kvn, wq, wk, wv, ta, tb, table_index, tm, tkv):
    n = x.shape[0]
    tile = lambda w: pl.BlockSpec((tm, w), lambda i: (i, 0))
    bf = lambda w: jax.ShapeDtypeStruct((n, w), BF16)
    f32 = lambda w: jax.ShapeDtypeStruct((n, w), F32)
    return pl.pallas_call(
        _stage_a_kernel,
        grid=(n // tm,),
        in_specs=[tile(D_MODEL),
                  pl.BlockSpec((1, N_MOD, D_MODEL), lambda i: (mod_index(i), 0, 0)),
                  _const_spec((1, D_MODEL)),
                  _const_spec((D_MODEL, W_BIG_COLS)),
                  _const_spec((D_MODEL, 2 * D_MODEL)),
                  _const_spec((D_MODEL, W_SMALL_COLS)),
                  _const_spec((1, MISC_W)),
                  _const_spec((1, Q_LORA)),
                  _const_spec((1, KV_LORA)),
                  _const_spec((Q_LORA, H_A * HEAD_PAD)),
                  _const_spec((KV_LORA, H_A * HEAD_PAD)),
                  _const_spec((H_A * V_DIM, KV_LORA)),
                  pl.BlockSpec((tm, LANE), lambda i: (table_index(i), 0)),
                  pl.BlockSpec((tm, LANE), lambda i: (table_index(i), 0))],
        out_specs=[tile(MLSTM_W), pl.BlockSpec((MLSTM_W, tm), lambda i: (0, i)), tile(MLSTM_W), tile(MLSTM_W),
                   tile(D_MODEL), tile(D_MODEL), tile(MISC_W),
                   tile(H_A * HEAD_PAD), tile(H_A * HEAD_PAD),
                   pl.BlockSpec((tm // tkv, H_A * V_DIM, tkv), lambda i: (i, 0, 0)),
                   tile(KV_LORA), tile(ROPE_DIM)],
        out_shape=[bf(MLSTM_W), jax.ShapeDtypeStruct((MLSTM_W, n), BF16), bf(MLSTM_W), bf(MLSTM_W),
                   bf(D_MODEL), bf(D_MODEL), f32(MISC_W),
                   bf(H_A * HEAD_PAD), bf(H_A * HEAD_PAD),
                   jax.ShapeDtypeStruct((n // tkv, H_A * V_DIM, tkv), BF16),
                   f32(KV_LORA), f32(ROPE_DIM)],
        compiler_params=_cparams(("arbitrary",)),
        name="stage_a",
    )(x, mods, pre1, *wa, gbias, qn, kvn, wq, wk, wv, ta, tb)


def _kv_cache_kernel(ckv_ref, kr_ref, wk_ref, wv_ref, k_ref, v_ref):
    ckv_b = ckv_ref[...].astype(BF16)
    kk = _bdot(ckv_b, wk_ref[...]) + jnp.concatenate([kr_ref[...]] * H_A, axis=1)
    k_ref[...] = kk.astype(BF16)
    v_ref[0] = _bdot(wv_ref[...], ckv_ref[...].T.astype(BF16)).astype(BF16)


def _kv_cache(ckv, kr, wk, wv, tm):
    n = ckv.shape[0]
    return pl.pallas_call(
        _kv_cache_kernel,
        grid=(n // tm,),
        in_specs=[pl.BlockSpec((tm, KV_LORA), lambda i: (i, 0)),
                  pl.BlockSpec((tm, LANE), lambda i: (i, 0)),
                  _const_spec((KV_LORA, H_A * HEAD_PAD)),
                  _const_spec((H_A * V_DIM, KV_LORA))],
        out_specs=[pl.BlockSpec((tm, H_A * HEAD_PAD), lambda i: (i, 0)),
                   pl.BlockSpec((1, H_A * V_DIM, tm), lambda i: (i, 0, 0))],
        out_shape=[jax.ShapeDtypeStruct((n, H_A * HEAD_PAD), BF16),
                   jax.ShapeDtypeStruct((n // tm, H_A * V_DIM, tm), BF16)],
        compiler_params=_cparams(("arbitrary",)),
        name="kv_cache",
    )(ckv, kr, wk, wv)


GATE_ROWS = 32
MLSTM_SEQS_PER_STEP = 4
MLSTM_CHUNKS_PER_STEP = 4
GATE_CHUNKS_PER_STEP = 32


def _gate_rows_kernel(g_ref, cols_ref, rows_ref):
    nj = 2 * H_M
    nch = g_ref.shape[0] // CHUNK
    n_rows = nch * nj
    row = lax.broadcasted_iota(jnp.int32, (CHUNK, CHUNK), 0)
    col = lax.broadcasted_iota(jnp.int32, (CHUNK, CHUNK), 1)
    lower = (col <= row).astype(F32)
    upper = (col >= row).astype(F32)
    hi = lax.Precision.HIGHEST
    rowi = lax.broadcasted_iota(jnp.int32, (n_rows, CHUNK), 0)
    lane = lax.broadcasted_iota(jnp.int32, (n_rows, CHUNK), 1)
    is_fwd = rowi % nj < H_M

    gi, gf = GATE_LANE0, GATE_LANE0 + nj
    i_rows, f_rows = [], []
    for c in range(nch):
        g_t = g_ref[c * CHUNK:(c + 1) * CHUNK, :].T
        i_rows.append(g_t[gi:gi + nj])
        f_rows.append(g_t[gf:gf + nj])
    i_all = jnp.concatenate(i_rows, axis=0)
    f_all = jnp.concatenate(f_rows, axis=0)
    b = jnp.where(is_fwd, jnp.dot(f_all, upper, preferred_element_type=F32, precision=hi),
                  jnp.dot(f_all, lower, preferred_element_type=F32, precision=hi))
    a = i_all - b
    cm = a
    shift = 1
    while shift < CHUNK:
        y_f = jnp.where(lane >= shift, pltpu.roll(cm, shift, 1), -jnp.inf)
        y_b = jnp.where(lane < CHUNK - shift, pltpu.roll(cm, CHUNK - shift, 1), -jnp.inf)
        cm = jnp.maximum(cm, jnp.where(is_fwd, y_f, y_b))
        shift *= 2
    a_max = jnp.broadcast_to(jnp.max(a, axis=1, keepdims=True), a.shape)
    f_sum = jnp.broadcast_to(jnp.sum(f_all, axis=1, keepdims=True), a.shape)
    zeros8 = jnp.zeros((nj, CHUNK), F32)
    for c in range(nch):
        sl = slice(c * nj, (c + 1) * nj)
        stack = jnp.concatenate([cm[sl], b[sl]] + [zeros8] * (CHUNK // nj - 2), axis=0)
        cols_ref[c * CHUNK:(c + 1) * CHUNK, :] = stack.T
        rows_ref[c] = jnp.concatenate([a[sl], a_max[sl], f_sum[sl], zeros8], axis=0)


def _gate_rows(gates, chunks_per_step):
    n = gates.shape[0]
    tm = chunks_per_step * CHUNK
    return pl.pallas_call(
        _gate_rows_kernel,
        grid=(n // tm,),
        in_specs=[pl.BlockSpec((tm, MISC_W), lambda i: (i, 0))],
        out_specs=[pl.BlockSpec((tm, LANE), lambda i: (i, 0)),
                   pl.BlockSpec((chunks_per_step, GATE_ROWS, CHUNK), lambda i: (i, 0, 0))],
        out_shape=[jax.ShapeDtypeStruct((n, LANE), F32),
                   jax.ShapeDtypeStruct((n // CHUNK, GATE_ROWS, CHUNK), F32)],
        compiler_params=_cparams(("arbitrary",)),
        name="gate_rows",
    )(gates)


def _mlstm_kernel(*refs, has_state, n_seq):
    if has_state:
        (qf_ref, ktf_ref, vf_ref, colsf_ref, rf_ref, qb_ref, ktb_ref, vb_ref, colsb_ref, rb_ref, c0_ref, m0_ref,
         hf_ref, hb_ref, cn_ref, nn_ref, mn_ref, c_s, m_s) = refs
    else:
        (qf_ref, ktf_ref, vf_ref, colsf_ref, rf_ref, qb_ref, ktb_ref, vb_ref, colsb_ref, rb_ref,
         hf_ref, hb_ref, cn_ref, nn_ref, mn_ref, c_s, m_s) = refs
    step = pl.program_id(1)

    nj = 2 * H_M

    @pl.when(step == 0)
    def _():
        if has_state:
            for sq in range(n_seq):
                c_s[sq * nj:(sq + 1) * nj] = c0_ref[sq]
                m_s[sq * nj:(sq + 1) * nj] = m0_ref[sq]
        else:
            c_s[...] = jnp.zeros(c_s.shape, F32)
            m_s[...] = jnp.full(m_s.shape, M_INIT, F32)

    row = lax.broadcasted_iota(jnp.int32, (CHUNK, CHUNK), 0)
    col = lax.broadcasted_iota(jnp.int32, (CHUNK, CHUNK), 1)
    is_fwd = lax.broadcasted_iota(jnp.int32, (nj, CHUNK), 0) < H_M
    ones_v = jnp.ones((CHUNK, DH_M), BF16)
    dirs = ((qf_ref, ktf_ref, vf_ref, colsf_ref, hf_ref), (qb_ref, ktb_ref, vb_ref, colsb_ref, hb_ref))
    chains = [(sq, d, hd) for sq in range(n_seq) for d in range(2) for hd in range(H_M)]
    n_sub = rf_ref.shape[0] // n_seq
    for sub in range(n_sub):
        sub_of = (sub, n_sub - 1 - sub)
        gate = []
        for sq in range(n_seq):
            rows_f, rows_b = rf_ref[sq * n_sub + sub_of[0]], rb_ref[sq * n_sub + sub_of[1]]
            pick = lambda g: jnp.where(is_fwd, rows_f[g * nj:(g + 1) * nj], rows_b[g * nj:(g + 1) * nj])
            a8, a_max8, f_sum8 = pick(0), pick(1), pick(2)
            m8 = m_s[sq * nj:(sq + 1) * nj]
            mx8 = jnp.maximum(m8, a_max8)
            gate.append((a8, m8, jnp.exp(a8 - mx8), jnp.exp(m8 - mx8)))
            m_s[sq * nj:(sq + 1) * nj] = f_sum8 + mx8
        tok = lambda sq, d: slice((sq * n_sub + sub_of[d]) * CHUNK, (sq * n_sub + sub_of[d] + 1) * CHUNK)

        first = []
        for sq, d, hd in chains:
            q_ref, kt_ref, v_ref, _, _ = dirs[d]
            j = d * H_M + hd
            sl = slice(hd * DH_M, (hd + 1) * DH_M)
            q = q_ref[tok(sq, d), sl]
            k_t = kt_ref[sl, tok(sq, d)]
            v_aug = jnp.concatenate([v_ref[tok(sq, d), sl], ones_v], axis=1)
            qk = _bdot(q, k_t)
            c_st = c_s[sq * nj + j]
            qc = _bdot(q, c_st.astype(BF16))
            upd = _bdot((k_t.astype(F32) * gate[sq][2][j:j + 1, :]).astype(BF16), v_aug)
            dec = gate[sq][3][j:j + 1, :]
            c_s[sq * nj + j] = jnp.concatenate([dec, dec], axis=1) * c_st + upd
            first.append((qk, qc, v_aug))
        for (sq, d, hd), (qk, qc, v_aug) in zip(chains, first):
            cols_ref, h_ref = dirs[d][3], dirs[d][4]
            j = d * H_M + hd
            sl = slice(hd * DH_M, (hd + 1) * DH_M)
            mask = (col <= row) if d == 0 else (col >= row)
            a8, m8 = gate[sq][0], gate[sq][1]
            m_row = m8[j:j + 1, :]
            u = jnp.maximum(m_row, cols_ref[tok(sq, d), j:j + 1])
            w = jnp.exp(jnp.where(mask, a8[j:j + 1, :] - u, -jnp.inf))
            s_inter = jnp.exp(m_row - u)
            intra = _bdot((qk * w).astype(BF16), v_aug)
            num = s_inter * qc[:, :DH_M] + intra[:, :DH_M]
            den = s_inter * qc[:, DH_M:] + intra[:, DH_M:]
            h_ref[tok(sq, d), sl] = num / jnp.maximum(
                jnp.abs(den), jnp.exp(-(cols_ref[tok(sq, d), nj + j:nj + j + 1] + u)))

    @pl.when(step == pl.num_programs(1) - 1)
    def _():
        for sq in range(n_seq):
            for j in range(nj):
                c_aug = c_s[sq * nj + j]
                cn_ref[sq, j] = c_aug[:, :DH_M]
                nn_ref[sq, j:j + 1, :] = c_aug[:, DH_M:].T[0:1, :]
            mn_ref[sq] = m_s[sq * nj:(sq + 1) * nj]


def _mlstm(mq, mkt, mv, gate_cols, gate_rows, c0, m0, batch, seq):
    has_state = c0 is not None
    n_sub = min(MLSTM_CHUNKS_PER_STEP, seq // CHUNK)
    nc = seq // (n_sub * CHUNK)
    n_seq = MLSTM_SEQS_PER_STEP if nc == 1 and batch % MLSTM_SEQS_PER_STEP == 0 else 1
    tm = n_seq * n_sub * CHUNK
    n = batch * seq
    nj = 2 * H_M
    fwd = lambda w: pl.BlockSpec((tm, w), lambda b, c: (b * nc + c, 0))
    bwd = lambda w: pl.BlockSpec((tm, w), lambda b, c: (b * nc + nc - 1 - c, 0))
    fwd_t = pl.BlockSpec((MLSTM_W, tm), lambda b, c: (0, b * nc + c))
    bwd_t = pl.BlockSpec((MLSTM_W, tm), lambda b, c: (0, b * nc + nc - 1 - c))
    st_m = pl.BlockSpec((n_seq, nj, LANE), lambda b, c: (b, 0, 0))
    st_c = pl.BlockSpec((n_seq, nj, DH_M, 2 * DH_M), lambda b, c: (b, 0, 0, 0))
    fwd_r = pl.BlockSpec((n_seq * n_sub, GATE_ROWS, CHUNK), lambda b, c: (b * nc + c, 0, 0))
    bwd_r = pl.BlockSpec((n_seq * n_sub, GATE_ROWS, CHUNK), lambda b, c: (b * nc + nc - 1 - c, 0, 0))
    in_specs = [fwd(MLSTM_W), fwd_t, fwd(MLSTM_W), fwd(LANE), fwd_r,
                bwd(MLSTM_W), bwd_t, bwd(MLSTM_W), bwd(LANE), bwd_r]
    args = [mq, mkt, mv, gate_cols, gate_rows, mq, mkt, mv, gate_cols, gate_rows]
    if has_state:
        in_specs += [st_c, st_m]
        args += [c0, m0]
    return pl.pallas_call(
        functools.partial(_mlstm_kernel, has_state=has_state, n_seq=n_seq),
        grid=(batch // n_seq, nc),
        in_specs=in_specs,
        out_specs=[fwd(MLSTM_W), bwd(MLSTM_W),
                   pl.BlockSpec((n_seq, nj, DH_M, DH_M), lambda b, c: (b, 0, 0, 0)), st_m, st_m],
        out_shape=[jax.ShapeDtypeStruct((n, MLSTM_W), F32), jax.ShapeDtypeStruct((n, MLSTM_W), F32),
                   jax.ShapeDtypeStruct((batch, nj, DH_M, DH_M), F32),
                   jax.ShapeDtypeStruct((batch, nj, DH_M), F32),
                   jax.ShapeDtypeStruct((batch, nj, LANE), F32)],
        scratch_shapes=[pltpu.VMEM((n_seq * nj, DH_M, 2 * DH_M), F32), pltpu.VMEM((n_seq * nj, LANE), F32)],
        compiler_params=_cparams(("arbitrary", "arbitrary")),
        name="mlstm",
    )(*args)


def _attn_kernel(*refs, tiles_per_step, has_cache, n_seq, shared_keys):
    if has_cache:
        q_ref, k_ref, vt_ref, kc_ref, vct_ref, o_ref, qt_s, acc_s, m_s, ot_s, s_scr = refs
    else:
        q_ref, k_ref, vt_ref, o_ref, qt_s, acc_s, m_s, ot_s, s_scr = refs
    tk = vt_ref.shape[2]
    n_tiles = vt_ref.shape[0] // (1 if shared_keys else n_seq)
    tq = q_ref.shape[0] // n_seq
    d_q, d_o = H_A * HEAD_PAD, H_A * V_DIM
    for sq in range(n_seq):
        qt_s[sq * d_q:(sq + 1) * d_q, :] = q_ref[sq * tq:(sq + 1) * tq, :].astype(F32).T.astype(BF16)
    acc_s[...] = jnp.zeros(acc_s.shape, F32)
    m_s[...] = jnp.full(m_s.shape, -jnp.inf, F32)

    n_slots = s_scr.shape[0]

    def scores(tile, head, slot):
        get_k, _, width, sq = tile
        stream = sq * H_A + head
        s = _bdot(get_k(head), qt_s[stream * HEAD_PAD:(stream + 1) * HEAD_PAD, :])
        s_scr[slot, 0:width, :] = s
        return jnp.max(s, axis=0, keepdims=True)

    def update(tiles, pending, next_tile):
        items = [(tile, head) for tile in tiles for head in range(H_A)]
        ahead = items + ([(next_tile, h) for h in range(QK_LOOKAHEAD)] if next_tile is not None else [])
        pending = list(pending)
        for idx, ((_, get_vt, width, sq), head) in enumerate(items):
            stream = sq * H_A + head
            m_tile = pending.pop(0)
            if idx + QK_LOOKAHEAD < len(ahead):
                pending.append(scores(*ahead[idx + QK_LOOKAHEAD], (idx + QK_LOOKAHEAD) % n_slots))
            m_old = m_s[stream:stream + 1, :]
            m_new = jnp.maximum(m_old, m_tile)
            alpha = jnp.exp2(m_old - m_new)
            p = jnp.exp2((s_scr[idx % n_slots, 0:width, :] - m_new).astype(BF16))
            ones = jnp.ones((ONES_ROWS, width), BF16)
            lhs = jnp.concatenate([get_vt(head), ones], axis=0)
            acc_s[stream] = alpha * acc_s[stream] + _bdot(lhs, p)
            m_s[stream:stream + 1, :] = m_new
        return tuple(pending)

    def main_tile(t, sq=0):
        base = 0 if shared_keys else sq * n_tiles
        start = (base + t) * tk if isinstance(t, int) else pl.multiple_of(t * tk, tk)
        rows = pl.ds(start, tk)
        return (lambda h: k_ref[rows, h * HEAD_PAD:(h + 1) * HEAD_PAD],
                lambda h: vt_ref[base + t, h * V_DIM:(h + 1) * V_DIM, :], tk, sq)

    def body(sq, i, pending):
        first = i * tiles_per_step
        return update([main_tile(first + u, sq) for u in range(tiles_per_step)], pending,
                      main_tile(first + tiles_per_step, sq))

    n_loop = (n_tiles - 1) // tiles_per_step

    def tail_tiles(sq):
        tiles = [main_tile(t, sq) for t in range(n_loop * tiles_per_step, n_tiles)]
        if has_cache:
            tiles.append((lambda h: kc_ref[:, h * HEAD_PAD:(h + 1) * HEAD_PAD],
                          lambda h: vct_ref[0, h * V_DIM:(h + 1) * V_DIM, :], kc_ref.shape[0], sq))
        return tiles

    pending = tuple(scores(main_tile(0, 0), h, h) for h in range(QK_LOOKAHEAD))
    if n_loop:
        assert (tiles_per_step * H_A) % n_slots == 0 and (len(tail_tiles(0)) * H_A) % n_slots == 0
        for sq in range(n_seq):
            pending = lax.fori_loop(0, n_loop, functools.partial(body, sq), pending)
            pending = update(tail_tiles(sq), pending, main_tile(0, sq + 1) if sq + 1 < n_seq else None)
    else:
        update([tile for sq in range(n_seq) for tile in tail_tiles(sq)], pending, None)
    for stream in range(n_seq * H_A):
        acc = acc_s[stream]
        ot_s[stream * V_DIM:(stream + 1) * V_DIM, :] = acc[0:V_DIM] * (1.0 / acc[V_DIM:V_DIM + 1])
    for sq in range(n_seq):
        o_ref[sq * tq:(sq + 1) * tq, :] = ot_s[sq * d_o:(sq + 1) * d_o, :].T.astype(BF16)


def _attention(q, k, vt, kc, vct, batch, seq, tq, tiles_per_step, n_seq=1):
    nq = seq // tq
    tk = vt.shape[2]
    n_tiles = seq // tk
    has_cache = kc is not None
    shared_keys = n_seq > 1 and nq > 1
    if shared_keys:
        assert nq % n_seq == 0
        nq, key_seqs, grid = nq // n_seq, 1, (batch, nq // n_seq)
    else:
        assert n_seq == 1 or (nq == 1 and n_tiles == 1 and not has_cache and batch % n_seq == 0)
        key_seqs, grid = n_seq, (batch // n_seq, nq)
    max_width = max(tk, kc.shape[0] // batch) if has_cache else tk
    in_specs = [pl.BlockSpec((n_seq * tq, H_A * HEAD_PAD), lambda b, i: (b * nq + i, 0)),
                pl.BlockSpec((key_seqs * seq, H_A * HEAD_PAD), lambda b, i: (b, 0)),
                pl.BlockSpec((key_seqs * n_tiles, H_A * V_DIM, tk), lambda b, i: (b, 0, 0))]
    args = [q, k, vt]
    if has_cache:
        past = kc.shape[0] // batch
        in_specs += [pl.BlockSpec((past, H_A * HEAD_PAD), lambda b, i: (b, 0)),
                     pl.BlockSpec((1, H_A * V_DIM, past), lambda b, i: (b, 0, 0))]
        args += [kc, vct]
    return pl.pallas_call(
        functools.partial(_attn_kernel, tiles_per_step=tiles_per_step, has_cache=has_cache, n_seq=n_seq,
                          shared_keys=shared_keys),
        grid=grid,
        in_specs=in_specs,
        out_specs=pl.BlockSpec((n_seq * tq, H_A * V_DIM), lambda b, i: (b * nq + i, 0)),
        out_shape=jax.ShapeDtypeStruct((batch * seq, H_A * V_DIM), BF16),
        scratch_shapes=[pltpu.VMEM((n_seq * H_A * HEAD_PAD, tq), BF16),
                        pltpu.VMEM((n_seq * H_A, V_DIM + ONES_ROWS, tq), F32),
                        pltpu.VMEM((n_seq * H_A, tq), F32),
                        pltpu.VMEM((n_seq * H_A * V_DIM, tq), F32),
                        pltpu.VMEM((QK_LOOKAHEAD + 1, max_width, tq), F32)],
        compiler_params=_cparams(("arbitrary", "arbitrary")),
        name="mla_attention",
    )(*args)


def _stage_c_kernel(x_ref, mod_ref, hf_ref, hb_ref, og_ref, attn_ref, ga_ref, gb_ref, hn_ref,
                    post1_ref, pre2_ref, post2_ref, wmo_ref, wao_ref, wout_ref, w1_ref, w2_ref, y_ref):
    mod = mod_ref[0]
    gate1, shift2, scale2, gate2 = mod[2:3], mod[3:4], mod[4:5], mod[5:6]
    hn = hn_ref[...]
    rows = ROW_GROUP
    groups = [slice(r * rows, (r + 1) * rows) for r in range(x_ref.shape[0] // rows)]
    n_ff = D_FF // D_MODEL

    def mixer_in(rs):
        hm = hf_ref[rs, :] + hb_ref[rs, :]
        heads = [_rms(hm[:, hd * DH_M:(hd + 1) * DH_M], hn[:, hd * DH_M:(hd + 1) * DH_M]) for hd in range(H_M)]
        return (jnp.concatenate(heads, axis=1) * og_ref[rs, :].astype(F32)).astype(BF16)

    def branch_out(hm, rs):
        return _bdot(hm, wmo_ref[...]), _bdot(attn_ref[rs, :], wao_ref[...])

    def merge(y_m, y_a, rs):
        return (ga_ref[rs, :].astype(F32) * y_m + gb_ref[rs, :].astype(F32) * y_a).astype(BF16)

    def mid(mix, rs):
        x1 = x_ref[rs, :] + gate1 * _rms(mix, post1_ref[...])
        return x1, (_rms(x1, pre2_ref[...]) * (1.0 + scale2) + shift2).astype(BF16)

    def mlp(h2):
        up = lambda c: _bdot(h2, w1_ref[:, c * D_MODEL:(c + 1) * D_MODEL])
        ff = jnp.zeros((rows, D_MODEL), F32)
        nxt = up(0)
        for c in range(n_ff):
            a = jnp.maximum(nxt, 0.0)
            if c + 1 < n_ff:
                nxt = up(c + 1)
            ff = ff + _bdot((a * a).astype(BF16), w2_ref[c * D_MODEL:(c + 1) * D_MODEL, :])
        return ff

    hms = [mixer_in(rs) for rs in groups]
    ys = [branch_out(hm, rs) for hm, rs in zip(hms, groups)]
    mixes = [_bdot(merge(*y, rs), wout_ref[...]) for y, rs in zip(ys, groups)]
    mids = [mid(mix, rs) for mix, rs in zip(mixes, groups)]
    ffs = [mlp(h2) for _, h2 in mids]
    for (x1, _), ff, rs in zip(mids, ffs, groups):
        y_ref[rs, :] = x1 + gate2 * _rms(ff, post2_ref[...])


def _stage_c(x, mods, mod_index, hf, hb, og, attn, ga, gb, hn, post1, pre2, post2, wmo, wao, wout, w1, w2, tm):
    n = x.shape[0]
    tile = lambda w: pl.BlockSpec((tm, w), lambda i: (i, 0))
    return pl.pallas_call(
        _stage_c_kernel,
        grid=(n // tm,),
        in_specs=[tile(D_MODEL),
                  pl.BlockSpec((1, N_MOD, D_MODEL), lambda i: (mod_index(i), 0, 0)),
                  tile(MLSTM_W), tile(MLSTM_W), tile(MLSTM_W), tile(H_A * V_DIM),
                  tile(D_MODEL), tile(D_MODEL),
                  _const_spec((1, MLSTM_W)), _const_spec((1, D_MODEL)), _const_spec((1, D_MODEL)),
                  _const_spec((1, D_MODEL)),
                  _const_spec((MLSTM_W, D_MODEL)), _const_spec((H_A * V_DIM, D_MODEL)),
                  _const_spec((D_MODEL, D_MODEL)), _const_spec((D_MODEL, D_FF)), _const_spec((D_FF, D_MODEL))],
        out_specs=tile(D_MODEL),
        out_shape=jax.ShapeDtypeStruct((n, D_MODEL), F32),
        compiler_params=_cparams(("arbitrary",)),
        name="stage_c",
    )(x, mods, hf, hb, og, attn, ga, gb, hn, post1, pre2, post2, wmo, wao, wout, w1, w2)


def _rope_tables(n_tokens):
    pos = np.arange(n_tokens)
    row = (pos // GRID_W).astype(np.float32)
    col = (pos % GRID_W).astype(np.float32)
    inv = (ROPE_BASE ** (-np.arange(0, AX_DIM, 2, dtype=np.float32) / AX_DIM)).astype(np.float32)
    ang = np.concatenate([row[:, None] * inv, col[:, None] * inv], axis=-1)
    cos = np.cos(ang.astype(np.float64)).astype(np.float32)
    sin = np.sin(ang.astype(np.float64)).astype(np.float32)
    ones = np.ones((n_tokens, NOPE), np.float32)
    zeros = np.zeros((n_tokens, NOPE), np.float32)
    pad = np.zeros((n_tokens, HEAD_PAD - NOPE - ROPE_DIM), np.float32)
    ta = np.concatenate([ones, cos, cos, pad], axis=-1)
    tb = np.concatenate([zeros, -sin, sin, pad], axis=-1)
    return jnp.asarray(ta), jnp.asarray(tb)


def _plain_tables(n_tokens):
    ones = jnp.ones((n_tokens, NOPE + ROPE_DIM), F32)
    pad = jnp.zeros((n_tokens, HEAD_PAD - NOPE - ROPE_DIM), F32)
    return jnp.concatenate([ones, pad], axis=-1), jnp.zeros((n_tokens, HEAD_PAD), F32)


def kernel(x_prompt, x_sample, cache_mla_ckv, cache_mla_krope, state_mlstm_C, state_mlstm_n, state_mlstm_m,
           c, c_ctx, w_ada, b_ada, norm_pre1, norm_post1, norm_pre2, norm_post2, w_in, mlstm_gate_b,
           mla_q_norm, mla_kv_norm, w_uq, w_ukv, w_mla_o, mlstm_head_norm, w_mlstm_o, w_out, w_mlp1, w_mlp2):
    bp, sp, _ = x_prompt.shape
    bs, ss, _ = x_sample.shape
    depth = w_in.shape[0]
    past = cache_mla_ckv.shape[2]
    nj = 2 * H_M
    even = np.arange(0, ROPE_DIM, 2)
    odd = np.arange(1, ROPE_DIM, 2)
    perm = np.concatenate([even, odd])
    perm_sw = np.concatenate([odd, even])

    xp = x_prompt.reshape(bp * sp, D_MODEL)
    xs = x_sample.reshape(bs * ss, D_MODEL)
    cc = jnp.zeros((8, D_MODEL), F32).at[:bs].set(c).at[bs].set(c_ctx)
    ta_lat, tb_lat = _rope_tables(ss)
    assert past == ATTN_KEY_TILE
    ta_ctx, tb_ctx = _plain_tables(TM_A)

    new_ckv, new_krope, new_c, new_n, new_m = [], [], [], [], []
    for l in range(depth):
        cols = np.cumsum((W_BIG_COLS, N_GATES, Q_LORA, KV_LORA, ROPE_DIM))
        w_big, w_g, w_cq, w_ckv, w_kr, w_merge = jnp.split(w_in[l].astype(BF16), cols.tolist(), axis=1)
        w_g = w_g.reshape(D_MODEL, 2, 2, H_M).transpose(0, 2, 1, 3).reshape(D_MODEL, N_GATES)
        w_misc = jnp.concatenate([w_kr, w_g, jnp.zeros((D_MODEL, MISC_W - 3 * ROPE_DIM - N_GATES), BF16),
                                  w_kr[:, perm], w_kr[:, perm_sw]], axis=1)
        wa = (w_big, w_merge, jnp.concatenate([w_cq, w_ckv, w_misc], axis=1))
        gbias = jnp.zeros((1, MISC_W), F32).at[0, GATE_LANE0:GATE_LANE0 + N_GATES].set(
            mlstm_gate_b[l].transpose(1, 0, 2).reshape(N_GATES))
        uq = w_uq[l]
        wq = jnp.concatenate([uq[..., :NOPE], uq[..., NOPE:][..., perm], uq[..., NOPE:][..., perm_sw]],
                             axis=-1).reshape(Q_LORA, H_A * HEAD_PAD).astype(BF16)
        ukv = w_ukv[l]
        wk = jnp.concatenate([ukv[..., :NOPE], jnp.zeros((KV_LORA, H_A, HEAD_PAD - NOPE), F32)],
                             axis=-1).reshape(KV_LORA, H_A * HEAD_PAD).astype(BF16)
        wv = ukv[..., NOPE:].reshape(KV_LORA, H_A * V_DIM).T.astype(BF16)
        wmo = w_mlstm_o[l].astype(BF16)
        wao = w_mla_o[l].astype(BF16)
        wout = w_out[l].astype(BF16)
        w1 = w_mlp1[l].astype(BF16)
        w2 = w_mlp2[l].astype(BF16)
        pre1, post1 = norm_pre1[l][None], norm_post1[l][None]
        pre2, post2 = norm_pre2[l][None], norm_post2[l][None]
        qn, kvn = mla_q_norm[l][None], mla_kv_norm[l][None]
        hn = mlstm_head_norm[l].reshape(1, MLSTM_W)

        mods = _modulation(cc, w_ada[l], b_ada[l][None]).reshape(8, N_MOD, D_MODEL)

        ctx_mod = lambda i: bs
        a = _stage_a(xp, mods, ctx_mod, pre1, wa, gbias, qn, kvn, wq, wk, wv, ta_ctx, tb_ctx,
                     lambda i: 0, TM_A, sp)
        mq, mk, mv, og, ga, gb, gates, q, k, v, ckv_n, kro = a
        hf, hb, c_fin, n_fin, m_fin = _mlstm(mq, mk, mv, *_gate_rows(gates, GATE_CHUNKS_PER_STEP),
                                             None, None, bp, sp)
        attn = _attention(q, k, v, None, None, bp, sp, sp, 1, n_seq=ATTN_SEQS_PER_STEP)
        xp = _stage_c(xp, mods, ctx_mod, hf, hb, og, attn, ga, gb, hn, post1, pre2, post2,
                      wmo, wao, wout, w1, w2, TM_C)
        new_ckv.append(ckv_n.reshape(bp, sp, KV_LORA))
        new_krope.append(kro.reshape(bp, sp, ROPE_DIM))
        new_c.append(c_fin.reshape(bp, 2, H_M, DH_M, DH_M))
        new_n.append(n_fin.reshape(bp, 2, H_M, DH_M))
        new_m.append(m_fin[:, :, 0].reshape(bp, 2, H_M))

        tiles_a, tiles_c = ss // TM_A, ss // TM_C
        a = _stage_a(xs, mods, lambda i: i // tiles_a, pre1, wa, gbias, qn, kvn, wq, wk, wv, ta_lat, tb_lat,
                     lambda i: i % tiles_a, TM_A, ATTN_KEY_TILE)
        mq, mk, mv, og, ga, gb, gates, q, k, v, _, _ = a
        kr_cache = jnp.zeros((bs * past, LANE), F32).at[:, NOPE:NOPE + ROPE_DIM].set(
            cache_mla_krope[:, l].reshape(bs * past, ROPE_DIM)[:, perm])
        kc, vc = _kv_cache(cache_mla_ckv[:, l].reshape(bs * past, KV_LORA), kr_cache, wk, wv, past)
        n0 = state_mlstm_n[:, l].reshape(bs, nj, DH_M, 1)
        c0 = jnp.concatenate([state_mlstm_C[:, l].reshape(bs, nj, DH_M, DH_M),
                              jnp.broadcast_to(n0, (bs, nj, DH_M, DH_M))], axis=-1)
        m0 = jnp.broadcast_to(state_mlstm_m[:, l].reshape(bs, nj, 1), (bs, nj, LANE))
        hf, hb, _, _, _ = _mlstm(mq, mk, mv, *_gate_rows(gates, GATE_CHUNKS_PER_STEP), c0, m0, bs, ss)
        attn = _attention(q, k, v, kc, vc, bs, ss, ATTN_QUERY_TILE, KEY_TILES_PER_STEP,
                          n_seq=ATTN_QUERY_TILES_PER_STEP)
        xs = _stage_c(xs, mods, lambda i: i // tiles_c, hf, hb, og, attn, ga, gb, hn, post1, pre2, post2,
                      wmo, wao, wout, w1, w2, TM_C)

    return (xp.reshape(bp, sp, D_MODEL), xs.reshape(bs, ss, D_MODEL),
            jnp.stack(new_ckv, axis=1), jnp.stack(new_krope, axis=1), jnp.stack(new_c, axis=1),
            jnp.stack(new_n, axis=1), jnp.stack(new_m, axis=1))
```
